```python
import jax, jax.numpy as jnp
from jax import lax
import numpy as np


D_MODEL = 1024
BATCH = 2
SEQ = 8192
DEPTH = 1

PLE_DIM = 256
D_MIX = D_MODEL
GDN_HEADS = 4
GDN_DK = 128
GDN_DV = 128
GDN_QK_W = GDN_HEADS * GDN_DK
GDN_V_W = GDN_HEADS * GDN_DV
GDN_CONV = 4
GDN_CHUNK = 64
CONF_CH = D_MIX - GDN_V_W
CONF_WIDTH = 31
N_GROUPS = 8
EXPERTS_PER_GROUP = 8
N_EXPERTS = N_GROUPS * EXPERTS_PER_GROUP
TOP_K = 2
D_EXPERT = 256
MOE_BLOCK = 128
EPS = 1e-6
IN_COLS = 2 * GDN_QK_W + 2 * GDN_V_W + 2 * GDN_HEADS + 2 * CONF_CH

kernel_name = 'hybrid_gdn_conformer_hmoe_block'


def _rmsnorm(x, g):
    xf = x.astype(jnp.float32)
    y = xf * lax.rsqrt(jnp.mean(xf * xf, axis=-1, keepdims=True) + EPS)
    return (y * g.astype(jnp.float32)).astype(x.dtype)


def _layernorm(x, g, b):
    xf = x.astype(jnp.float32)
    mu = jnp.mean(xf, axis=-1, keepdims=True)
    var = jnp.mean(jnp.square(xf - mu), axis=-1, keepdims=True)
    y = (xf - mu) * lax.rsqrt(var + EPS)
    return (y * g.astype(jnp.float32) + b.astype(jnp.float32)).astype(x.dtype)


def _l2norm(x):
    return x * lax.rsqrt(jnp.sum(x * x, axis=-1, keepdims=True) + EPS)


def _causal_depthwise_conv(x, w):
    K, C = w.shape
    return lax.conv_general_dilated(
        x, w[:, None, :].astype(x.dtype), window_strides=(1,), padding=[(K - 1, 0)],
        dimension_numbers=('NWC', 'WIO', 'NWC'), feature_group_count=C)


def _gated_delta_rule_chunked(q, k, v, g, beta):
    B, T, H, DK = q.shape
    DV = v.shape[-1]
    C = GDN_CHUNK
    NC = T // C

    def chunks(t):
        return jnp.moveaxis(t.reshape((B, NC, C, H) + t.shape[3:]), 3, 1)

    q, k, v, g, beta = chunks(q), chunks(k), chunks(v), chunks(g), chunks(beta)
    gc = jnp.cumsum(g, axis=-1)
    causal = jnp.tril(jnp.ones((C, C), dtype=bool))
    strict = jnp.tril(jnp.ones((C, C), dtype=bool), -1)
    decay = jnp.exp(jnp.where(causal, gc[..., :, None] - gc[..., None, :], -jnp.inf))
    k_beta = k * beta[..., None]
    v_beta = v * beta[..., None]
    L = jnp.where(strict, jnp.einsum('bhncd,bhnsd->bhncs', k_beta, k) * decay, 0.0)
    eye = jnp.broadcast_to(jnp.eye(C, dtype=L.dtype), L.shape)
    Tm = lax.linalg.triangular_solve(L + eye, eye, left_side=True, lower=True, unit_diagonal=True)
    u = jnp.einsum('bhncs,bhnse->bhnce', Tm, v_beta)
    w = jnp.einsum('bhncs,bhnsd->bhncd', Tm, k_beta * jnp.exp(gc)[..., None])
    qk = jnp.einsum('bhncd,bhnsd->bhncs', q, k) * decay
    q_dec = q * jnp.exp(gc)[..., None]
    k_dec = k * jnp.exp(gc[..., -1:] - gc)[..., None]
    g_tot = jnp.exp(gc[..., -1])
    xs = tuple(jnp.moveaxis(t, 2, 0) for t in (u, w, qk, q_dec, k_dec, g_tot))

    def step(S, xs_n):
        u_n, w_n, qk_n, qd_n, kd_n, gt_n = xs_n
        v_new = u_n - jnp.einsum('bhcd,bhde->bhce', w_n, S)
        o_n = jnp.einsum('bhcd,bhde->bhce', qd_n, S) + jnp.einsum('bhcs,bhse->bhce', qk_n, v_new)
        S = S * gt_n[..., None, None] + jnp.einsum('bhcd,bhce->bhde', kd_n, v_new)
        return S, o_n

    S0 = jnp.zeros((B, H, DK, DV), jnp.float32)
    _, o = lax.scan(step, S0, xs)
    return o.transpose(1, 0, 3, 2, 4).reshape(B, T, H, DV)


def _mixers(n1, w_in, w_conv_qkv, a_log, dt_bias, g_gdn_out, b_glu, w_conv_dw, b_conv_dw,
            ln_conv_g, ln_conv_b):
    B, T, _ = n1.shape
    proj = n1 @ w_in
    o1 = 2 * GDN_QK_W + GDN_V_W
    o2 = o1 + GDN_V_W
    o3 = o2 + GDN_HEADS
    o4 = o3 + GDN_HEADS
    qkv, z, b_raw, a_raw, glu_in = jnp.split(proj, [o1, o2, o3, o4], axis=-1)

    qkv = jax.nn.silu(_causal_depthwise_conv(qkv, w_conv_qkv)).astype(jnp.float32)
    q, k, v = jnp.split(qkv, [GDN_QK_W, 2 * GDN_QK_W], axis=-1)
    q = _l2norm(q.reshape(B, T, GDN_HEADS, GDN_DK)) * (GDN_DK ** -0.5)
    k = _l2norm(k.reshape(B, T, GDN_HEADS, GDN_DK))
    v = v.reshape(B, T, GDN_HEADS, GDN_DV)
    beta = jax.nn.sigmoid(b_raw.astype(jnp.float32))
    g = -jnp.exp(a_log.astype(jnp.float32)) * jax.nn.softplus(
        a_raw.astype(jnp.float32) + dt_bias.astype(jnp.float32))
    o = _gated_delta_rule_chunked(q, k, v, g, beta)
    o = (o * lax.rsqrt(jnp.mean(o * o, axis=-1, keepdims=True) + EPS) * g_gdn_out.astype(jnp.float32)
         * jax.nn.silu(z.astype(jnp.float32).reshape(B, T, GDN_HEADS, GDN_DV)))
    y_gdn = o.reshape(B, T, GDN_V_W).astype(n1.dtype)

    u = glu_in + b_glu
    u = u[..., :CONF_CH] * jax.nn.sigmoid(u[..., CONF_CH:])
    c = _causal_depthwise_conv(u, w_conv_dw) + b_conv_dw
    y_conv = jax.nn.silu(_layernorm(c, ln_conv_g, ln_conv_b))

    return jnp.concatenate([y_gdn, y_conv], axis=-1)


def _hierarchical_moe(xn, w_group_router, w_expert_router, w_up, w_down):
    B, T, D = xn.shape
    N = B * T
    xf = xn.reshape(N, D)
    group_p = jax.nn.softmax((xf @ w_group_router).astype(jnp.float32), axis=-1)
    g_w, g_idx = lax.top_k(group_p, 1)
    e_logits = (xf @ w_expert_router).astype(jnp.float32).reshape(N, N_GROUPS, EXPERTS_PER_GROUP)
    in_group = jnp.take_along_axis(e_logits, g_idx[:, :, None], axis=1)[:, 0]
    e_w, e_idx = lax.top_k(jax.nn.softmax(in_group, axis=-1), TOP_K)
    gate = g_w * (e_w / jnp.sum(e_w, axis=-1, keepdims=True))
    expert = g_idx * EXPERTS_PER_GROUP + e_idx

    NK = N * TOP_K
    flat_e = expert.reshape(-1)
    flat_tok = jnp.repeat(jnp.arange(N, dtype=jnp.int32), TOP_K)
    flat_w = gate.reshape(-1)
    order = jnp.argsort(flat_e)
    se, st, sw = flat_e[order], flat_tok[order], flat_w[order]
    counts = jnp.bincount(flat_e, length=N_EXPERTS)
    starts = jnp.cumsum(counts) - counts
    padded = (counts + MOE_BLOCK - 1) // MOE_BLOCK * MOE_BLOCK
    pend = jnp.cumsum(padded)
    pstart = pend - padded
    pos = pstart[se] + jnp.arange(NK, dtype=jnp.int32) - starts[se]
    nb = -(-NK // MOE_BLOCK) + N_EXPERTS
    P = nb * MOE_BLOCK
    buf_tok = jnp.zeros((P,), jnp.int32).at[pos].set(st)
    buf_w = jnp.zeros((P,), jnp.float32).at[pos].set(sw)
    block_expert = jnp.minimum(
        jnp.searchsorted(pend, jnp.arange(nb, dtype=jnp.int32) * MOE_BLOCK, side='right'), N_EXPERTS - 1)
    xb = xf[buf_tok].reshape(nb, MOE_BLOCK, D)

    def expert_block(args):
        x_blk, e = args
        a, b = jnp.split(x_blk @ w_up[e], 2, axis=-1)
        return (jax.nn.silu(a) * b) @ w_down[e]

    yb = lax.map(expert_block, (xb, block_expert)).reshape(P, D)
    y = jax.ops.segment_sum(yb * buf_w[:, None].astype(yb.dtype), buf_tok, num_segments=N)
    return y.reshape(B, T, D)


def setup_inputs(seed: int = 0) -> dict:
    key = jax.random.key(seed)
    ks = jax.random.split(key, 24)
    f32 = jnp.float32

    def nrm(k, shape, scale):
        return jax.random.normal(k, shape, f32) * scale

    def gain(k, shape):
        return 1.0 + 0.05 * jax.random.normal(k, shape, f32)

    dt = jnp.exp(jax.random.uniform(ks[5], (DEPTH, GDN_HEADS), f32, np.log(1e-3), np.log(1e-1)))
    return {
        'x': nrm(ks[0], (BATCH, SEQ, D_MODEL), 1.0),
        'p': nrm(ks[1], (DEPTH, BATCH, SEQ, PLE_DIM), 1.0),
        'g_mix': gain(ks[2], (DEPTH, D_MODEL)),
        'w_in': nrm(ks[3], (DEPTH, D_MODEL, IN_COLS), D_MODEL ** -0.5),
        'w_conv_qkv': nrm(ks[4], (DEPTH, GDN_CONV, 2 * GDN_QK_W + GDN_V_W), GDN_CONV ** -0.5),
        'a_log': jnp.log(jax.random.uniform(ks[6], (DEPTH, GDN_HEADS), f32, 1.0, 16.0)),
        'dt_bias': dt + jnp.log(-jnp.expm1(-dt)),
        'g_gdn_out': gain(ks[7], (DEPTH, GDN_DV)),
        'b_glu': nrm(ks[8], (DEPTH, 2 * CONF_CH), 0.02),
        'w_conv_dw': nrm(ks[9], (DEPTH, CONF_WIDTH, CONF_CH), CONF_WIDTH ** -0.5),
        'b_conv_dw': nrm(ks[10], (DEPTH, CONF_CH), 0.02),
        'ln_conv_g': gain(ks[11], (DEPTH, CONF_CH)),
        'ln_conv_b': nrm(ks[12], (DEPTH, CONF_CH), 0.02),
        'w_out': nrm(ks[13], (DEPTH, D_MIX, D_MODEL), D_MIX ** -0.5),
        'g_moe': gain(ks[14], (DEPTH, D_MODEL)),
        'w_group_router': nrm(ks[15], (DEPTH, D_MODEL, N_GROUPS), D_MODEL ** -0.5),
        'w_expert_router': nrm(ks[16], (DEPTH, D_MODEL, N_EXPERTS), D_MODEL ** -0.5),
        'w_up': nrm(ks[17], (DEPTH, N_EXPERTS, D_MODEL, 2 * D_EXPERT), D_MODEL ** -0.5),
        'w_down': nrm(ks[18], (DEPTH, N_EXPERTS, D_EXPERT, D_MODEL), D_EXPERT ** -0.5),
        'g_ple': gain(ks[19], (DEPTH, D_MODEL)),
        'w_ple_gate': nrm(ks[20], (DEPTH, D_MODEL, D_MODEL), D_MODEL ** -0.5),
        'w_ple_proj': nrm(ks[21], (DEPTH, PLE_DIM, D_MODEL), PLE_DIM ** -0.5),
        'g_final': gain(ks[22], (D_MODEL,)),
    }


def reference(x, p, g_mix, w_in, w_conv_qkv, a_log, dt_bias, g_gdn_out, b_glu, w_conv_dw,
              b_conv_dw, ln_conv_g, ln_conv_b, w_out, g_moe, w_group_router, w_expert_router,
              w_up, w_down, g_ple, w_ple_gate, w_ple_proj, g_final):
    h = x
    for i in range(DEPTH):
        n1 = _rmsnorm(h, g_mix[i])
        mix = _mixers(n1, w_in[i], w_conv_qkv[i], a_log[i], dt_bias[i], g_gdn_out[i], b_glu[i],
                      w_conv_dw[i], b_conv_dw[i], ln_conv_g[i], ln_conv_b[i])
        h = h + mix @ w_out[i]
        h = h + _hierarchical_moe(_rmsnorm(h, g_moe[i]), w_group_router[i], w_expert_router[i],
                                  w_up[i], w_down[i])
        ple_gate = jax.nn.sigmoid(_rmsnorm(h, g_ple[i]) @ w_ple_gate[i])
        h = h + ple_gate * (p[i] @ w_ple_proj[i])
    return _rmsnorm(h, g_final)
```

```python
import functools

import jax
import jax.numpy as jnp
from jax import lax
from jax.experimental import pallas as pl
from jax.experimental.pallas import tpu as pltpu

F32 = jnp.float32
BF16 = jnp.bfloat16
I32 = jnp.int32
U32 = jnp.uint32

EPS = 1e-6
GDN_HEADS = 4
GDN_DK = 128
GDN_CONV = 4
GDN_CHUNK = 64
CONF_CH = 512
CONF_WIDTH = 31
N_GROUPS = 8
EXPERTS_PER_GROUP = 8
N_EXPERTS = 64
D_EXPERT = 256
LANES = 128
ROUTE_LANE0 = N_GROUPS
EXPERT_BLOCK = 256
QKV_HALO = 8
CONF_HALO = 32
NEG = -1e30
VMEM_LIMIT = 56 * 1024 * 1024


def _dot(a, b):
    return jnp.dot(a, b, preferred_element_type=F32)


def _dot_nt(a, b):
    return lax.dot_general(a, b, (((1,), (1,)), ((), ())), preferred_element_type=F32)


def _dot_tn(a, b):
    return lax.dot_general(a, b, (((0,), (0,)), ((), ())), preferred_element_type=F32)


def _silu(x):
    return x * jax.nn.sigmoid(x)


def _split_bf16(x):
    hi = x.astype(BF16)
    lo = (x - hi.astype(F32)).astype(BF16)
    return hi, lo


def _pack_rows(y):
    c = y.shape[1] // 2
    bits = lax.bitcast_convert_type(y.astype(BF16).astype(F32), U32)
    return (bits[:, :c] & jnp.uint32(0xFFFF0000)) | (bits[:, c:] >> 16)


def _unpack_rows(w):
    hi = lax.bitcast_convert_type(w & jnp.uint32(0xFFFF0000), F32)
    lo = lax.bitcast_convert_type(w << 16, F32)
    return jnp.concatenate([hi, lo], axis=1)


def _inproj_kernel(x_ref, gmix_ref, wqkv_ref, wz_ref, wbah_ref, wbal_ref, wglu_ref, cw_ref, pa_ref,
                   bglu_ref, cdw_ref, cvec_ref,
                   q_ref, k_ref, v_ref, bg_ref, zs_ref, yc_ref,
                   extq, extu, *, tiles_per_seq, tm):
    i = pl.program_id(0)
    first = (i % tiles_per_seq) == 0

    x = x_ref[...]
    n1 = x * lax.rsqrt(jnp.mean(x * x, axis=-1, keepdims=True) + EPS) * gmix_ref[...]
    n1h, n1l = _split_bf16(n1)

    @pl.when(first)
    def _():
        extq[0:QKV_HALO, :] = jnp.zeros((QKV_HALO, extq.shape[1]), F32)
        extu[0:CONF_HALO, :] = jnp.zeros((CONF_HALO, extu.shape[1]), F32)

    @pl.when(jnp.logical_not(first))
    def _():
        extq[0:QKV_HALO, :] = extq[tm:tm + QKV_HALO, :]
        extu[0:CONF_HALO, :] = extu[tm:tm + CONF_HALO, :]

    extq[QKV_HALO:QKV_HALO + tm, :] = _dot(n1h, wqkv_ref[...])
    conv = None
    for kk in range(GDN_CONV):
        off = QKV_HALO - (GDN_CONV - 1) + kk
        term = cw_ref[kk:kk + 1, :] * extq[off:off + tm, :]
        conv = term if conv is None else conv + term
    qkv = _silu(conv)
    hw = GDN_HEADS * GDN_DK
    for h in range(GDN_HEADS):
        cs = slice(h * GDN_DK, (h + 1) * GDN_DK)
        qh = qkv[:, cs]
        q_ref[:, cs] = qh * lax.rsqrt(jnp.sum(qh * qh, axis=-1, keepdims=True) + EPS) * (GDN_DK ** -0.5)
        kh = qkv[:, hw + h * GDN_DK: hw + (h + 1) * GDN_DK]
        k_ref[:, cs] = kh * lax.rsqrt(jnp.sum(kh * kh, axis=-1, keepdims=True) + EPS)
    v_ref[...] = qkv[:, 2 * hw:]

    zs_ref[...] = _silu(_dot(n1h, wz_ref[...])).astype(BF16)

    ba = _dot(n1h, wbah_ref[...]) + _dot(n1l, wbah_ref[...]) + _dot(n1h, wbal_ref[...])
    lane = lax.broadcasted_iota(I32, ba.shape, 1)
    row = lax.broadcasted_iota(I32, ba.shape, 0)
    beta = jax.nn.sigmoid(ba)
    sp_in = ba + pa_ref[1:2, :]
    softplus = jnp.maximum(sp_in, 0.0) + jnp.log1p(jnp.exp(-jnp.abs(sp_in)))
    g = -jnp.exp(pa_ref[0:1, :]) * softplus
    rin = row & (GDN_CHUNK - 1)
    s = 1
    while s < GDN_CHUNK:
        g = g + jnp.where(rin >= s, pltpu.roll(g, s, 0), 0.0)
        s *= 2
    bg_ref[...] = jnp.where(lane < GDN_HEADS, beta, g)

    glu = _dot(n1h, wglu_ref[...]) + bglu_ref[...]
    extu[CONF_HALO:CONF_HALO + tm, :] = glu[:, :CONF_CH] * jax.nn.sigmoid(glu[:, CONF_CH:])
    c = None
    for kk in range(CONF_WIDTH):
        off = CONF_HALO - (CONF_WIDTH - 1) + kk
        term = cdw_ref[kk:kk + 1, :] * extu[off:off + tm, :]
        c = term if c is None else c + term
    c = c + cvec_ref[0:1, :]
    mu = jnp.mean(c, axis=-1, keepdims=True)
    cc = c - mu
    var = jnp.mean(cc * cc, axis=-1, keepdims=True)
    yln = cc * lax.rsqrt(var + EPS) * cvec_ref[1:2, :] + cvec_ref[2:3, :]
    yc_ref[...] = _silu(yln).astype(BF16)


def _gdn_kernel(q_ref, k_ref, v_ref, bg_ref, zs_ref, gout_ref, y_ref, s_ref, *, cb):
    n = pl.program_id(1)

    @pl.when(n == 0)
    def _():
        s_ref[...] = jnp.zeros(s_ref.shape, F32)

    C = GDN_CHUNK
    r = lax.broadcasted_iota(I32, (C, C), 0)
    c = lax.broadcasted_iota(I32, (C, C), 1)
    causal = r >= c
    strict = r > c
    eye = r == c
    eyef = jnp.where(eye, 1.0, 0.0).astype(F32)
    gout = gout_ref[...]

    for ci in range(cb):
        rows = slice(ci * C, (ci + 1) * C)
        bg = bg_ref[rows, :]
        for h in range(GDN_HEADS):
            cols = slice(h * GDN_DK, (h + 1) * GDN_DK)
            q = q_ref[rows, cols]
            k = k_ref[rows, cols]
            v = v_ref[rows, cols]
            beta = bg[:, h:h + 1]
            gc = bg[:, GDN_HEADS + h:GDN_HEADS + h + 1]
            gc_row = jnp.sum(jnp.where(eye, gc, 0.0), axis=0, keepdims=True)
            gc_last = gc[C - 1:C, :]
            decay = jnp.where(causal, jnp.exp(jnp.minimum(gc - gc_row, 0.0)), 0.0)
            eg = jnp.exp(gc)
            kb = k * beta
            kbf = k.astype(BF16)
            kq = _dot_nt(jnp.concatenate([kb, q], axis=0).astype(BF16), kbf)
            a_mat = jnp.where(strict, kq[:C] * decay, 0.0)
            qk = kq[C:] * decay
            xp = -a_mat
            tm_ = eyef + xp
            for _ in range(5):
                xpb = xp.astype(BF16)
                xp = _dot(xpb, xpb)
                tm_ = tm_ + _dot(tm_.astype(BF16), xp.astype(BF16))
            rhs = jnp.concatenate([v * beta, kb * eg], axis=1).astype(BF16)
            uw = _dot(tm_.astype(BF16), rhs)
            u = uw[:, :GDN_DK]
            w = uw[:, GDN_DK:]
            s_old = s_ref[h]
            wq = _dot(jnp.concatenate([w, q * eg], axis=0).astype(BF16), s_old.astype(BF16))
            v_new = u - wq[:C]
            v_newb = v_new.astype(BF16)
            o = wq[C:] + _dot(qk.astype(BF16), v_newb)
            k_dec = k * jnp.exp(gc_last - gc)
            s_ref[h] = s_old * jnp.exp(gc_last) + _dot_tn(k_dec.astype(BF16), v_newb)
            on = o * lax.rsqrt(jnp.mean(o * o, axis=-1, keepdims=True) + EPS) * gout
            y_ref[rows, cols] = (on * zs_ref[rows, cols].astype(F32)).astype(BF16)


def _router_kernel(x_ref, yg_ref, yc_ref, wo1_ref, wo2_ref, gmoe_ref, wrh_ref, wrl_ref, ltri_ref,
                   h1_ref, xnp_ref, ri_ref, rf_ref, cnt_ref, carry):
    i = pl.program_id(0)

    @pl.when(i == 0)
    def _():
        carry[...] = jnp.zeros(carry.shape, F32)

    h1 = x_ref[...] + _dot(yg_ref[...], wo1_ref[...]) + _dot(yc_ref[...], wo2_ref[...])
    h1_ref[...] = h1
    xn = h1 * lax.rsqrt(jnp.mean(h1 * h1, axis=-1, keepdims=True) + EPS) * gmoe_ref[...]
    xnh, xnl = _split_bf16(xn)
    xnp_ref[...] = _pack_rows(xnh.astype(F32))
    lg = _dot(xnh, wrh_ref[...]) + _dot(xnl, wrh_ref[...]) + _dot(xnh, wrl_ref[...])

    lane = lax.broadcasted_iota(I32, lg.shape, 1)
    lanef = lane.astype(F32)
    gl = jnp.where(lane < N_GROUPS, lg, NEG)
    gmax = jnp.max(gl, axis=-1, keepdims=True)
    g_w = 1.0 / jnp.sum(jnp.exp(gl - gmax), axis=-1, keepdims=True)
    gidx = jnp.min(jnp.where(gl == gmax, lanef, float(LANES)), axis=-1, keepdims=True)
    lane_group = ((lane - ROUTE_LANE0) >> 3).astype(F32)
    in_group = (lane >= ROUTE_LANE0) & (lane < ROUTE_LANE0 + N_EXPERTS) & (lane_group == gidx)
    el = jnp.where(in_group, lg, NEG)
    m1 = jnp.max(el, axis=-1, keepdims=True)
    i1 = jnp.min(jnp.where(el == m1, lanef, float(LANES)), axis=-1, keepdims=True)
    el2 = jnp.where(lanef == i1, NEG, el)
    m2 = jnp.max(el2, axis=-1, keepdims=True)
    i2 = jnp.min(jnp.where(el2 == m2, lanef, float(LANES)), axis=-1, keepdims=True)
    ratio = jnp.exp(m2 - m1)
    gate1 = g_w / (1.0 + ratio)
    gate2 = g_w * ratio / (1.0 + ratio)

    oh1 = lanef == i1
    oh2 = lanef == i2
    oh = jnp.where(oh1 | oh2, 1.0, 0.0).astype(F32)
    cum = _dot(ltri_ref[...], oh.astype(BF16)) + carry[...]
    rank1 = jnp.sum(jnp.where(oh1, cum, 0.0), axis=-1, keepdims=True)
    rank2 = jnp.sum(jnp.where(oh2, cum, 0.0), axis=-1, keepdims=True)
    carry[...] = carry[...] + jnp.sum(oh, axis=0, keepdims=True)

    routes = jnp.where(lane == 0, i1 - ROUTE_LANE0,
                       jnp.where(lane == 1, i2 - ROUTE_LANE0,
                                 jnp.where(lane == 2, rank1, jnp.where(lane == 3, rank2, 0.0))))
    ri_ref[...] = routes.astype(I32)
    rf_ref[...] = jnp.where(lane == 0, gate1, jnp.where(lane == 1, gate2, 0.0))
    cnt_ref[...] = jnp.broadcast_to(carry[...], cnt_ref.shape)


def _pos_kernel(ri_ref, cnt_ref, post_ref, blk_ref, *, tm):
    shift = EXPERT_BLOCK.bit_length() - 1
    cnt = cnt_ref[...].astype(I32)
    padded = ((cnt + (EXPERT_BLOCK - 1)) >> shift) << shift
    lane8 = lax.broadcasted_iota(I32, padded.shape, 1)
    pend = padded
    s = 1
    while s < LANES:
        pend = pend + jnp.where(lane8 >= s, pltpu.roll(pend, s, 1), 0)
        s *= 2
    pstart = (pend - padded)[0:1, :].astype(F32)

    ri = ri_ref[...]
    lane = lax.broadcasted_iota(I32, ri.shape, 1)
    e1 = ri[:, 0:1] + ROUTE_LANE0
    e2 = ri[:, 1:2] + ROUTE_LANE0
    ps1 = jnp.sum(jnp.where(lane == e1, pstart, 0.0), axis=-1, keepdims=True)
    ps2 = jnp.sum(jnp.where(lane == e2, pstart, 0.0), axis=-1, keepdims=True)
    pos1 = ps1 + ri[:, 2:3].astype(F32)
    pos2 = ps2 + ri[:, 3:4].astype(F32)
    posmat = jnp.where(lane == 0, pos1, jnp.where(lane == 1, pos2, 0.0))
    for j in range(tm // LANES):
        pt = posmat[j * LANES:(j + 1) * LANES, :].T
        post_ref[:, j * LANES:(j + 1) * LANES] = pt[0:8, :].astype(I32)

    pendf = pend.astype(F32)[0:1, :]
    rr = lax.broadcasted_iota(I32, (LANES, LANES), 0)
    cc = lax.broadcasted_iota(I32, (LANES, LANES), 1)
    pend_col = jnp.sum(jnp.where(rr == cc, pendf, 0.0), axis=1, keepdims=True)
    nbl = blk_ref.shape[1]
    er = lax.broadcasted_iota(I32, (LANES, nbl), 0)
    jb = (lax.broadcasted_iota(I32, (LANES, nbl), 1) * EXPERT_BLOCK).astype(F32)
    valid = (er >= ROUTE_LANE0) & (er < ROUTE_LANE0 + N_EXPERTS)
    be = jnp.sum(jnp.where(valid & (pend_col <= jb), 1.0, 0.0), axis=0, keepdims=True)
    be = jnp.minimum(be, float(N_EXPERTS - 1))
    nused = jnp.max(pendf, axis=-1, keepdims=True) * (1.0 / EXPERT_BLOCK)
    rowi = lax.broadcasted_iota(I32, blk_ref.shape, 0)
    blk_ref[...] = jnp.where(rowi == 0, be, nused).astype(I32)


def _dispatch_kernel(pos1_ref, pos2_ref, xnp_ref, xs_in_ref, xs_ref, sem, *, tm):
    del xs_in_ref

    def row_copy(r, p):
        return pltpu.make_async_copy(xnp_ref.at[pl.ds(r, 1)], xs_ref.at[pl.ds(p, 1)], sem)

    def issue(r, carry):
        row_copy(r, pos1_ref[0, 0, r]).start()
        row_copy(r, pos2_ref[0, 0, r]).start()
        return carry

    lax.fori_loop(0, tm, issue, 0, unroll=8)

    def drain(r, carry):
        row_copy(0, 0).wait()
        row_copy(0, 0).wait()
        return carry

    lax.fori_loop(0, tm, drain, 0, unroll=8)


def _expert_kernel(be_ref, nu_ref, xs_ref, wup_ref, wdn_ref, yb_ref, wupb, wdnb):
    j = pl.program_id(0)
    prev = be_ref[jnp.maximum(j - 1, 0)]
    changed = (j == 0) | (be_ref[j] != prev)

    @pl.when(changed)
    def _():
        wupb[...] = wup_ref[...].astype(BF16)
        wdnb[...] = wdn_ref[...].astype(BF16)

    @pl.when(j < nu_ref[0])
    def _():
        xb = _unpack_rows(xs_ref[...]).astype(BF16)
        ab = _dot(xb, wupb[...])
        hmid = _silu(ab[:, :D_EXPERT]) * ab[:, D_EXPERT:]
        yb_ref[...] = _pack_rows(_dot(hmid.astype(BF16), wdnb[...]))

    @pl.when(j >= nu_ref[0])
    def _():
        yb_ref[...] = jnp.zeros(yb_ref.shape, U32)


def _ple_kernel(pos1_ref, pos2_ref, h1_ref, rf_ref, p_ref, yb_ref, gple_ref, wg_ref, wp_ref, gfin_ref,
                out_ref, ybuf, sem, *, tm):
    def row_copy(k, r, p):
        return pltpu.make_async_copy(yb_ref.at[pl.ds(p, 1)], ybuf.at[k, pl.ds(r, 1)], sem)

    def issue(r, carry):
        row_copy(0, r, pos1_ref[0, 0, r]).start()
        row_copy(1, r, pos2_ref[0, 0, r]).start()
        return carry

    lax.fori_loop(0, tm, issue, 0, unroll=8)

    pp = _dot(p_ref[...].astype(BF16), wp_ref[...])

    def drain(r, carry):
        row_copy(0, 0, 0).wait()
        row_copy(1, 0, 0).wait()
        return carry

    lax.fori_loop(0, tm, drain, 0, unroll=8)

    rf = rf_ref[...]
    h2 = h1_ref[...] + (rf[:, 0:1] * _unpack_rows(ybuf[0]) + rf[:, 1:2] * _unpack_rows(ybuf[1]))
    hn = h2 * lax.rsqrt(jnp.mean(h2 * h2, axis=-1, keepdims=True) + EPS) * gple_ref[...]
    gate = jax.nn.sigmoid(_dot(hn.astype(BF16), wg_ref[...]))
    h3 = h2 + gate * pp
    out_ref[...] = h3 * lax.rsqrt(jnp.mean(h3 * h3, axis=-1, keepdims=True) + EPS) * gfin_ref[...]


def _params(sem):
    return pltpu.CompilerParams(dimension_semantics=sem, vmem_limit_bytes=VMEM_LIMIT)


def _full(shape):
    nd = len(shape)
    return pl.BlockSpec(shape, lambda *_: (0,) * nd)


def _pad_lanes(a, lane0=0):
    out = jnp.zeros((a.shape[0], LANES), a.dtype)
    return out.at[:, lane0:lane0 + a.shape[1]].set(a)


def _layer(x2, p2, B, T, g_mix, w_in, w_conv_qkv, a_log, dt_bias, g_gdn_out, b_glu, w_conv_dw,
           b_conv_dw, ln_conv_g, ln_conv_b, w_out, g_moe, w_group_router, w_expert_router, w_up,
           w_down, g_ple, w_ple_gate, w_ple_proj, g_out):
    N, D = x2.shape
    hw = GDN_HEADS * GDN_DK
    tm = min(512, T)
    nt = N // tm

    o1 = 3 * hw
    o2 = o1 + hw
    o4 = o2 + 2 * GDN_HEADS
    wqkv = w_in[:, :o1].astype(BF16)
    wz = w_in[:, o1:o2].astype(BF16)
    wbah, wbal = _split_bf16(_pad_lanes(w_in[:, o2:o4]))
    wglu = w_in[:, o4:].astype(BF16)
    pa = jnp.concatenate([_pad_lanes(a_log[None, :], GDN_HEADS), _pad_lanes(dt_bias[None, :], GDN_HEADS),
                          jnp.zeros((6, LANES), F32)], axis=0)
    cdw = jnp.concatenate([w_conv_dw, jnp.zeros((1, CONF_CH), F32)], axis=0)
    cvec = jnp.concatenate([b_conv_dw[None], ln_conv_g[None], ln_conv_b[None], jnp.zeros((5, CONF_CH), F32)], 0)

    row_spec = lambda w: pl.BlockSpec((tm, w), lambda i: (i, 0))
    q, k, v, bg, zs, yc = pl.pallas_call(
        functools.partial(_inproj_kernel, tiles_per_seq=T // tm, tm=tm),
        grid=(nt,),
        in_specs=[row_spec(D), _full((1, D)), _full(wqkv.shape), _full(wz.shape), _full(wbah.shape),
                  _full(wbal.shape), _full(wglu.shape), _full(w_conv_qkv.shape), _full(pa.shape),
                  _full((1, 2 * CONF_CH)), _full(cdw.shape), _full(cvec.shape)],
        out_specs=[row_spec(hw), row_spec(hw), row_spec(hw), row_spec(LANES), row_spec(hw), row_spec(CONF_CH)],
        out_shape=[jax.ShapeDtypeStruct((N, hw), F32)] * 3 + [jax.ShapeDtypeStruct((N, LANES), F32),
                   jax.ShapeDtypeStruct((N, hw), BF16), jax.ShapeDtypeStruct((N, CONF_CH), BF16)],
        scratch_shapes=[pltpu.VMEM((tm + QKV_HALO, 3 * hw), F32), pltpu.VMEM((tm + CONF_HALO, CONF_CH), F32)],
        compiler_params=_params(("arbitrary",)),
        name="inproj",
    )(x2, g_mix[None], wqkv, wz, wbah, wbal, wglu, w_conv_qkv, pa, b_glu[None], cdw, cvec)

    cb = 4
    rows = cb * GDN_CHUNK
    seq_spec = lambda w: pl.BlockSpec((rows, w), lambda b, n: (b * (T // rows) + n, 0))
    ygdn = pl.pallas_call(
        functools.partial(_gdn_kernel, cb=cb),
        grid=(B, T // rows),
        in_specs=[seq_spec(hw), seq_spec(hw), seq_spec(hw), seq_spec(LANES), seq_spec(hw),
                  pl.BlockSpec((1, GDN_DK), lambda b, n: (0, 0))],
        out_specs=seq_spec(hw),
        out_shape=jax.ShapeDtypeStruct((N, hw), BF16),
        scratch_shapes=[pltpu.VMEM((GDN_HEADS, GDN_DK, GDN_DK), F32)],
        compiler_params=_params(("arbitrary", "arbitrary")),
        name="gdn",
    )(q, k, v, bg, zs, g_gdn_out[None])

    wo1 = w_out[:hw].astype(BF16)
    wo2 = w_out[hw:].astype(BF16)
    wrh, wrl = _split_bf16(_pad_lanes(jnp.concatenate([w_group_router, w_expert_router], axis=1)))
    ltri = jnp.tril(jnp.ones((tm, tm), F32), -1).astype(BF16)
    h1, xnp, ri, rf, cnt = pl.pallas_call(
        _router_kernel,
        grid=(nt,),
        in_specs=[row_spec(D), row_spec(hw), row_spec(CONF_CH), _full(wo1.shape), _full(wo2.shape),
                  _full((1, D)), _full(wrh.shape), _full(wrl.shape), _full(ltri.shape)],
        out_specs=[row_spec(D), row_spec(D // 2), row_spec(LANES), row_spec(LANES), _full((8, LANES))],
        out_shape=[jax.ShapeDtypeStruct((N, D), F32), jax.ShapeDtypeStruct((N, D // 2), U32),
                   jax.ShapeDtypeStruct((N, LANES), I32), jax.ShapeDtypeStruct((N, LANES), F32),
                   jax.ShapeDtypeStruct((8, LANES), F32)],
        scratch_shapes=[pltpu.VMEM((1, LANES), F32)],
        compiler_params=_params(("arbitrary",)),
        name="router",
    )(x2, ygdn, yc, wo1, wo2, g_moe[None], wrh, wrl, ltri)

    nb = (2 * N) // EXPERT_BLOCK + N_EXPERTS
    nbl = -(-nb // LANES) * LANES
    post, blk = pl.pallas_call(
        functools.partial(_pos_kernel, tm=tm),
        grid=(nt,),
        in_specs=[row_spec(LANES), _full((8, LANES))],
        out_specs=[pl.BlockSpec((8, tm), lambda i: (0, i)), _full((8, nbl))],
        out_shape=[jax.ShapeDtypeStruct((8, N), I32), jax.ShapeDtypeStruct((8, nbl), I32)],
        compiler_params=_params(("arbitrary",)),
        name="positions",
    )(ri, cnt)
    pos1 = post[0].reshape(nt, 1, tm)
    pos2 = post[1].reshape(nt, 1, tm)
    block_expert = blk[0, :nb]
    n_used = blk[1, :1]

    P = nb * EXPERT_BLOCK
    smem_spec = pl.BlockSpec((1, 1, tm), lambda i: (i, 0, 0), memory_space=pltpu.SMEM)
    any_spec = pl.BlockSpec(memory_space=pl.ANY)
    xs = pl.pallas_call(
        functools.partial(_dispatch_kernel, tm=tm),
        grid=(nt,),
        in_specs=[smem_spec, smem_spec, row_spec(D // 2), any_spec],
        out_specs=any_spec,
        out_shape=jax.ShapeDtypeStruct((P, D // 2), U32),
        scratch_shapes=[pltpu.SemaphoreType.DMA],
        input_output_aliases={3: 0},
        compiler_params=_params(("arbitrary",)),
        name="dispatch",
    )(pos1, pos2, xnp, jnp.zeros((P, D // 2), U32))

    yb = pl.pallas_call(
        _expert_kernel,
        grid_spec=pltpu.PrefetchScalarGridSpec(
            num_scalar_prefetch=2,
            grid=(nb,),
            in_specs=[pl.BlockSpec((EXPERT_BLOCK, D // 2), lambda j, be, nu: (j, 0)),
                      pl.BlockSpec((None, D, 2 * D_EXPERT), lambda j, be, nu: (be[j], 0, 0)),
                      pl.BlockSpec((None, D_EXPERT, D), lambda j, be, nu: (be[j], 0, 0))],
            out_specs=pl.BlockSpec((EXPERT_BLOCK, D // 2), lambda j, be, nu: (j, 0)),
            scratch_shapes=[pltpu.VMEM((D, 2 * D_EXPERT), BF16), pltpu.VMEM((D_EXPERT, D), BF16)]),
        out_shape=jax.ShapeDtypeStruct((P, D // 2), U32),
        compiler_params=_params(("arbitrary",)),
        name="experts",
    )(block_expert, n_used, xs, w_up, w_down)

    wg = w_ple_gate.astype(BF16)
    wp = w_ple_proj.astype(BF16)
    out = pl.pallas_call(
        functools.partial(_ple_kernel, tm=tm),
        grid=(nt,),
        in_specs=[smem_spec, smem_spec, row_spec(D), row_spec(LANES), row_spec(p2.shape[1]), any_spec,
                  _full((1, D)), _full(wg.shape), _full(wp.shape), _full((1, D))],
        out_specs=row_spec(D),
        out_shape=jax.ShapeDtypeStruct((N, D), F32),
        scratch_shapes=[pltpu.VMEM((2, tm, D // 2), U32), pltpu.SemaphoreType.DMA],
        compiler_params=_params(("arbitrary",)),
        name="ple",
    )(pos1, pos2, h1, rf, p2, yb, g_ple[None], wg, wp, g_out[None])
    return out


def kernel(x, p, g_mix, w_in, w_conv_qkv, a_log, dt_bias, g_gdn_out, b_glu, w_conv_dw, b_conv_dw, ln_conv_g, ln_conv_b, w_out, g_moe, w_group_router, w_expert_router, w_up, w_down, g_ple, w_ple_gate, w_ple_proj, g_final):
    B, T, D = x.shape
    depth = p.shape[0]
    assert depth == 1, "the final RMSNorm is fused into the (single) layer's last kernel"
    h = _layer(x.reshape(B * T, D), p[0].reshape(B * T, -1), B, T, g_mix[0], w_in[0], w_conv_qkv[0],
               a_log[0], dt_bias[0], g_gdn_out[0], b_glu[0], w_conv_dw[0], b_conv_dw[0], ln_conv_g[0],
               ln_conv_b[0], w_out[0], g_moe[0], w_group_router[0], w_expert_router[0], w_up[0],
               w_down[0], g_ple[0], w_ple_gate[0], w_ple_proj[0], g_final)
    return h.reshape(B, T, D)
```

```python
import functools

import jax
import jax.numpy as jnp
from jax import lax
from jax.experimental import pallas as pl
from jax.experimental.pallas import tpu as pltpu

F32 = jnp.float32
BF16 = jnp.bfloat16
I32 = jnp.int32
U32 = jnp.uint32

EPS = 1e-6
GDN_HEADS = 4
GDN_DK = 128
GDN_CONV = 4
GDN_CHUNK = 64
CONF_CH = 512
CONF_WIDTH = 31
N_GROUPS = 8
EXPERTS_PER_GROUP = 8
N_EXPERTS = 64
D_EXPERT = 256
LANES = 128
ROUTE_LANE0 = N_GROUPS
EXPERT_BLOCK = 256
QKV_HALO = 8
CONF_HALO = 32
NEG = -1e30
VMEM_LIMIT = 56 * 1024 * 1024


def _dot(a, b):
    return jnp.dot(a, b, preferred_element_type=F32)


def _dot_nt(a, b):
    return lax.dot_general(a, b, (((1,), (1,)), ((), ())), preferred_element_type=F32)


def _dot_tn(a, b):
    return lax.dot_general(a, b, (((0,), (0,)), ((), ())), preferred_element_type=F32)


def _silu(x):
    return x * jax.nn.sigmoid(x)


def _split_bf16(x):
    hi = x.astype(BF16)
    lo = (x - hi.astype(F32)).astype(BF16)
    return hi, lo


def _inproj_kernel(x_ref, gmix_ref, wqkv_ref, wz_ref, wbah_ref, wbal_ref, wglu_ref, cw_ref, pa_ref,
                   bglu_ref, cdw_ref, cvec_ref,
                   q_ref, k_ref, v_ref, bg_ref, zs_ref, yc_ref,
                   extq, extu, *, tiles_per_seq, tm):
    i = pl.program_id(0)
    first = (i % tiles_per_seq) == 0

    x = x_ref[...]
    n1 = x * lax.rsqrt(jnp.mean(x * x, axis=-1, keepdims=True) + EPS) * gmix_ref[...]
    n1h, n1l = _split_bf16(n1)

    @pl.when(first)
    def _():
        extq[0:QKV_HALO, :] = jnp.zeros((QKV_HALO, extq.shape[1]), F32)
        extu[0:CONF_HALO, :] = jnp.zeros((CONF_HALO, extu.shape[1]), F32)

    @pl.when(jnp.logical_not(first))
    def _():
        extq[0:QKV_HALO, :] = extq[tm:tm + QKV_HALO, :]
        extu[0:CONF_HALO, :] = extu[tm:tm + CONF_HALO, :]

    extq[QKV_HALO:QKV_HALO + tm, :] = _dot(n1h, wqkv_ref[...])
    conv = None
    for kk in range(GDN_CONV):
        off = QKV_HALO - (GDN_CONV - 1) + kk
        term = cw_ref[kk:kk + 1, :] * extq[off:off + tm, :]
        conv = term if conv is None else conv + term
    qkv = _silu(conv)
    hw = GDN_HEADS * GDN_DK
    for h in range(GDN_HEADS):
        cs = slice(h * GDN_DK, (h + 1) * GDN_DK)
        qh = qkv[:, cs]
        q_ref[:, cs] = qh * lax.rsqrt(jnp.sum(qh * qh, axis=-1, keepdims=True) + EPS) * (GDN_DK ** -0.5)
        kh = qkv[:, hw + h * GDN_DK: hw + (h + 1) * GDN_DK]
        k_ref[:, cs] = kh * lax.rsqrt(jnp.sum(kh * kh, axis=-1, keepdims=True) + EPS)
    v_ref[...] = qkv[:, 2 * hw:]

    zs_ref[...] = _silu(_dot(n1h, wz_ref[...])).astype(BF16)

    ba = _dot(n1h, wbah_ref[...]) + _dot(n1l, wbah_ref[...]) + _dot(n1h, wbal_ref[...])
    lane = lax.broadcasted_iota(I32, ba.shape, 1)
    row = lax.broadcasted_iota(I32, ba.shape, 0)
    beta = jax.nn.sigmoid(ba)
    sp_in = ba + pa_ref[1:2, :]
    softplus = jnp.maximum(sp_in, 0.0) + jnp.log1p(jnp.exp(-jnp.abs(sp_in)))
    g = -jnp.exp(pa_ref[0:1, :]) * softplus
    rin = row & (GDN_CHUNK - 1)
    s = 1
    while s < GDN_CHUNK:
        g = g + jnp.where(rin >= s, pltpu.roll(g, s, 0), 0.0)
        s *= 2
    bg_ref[...] = jnp.where(lane < GDN_HEADS, beta, g)

    glu = _dot(n1h, wglu_ref[...]) + bglu_ref[...]
    extu[CONF_HALO:CONF_HALO + tm, :] = glu[:, :CONF_CH] * jax.nn.sigmoid(glu[:, CONF_CH:])
    c = None
    for kk in range(CONF_WIDTH):
        off = CONF_HALO - (CONF_WIDTH - 1) + kk
        term = cdw_ref[kk:kk + 1, :] * extu[off:off + tm, :]
        c = term if c is None else c + term
    c = c + cvec_ref[0:1, :]
    mu = jnp.mean(c, axis=-1, keepdims=True)
    cc = c - mu
    var = jnp.mean(cc * cc, axis=-1, keepdims=True)
    yln = cc * lax.rsqrt(var + EPS) * cvec_ref[1:2, :] + cvec_ref[2:3, :]
    yc_ref[...] = _silu(yln).astype(BF16)


def _gdn_kernel(q_ref, k_ref, v_ref, bg_ref, zs_ref, gout_ref, y_ref, s_ref, *, cb, nbatch, group):
    n = pl.program_id(0)

    @pl.when(n == 0)
    def _():
        s_ref[...] = jnp.zeros(s_ref.shape, F32)

    C = GDN_CHUNK
    r = lax.broadcasted_iota(I32, (C, C), 0)
    c = lax.broadcasted_iota(I32, (C, C), 1)
    causal = r >= c
    strict = r > c
    eye = r == c
    eyef = jnp.where(eye, 1.0, 0.0).astype(F32)
    gout = gout_ref[...]

    def front(b, ci, h):
        rows = slice(ci * C, (ci + 1) * C)
        cols = slice(h * GDN_DK, (h + 1) * GDN_DK)
        q = q_ref[b, rows, cols]
        k = k_ref[b, rows, cols]
        v = v_ref[b, rows, cols]
        beta = bg_ref[b, rows, h:h + 1]
        gc = bg_ref[b, rows, GDN_HEADS + h:GDN_HEADS + h + 1]
        gc_row = jnp.sum(jnp.where(eye, gc, 0.0), axis=0, keepdims=True)
        gc_last = gc[C - 1:C, :]
        eg = jnp.exp(gc)
        kb = k * beta
        return dict(
            decay=jnp.where(causal, jnp.exp(jnp.minimum(gc - gc_row, 0.0)), 0.0),
            lhs=jnp.concatenate([kb, q], axis=0).astype(BF16),
            kbf=k.astype(BF16),
            rhs=jnp.concatenate([v * beta, kb * eg], axis=1).astype(BF16),
            qd=q * eg,
            kd=(k * jnp.exp(gc_last - gc)).astype(BF16),
            gl=jnp.exp(gc_last))

    insts = [(b, ci, h) for ci in range(cb) for b in range(nbatch) for h in range(GDN_HEADS)]
    done = {}
    for g0 in range(0, len(insts), group):
        grp = insts[g0:g0 + group]
        st = [front(*key) for key in grp]
        kq = [_dot_nt(s["lhs"], s["kbf"]) for s in st]
        xp = [-jnp.where(strict, m[:C] * s["decay"], 0.0) for m, s in zip(kq, st)]
        qk = [(m[C:] * s["decay"]).astype(BF16) for m, s in zip(kq, st)]
        tinv = [eyef + x for x in xp]
        for _ in range(5):
            xpb = [x.astype(BF16) for x in xp]
            xp = [_dot(x, x) for x in xpb]
            tinv = [t + _dot(t.astype(BF16), x.astype(BF16)) for t, x in zip(tinv, xp)]
        uw = [_dot(t.astype(BF16), s["rhs"]) for t, s in zip(tinv, st)]
        for key, s, m, qkm in zip(grp, st, uw, qk):
            done[key] = dict(u=m[:, :GDN_DK],
                             wq=jnp.concatenate([m[:, GDN_DK:], s["qd"]], axis=0).astype(BF16),
                             qk=qkm, kd=s["kd"], gl=s["gl"])

    chains = [(b, h) for b in range(nbatch) for h in range(GDN_HEADS)]
    for ci in range(cb):
        rows = slice(ci * C, (ci + 1) * C)
        dd = [done[(b, ci, h)] for b, h in chains]
        s_old = [s_ref[b * GDN_HEADS + h] for b, h in chains]
        wq = [_dot(d["wq"], s.astype(BF16)) for d, s in zip(dd, s_old)]
        v_new = [(d["u"] - m[:C]).astype(BF16) for d, m in zip(dd, wq)]
        o = [m[C:] + _dot(d["qk"], vn) for d, m, vn in zip(dd, wq, v_new)]
        s_new = [s * d["gl"] + _dot_tn(d["kd"], vn) for d, s, vn in zip(dd, s_old, v_new)]
        for (b, h), sn, oo in zip(chains, s_new, o):
            cols = slice(h * GDN_DK, (h + 1) * GDN_DK)
            s_ref[b * GDN_HEADS + h] = sn
            on = oo * lax.rsqrt(jnp.mean(oo * oo, axis=-1, keepdims=True) + EPS) * gout
            y_ref[b, rows, cols] = (on * zs_ref[b, rows, cols].astype(F32)).astype(BF16)


def _router_kernel(x_ref, yg_ref, yc_ref, wo1_ref, wo2_ref, gmoe_ref, wrh_ref, wrl_ref, ltri_ref,
                   h1_ref, xn_ref, ri_ref, rf_ref, cnt_ref, carry):
    i = pl.program_id(0)

    @pl.when(i == 0)
    def _():
        carry[...] = jnp.zeros(carry.shape, F32)

    h1 = x_ref[...] + _dot(yg_ref[...], wo1_ref[...]) + _dot(yc_ref[...], wo2_ref[...])
    h1_ref[...] = h1
    xn = h1 * lax.rsqrt(jnp.mean(h1 * h1, axis=-1, keepdims=True) + EPS) * gmoe_ref[...]
    xnh, xnl = _split_bf16(xn)
    xn_ref[...] = xn
    lg = _dot(xnh, wrh_ref[...]) + _dot(xnl, wrh_ref[...]) + _dot(xnh, wrl_ref[...])

    lane = lax.broadcasted_iota(I32, lg.shape, 1)
    lanef = lane.astype(F32)
    gl = jnp.where(lane < N_GROUPS, lg, NEG)
    gmax = jnp.max(gl, axis=-1, keepdims=True)
    g_w = 1.0 / jnp.sum(jnp.exp(gl - gmax), axis=-1, keepdims=True)
    gidx = jnp.min(jnp.where(gl == gmax, lanef, float(LANES)), axis=-1, keepdims=True)
    lane_group = ((lane - ROUTE_LANE0) >> 3).astype(F32)
    in_group = (lane >= ROUTE_LANE0) & (lane < ROUTE_LANE0 + N_EXPERTS) & (lane_group == gidx)
    el = jnp.where(in_group, lg, NEG)
    m1 = jnp.max(el, axis=-1, keepdims=True)
    i1 = jnp.min(jnp.where(el == m1, lanef, float(LANES)), axis=-1, keepdims=True)
    el2 = jnp.where(lanef == i1, NEG, el)
    m2 = jnp.max(el2, axis=-1, keepdims=True)
    i2 = jnp.min(jnp.where(el2 == m2, lanef, float(LANES)), axis=-1, keepdims=True)
    ratio = jnp.exp(m2 - m1)
    gate1 = g_w / (1.0 + ratio)
    gate2 = g_w * ratio / (1.0 + ratio)

    oh1 = lanef == i1
    oh2 = lanef == i2
    oh = jnp.where(oh1 | oh2, 1.0, 0.0).astype(F32)
    cum = _dot(ltri_ref[...], oh.astype(BF16)) + carry[...]
    rank1 = jnp.sum(jnp.where(oh1, cum, 0.0), axis=-1, keepdims=True)
    rank2 = jnp.sum(jnp.where(oh2, cum, 0.0), axis=-1, keepdims=True)
    carry[...] = carry[...] + jnp.sum(oh, axis=0, keepdims=True)

    routes = jnp.where(lane == 0, i1 - ROUTE_LANE0,
                       jnp.where(lane == 1, i2 - ROUTE_LANE0,
                                 jnp.where(lane == 2, rank1, jnp.where(lane == 3, rank2, 0.0))))
    ri_ref[...] = routes.astype(I32)
    rf_ref[...] = jnp.where(lane == 0, gate1, jnp.where(lane == 1, gate2, 0.0))
    cnt_ref[...] = jnp.broadcast_to(carry[...], cnt_ref.shape)


def _pos_kernel(ri_ref, cnt_ref, post_ref, blk_ref, *, tm):
    shift = EXPERT_BLOCK.bit_length() - 1
    cnt = cnt_ref[...].astype(I32)
    padded = ((cnt + (EXPERT_BLOCK - 1)) >> shift) << shift
    lane8 = lax.broadcasted_iota(I32, padded.shape, 1)
    pend = padded
    s = 1
    while s < LANES:
        pend = pend + jnp.where(lane8 >= s, pltpu.roll(pend, s, 1), 0)
        s *= 2
    pstart = (pend - padded)[0:1, :].astype(F32)

    ri = ri_ref[...]
    lane = lax.broadcasted_iota(I32, ri.shape, 1)
    e1 = ri[:, 0:1] + ROUTE_LANE0
    e2 = ri[:, 1:2] + ROUTE_LANE0
    ps1 = jnp.sum(jnp.where(lane == e1, pstart, 0.0), axis=-1, keepdims=True)
    ps2 = jnp.sum(jnp.where(lane == e2, pstart, 0.0), axis=-1, keepdims=True)
    pos1 = ps1 + ri[:, 2:3].astype(F32)
    pos2 = ps2 + ri[:, 3:4].astype(F32)
    posmat = jnp.where(lane == 0, pos1, jnp.where(lane == 1, pos2, 0.0))
    for j in range(tm // LANES):
        pt = posmat[j * LANES:(j + 1) * LANES, :].T
        post_ref[:, j * LANES:(j + 1) * LANES] = pt[0:8, :].astype(I32)

    pendf = pend.astype(F32)[0:1, :]
    rr = lax.broadcasted_iota(I32, (LANES, LANES), 0)
    cc = lax.broadcasted_iota(I32, (LANES, LANES), 1)
    to_col = lambda row: jnp.sum(jnp.where(rr == cc, row, 0.0), axis=1, keepdims=True)
    pend_col = to_col(pendf)
    pstart_col = to_col(pstart)
    cend_col = pstart_col + to_col(cnt_ref[0:1, :])
    nbl = blk_ref.shape[1]
    er = lax.broadcasted_iota(I32, (LANES, nbl), 0)
    jb = (lax.broadcasted_iota(I32, (LANES, nbl), 1) * EXPERT_BLOCK).astype(F32)
    valid = (er >= ROUTE_LANE0) & (er < ROUTE_LANE0 + N_EXPERTS)
    be = jnp.sum(jnp.where(valid & (pend_col <= jb), 1.0, 0.0), axis=0, keepdims=True)
    be = jnp.minimum(be, float(N_EXPERTS - 1))
    nused = jnp.max(pendf, axis=-1, keepdims=True) * (1.0 / EXPERT_BLOCK)
    inblk = valid & (pstart_col <= jb) & (jb < pend_col)
    nvalid = jnp.sum(jnp.where(inblk, jnp.clip(cend_col - jb, 0.0, float(EXPERT_BLOCK)), 0.0),
                     axis=0, keepdims=True)
    rowi = lax.broadcasted_iota(I32, blk_ref.shape, 0)
    blk_ref[...] = jnp.where(rowi == 0, be, jnp.where(rowi == 1, nused, nvalid)).astype(I32)


def _slots_kernel(pos1_ref, pos2_ref, slot_ref, *, tm):
    i = pl.program_id(0)

    @pl.when(i == 0)
    def _():
        def fill(p, carry):
            slot_ref[p] = 0
            return carry

        lax.fori_loop(0, slot_ref.shape[0], fill, 0, unroll=8)

    def body(r, carry):
        slot = 2 * (i * tm + r)
        slot_ref[pos1_ref[0, 0, r]] = slot
        slot_ref[pos2_ref[0, 0, r]] = slot + 1
        return carry

    lax.fori_loop(0, tm, body, 0, unroll=8)


def _expert_kernel(be_ref, nu_ref, nv_ref, slc_ref, sln_ref, xn_ref, wup_ref, wdn_ref, y2_ref,
                   wupb, wdnb, xbuf, ybuf, gsem, ssem):
    j = pl.program_id(0)
    last = pl.num_programs(0) - 1
    cur = j % 2
    nxt = 1 - cur
    n_used = nu_ref[0]

    def gather_copy(tok, b, r):
        return pltpu.make_async_copy(xn_ref.at[pl.ds(tok, 1)], xbuf.at[b, pl.ds(r, 1)], gsem.at[b])

    def scatter_copy(slot, b, r):
        return pltpu.make_async_copy(ybuf.at[b, pl.ds(r, 1)], y2_ref.at[pl.ds(slot, 1)], ssem.at[b])

    def for_rows(nrows, fn):
        def body(r, carry):
            fn(r)
            return carry

        lax.fori_loop(0, nrows, body, 0)

    @pl.when(j == 0)
    def _():
        xbuf[...] = jnp.zeros(xbuf.shape, F32)
        for_rows(nv_ref[0], lambda r: gather_copy(slc_ref[0, 0, r] >> 1, 0, r).start())

    @pl.when(j + 1 < n_used)
    def _():
        for_rows(nv_ref[jnp.minimum(j + 1, last)],
                 lambda r: gather_copy(sln_ref[0, 0, r] >> 1, nxt, r).start())

    prev = be_ref[jnp.maximum(j - 1, 0)]

    @pl.when((j == 0) | (be_ref[j] != prev))
    def _():
        wupb[...] = wup_ref[...].astype(BF16)
        wdnb[...] = wdn_ref[...].astype(BF16)

    @pl.when(j < n_used)
    def _():
        for_rows(nv_ref[j], lambda r: gather_copy(0, cur, 0).wait())
        ab = _dot(xbuf[cur].astype(BF16), wupb[...])
        hmid = _silu(ab[:, :D_EXPERT]) * ab[:, D_EXPERT:]
        ybuf[cur] = _dot(hmid.astype(BF16), wdnb[...])

    @pl.when((j >= 1) & (j <= n_used))
    def _():
        for_rows(nv_ref[jnp.maximum(j - 1, 0)], lambda r: scatter_copy(0, nxt, 0).wait())

    @pl.when(j < n_used)
    def _():
        for_rows(nv_ref[j], lambda r: scatter_copy(slc_ref[0, 0, r], cur, r).start())

    @pl.when((j == last) & (j < n_used))
    def _():
        for_rows(nv_ref[j], lambda r: scatter_copy(0, cur, 0).wait())


def _ple_kernel(h1_ref, rf_ref, p_ref, y2_ref, gple_ref, wg_ref, wp_ref, gfin_ref, out_ref):
    d = h1_ref.shape[1]
    rf = rf_ref[...]
    h2 = h1_ref[...] + (rf[:, 0:1] * y2_ref[:, :d] + rf[:, 1:2] * y2_ref[:, d:])
    hn = h2 * lax.rsqrt(jnp.mean(h2 * h2, axis=-1, keepdims=True) + EPS) * gple_ref[...]
    gate = jax.nn.sigmoid(_dot(hn.astype(BF16), wg_ref[...]))
    h3 = h2 + gate * _dot(p_ref[...].astype(BF16), wp_ref[...])
    out_ref[...] = h3 * lax.rsqrt(jnp.mean(h3 * h3, axis=-1, keepdims=True) + EPS) * gfin_ref[...]


def _params(sem):
    return pltpu.CompilerParams(dimension_semantics=sem, vmem_limit_bytes=VMEM_LIMIT)


def _full(shape):
    nd = len(shape)
    return pl.BlockSpec(shape, lambda *_: (0,) * nd)


def _pad_lanes(a, lane0=0):
    out = jnp.zeros((a.shape[0], LANES), a.dtype)
    return out.at[:, lane0:lane0 + a.shape[1]].set(a)


def _layer(x2, p2, B, T, g_mix, w_in, w_conv_qkv, a_log, dt_bias, g_gdn_out, b_glu, w_conv_dw,
           b_conv_dw, ln_conv_g, ln_conv_b, w_out, g_moe, w_group_router, w_expert_router, w_up,
           w_down, g_ple, w_ple_gate, w_ple_proj, g_out):
    N, D = x2.shape
    hw = GDN_HEADS * GDN_DK
    tm = min(512, T)
    nt = N // tm

    o1 = 3 * hw
    o2 = o1 + hw
    o4 = o2 + 2 * GDN_HEADS
    wqkv = w_in[:, :o1].astype(BF16)
    wz = w_in[:, o1:o2].astype(BF16)
    wbah, wbal = _split_bf16(_pad_lanes(w_in[:, o2:o4]))
    wglu = w_in[:, o4:].astype(BF16)
    pa = jnp.concatenate([_pad_lanes(a_log[None, :], GDN_HEADS), _pad_lanes(dt_bias[None, :], GDN_HEADS),
                          jnp.zeros((6, LANES), F32)], axis=0)
    cdw = jnp.concatenate([w_conv_dw, jnp.zeros((1, CONF_CH), F32)], axis=0)
    cvec = jnp.concatenate([b_conv_dw[None], ln_conv_g[None], ln_conv_b[None], jnp.zeros((5, CONF_CH), F32)], 0)

    row_spec = lambda w: pl.BlockSpec((tm, w), lambda i: (i, 0))
    q, k, v, bg, zs, yc = pl.pallas_call(
        functools.partial(_inproj_kernel, tiles_per_seq=T // tm, tm=tm),
        grid=(nt,),
        in_specs=[row_spec(D), _full((1, D)), _full(wqkv.shape), _full(wz.shape), _full(wbah.shape),
                  _full(wbal.shape), _full(wglu.shape), _full(w_conv_qkv.shape), _full(pa.shape),
                  _full((1, 2 * CONF_CH)), _full(cdw.shape), _full(cvec.shape)],
        out_specs=[row_spec(hw), row_spec(hw), row_spec(hw), row_spec(LANES), row_spec(hw), row_spec(CONF_CH)],
        out_shape=[jax.ShapeDtypeStruct((N, hw), F32)] * 3 + [jax.ShapeDtypeStruct((N, LANES), F32),
                   jax.ShapeDtypeStruct((N, hw), BF16), jax.ShapeDtypeStruct((N, CONF_CH), BF16)],
        scratch_shapes=[pltpu.VMEM((tm + QKV_HALO, 3 * hw), F32), pltpu.VMEM((tm + CONF_HALO, CONF_CH), F32)],
        compiler_params=_params(("arbitrary",)),
        name="inproj",
    )(x2, g_mix[None], wqkv, wz, wbah, wbal, wglu, w_conv_qkv, pa, b_glu[None], cdw, cvec)

    cb = 4
    rows = cb * GDN_CHUNK
    seq_spec = lambda w: pl.BlockSpec((B, rows, w), lambda n: (0, n, 0))
    r3 = lambda a: a.reshape(B, T, a.shape[-1])
    ygdn = pl.pallas_call(
        functools.partial(_gdn_kernel, cb=cb, nbatch=B, group=2 * GDN_HEADS),
        grid=(T // rows,),
        in_specs=[seq_spec(hw), seq_spec(hw), seq_spec(hw), seq_spec(LANES), seq_spec(hw),
                  pl.BlockSpec((1, GDN_DK), lambda n: (0, 0))],
        out_specs=seq_spec(hw),
        out_shape=jax.ShapeDtypeStruct((B, T, hw), BF16),
        scratch_shapes=[pltpu.VMEM((B * GDN_HEADS, GDN_DK, GDN_DK), F32)],
        compiler_params=_params(("arbitrary",)),
        name="gdn",
    )(r3(q), r3(k), r3(v), r3(bg), r3(zs), g_gdn_out[None]).reshape(N, hw)

    wo1 = w_out[:hw].astype(BF16)
    wo2 = w_out[hw:].astype(BF16)
    wrh, wrl = _split_bf16(_pad_lanes(jnp.concatenate([w_group_router, w_expert_router], axis=1)))
    ltri = jnp.tril(jnp.ones((tm, tm), F32), -1).astype(BF16)
    h1, xn, ri, rf, cnt = pl.pallas_call(
        _router_kernel,
        grid=(nt,),
        in_specs=[row_spec(D), row_spec(hw), row_spec(CONF_CH), _full(wo1.shape), _full(wo2.shape),
                  _full((1, D)), _full(wrh.shape), _full(wrl.shape), _full(ltri.shape)],
        out_specs=[row_spec(D), row_spec(D), row_spec(LANES), row_spec(LANES), _full((8, LANES))],
        out_shape=[jax.ShapeDtypeStruct((N, D), F32), jax.ShapeDtypeStruct((N, D), F32),
                   jax.ShapeDtypeStruct((N, LANES), I32), jax.ShapeDtypeStruct((N, LANES), F32),
                   jax.ShapeDtypeStruct((8, LANES), F32)],
        scratch_shapes=[pltpu.VMEM((1, LANES), F32)],
        compiler_params=_params(("arbitrary",)),
        name="router",
    )(x2, ygdn, yc, wo1, wo2, g_moe[None], wrh, wrl, ltri)

    nb = (2 * N) // EXPERT_BLOCK + N_EXPERTS
    nbl = -(-nb // LANES) * LANES
    post, blk = pl.pallas_call(
        functools.partial(_pos_kernel, tm=tm),
        grid=(nt,),
        in_specs=[row_spec(LANES), _full((8, LANES))],
        out_specs=[pl.BlockSpec((8, tm), lambda i: (0, i)), _full((8, nbl))],
        out_shape=[jax.ShapeDtypeStruct((8, N), I32), jax.ShapeDtypeStruct((8, nbl), I32)],
        compiler_params=_params(("arbitrary",)),
        name="positions",
    )(ri, cnt)
    pos1 = post[0].reshape(nt, 1, tm)
    pos2 = post[1].reshape(nt, 1, tm)
    block_expert = blk[0, :nb]
    n_used = blk[1, :1]
    n_valid = blk[2, :nb]

    P = nb * EXPERT_BLOCK
    smem_spec = pl.BlockSpec((1, 1, tm), lambda i: (i, 0, 0), memory_space=pltpu.SMEM)
    slots = pl.pallas_call(
        functools.partial(_slots_kernel, tm=tm),
        grid=(nt,),
        in_specs=[smem_spec, smem_spec],
        out_specs=pl.BlockSpec((P,), lambda i: (0,), memory_space=pltpu.SMEM),
        out_shape=jax.ShapeDtypeStruct((P,), I32),
        compiler_params=_params(("arbitrary",)),
        name="slots",
    )(pos1, pos2).reshape(nb, 1, EXPERT_BLOCK)

    any_spec = pl.BlockSpec(memory_space=pl.ANY)
    y2 = pl.pallas_call(
        _expert_kernel,
        grid_spec=pltpu.PrefetchScalarGridSpec(
            num_scalar_prefetch=3,
            grid=(nb,),
            in_specs=[pl.BlockSpec((1, 1, EXPERT_BLOCK), lambda j, be, nu, nv: (j, 0, 0),
                                   memory_space=pltpu.SMEM),
                      pl.BlockSpec((1, 1, EXPERT_BLOCK), lambda j, be, nu, nv: (jnp.minimum(j + 1, nb - 1), 0, 0),
                                   memory_space=pltpu.SMEM),
                      any_spec,
                      pl.BlockSpec((None, D, 2 * D_EXPERT), lambda j, be, nu, nv: (be[j], 0, 0)),
                      pl.BlockSpec((None, D_EXPERT, D), lambda j, be, nu, nv: (be[j], 0, 0))],
            out_specs=any_spec,
            scratch_shapes=[pltpu.VMEM((D, 2 * D_EXPERT), BF16), pltpu.VMEM((D_EXPERT, D), BF16),
                            pltpu.VMEM((2, EXPERT_BLOCK, D), F32), pltpu.VMEM((2, EXPERT_BLOCK, D), F32),
                            pltpu.SemaphoreType.DMA((2,)), pltpu.SemaphoreType.DMA((2,))]),
        out_shape=jax.ShapeDtypeStruct((2 * N, D), F32),
        compiler_params=_params(("arbitrary",)),
        name="experts",
    )(block_expert, n_used, n_valid, slots, slots, xn, w_up, w_down)

    wg = w_ple_gate.astype(BF16)
    wp = w_ple_proj.astype(BF16)
    out = pl.pallas_call(
        _ple_kernel,
        grid=(nt,),
        in_specs=[row_spec(D), row_spec(LANES), row_spec(p2.shape[1]), row_spec(2 * D),
                  _full((1, D)), _full(wg.shape), _full(wp.shape), _full((1, D))],
        out_specs=row_spec(D),
        out_shape=jax.ShapeDtypeStruct((N, D), F32),
        compiler_params=_params(("arbitrary",)),
        name="ple",
    )(h1, rf, p2, y2.reshape(N, 2 * D), g_ple[None], wg, wp, g_out[None])
    return out


def kernel(x, p, g_mix, w_in, w_conv_qkv, a_log, dt_bias, g_gdn_out, b_glu, w_conv_dw, b_conv_dw, ln_conv_g, ln_conv_b, w_out, g_moe, w_group_router, w_expert_router, w_up, w_down, g_ple, w_ple_gate, w_ple_proj, g_final):
    B, T, D = x.shape
    depth = p.shape[0]
    assert depth == 1, "the final RMSNorm is fused into the (single) layer's last kernel"
    h = _layer(x.reshape(B * T, D), p[0].reshape(B * T, -1), B, T, g_mix[0], w_in[0], w_conv_qkv[0],
               a_log[0], dt_bias[0], g_gdn_out[0], b_glu[0], w_conv_dw[0], b_conv_dw[0], ln_conv_g[0],
               ln_conv_b[0], w_out[0], g_moe[0], w_group_router[0], w_expert_router[0], w_up[0],
               w_down[0], g_ple[0], w_ple_gate[0], w_ple_proj[0], g_final)
    return h.reshape(B, T, D)
```

```python
import functools

import jax
import jax.numpy as jnp
from jax import lax
from jax.experimental import pallas as pl
from jax.experimental.pallas import tpu as pltpu

F32 = jnp.float32
BF16 = jnp.bfloat16
I32 = jnp.int32
U32 = jnp.uint32

EPS = 1e-6
GDN_HEADS = 4
GDN_DK = 128
GDN_CONV = 4
GDN_CHUNK = 64
CONF_CH = 512
CONF_WIDTH = 31
N_GROUPS = 8
EXPERTS_PER_GROUP = 8
N_EXPERTS = 64
D_EXPERT = 256
LANES = 128
TOKEN_TILE_ROWS = 8
ROUTE_LANE0 = N_GROUPS
EXPERT_BLOCK = 256
QKV_HALO = 8
CONF_HALO = 32
NEG = -1e30
VMEM_LIMIT = 56 * 1024 * 1024


def _dot(a, b):
    return jnp.dot(a, b, preferred_element_type=F32)


def _dot_nt(a, b):
    return lax.dot_general(a, b, (((1,), (1,)), ((), ())), preferred_element_type=F32)


def _dot_tn(a, b):
    return lax.dot_general(a, b, (((0,), (0,)), ((), ())), preferred_element_type=F32)


def _silu(x):
    return x * jax.nn.sigmoid(x)


def _split_bf16(x):
    hi = x.astype(BF16)
    lo = (x - hi.astype(F32)).astype(BF16)
    return hi, lo


def _inproj_kernel(x_ref, gmix_ref, wqkv_ref, wz_ref, wbah_ref, wbal_ref, wglu_ref, cw_ref, pa_ref,
                   bglu_ref, cdw_ref, cvec_ref,
                   q_ref, k_ref, v_ref, bg_ref, zs_ref, yc_ref,
                   extq, extu, *, tiles_per_seq, tm):
    i = pl.program_id(0)
    first = (i % tiles_per_seq) == 0

    x = x_ref[...]
    n1 = x * lax.rsqrt(jnp.mean(x * x, axis=-1, keepdims=True) + EPS) * gmix_ref[...]
    n1h, n1l = _split_bf16(n1)

    @pl.when(first)
    def _():
        extq[0:QKV_HALO, :] = jnp.zeros((QKV_HALO, extq.shape[1]), F32)
        extu[0:CONF_HALO, :] = jnp.zeros((CONF_HALO, extu.shape[1]), F32)

    @pl.when(jnp.logical_not(first))
    def _():
        extq[0:QKV_HALO, :] = extq[tm:tm + QKV_HALO, :]
        extu[0:CONF_HALO, :] = extu[tm:tm + CONF_HALO, :]

    extq[QKV_HALO:QKV_HALO + tm, :] = _dot(n1h, wqkv_ref[...])
    conv = None
    for kk in range(GDN_CONV):
        off = QKV_HALO - (GDN_CONV - 1) + kk
        term = cw_ref[kk:kk + 1, :] * extq[off:off + tm, :]
        conv = term if conv is None else conv + term
    qkv = _silu(conv)
    hw = GDN_HEADS * GDN_DK
    for h in range(GDN_HEADS):
        cs = slice(h * GDN_DK, (h + 1) * GDN_DK)
        qh = qkv[:, cs]
        q_ref[:, cs] = qh * lax.rsqrt(jnp.sum(qh * qh, axis=-1, keepdims=True) + EPS) * (GDN_DK ** -0.5)
        kh = qkv[:, hw + h * GDN_DK: hw + (h + 1) * GDN_DK]
        k_ref[:, cs] = kh * lax.rsqrt(jnp.sum(kh * kh, axis=-1, keepdims=True) + EPS)
    v_ref[...] = qkv[:, 2 * hw:]

    zs_ref[...] = _silu(_dot(n1h, wz_ref[...])).astype(BF16)

    ba = _dot(n1h, wbah_ref[...]) + _dot(n1l, wbah_ref[...]) + _dot(n1h, wbal_ref[...])
    lane = lax.broadcasted_iota(I32, ba.shape, 1)
    row = lax.broadcasted_iota(I32, ba.shape, 0)
    beta = jax.nn.sigmoid(ba)
    sp_in = ba + pa_ref[1:2, :]
    softplus = jnp.maximum(sp_in, 0.0) + jnp.log1p(jnp.exp(-jnp.abs(sp_in)))
    g = -jnp.exp(pa_ref[0:1, :]) * softplus
    rin = row & (GDN_CHUNK - 1)
    s = 1
    while s < GDN_CHUNK:
        g = g + jnp.where(rin >= s, pltpu.roll(g, s, 0), 0.0)
        s *= 2
    bg_ref[...] = jnp.where(lane < GDN_HEADS, beta, g)

    glu = _dot(n1h, wglu_ref[...]) + bglu_ref[...]
    extu[CONF_HALO:CONF_HALO + tm, :] = glu[:, :CONF_CH] * jax.nn.sigmoid(glu[:, CONF_CH:])
    c = None
    for kk in range(CONF_WIDTH):
        off = CONF_HALO - (CONF_WIDTH - 1) + kk
        term = cdw_ref[kk:kk + 1, :] * extu[off:off + tm, :]
        c = term if c is None else c + term
    c = c + cvec_ref[0:1, :]
    mu = jnp.mean(c, axis=-1, keepdims=True)
    cc = c - mu
    var = jnp.mean(cc * cc, axis=-1, keepdims=True)
    yln = cc * lax.rsqrt(var + EPS) * cvec_ref[1:2, :] + cvec_ref[2:3, :]
    yc_ref[...] = _silu(yln).astype(BF16)


def _gdn_kernel(q_ref, k_ref, v_ref, bg_ref, zs_ref, gout_ref, y_ref, s_ref, *, cb, nbatch, group):
    n = pl.program_id(0)

    @pl.when(n == 0)
    def _():
        s_ref[...] = jnp.zeros(s_ref.shape, F32)

    C = GDN_CHUNK
    r = lax.broadcasted_iota(I32, (C, C), 0)
    c = lax.broadcasted_iota(I32, (C, C), 1)
    causal = r >= c
    strict = r > c
    eye = r == c
    eyef = jnp.where(eye, 1.0, 0.0).astype(F32)
    gout = gout_ref[...]

    def front(b, ci, h):
        rows = slice(ci * C, (ci + 1) * C)
        cols = slice(h * GDN_DK, (h + 1) * GDN_DK)
        q = q_ref[b, rows, cols]
        k = k_ref[b, rows, cols]
        v = v_ref[b, rows, cols]
        beta = bg_ref[b, rows, h:h + 1]
        gc = bg_ref[b, rows, GDN_HEADS + h:GDN_HEADS + h + 1]
        gc_row = jnp.sum(jnp.where(eye, gc, 0.0), axis=0, keepdims=True)
        gc_last = gc[C - 1:C, :]
        eg = jnp.exp(gc)
        kb = k * beta
        return dict(
            decay=jnp.where(causal, jnp.exp(jnp.minimum(gc - gc_row, 0.0)), 0.0),
            lhs=jnp.concatenate([kb, q], axis=0).astype(BF16),
            kbf=k.astype(BF16),
            rhs=jnp.concatenate([v * beta, kb * eg], axis=1).astype(BF16),
            qd=q * eg,
            kd=(k * jnp.exp(gc_last - gc)).astype(BF16),
            gl=jnp.exp(gc_last))

    insts = [(b, ci, h) for ci in range(cb) for b in range(nbatch) for h in range(GDN_HEADS)]
    done = {}
    for g0 in range(0, len(insts), group):
        grp = insts[g0:g0 + group]
        st = [front(*key) for key in grp]
        kq = [_dot_nt(s["lhs"], s["kbf"]) for s in st]
        xp = [-jnp.where(strict, m[:C] * s["decay"], 0.0) for m, s in zip(kq, st)]
        qk = [(m[C:] * s["decay"]).astype(BF16) for m, s in zip(kq, st)]
        tinv = [eyef + x for x in xp]
        for _ in range(5):
            xpb = [x.astype(BF16) for x in xp]
            xp = [_dot(x, x) for x in xpb]
            tinv = [t + _dot(t.astype(BF16), x.astype(BF16)) for t, x in zip(tinv, xp)]
        uw = [_dot(t.astype(BF16), s["rhs"]) for t, s in zip(tinv, st)]
        for key, s, m, qkm in zip(grp, st, uw, qk):
            done[key] = dict(u=m[:, :GDN_DK],
                             wq=jnp.concatenate([m[:, GDN_DK:], s["qd"]], axis=0).astype(BF16),
                             qk=qkm, kd=s["kd"], gl=s["gl"])

    chains = [(b, h) for b in range(nbatch) for h in range(GDN_HEADS)]
    for ci in range(cb):
        rows = slice(ci * C, (ci + 1) * C)
        dd = [done[(b, ci, h)] for b, h in chains]
        s_old = [s_ref[b * GDN_HEADS + h] for b, h in chains]
        wq = [_dot(d["wq"], s.astype(BF16)) for d, s in zip(dd, s_old)]
        v_new = [(d["u"] - m[:C]).astype(BF16) for d, m in zip(dd, wq)]
        o = [m[C:] + _dot(d["qk"], vn) for d, m, vn in zip(dd, wq, v_new)]
        s_new = [s * d["gl"] + _dot_tn(d["kd"], vn) for d, s, vn in zip(dd, s_old, v_new)]
        for (b, h), sn, oo in zip(chains, s_new, o):
            cols = slice(h * GDN_DK, (h + 1) * GDN_DK)
            s_ref[b * GDN_HEADS + h] = sn
            on = oo * lax.rsqrt(jnp.mean(oo * oo, axis=-1, keepdims=True) + EPS) * gout
            y_ref[b, rows, cols] = (on * zs_ref[b, rows, cols].astype(F32)).astype(BF16)


def _router_kernel(x_ref, yg_ref, yc_ref, wo1_ref, wo2_ref, gmoe_ref, wrh_ref, wrl_ref, ltri_ref,
                   h1_ref, xn_ref, ri_ref, rf_ref, cnt_ref, carry):
    i = pl.program_id(0)

    @pl.when(i == 0)
    def _():
        carry[...] = jnp.zeros(carry.shape, F32)

    h1 = x_ref[...] + _dot(yg_ref[...], wo1_ref[...]) + _dot(yc_ref[...], wo2_ref[...])
    h1_ref[...] = h1
    xn = h1 * lax.rsqrt(jnp.mean(h1 * h1, axis=-1, keepdims=True) + EPS) * gmoe_ref[...]
    xnh, xnl = _split_bf16(xn)
    for f in range(TOKEN_TILE_ROWS):
        xn_ref[pl.ds(f, xn.shape[0], stride=TOKEN_TILE_ROWS), :] = xn[:, f * LANES:(f + 1) * LANES]
    lg = _dot(xnh, wrh_ref[...]) + _dot(xnl, wrh_ref[...]) + _dot(xnh, wrl_ref[...])

    lane = lax.broadcasted_iota(I32, lg.shape, 1)
    lanef = lane.astype(F32)
    gl = jnp.where(lane < N_GROUPS, lg, NEG)
    gmax = jnp.max(gl, axis=-1, keepdims=True)
    g_w = 1.0 / jnp.sum(jnp.exp(gl - gmax), axis=-1, keepdims=True)
    gidx = jnp.min(jnp.where(gl == gmax, lanef, float(LANES)), axis=-1, keepdims=True)
    lane_group = ((lane - ROUTE_LANE0) >> 3).astype(F32)
    in_group = (lane >= ROUTE_LANE0) & (lane < ROUTE_LANE0 + N_EXPERTS) & (lane_group == gidx)
    el = jnp.where(in_group, lg, NEG)
    m1 = jnp.max(el, axis=-1, keepdims=True)
    i1 = jnp.min(jnp.where(el == m1, lanef, float(LANES)), axis=-1, keepdims=True)
    el2 = jnp.where(lanef == i1, NEG, el)
    m2 = jnp.max(el2, axis=-1, keepdims=True)
    i2 = jnp.min(jnp.where(el2 == m2, lanef, float(LANES)), axis=-1, keepdims=True)
    ratio = jnp.exp(m2 - m1)
    gate1 = g_w / (1.0 + ratio)
    gate2 = g_w * ratio / (1.0 + ratio)

    oh1 = lanef == i1
    oh2 = lanef == i2
    oh = jnp.where(oh1 | oh2, 1.0, 0.0).astype(F32)
    cum = _dot(ltri_ref[...], oh.astype(BF16)) + carry[...]
    rank1 = jnp.sum(jnp.where(oh1, cum, 0.0), axis=-1, keepdims=True)
    rank2 = jnp.sum(jnp.where(oh2, cum, 0.0), axis=-1, keepdims=True)
    carry[...] = carry[...] + jnp.sum(oh, axis=0, keepdims=True)

    routes = jnp.where(lane == 0, i1 - ROUTE_LANE0,
                       jnp.where(lane == 1, i2 - ROUTE_LANE0,
                                 jnp.where(lane == 2, rank1, jnp.where(lane == 3, rank2, 0.0))))
    ri_ref[...] = routes.astype(I32)
    rf_ref[...] = jnp.where(lane == 0, gate1, jnp.where(lane == 1, gate2, 0.0))
    cnt_ref[...] = jnp.broadcast_to(carry[...], cnt_ref.shape)


def _pos_kernel(ri_ref, cnt_ref, post_ref, blk_ref, ends_ref, *, tm):
    shift = EXPERT_BLOCK.bit_length() - 1
    cnt = cnt_ref[...].astype(I32)
    padded = ((cnt + (EXPERT_BLOCK - 1)) >> shift) << shift
    lane8 = lax.broadcasted_iota(I32, padded.shape, 1)
    pend = padded
    s = 1
    while s < LANES:
        pend = pend + jnp.where(lane8 >= s, pltpu.roll(pend, s, 1), 0)
        s *= 2
    pstart = (pend - padded)[0:1, :].astype(F32)

    ri = ri_ref[...]
    lane = lax.broadcasted_iota(I32, ri.shape, 1)
    e1 = ri[:, 0:1] + ROUTE_LANE0
    e2 = ri[:, 1:2] + ROUTE_LANE0
    ps1 = jnp.sum(jnp.where(lane == e1, pstart, 0.0), axis=-1, keepdims=True)
    ps2 = jnp.sum(jnp.where(lane == e2, pstart, 0.0), axis=-1, keepdims=True)
    pos1 = ps1 + ri[:, 2:3].astype(F32)
    pos2 = ps2 + ri[:, 3:4].astype(F32)
    posmat = jnp.where(lane == 0, pos1, jnp.where(lane == 1, pos2, 0.0))
    for j in range(tm // LANES):
        pt = posmat[j * LANES:(j + 1) * LANES, :].T
        post_ref[:, j * LANES:(j + 1) * LANES] = pt[0:8, :].astype(I32)

    pendf = pend.astype(F32)[0:1, :]
    rr = lax.broadcasted_iota(I32, (LANES, LANES), 0)
    cc = lax.broadcasted_iota(I32, (LANES, LANES), 1)
    to_col = lambda row: jnp.sum(jnp.where(rr == cc, row, 0.0), axis=1, keepdims=True)
    pend_col = to_col(pendf)
    pstart_col = to_col(pstart)
    cend_col = pstart_col + to_col(cnt_ref[0:1, :])
    nbl = blk_ref.shape[1]
    er = lax.broadcasted_iota(I32, (LANES, nbl), 0)
    jb = (lax.broadcasted_iota(I32, (LANES, nbl), 1) * EXPERT_BLOCK).astype(F32)
    valid = (er >= ROUTE_LANE0) & (er < ROUTE_LANE0 + N_EXPERTS)
    be = jnp.sum(jnp.where(valid & (pend_col <= jb), 1.0, 0.0), axis=0, keepdims=True)
    be = jnp.minimum(be, float(N_EXPERTS - 1))
    nused = jnp.max(pendf, axis=-1, keepdims=True) * (1.0 / EXPERT_BLOCK)
    inblk = valid & (pstart_col <= jb) & (jb < pend_col)
    nvalid = jnp.sum(jnp.where(inblk, jnp.clip(cend_col - jb, 0.0, float(EXPERT_BLOCK)), 0.0),
                     axis=0, keepdims=True)
    rowi = lax.broadcasted_iota(I32, blk_ref.shape, 0)
    blk = jnp.where(rowi == 0, be, jnp.where(rowi == 1, nused, nvalid)).astype(I32)
    blk_ref[...] = blk
    cend_row = pstart + cnt_ref[0:1, :]
    rowe = lax.broadcasted_iota(I32, ends_ref.shape, 0)
    ends_ref[...] = jnp.where(rowe == 0, cend_row, pendf).astype(I32)


def _slots_kernel(cend_ref, pend_ref, pos1_ref, pos2_ref, tok_ref, dst_ref, *, tm, n_tokens):
    i = pl.program_id(0)

    @pl.when(i == 0)
    def _():
        def fill(p, c):
            tok_ref[p] = 0
            dst_ref[p] = 0
            return c

        def per_expert(e, carry):
            return lax.fori_loop(cend_ref[e], pend_ref[e], fill, carry)

        lax.fori_loop(0, N_EXPERTS, per_expert, 0)
        lax.fori_loop(pend_ref[N_EXPERTS - 1], tok_ref.shape[0], fill, 0)

    def body(r, carry):
        t = i * tm + r
        p1 = pos1_ref[0, 0, r]
        p2 = pos2_ref[0, 0, r]
        tok_ref[p1] = t
        tok_ref[p2] = t
        dst_ref[p1] = t
        dst_ref[p2] = n_tokens + t
        return carry

    lax.fori_loop(0, tm, body, 0, unroll=16)


def _expert_kernel(be_ref, nu_ref, nv_ref, tokc_ref, tokn_ref, dstc_ref, xn_ref, wup_ref, wdn_ref, y2_ref,
                   wupb, wdnb, xbuf, ybuf, gsem, ssem):
    j = pl.program_id(0)
    last = pl.num_programs(0) - 1
    cur = j % 2
    nxt = 1 - cur
    n_used = nu_ref[0]
    R = TOKEN_TILE_ROWS
    G = 8
    buf_rows = EXPERT_BLOCK * R

    def gather_row(tok_ref, b, r, prio):
        return pltpu.make_async_copy(xn_ref.at[pl.ds(tok_ref[0, 0, r] * R, R), :],
                                     xbuf.at[pl.ds(b * buf_rows + r * R, R), :], gsem.at[b]).start(priority=prio)

    def scatter_row(b, r, prio):
        return pltpu.make_async_copy(ybuf.at[pl.ds(b * buf_rows + r * R, R), :],
                                     y2_ref.at[pl.ds(dstc_ref[0, 0, r] * R, R), :], ssem.at[b]).start(priority=prio)

    def for_rows(nrows, row_fn):
        def group(g, carry):
            for u in range(G):
                row_fn(g * G + u, u % 2)
            return carry

        def single(r, carry):
            row_fn(r, 0)
            return carry

        full = nrows >> 3
        lax.fori_loop(0, full, group, 0)
        lax.fori_loop(full * G, nrows, single, 0)

    def gather(tok_ref, b, nrows):
        for_rows(nrows, lambda r, prio: gather_row(tok_ref, b, r, prio))

    def scatter(b, nrows):
        for_rows(nrows, lambda r, prio: scatter_row(b, r, prio))

    def wait_rows(sem, nrows):
        def group(g, carry):
            pltpu.make_async_copy(xn_ref.at[pl.ds(0, G * R), :], xbuf.at[pl.ds(0, G * R), :], sem).wait()
            return carry

        def single(r, carry):
            pltpu.make_async_copy(xn_ref.at[pl.ds(0, R), :], xbuf.at[pl.ds(0, R), :], sem).wait()
            return carry

        full = nrows >> 3
        lax.fori_loop(0, full, group, 0)
        lax.fori_loop(full * G, nrows, single, 0)

    @pl.when(j == 0)
    def _():
        xbuf[...] = jnp.zeros(xbuf.shape, F32)
        gather(tokc_ref, 0, nv_ref[0])

    @pl.when(j + 1 < n_used)
    def _():
        gather(tokn_ref, nxt, nv_ref[jnp.minimum(j + 1, last)])

    prev = be_ref[jnp.maximum(j - 1, 0)]

    @pl.when((j == 0) | (be_ref[j] != prev))
    def _():
        wupb[...] = wup_ref[...].astype(BF16)
        wdnb[...] = wdn_ref[...].astype(BF16)

    @pl.when(j < n_used)
    def _():
        wait_rows(gsem.at[cur], nv_ref[j])
        base = cur * buf_rows
        x = jnp.concatenate([xbuf[pl.ds(base + f, EXPERT_BLOCK, stride=R), :] for f in range(R)], axis=1)
        ab = _dot(x.astype(BF16), wupb[...])
        hmid = _silu(ab[:, :D_EXPERT]) * ab[:, D_EXPERT:]
        y = _dot(hmid.astype(BF16), wdnb[...])
        for f in range(R):
            ybuf[pl.ds(base + f, EXPERT_BLOCK, stride=R), :] = y[:, f * LANES:(f + 1) * LANES]

    @pl.when((j >= 1) & (j <= n_used))
    def _():
        wait_rows(ssem.at[nxt], nv_ref[jnp.maximum(j - 1, 0)])

    @pl.when(j < n_used)
    def _():
        scatter(cur, nv_ref[j])

    @pl.when((j == last) & (j < n_used))
    def _():
        wait_rows(ssem.at[cur], nv_ref[j])


def _ple_kernel(h1_ref, rf_ref, p_ref, ya_ref, yb_ref, gple_ref, wg_ref, wp_ref, gfin_ref, out_ref):
    tm = h1_ref.shape[0]
    R = TOKEN_TILE_ROWS
    rows = lambda ref: jnp.concatenate([ref[pl.ds(f, tm, stride=R), :] for f in range(R)], axis=1)
    rf = rf_ref[...]
    h2 = h1_ref[...] + (rf[:, 0:1] * rows(ya_ref) + rf[:, 1:2] * rows(yb_ref))
    hn = h2 * lax.rsqrt(jnp.mean(h2 * h2, axis=-1, keepdims=True) + EPS) * gple_ref[...]
    gate = jax.nn.sigmoid(_dot(hn.astype(BF16), wg_ref[...]))
    h3 = h2 + gate * _dot(p_ref[...].astype(BF16), wp_ref[...])
    out_ref[...] = h3 * lax.rsqrt(jnp.mean(h3 * h3, axis=-1, keepdims=True) + EPS) * gfin_ref[...]


def _params(sem):
    return pltpu.CompilerParams(dimension_semantics=sem, vmem_limit_bytes=VMEM_LIMIT)


def _full(shape):
    nd = len(shape)
    return pl.BlockSpec(shape, lambda *_: (0,) * nd)


def _pad_lanes(a, lane0=0):
    out = jnp.zeros((a.shape[0], LANES), a.dtype)
    return out.at[:, lane0:lane0 + a.shape[1]].set(a)


def _layer(x2, p2, B, T, g_mix, w_in, w_conv_qkv, a_log, dt_bias, g_gdn_out, b_glu, w_conv_dw,
           b_conv_dw, ln_conv_g, ln_conv_b, w_out, g_moe, w_group_router, w_expert_router, w_up,
           w_down, g_ple, w_ple_gate, w_ple_proj, g_out):
    N, D = x2.shape
    hw = GDN_HEADS * GDN_DK
    tm = min(512, T)
    nt = N // tm

    o1 = 3 * hw
    o2 = o1 + hw
    o4 = o2 + 2 * GDN_HEADS
    wqkv = w_in[:, :o1].astype(BF16)
    wz = w_in[:, o1:o2].astype(BF16)
    wbah, wbal = _split_bf16(_pad_lanes(w_in[:, o2:o4]))
    wglu = w_in[:, o4:].astype(BF16)
    pa = jnp.concatenate([_pad_lanes(a_log[None, :], GDN_HEADS), _pad_lanes(dt_bias[None, :], GDN_HEADS),
                          jnp.zeros((6, LANES), F32)], axis=0)
    cdw = jnp.concatenate([w_conv_dw, jnp.zeros((1, CONF_CH), F32)], axis=0)
    cvec = jnp.concatenate([b_conv_dw[None], ln_conv_g[None], ln_conv_b[None], jnp.zeros((5, CONF_CH), F32)], 0)

    row_spec = lambda w: pl.BlockSpec((tm, w), lambda i: (i, 0))
    q, k, v, bg, zs, yc = pl.pallas_call(
        functools.partial(_inproj_kernel, tiles_per_seq=T // tm, tm=tm),
        grid=(nt,),
        in_specs=[row_spec(D), _full((1, D)), _full(wqkv.shape), _full(wz.shape), _full(wbah.shape),
                  _full(wbal.shape), _full(wglu.shape), _full(w_conv_qkv.shape), _full(pa.shape),
                  _full((1, 2 * CONF_CH)), _full(cdw.shape), _full(cvec.shape)],
        out_specs=[row_spec(hw), row_spec(hw), row_spec(hw), row_spec(LANES), row_spec(hw), row_spec(CONF_CH)],
        out_shape=[jax.ShapeDtypeStruct((N, hw), F32)] * 3 + [jax.ShapeDtypeStruct((N, LANES), F32),
                   jax.ShapeDtypeStruct((N, hw), BF16), jax.ShapeDtypeStruct((N, CONF_CH), BF16)],
        scratch_shapes=[pltpu.VMEM((tm + QKV_HALO, 3 * hw), F32), pltpu.VMEM((tm + CONF_HALO, CONF_CH), F32)],
        compiler_params=_params(("arbitrary",)),
        name="inproj",
    )(x2, g_mix[None], wqkv, wz, wbah, wbal, wglu, w_conv_qkv, pa, b_glu[None], cdw, cvec)

    cb = 4
    rows = cb * GDN_CHUNK
    seq_spec = lambda w: pl.BlockSpec((B, rows, w), lambda n: (0, n, 0))
    r3 = lambda a: a.reshape(B, T, a.shape[-1])
    ygdn = pl.pallas_call(
        functools.partial(_gdn_kernel, cb=cb, nbatch=B, group=2 * GDN_HEADS),
        grid=(T // rows,),
        in_specs=[seq_spec(hw), seq_spec(hw), seq_spec(hw), seq_spec(LANES), seq_spec(hw),
                  pl.BlockSpec((1, GDN_DK), lambda n: (0, 0))],
        out_specs=seq_spec(hw),
        out_shape=jax.ShapeDtypeStruct((B, T, hw), BF16),
        scratch_shapes=[pltpu.VMEM((B * GDN_HEADS, GDN_DK, GDN_DK), F32)],
        compiler_params=_params(("arbitrary",)),
        name="gdn",
    )(r3(q), r3(k), r3(v), r3(bg), r3(zs), g_gdn_out[None]).reshape(N, hw)

    wo1 = w_out[:hw].astype(BF16)
    wo2 = w_out[hw:].astype(BF16)
    wrh, wrl = _split_bf16(_pad_lanes(jnp.concatenate([w_group_router, w_expert_router], axis=1)))
    ltri = jnp.tril(jnp.ones((tm, tm), F32), -1).astype(BF16)
    assert D == TOKEN_TILE_ROWS * LANES
    tile_spec = pl.BlockSpec((tm * TOKEN_TILE_ROWS, LANES), lambda i: (i, 0))
    h1, xn, ri, rf, cnt = pl.pallas_call(
        _router_kernel,
        grid=(nt,),
        in_specs=[row_spec(D), row_spec(hw), row_spec(CONF_CH), _full(wo1.shape), _full(wo2.shape),
                  _full((1, D)), _full(wrh.shape), _full(wrl.shape), _full(ltri.shape)],
        out_specs=[row_spec(D), tile_spec, row_spec(LANES), row_spec(LANES), _full((8, LANES))],
        out_shape=[jax.ShapeDtypeStruct((N, D), F32), jax.ShapeDtypeStruct((N * TOKEN_TILE_ROWS, LANES), F32),
                   jax.ShapeDtypeStruct((N, LANES), I32), jax.ShapeDtypeStruct((N, LANES), F32),
                   jax.ShapeDtypeStruct((8, LANES), F32)],
        scratch_shapes=[pltpu.VMEM((1, LANES), F32)],
        compiler_params=_params(("arbitrary",)),
        name="router",
    )(x2, ygdn, yc, wo1, wo2, g_moe[None], wrh, wrl, ltri)

    nb = (2 * N) // EXPERT_BLOCK + N_EXPERTS
    nbl = -(-nb // LANES) * LANES
    post, blk, ends = pl.pallas_call(
        functools.partial(_pos_kernel, tm=tm),
        grid=(nt,),
        in_specs=[row_spec(LANES), _full((8, LANES))],
        out_specs=[pl.BlockSpec((8, tm), lambda i: (0, i)), _full((8, nbl)), _full((8, LANES))],
        out_shape=[jax.ShapeDtypeStruct((8, N), I32), jax.ShapeDtypeStruct((8, nbl), I32),
                   jax.ShapeDtypeStruct((8, LANES), I32)],
        compiler_params=_params(("arbitrary",)),
        name="positions",
    )(ri, cnt)
    pos1 = post[0].reshape(nt, 1, tm)
    pos2 = post[1].reshape(nt, 1, tm)
    block_expert = blk[0, :nb]
    n_used = blk[1, :1]
    n_valid = blk[2, :nb]

    P = nb * EXPERT_BLOCK
    real_end = ends[0, ROUTE_LANE0:ROUTE_LANE0 + N_EXPERTS]
    padded_end = ends[1, ROUTE_LANE0:ROUTE_LANE0 + N_EXPERTS]
    smem_spec = pl.BlockSpec((1, 1, tm), lambda i, *_: (i, 0, 0), memory_space=pltpu.SMEM)
    flat_smem = pl.BlockSpec((P,), lambda i, *_: (0,), memory_space=pltpu.SMEM)
    tok, dst = pl.pallas_call(
        functools.partial(_slots_kernel, tm=tm, n_tokens=N),
        grid_spec=pltpu.PrefetchScalarGridSpec(
            num_scalar_prefetch=2, grid=(nt,), in_specs=[smem_spec, smem_spec],
            out_specs=[flat_smem, flat_smem]),
        out_shape=[jax.ShapeDtypeStruct((P,), I32)] * 2,
        compiler_params=_params(("arbitrary",)),
        name="slots",
    )(real_end, padded_end, pos1, pos2)
    tok = tok.reshape(nb, 1, EXPERT_BLOCK)
    dst = dst.reshape(nb, 1, EXPERT_BLOCK)

    any_spec = pl.BlockSpec(memory_space=pl.ANY)
    blk_smem = lambda off: pl.BlockSpec((1, 1, EXPERT_BLOCK),
                                        lambda j, be, nu, nv: (jnp.minimum(j + off, nb - 1), 0, 0),
                                        memory_space=pltpu.SMEM)
    buf = pltpu.VMEM((2 * EXPERT_BLOCK * TOKEN_TILE_ROWS, LANES), F32)
    y2 = pl.pallas_call(
        _expert_kernel,
        grid_spec=pltpu.PrefetchScalarGridSpec(
            num_scalar_prefetch=3,
            grid=(nb,),
            in_specs=[blk_smem(0), blk_smem(1), blk_smem(0), any_spec,
                      pl.BlockSpec((None, D, 2 * D_EXPERT), lambda j, be, nu, nv: (be[j], 0, 0)),
                      pl.BlockSpec((None, D_EXPERT, D), lambda j, be, nu, nv: (be[j], 0, 0))],
            out_specs=any_spec,
            scratch_shapes=[pltpu.VMEM((D, 2 * D_EXPERT), BF16), pltpu.VMEM((D_EXPERT, D), BF16), buf, buf,
                            pltpu.SemaphoreType.DMA((2,)), pltpu.SemaphoreType.DMA((2,))]),
        out_shape=jax.ShapeDtypeStruct((2 * N * TOKEN_TILE_ROWS, LANES), F32),
        compiler_params=_params(("arbitrary",)),
        name="experts",
    )(block_expert, n_used, n_valid, tok, tok, dst, xn, w_up, w_down)

    wg = w_ple_gate.astype(BF16)
    wp = w_ple_proj.astype(BF16)
    out = pl.pallas_call(
        _ple_kernel,
        grid=(nt,),
        in_specs=[row_spec(D), row_spec(LANES), row_spec(p2.shape[1]), tile_spec,
                  pl.BlockSpec((tm * TOKEN_TILE_ROWS, LANES), lambda i: (nt + i, 0)),
                  _full((1, D)), _full(wg.shape), _full(wp.shape), _full((1, D))],
        out_specs=row_spec(D),
        out_shape=jax.ShapeDtypeStruct((N, D), F32),
        compiler_params=_params(("arbitrary",)),
        name="ple",
    )(h1, rf, p2, y2, y2, g_ple[None], wg, wp, g_out[None])
    return out


def kernel(x, p, g_mix, w_in, w_conv_qkv, a_log, dt_bias, g_gdn_out, b_glu, w_conv_dw, b_conv_dw, ln_conv_g, ln_conv_b, w_out, g_moe, w_group_router, w_expert_router, w_up, w_down, g_ple, w_ple_gate, w_ple_proj, g_final):
    B, T, D = x.shape
    depth = p.shape[0]
    assert depth == 1, "the final RMSNorm is fused into the (single) layer's last kernel"
    h = _layer(x.reshape(B * T, D), p[0].reshape(B * T, -1), B, T, g_mix[0], w_in[0], w_conv_qkv[0],
               a_log[0], dt_bias[0], g_gdn_out[0], b_glu[0], w_conv_dw[0], b_conv_dw[0], ln_conv_g[0],
               ln_conv_b[0], w_out[0], g_moe[0], w_group_router[0], w_expert_router[0], w_up[0],
               w_down[0], g_ple[0], w_ple_gate[0], w_ple_proj[0], g_final)
    return h.reshape(B, T, D)
```

```python
import functools

import jax
import jax.numpy as jnp
from jax import lax
from jax.experimental import pallas as pl
from jax.experimental.pallas import tpu as pltpu

F32 = jnp.float32
BF16 = jnp.bfloat16
I32 = jnp.int32
U32 = jnp.uint32

EPS = 1e-6
GDN_HEADS = 4
GDN_DK = 128
GDN_CONV = 4
GDN_CHUNK = 64
CONF_CH = 512
CONF_WIDTH = 31
N_GROUPS = 8
EXPERTS_PER_GROUP = 8
N_EXPERTS = 64
D_EXPERT = 256
LANES = 128
SUBLANES = 8
TOKEN_TILE_ROWS = 8
ROUTE_LANE0 = N_GROUPS
EXPERT_BLOCK = 256
QKV_HALO = 8
CONF_HALO = 32
NEG = -1e30
VMEM_LIMIT = 56 * 1024 * 1024


def _dot(a, b):
    return jnp.dot(a, b, preferred_element_type=F32)


def _dot_nt(a, b):
    return lax.dot_general(a, b, (((1,), (1,)), ((), ())), preferred_element_type=F32)


def _dot_tn(a, b):
    return lax.dot_general(a, b, (((0,), (0,)), ((), ())), preferred_element_type=F32)


def _silu(x):
    return x * jax.nn.sigmoid(x)


def _split_bf16(x):
    hi = x.astype(BF16)
    lo = (x - hi.astype(F32)).astype(BF16)
    return hi, lo


def _causal_taps(ext_ref, w_ref, n_taps, halo, tm, cols):
    rows = ext_ref.shape[0]
    ext = ext_ref[:, cols]
    first = halo - (n_taps - 1)
    acc = None
    for res in range(SUBLANES):
        taps = [k for k in range(n_taps) if (first + k) % SUBLANES == res]
        if not taps:
            continue
        shifted = ext if res == 0 else pltpu.roll(ext, rows - res, 0)
        for k in taps:
            a = first + k - res
            term = w_ref[k:k + 1, cols] * shifted[a:a + tm, :]
            acc = term if acc is None else acc + term
    return acc


def _inproj_kernel(x_ref, gmix_ref, wqkv_ref, wz_ref, wbah_ref, wbal_ref, wglu_ref, cw_ref, pa_ref,
                   bglu_ref, cdw_ref, cvec_ref,
                   q_ref, k_ref, v_ref, bg_ref, zs_ref, yc_ref,
                   extq, extu, *, tiles_per_seq, tm):
    i = pl.program_id(0)
    first = (i % tiles_per_seq) == 0

    x = x_ref[...]
    n1 = x * lax.rsqrt(jnp.mean(x * x, axis=-1, keepdims=True) + EPS) * gmix_ref[...]
    n1h, n1l = _split_bf16(n1)

    @pl.when(first)
    def _():
        extq[0:QKV_HALO, :] = jnp.zeros((QKV_HALO, extq.shape[1]), F32)
        extu[0:CONF_HALO, :] = jnp.zeros((CONF_HALO, extu.shape[1]), F32)

    @pl.when(jnp.logical_not(first))
    def _():
        extq[0:QKV_HALO, :] = extq[tm:tm + QKV_HALO, :]
        extu[0:CONF_HALO, :] = extu[tm:tm + CONF_HALO, :]

    extq[QKV_HALO:QKV_HALO + tm, :] = _dot(n1h, wqkv_ref[...])
    hw = GDN_HEADS * GDN_DK
    qc = _silu(_causal_taps(extq, cw_ref, GDN_CONV, QKV_HALO, tm, slice(0, hw)))
    kc = _silu(_causal_taps(extq, cw_ref, GDN_CONV, QKV_HALO, tm, slice(hw, 2 * hw)))
    for h in range(GDN_HEADS):
        cs = slice(h * GDN_DK, (h + 1) * GDN_DK)
        qh = qc[:, cs]
        q_ref[:, cs] = qh * lax.rsqrt(jnp.sum(qh * qh, axis=-1, keepdims=True) + EPS) * (GDN_DK ** -0.5)
        kh = kc[:, cs]
        k_ref[:, cs] = kh * lax.rsqrt(jnp.sum(kh * kh, axis=-1, keepdims=True) + EPS)
    v_ref[...] = _silu(_causal_taps(extq, cw_ref, GDN_CONV, QKV_HALO, tm, slice(2 * hw, 3 * hw)))

    zs_ref[...] = _silu(_dot(n1h, wz_ref[...])).astype(BF16)

    ba = _dot(n1h, wbah_ref[...]) + _dot(n1l, wbah_ref[...]) + _dot(n1h, wbal_ref[...])
    lane = lax.broadcasted_iota(I32, ba.shape, 1)
    row = lax.broadcasted_iota(I32, ba.shape, 0)
    beta = jax.nn.sigmoid(ba)
    sp_in = ba + pa_ref[1:2, :]
    softplus = jnp.maximum(sp_in, 0.0) + jnp.log1p(jnp.exp(-jnp.abs(sp_in)))
    g = -jnp.exp(pa_ref[0:1, :]) * softplus
    rin = row & (GDN_CHUNK - 1)
    s = 1
    while s < GDN_CHUNK:
        g = g + jnp.where(rin >= s, pltpu.roll(g, s, 0), 0.0)
        s *= 2
    bg_ref[...] = jnp.where(lane < GDN_HEADS, beta, g)

    glu = _dot(n1h, wglu_ref[...]) + bglu_ref[...]
    extu[CONF_HALO:CONF_HALO + tm, :] = glu[:, :CONF_CH] * jax.nn.sigmoid(glu[:, CONF_CH:])
    c = _causal_taps(extu, cdw_ref, CONF_WIDTH, CONF_HALO, tm, slice(0, CONF_CH)) + cvec_ref[0:1, :]
    mu = jnp.mean(c, axis=-1, keepdims=True)
    cc = c - mu
    var = jnp.mean(cc * cc, axis=-1, keepdims=True)
    yln = cc * lax.rsqrt(var + EPS) * cvec_ref[1:2, :] + cvec_ref[2:3, :]
    yc_ref[...] = _silu(yln).astype(BF16)


def _gdn_kernel(q_ref, k_ref, v_ref, bg_ref, zs_ref, gout_ref, y_ref, s_ref, *, cb, nbatch, group):
    n = pl.program_id(0)

    @pl.when(n == 0)
    def _():
        s_ref[...] = jnp.zeros(s_ref.shape, F32)

    C = GDN_CHUNK
    r = lax.broadcasted_iota(I32, (C, C), 0)
    c = lax.broadcasted_iota(I32, (C, C), 1)
    causal = r >= c
    strict = r > c
    eye = r == c
    eyef = jnp.where(eye, 1.0, 0.0).astype(F32)
    gout = gout_ref[...]

    def front(b, ci, h):
        rows = slice(ci * C, (ci + 1) * C)
        cols = slice(h * GDN_DK, (h + 1) * GDN_DK)
        q = q_ref[b, rows, cols]
        k = k_ref[b, rows, cols]
        v = v_ref[b, rows, cols]
        beta = bg_ref[b, rows, h:h + 1]
        gc = bg_ref[b, rows, GDN_HEADS + h:GDN_HEADS + h + 1]
        gc_row = jnp.sum(jnp.where(eye, gc, 0.0), axis=0, keepdims=True)
        gc_last = gc[C - 1:C, :]
        eg = jnp.exp(gc)
        kb = k * beta
        return dict(
            decay=jnp.where(causal, jnp.exp(jnp.minimum(gc - gc_row, 0.0)), 0.0),
            lhs=jnp.concatenate([kb, q], axis=0).astype(BF16),
            kbf=k.astype(BF16),
            rhs=jnp.concatenate([v * beta, kb * eg], axis=1).astype(BF16),
            qd=q * eg,
            kd=(k * jnp.exp(gc_last - gc)).astype(BF16),
            gl=jnp.exp(gc_last))

    insts = [(b, ci, h) for ci in range(cb) for b in range(nbatch) for h in range(GDN_HEADS)]
    done = {}
    for g0 in range(0, len(insts), group):
        grp = insts[g0:g0 + group]
        st = [front(*key) for key in grp]
        kq = [_dot_nt(s["lhs"], s["kbf"]) for s in st]
        xp = [-jnp.where(strict, m[:C] * s["decay"], 0.0) for m, s in zip(kq, st)]
        qk = [(m[C:] * s["decay"]).astype(BF16) for m, s in zip(kq, st)]
        tinv = [eyef + x for x in xp]
        for _ in range(5):
            xpb = [x.astype(BF16) for x in xp]
            xp = [_dot(x, x) for x in xpb]
            tinv = [t + _dot(t.astype(BF16), x.astype(BF16)) for t, x in zip(tinv, xp)]
        uw = [_dot(t.astype(BF16), s["rhs"]) for t, s in zip(tinv, st)]
        for key, s, m, qkm in zip(grp, st, uw, qk):
            done[key] = dict(u=m[:, :GDN_DK],
                             wq=jnp.concatenate([m[:, GDN_DK:], s["qd"]], axis=0).astype(BF16),
                             qk=qkm, kd=s["kd"], gl=s["gl"])

    chains = [(b, h) for b in range(nbatch) for h in range(GDN_HEADS)]
    for ci in range(cb):
        rows = slice(ci * C, (ci + 1) * C)
        dd = [done[(b, ci, h)] for b, h in chains]
        s_old = [s_ref[b * GDN_HEADS + h] for b, h in chains]
        wq = [_dot(d["wq"], s.astype(BF16)) for d, s in zip(dd, s_old)]
        v_new = [(d["u"] - m[:C]).astype(BF16) for d, m in zip(dd, wq)]
        o = [m[C:] + _dot(d["qk"], vn) for d, m, vn in zip(dd, wq, v_new)]
        s_new = [s * d["gl"] + _dot_tn(d["kd"], vn) for d, s, vn in zip(dd, s_old, v_new)]
        for (b, h), sn, oo in zip(chains, s_new, o):
            cols = slice(h * GDN_DK, (h + 1) * GDN_DK)
            s_ref[b * GDN_HEADS + h] = sn
            on = oo * lax.rsqrt(jnp.mean(oo * oo, axis=-1, keepdims=True) + EPS) * gout
            y_ref[b, rows, cols] = (on * zs_ref[b, rows, cols].astype(F32)).astype(BF16)


def _router_kernel(x_ref, yg_ref, yc_ref, wo1_ref, wo2_ref, gmoe_ref, wrh_ref, wrl_ref, ltri_ref,
                   h1_ref, xn_ref, ri_ref, rf_ref, cnt_ref, carry):
    i = pl.program_id(0)

    @pl.when(i == 0)
    def _():
        carry[...] = jnp.zeros(carry.shape, F32)

    h1 = x_ref[...] + _dot(yg_ref[...], wo1_ref[...]) + _dot(yc_ref[...], wo2_ref[...])
    h1_ref[...] = h1
    xn = h1 * lax.rsqrt(jnp.mean(h1 * h1, axis=-1, keepdims=True) + EPS) * gmoe_ref[...]
    xnh, xnl = _split_bf16(xn)
    for f in range(TOKEN_TILE_ROWS):
        xn_ref[pl.ds(f, xn.shape[0], stride=TOKEN_TILE_ROWS), :] = xn[:, f * LANES:(f + 1) * LANES]
    lg = _dot(xnh, wrh_ref[...]) + _dot(xnl, wrh_ref[...]) + _dot(xnh, wrl_ref[...])

    lane = lax.broadcasted_iota(I32, lg.shape, 1)
    lanef = lane.astype(F32)
    gl = jnp.where(lane < N_GROUPS, lg, NEG)
    gmax = jnp.max(gl, axis=-1, keepdims=True)
    g_w = 1.0 / jnp.sum(jnp.exp(gl - gmax), axis=-1, keepdims=True)
    gidx = jnp.min(jnp.where(gl == gmax, lanef, float(LANES)), axis=-1, keepdims=True)
    lane_group = ((lane - ROUTE_LANE0) >> 3).astype(F32)
    in_group = (lane >= ROUTE_LANE0) & (lane < ROUTE_LANE0 + N_EXPERTS) & (lane_group == gidx)
    el = jnp.where(in_group, lg, NEG)
    m1 = jnp.max(el, axis=-1, keepdims=True)
    i1 = jnp.min(jnp.where(el == m1, lanef, float(LANES)), axis=-1, keepdims=True)
    el2 = jnp.where(lanef == i1, NEG, el)
    m2 = jnp.max(el2, axis=-1, keepdims=True)
    i2 = jnp.min(jnp.where(el2 == m2, lanef, float(LANES)), axis=-1, keepdims=True)
    ratio = jnp.exp(m2 - m1)
    gate1 = g_w / (1.0 + ratio)
    gate2 = g_w * ratio / (1.0 + ratio)

    oh1 = lanef == i1
    oh2 = lanef == i2
    oh = jnp.where(oh1 | oh2, 1.0, 0.0).astype(F32)
    cum = _dot(ltri_ref[...], oh.astype(BF16)) + carry[...]
    rank1 = jnp.sum(jnp.where(oh1, cum, 0.0), axis=-1, keepdims=True)
    rank2 = jnp.sum(jnp.where(oh2, cum, 0.0), axis=-1, keepdims=True)
    carry[...] = carry[...] + jnp.sum(oh, axis=0, keepdims=True)

    routes = jnp.where(lane == 0, i1 - ROUTE_LANE0,
                       jnp.where(lane == 1, i2 - ROUTE_LANE0,
                                 jnp.where(lane == 2, rank1, jnp.where(lane == 3, rank2, 0.0))))
    ri_ref[...] = routes.astype(I32)
    rf_ref[...] = jnp.where(lane == 0, gate1, jnp.where(lane == 1, gate2, 0.0))
    cnt_ref[...] = jnp.broadcast_to(carry[...], cnt_ref.shape)


def _pos_kernel(ri_ref, cnt_ref, post_ref, blk_ref, ends_ref, *, tm):
    shift = EXPERT_BLOCK.bit_length() - 1
    cnt = cnt_ref[...].astype(I32)
    padded = ((cnt + (EXPERT_BLOCK - 1)) >> shift) << shift
    lane8 = lax.broadcasted_iota(I32, padded.shape, 1)
    pend = padded
    s = 1
    while s < LANES:
        pend = pend + jnp.where(lane8 >= s, pltpu.roll(pend, s, 1), 0)
        s *= 2
    pstart = (pend - padded)[0:1, :].astype(F32)

    ri = ri_ref[...]
    lane = lax.broadcasted_iota(I32, ri.shape, 1)
    e1 = ri[:, 0:1] + ROUTE_LANE0
    e2 = ri[:, 1:2] + ROUTE_LANE0
    ps1 = jnp.sum(jnp.where(lane == e1, pstart, 0.0), axis=-1, keepdims=True)
    ps2 = jnp.sum(jnp.where(lane == e2, pstart, 0.0), axis=-1, keepdims=True)
    pos1 = ps1 + ri[:, 2:3].astype(F32)
    pos2 = ps2 + ri[:, 3:4].astype(F32)
    posmat = jnp.where(lane == 0, pos1, jnp.where(lane == 1, pos2, 0.0))
    for j in range(tm // LANES):
        pt = posmat[j * LANES:(j + 1) * LANES, :].T
        post_ref[:, j * LANES:(j + 1) * LANES] = pt[0:8, :].astype(I32)

    pendf = pend.astype(F32)[0:1, :]
    rr = lax.broadcasted_iota(I32, (LANES, LANES), 0)
    cc = lax.broadcasted_iota(I32, (LANES, LANES), 1)
    to_col = lambda row: jnp.sum(jnp.where(rr == cc, row, 0.0), axis=1, keepdims=True)
    pend_col = to_col(pendf)
    pstart_col = to_col(pstart)
    cend_col = pstart_col + to_col(cnt_ref[0:1, :])
    nbl = blk_ref.shape[1]
    er = lax.broadcasted_iota(I32, (LANES, nbl), 0)
    jb = (lax.broadcasted_iota(I32, (LANES, nbl), 1) * EXPERT_BLOCK).astype(F32)
    valid = (er >= ROUTE_LANE0) & (er < ROUTE_LANE0 + N_EXPERTS)
    be = jnp.sum(jnp.where(valid & (pend_col <= jb), 1.0, 0.0), axis=0, keepdims=True)
    be = jnp.minimum(be, float(N_EXPERTS - 1))
    nused = jnp.max(pendf, axis=-1, keepdims=True) * (1.0 / EXPERT_BLOCK)
    inblk = valid & (pstart_col <= jb) & (jb < pend_col)
    nvalid = jnp.sum(jnp.where(inblk, jnp.clip(cend_col - jb, 0.0, float(EXPERT_BLOCK)), 0.0),
                     axis=0, keepdims=True)
    rowi = lax.broadcasted_iota(I32, blk_ref.shape, 0)
    blk = jnp.where(rowi == 0, be, jnp.where(rowi == 1, nused, nvalid)).astype(I32)
    blk_ref[...] = blk
    cend_row = pstart + cnt_ref[0:1, :]
    rowe = lax.broadcasted_iota(I32, ends_ref.shape, 0)
    ends_ref[...] = jnp.where(rowe == 0, cend_row, pendf).astype(I32)


def _slots_kernel(cend_ref, pend_ref, pos1_ref, pos2_ref, slot_ref, *, tm):
    i = pl.program_id(0)

    @pl.when(i == 0)
    def _():
        def fill(p, c):
            slot_ref[p] = 0
            return c

        def per_expert(e, carry):
            return lax.fori_loop(cend_ref[e], pend_ref[e], fill, carry)

        lax.fori_loop(0, N_EXPERTS, per_expert, 0)
        lax.fori_loop(pend_ref[N_EXPERTS - 1], slot_ref.shape[0], fill, 0)

    def body(r, carry):
        slot = 2 * (i * tm + r)
        slot_ref[pos1_ref[0, 0, r]] = slot
        slot_ref[pos2_ref[0, 0, r]] = slot + 1
        return carry

    lax.fori_loop(0, tm, body, 0, unroll=16)


def _expert_kernel(be_ref, nu_ref, nv_ref, slc_ref, sln_ref, xn_ref, wup_ref, wdn_ref, y2_ref,
                   wupb, wdnb, xbuf, ybuf, gsem, ssem, *, n_tokens):
    j = pl.program_id(0)
    last = pl.num_programs(0) - 1
    cur = j % 2
    nxt = 1 - cur
    n_used = nu_ref[0]
    R = TOKEN_TILE_ROWS
    G = 8
    g_shift = G.bit_length() - 1
    buf_rows = EXPERT_BLOCK * R

    def gather_row(slot_ref, b, r, prio):
        tok = slot_ref[0, 0, r] >> 1
        return pltpu.make_async_copy(xn_ref.at[pl.ds(tok * R, R), :],
                                     xbuf.at[pl.ds(b * buf_rows + r * R, R), :], gsem.at[b]).start(priority=prio)

    def scatter_row(b, r, prio):
        slot = slc_ref[0, 0, r]
        dst = (slot & 1) * n_tokens + (slot >> 1)
        return pltpu.make_async_copy(ybuf.at[pl.ds(b * buf_rows + r * R, R), :],
                                     y2_ref.at[pl.ds(dst * R, R), :], ssem.at[b]).start(priority=prio)

    def for_rows(nrows, row_fn):
        def group(g, carry):
            for u in range(G):
                row_fn(g * G + u, u % 2)
            return carry

        def single(r, carry):
            row_fn(r, 0)
            return carry

        full = nrows >> g_shift
        lax.fori_loop(0, full, group, 0)
        lax.fori_loop(full * G, nrows, single, 0)

    def gather(slot_ref, b, nrows):
        for_rows(nrows, lambda r, prio: gather_row(slot_ref, b, r, prio))

    def scatter(b, nrows):
        for_rows(nrows, lambda r, prio: scatter_row(b, r, prio))

    def wait_rows(sem, nrows):
        def group(g, carry):
            pltpu.make_async_copy(xn_ref.at[pl.ds(0, G * R), :], xbuf.at[pl.ds(0, G * R), :], sem).wait()
            return carry

        def single(r, carry):
            pltpu.make_async_copy(xn_ref.at[pl.ds(0, R), :], xbuf.at[pl.ds(0, R), :], sem).wait()
            return carry

        full = nrows >> g_shift
        lax.fori_loop(0, full, group, 0)
        lax.fori_loop(full * G, nrows, single, 0)

    @pl.when(j == 0)
    def _():
        xbuf[...] = jnp.zeros(xbuf.shape, F32)
        gather(slc_ref, 0, nv_ref[0])

    @pl.when(j + 1 < n_used)
    def _():
        gather(sln_ref, nxt, nv_ref[jnp.minimum(j + 1, last)])

    prev = be_ref[jnp.maximum(j - 1, 0)]

    @pl.when((j == 0) | (be_ref[j] != prev))
    def _():
        wupb[...] = wup_ref[...].astype(BF16)
        wdnb[...] = wdn_ref[...].astype(BF16)

    @pl.when(j < n_used)
    def _():
        wait_rows(gsem.at[cur], nv_ref[j])
        base = cur * buf_rows
        x = jnp.concatenate([xbuf[pl.ds(base + f, EXPERT_BLOCK, stride=R), :] for f in range(R)], axis=1)
        ab = _dot(x.astype(BF16), wupb[...])
        hmid = _silu(ab[:, :D_EXPERT]) * ab[:, D_EXPERT:]
        y = _dot(hmid.astype(BF16), wdnb[...])
        for f in range(R):
            ybuf[pl.ds(base + f, EXPERT_BLOCK, stride=R), :] = y[:, f * LANES:(f + 1) * LANES]

    @pl.when((j >= 1) & (j <= n_used))
    def _():
        wait_rows(ssem.at[nxt], nv_ref[jnp.maximum(j - 1, 0)])

    @pl.when(j < n_used)
    def _():
        scatter(cur, nv_ref[j])

    @pl.when((j == last) & (j < n_used))
    def _():
        wait_rows(ssem.at[cur], nv_ref[j])


def _ple_kernel(h1_ref, rf_ref, p_ref, ya_ref, yb_ref, gple_ref, wg_ref, wp_ref, gfin_ref, out_ref):
    tm = h1_ref.shape[0]
    R = TOKEN_TILE_ROWS
    rows = lambda ref: jnp.concatenate([ref[pl.ds(f, tm, stride=R), :] for f in range(R)], axis=1)
    rf = rf_ref[...]
    h2 = h1_ref[...] + (rf[:, 0:1] * rows(ya_ref) + rf[:, 1:2] * rows(yb_ref))
    hn = h2 * lax.rsqrt(jnp.mean(h2 * h2, axis=-1, keepdims=True) + EPS) * gple_ref[...]
    gate = jax.nn.sigmoid(_dot(hn.astype(BF16), wg_ref[...]))
    h3 = h2 + gate * _dot(p_ref[...].astype(BF16), wp_ref[...])
    out_ref[...] = h3 * lax.rsqrt(jnp.mean(h3 * h3, axis=-1, keepdims=True) + EPS) * gfin_ref[...]


def _params(sem):
    return pltpu.CompilerParams(dimension_semantics=sem, vmem_limit_bytes=VMEM_LIMIT)


def _full(shape):
    nd = len(shape)
    return pl.BlockSpec(shape, lambda *_: (0,) * nd)


def _pad_lanes(a, lane0=0):
    out = jnp.zeros((a.shape[0], LANES), a.dtype)
    return out.at[:, lane0:lane0 + a.shape[1]].set(a)


def _layer(x2, p2, B, T, g_mix, w_in, w_conv_qkv, a_log, dt_bias, g_gdn_out, b_glu, w_conv_dw,
           b_conv_dw, ln_conv_g, ln_conv_b, w_out, g_moe, w_group_router, w_expert_router, w_up,
           w_down, g_ple, w_ple_gate, w_ple_proj, g_out):
    N, D = x2.shape
    hw = GDN_HEADS * GDN_DK
    tm = min(512, T)
    nt = N // tm

    o1 = 3 * hw
    o2 = o1 + hw
    o4 = o2 + 2 * GDN_HEADS
    wqkv = w_in[:, :o1].astype(BF16)
    wz = w_in[:, o1:o2].astype(BF16)
    wbah, wbal = _split_bf16(_pad_lanes(w_in[:, o2:o4]))
    wglu = w_in[:, o4:].astype(BF16)
    pa = jnp.concatenate([_pad_lanes(a_log[None, :], GDN_HEADS), _pad_lanes(dt_bias[None, :], GDN_HEADS),
                          jnp.zeros((6, LANES), F32)], axis=0)
    cdw = jnp.concatenate([w_conv_dw, jnp.zeros((1, CONF_CH), F32)], axis=0)
    cvec = jnp.concatenate([b_conv_dw[None], ln_conv_g[None], ln_conv_b[None], jnp.zeros((5, CONF_CH), F32)], 0)

    row_spec = lambda w: pl.BlockSpec((tm, w), lambda i: (i, 0))
    q, k, v, bg, zs, yc = pl.pallas_call(
        functools.partial(_inproj_kernel, tiles_per_seq=T // tm, tm=tm),
        grid=(nt,),
        in_specs=[row_spec(D), _full((1, D)), _full(wqkv.shape), _full(wz.shape), _full(wbah.shape),
                  _full(wbal.shape), _full(wglu.shape), _full(w_conv_qkv.shape), _full(pa.shape),
                  _full((1, 2 * CONF_CH)), _full(cdw.shape), _full(cvec.shape)],
        out_specs=[row_spec(hw), row_spec(hw), row_spec(hw), row_spec(LANES), row_spec(hw), row_spec(CONF_CH)],
        out_shape=[jax.ShapeDtypeStruct((N, hw), F32)] * 3 + [jax.ShapeDtypeStruct((N, LANES), F32),
                   jax.ShapeDtypeStruct((N, hw), BF16), jax.ShapeDtypeStruct((N, CONF_CH), BF16)],
        scratch_shapes=[pltpu.VMEM((tm + QKV_HALO, 3 * hw), F32), pltpu.VMEM((tm + CONF_HALO, CONF_CH), F32)],
        compiler_params=_params(("arbitrary",)),
        name="inproj",
    )(x2, g_mix[None], wqkv, wz, wbah, wbal, wglu, w_conv_qkv, pa, b_glu[None], cdw, cvec)

    cb = 4
    rows = cb * GDN_CHUNK
    seq_spec = lambda w: pl.BlockSpec((B, rows, w), lambda n: (0, n, 0))
    r3 = lambda a: a.reshape(B, T, a.shape[-1])
    ygdn = pl.pallas_call(
        functools.partial(_gdn_kernel, cb=cb, nbatch=B, group=2 * GDN_HEADS),
        grid=(T // rows,),
        in_specs=[seq_spec(hw), seq_spec(hw), seq_spec(hw), seq_spec(LANES), seq_spec(hw),
                  pl.BlockSpec((1, GDN_DK), lambda n: (0, 0))],
        out_specs=seq_spec(hw),
        out_shape=jax.ShapeDtypeStruct((B, T, hw), BF16),
        scratch_shapes=[pltpu.VMEM((B * GDN_HEADS, GDN_DK, GDN_DK), F32)],
        compiler_params=_params(("arbitrary",)),
        name="gdn",
    )(r3(q), r3(k), r3(v), r3(bg), r3(zs), g_gdn_out[None]).reshape(N, hw)

    wo1 = w_out[:hw].astype(BF16)
    wo2 = w_out[hw:].astype(BF16)
    wrh, wrl = _split_bf16(_pad_lanes(jnp.concatenate([w_group_router, w_expert_router], axis=1)))
    ltri = jnp.tril(jnp.ones((tm, tm), F32), -1).astype(BF16)
    assert D == TOKEN_TILE_ROWS * LANES
    tile_spec = pl.BlockSpec((tm * TOKEN_TILE_ROWS, LANES), lambda i: (i, 0))
    h1, xn, ri, rf, cnt = pl.pallas_call(
        _router_kernel,
        grid=(nt,),
        in_specs=[row_spec(D), row_spec(hw), row_spec(CONF_CH), _full(wo1.shape), _full(wo2.shape),
                  _full((1, D)), _full(wrh.shape), _full(wrl.shape), _full(ltri.shape)],
        out_specs=[row_spec(D), tile_spec, row_spec(LANES), row_spec(LANES), _full((8, LANES))],
        out_shape=[jax.ShapeDtypeStruct((N, D), F32), jax.ShapeDtypeStruct((N * TOKEN_TILE_ROWS, LANES), F32),
                   jax.ShapeDtypeStruct((N, LANES), I32), jax.ShapeDtypeStruct((N, LANES), F32),
                   jax.ShapeDtypeStruct((8, LANES), F32)],
        scratch_shapes=[pltpu.VMEM((1, LANES), F32)],
        compiler_params=_params(("arbitrary",)),
        name="router",
    )(x2, ygdn, yc, wo1, wo2, g_moe[None], wrh, wrl, ltri)

    nb = (2 * N) // EXPERT_BLOCK + N_EXPERTS
    nbl = -(-nb // LANES) * LANES
    post, blk, ends = pl.pallas_call(
        functools.partial(_pos_kernel, tm=tm),
        grid=(nt,),
        in_specs=[row_spec(LANES), _full((8, LANES))],
        out_specs=[pl.BlockSpec((8, tm), lambda i: (0, i)), _full((8, nbl)), _full((8, LANES))],
        out_shape=[jax.ShapeDtypeStruct((8, N), I32), jax.ShapeDtypeStruct((8, nbl), I32),
                   jax.ShapeDtypeStruct((8, LANES), I32)],
        compiler_params=_params(("arbitrary",)),
        name="positions",
    )(ri, cnt)
    pos1 = post[0].reshape(nt, 1, tm)
    pos2 = post[1].reshape(nt, 1, tm)
    block_expert = blk[0, :nb]
    n_used = blk[1, :1]
    n_valid = blk[2, :nb]

    P = nb * EXPERT_BLOCK
    real_end = ends[0, ROUTE_LANE0:ROUTE_LANE0 + N_EXPERTS]
    padded_end = ends[1, ROUTE_LANE0:ROUTE_LANE0 + N_EXPERTS]
    smem_spec = pl.BlockSpec((1, 1, tm), lambda i, *_: (i, 0, 0), memory_space=pltpu.SMEM)
    flat_smem = pl.BlockSpec((P,), lambda i, *_: (0,), memory_space=pltpu.SMEM)
    slots = pl.pallas_call(
        functools.partial(_slots_kernel, tm=tm),
        grid_spec=pltpu.PrefetchScalarGridSpec(
            num_scalar_prefetch=2, grid=(nt,), in_specs=[smem_spec, smem_spec], out_specs=flat_smem),
        out_shape=jax.ShapeDtypeStruct((P,), I32),
        compiler_params=_params(("arbitrary",)),
        name="slots",
    )(real_end, padded_end, pos1, pos2).reshape(nb, 1, EXPERT_BLOCK)

    any_spec = pl.BlockSpec(memory_space=pl.ANY)
    blk_smem = lambda off: pl.BlockSpec((1, 1, EXPERT_BLOCK),
                                        lambda j, be, nu, nv: (jnp.minimum(j + off, nb - 1), 0, 0),
                                        memory_space=pltpu.SMEM)
    buf = pltpu.VMEM((2 * EXPERT_BLOCK * TOKEN_TILE_ROWS, LANES), F32)
    y2 = pl.pallas_call(
        functools.partial(_expert_kernel, n_tokens=N),
        grid_spec=pltpu.PrefetchScalarGridSpec(
            num_scalar_prefetch=3,
            grid=(nb,),
            in_specs=[blk_smem(0), blk_smem(1), any_spec,
                      pl.BlockSpec((None, D, 2 * D_EXPERT), lambda j, be, nu, nv: (be[j], 0, 0)),
                      pl.BlockSpec((None, D_EXPERT, D), lambda j, be, nu, nv: (be[j], 0, 0))],
            out_specs=any_spec,
            scratch_shapes=[pltpu.VMEM((D, 2 * D_EXPERT), BF16), pltpu.VMEM((D_EXPERT, D), BF16), buf, buf,
                            pltpu.SemaphoreType.DMA((2,)), pltpu.SemaphoreType.DMA((2,))]),
        out_shape=jax.ShapeDtypeStruct((2 * N * TOKEN_TILE_ROWS, LANES), F32),
        compiler_params=_params(("arbitrary",)),
        name="experts",
    )(block_expert, n_used, n_valid, slots, slots, xn, w_up, w_down)

    wg = w_ple_gate.astype(BF16)
    wp = w_ple_proj.astype(BF16)
    out = pl.pallas_call(
        _ple_kernel,
        grid=(nt,),
        in_specs=[row_spec(D), row_spec(LANES), row_spec(p2.shape[1]), tile_spec,
                  pl.BlockSpec((tm * TOKEN_TILE_ROWS, LANES), lambda i: (nt + i, 0)),
                  _full((1, D)), _full(wg.shape), _full(wp.shape), _full((1, D))],
        out_specs=row_spec(D),
        out_shape=jax.ShapeDtypeStruct((N, D), F32),
        compiler_params=_params(("arbitrary",)),
        name="ple",
    )(h1, rf, p2, y2, y2, g_ple[None], wg, wp, g_out[None])
    return out


def kernel(x, p, g_mix, w_in, w_conv_qkv, a_log, dt_bias, g_gdn_out, b_glu, w_conv_dw, b_conv_dw, ln_conv_g, ln_conv_b, w_out, g_moe, w_group_router, w_expert_router, w_up, w_down, g_ple, w_ple_gate, w_ple_proj, g_final):
    B, T, D = x.shape
    depth = p.shape[0]
    assert depth == 1, "the final RMSNorm is fused into the (single) layer's last kernel"
    h = _layer(x.reshape(B * T, D), p[0].reshape(B * T, -1), B, T, g_mix[0], w_in[0], w_conv_qkv[0],
               a_log[0], dt_bias[0], g_gdn_out[0], b_glu[0], w_conv_dw[0], b_conv_dw[0], ln_conv_g[0],
               ln_conv_b[0], w_out[0], g_moe[0], w_group_router[0], w_expert_router[0], w_up[0],
               w_down[0], g_ple[0], w_ple_gate[0], w_ple_proj[0], g_final)
    return h.reshape(B, T, D)
```

```python
import functools

import jax
import jax.numpy as jnp
from jax import lax
from jax.experimental import pallas as pl
from jax.experimental.pallas import tpu as pltpu

F32 = jnp.float32
BF16 = jnp.bfloat16
I32 = jnp.int32

EPS = 1e-6
GDN_HEADS = 4
GDN_DK = 128
GDN_CONV = 4
GDN_CHUNK = 64
CONF_CH = 512
CONF_WIDTH = 31
N_GROUPS = 8
EXPERTS_PER_GROUP = 8
N_EXPERTS = 64
D_EXPERT = 256
LANES = 128
SUBLANES = 8
TOKEN_TILE_ROWS = 8
ROUTE_LANE0 = N_GROUPS
EXPERT_BLOCK = 256
QKV_HALO = 8
CONF_HALO = 32
NEG = -1e30
VMEM_LIMIT = 56 * 1024 * 1024


def _dot(a, b):
    return jnp.dot(a, b, preferred_element_type=F32)


def _dot_nt(a, b):
    return lax.dot_general(a, b, (((1,), (1,)), ((), ())), preferred_element_type=F32)


def _dot_tn(a, b):
    return lax.dot_general(a, b, (((0,), (0,)), ((), ())), preferred_element_type=F32)


def _silu(x):
    return x * jax.nn.sigmoid(x)


def _split_bf16(x):
    hi = x.astype(BF16)
    lo = (x - hi.astype(F32)).astype(BF16)
    return hi, lo


def _causal_taps(ext_ref, w_ref, n_taps, halo, tm, cols):
    rows = ext_ref.shape[0]
    ext = ext_ref[:, cols]
    first = halo - (n_taps - 1)
    acc = None
    for res in range(SUBLANES):
        taps = [k for k in range(n_taps) if (first + k) % SUBLANES == res]
        if not taps:
            continue
        shifted = ext if res == 0 else pltpu.roll(ext, rows - res, 0)
        for k in taps:
            a = first + k - res
            term = w_ref[k:k + 1, cols] * shifted[a:a + tm, :]
            acc = term if acc is None else acc + term
    return acc


def _inproj_kernel(x_ref, gmix_ref, wqkv_ref, wz_ref, wbah_ref, wbal_ref, wglu_ref, cw_ref, pa_ref,
                   bglu_ref, cdw_ref, cvec_ref,
                   q_ref, k_ref, v_ref, bg_ref, zs_ref, yc_ref,
                   extq, extu, *, tiles_per_seq, tm):
    i = pl.program_id(0)
    first = (i % tiles_per_seq) == 0

    x = x_ref[...]
    n1 = x * lax.rsqrt(jnp.mean(x * x, axis=-1, keepdims=True) + EPS) * gmix_ref[...]
    n1h, n1l = _split_bf16(n1)

    @pl.when(first)
    def _():
        extq[0:QKV_HALO, :] = jnp.zeros((QKV_HALO, extq.shape[1]), F32)
        extu[0:CONF_HALO, :] = jnp.zeros((CONF_HALO, extu.shape[1]), F32)

    @pl.when(jnp.logical_not(first))
    def _():
        extq[0:QKV_HALO, :] = extq[tm:tm + QKV_HALO, :]
        extu[0:CONF_HALO, :] = extu[tm:tm + CONF_HALO, :]

    extq[QKV_HALO:QKV_HALO + tm, :] = _dot(n1h, wqkv_ref[...])
    hw = GDN_HEADS * GDN_DK
    qc = _silu(_causal_taps(extq, cw_ref, GDN_CONV, QKV_HALO, tm, slice(0, hw)))
    kc = _silu(_causal_taps(extq, cw_ref, GDN_CONV, QKV_HALO, tm, slice(hw, 2 * hw)))
    for h in range(GDN_HEADS):
        cs = slice(h * GDN_DK, (h + 1) * GDN_DK)
        qh = qc[:, cs]
        q_ref[:, cs] = qh * lax.rsqrt(jnp.sum(qh * qh, axis=-1, keepdims=True) + EPS) * (GDN_DK ** -0.5)
        kh = kc[:, cs]
        k_ref[:, cs] = kh * lax.rsqrt(jnp.sum(kh * kh, axis=-1, keepdims=True) + EPS)
    v_ref[...] = _silu(_causal_taps(extq, cw_ref, GDN_CONV, QKV_HALO, tm, slice(2 * hw, 3 * hw)))

    zs_ref[...] = _silu(_dot(n1h, wz_ref[...])).astype(BF16)

    ba = _dot(n1h, wbah_ref[...]) + _dot(n1l, wbah_ref[...]) + _dot(n1h, wbal_ref[...])
    lane = lax.broadcasted_iota(I32, ba.shape, 1)
    row = lax.broadcasted_iota(I32, ba.shape, 0)
    beta = jax.nn.sigmoid(ba)
    sp_in = ba + pa_ref[1:2, :]
    softplus = jnp.maximum(sp_in, 0.0) + jnp.log1p(jnp.exp(-jnp.abs(sp_in)))
    g = -jnp.exp(pa_ref[0:1, :]) * softplus
    rin = row & (GDN_CHUNK - 1)
    s = 1
    while s < GDN_CHUNK:
        g = g + jnp.where(rin >= s, pltpu.roll(g, s, 0), 0.0)
        s *= 2
    bg_ref[...] = jnp.where(lane < GDN_HEADS, beta, g)

    glu = _dot(n1h, wglu_ref[...]) + bglu_ref[...]
    extu[CONF_HALO:CONF_HALO + tm, :] = glu[:, :CONF_CH] * jax.nn.sigmoid(glu[:, CONF_CH:])
    c = _causal_taps(extu, cdw_ref, CONF_WIDTH, CONF_HALO, tm, slice(0, CONF_CH)) + cvec_ref[0:1, :]
    mu = jnp.mean(c, axis=-1, keepdims=True)
    cc = c - mu
    var = jnp.mean(cc * cc, axis=-1, keepdims=True)
    yln = cc * lax.rsqrt(var + EPS) * cvec_ref[1:2, :] + cvec_ref[2:3, :]
    yc_ref[...] = _silu(yln).astype(BF16)


def _gdn_kernel(q_ref, k_ref, v_ref, bg_ref, zs_ref, gout_ref, y_ref, s_ref, *, cb, nbatch, group):
    n = pl.program_id(0)

    @pl.when(n == 0)
    def _():
        s_ref[...] = jnp.zeros(s_ref.shape, F32)

    C = GDN_CHUNK
    r = lax.broadcasted_iota(I32, (C, C), 0)
    c = lax.broadcasted_iota(I32, (C, C), 1)
    causal = r >= c
    strict = r > c
    eye = r == c
    eyef = jnp.where(eye, 1.0, 0.0).astype(F32)
    gout = gout_ref[...]

    def front(b, ci, h):
        rows = slice(ci * C, (ci + 1) * C)
        cols = slice(h * GDN_DK, (h + 1) * GDN_DK)
        q = q_ref[b, rows, cols]
        k = k_ref[b, rows, cols]
        v = v_ref[b, rows, cols]
        beta = bg_ref[b, rows, h:h + 1]
        gc = bg_ref[b, rows, GDN_HEADS + h:GDN_HEADS + h + 1]
        gc_row = jnp.sum(jnp.where(eye, gc, 0.0), axis=0, keepdims=True)
        gc_last = gc[C - 1:C, :]
        eg = jnp.exp(gc)
        kb = k * beta
        return dict(
            decay=jnp.where(causal, jnp.exp(jnp.minimum(gc - gc_row, 0.0)), 0.0),
            lhs=jnp.concatenate([kb, q], axis=0).astype(BF16),
            kbf=k.astype(BF16),
            rhs=jnp.concatenate([v * beta, kb * eg], axis=1).astype(BF16),
            qd=q * eg,
            kd=(k * jnp.exp(gc_last - gc)).astype(BF16),
            gl=jnp.exp(gc_last))

    insts = [(b, ci, h) for ci in range(cb) for b in range(nbatch) for h in range(GDN_HEADS)]
    done = {}
    for g0 in range(0, len(insts), group):
        grp = insts[g0:g0 + group]
        st = [front(*key) for key in grp]
        kq = [_dot_nt(s["lhs"], s["kbf"]) for s in st]
        xp = [-jnp.where(strict, m[:C] * s["decay"], 0.0) for m, s in zip(kq, st)]
        qk = [(m[C:] * s["decay"]).astype(BF16) for m, s in zip(kq, st)]
        tinv = [eyef + x for x in xp]
        for _ in range(5):
            xpb = [x.astype(BF16) for x in xp]
            xp = [_dot(x, x) for x in xpb]
            tinv = [t + _dot(t.astype(BF16), x.astype(BF16)) for t, x in zip(tinv, xp)]
        uw = [_dot(t.astype(BF16), s["rhs"]) for t, s in zip(tinv, st)]
        for key, s, m, qkm in zip(grp, st, uw, qk):
            done[key] = dict(u=m[:, :GDN_DK],
                             wq=jnp.concatenate([m[:, GDN_DK:], s["qd"]], axis=0).astype(BF16),
                             qk=qkm, kd=s["kd"], gl=s["gl"])

    chains = [(b, h) for b in range(nbatch) for h in range(GDN_HEADS)]
    for ci in range(cb):
        rows = slice(ci * C, (ci + 1) * C)
        dd = [done[(b, ci, h)] for b, h in chains]
        s_old = [s_ref[b * GDN_HEADS + h] for b, h in chains]
        wq = [_dot(d["wq"], s.astype(BF16)) for d, s in zip(dd, s_old)]
        v_new = [(d["u"] - m[:C]).astype(BF16) for d, m in zip(dd, wq)]
        o = [m[C:] + _dot(d["qk"], vn) for d, m, vn in zip(dd, wq, v_new)]
        s_new = [s * d["gl"] + _dot_tn(d["kd"], vn) for d, s, vn in zip(dd, s_old, v_new)]
        for (b, h), sn, oo in zip(chains, s_new, o):
            cols = slice(h * GDN_DK, (h + 1) * GDN_DK)
            s_ref[b * GDN_HEADS + h] = sn
            on = oo * lax.rsqrt(jnp.mean(oo * oo, axis=-1, keepdims=True) + EPS) * gout
            y_ref[b, rows, cols] = (on * zs_ref[b, rows, cols].astype(F32)).astype(BF16)


def _router_kernel(x_ref, yg_ref, yc_ref, wo1_ref, wo2_ref, gmoe_ref, wrh_ref, wrl_ref, ltri_ref,
                   h1_ref, xn_ref, ri_ref, rf_ref, cnt_ref, carry):
    i = pl.program_id(0)

    @pl.when(i == 0)
    def _():
        carry[...] = jnp.zeros(carry.shape, F32)

    h1 = x_ref[...] + _dot(yg_ref[...], wo1_ref[...]) + _dot(yc_ref[...], wo2_ref[...])
    h1_ref[...] = h1
    xn = h1 * lax.rsqrt(jnp.mean(h1 * h1, axis=-1, keepdims=True) + EPS) * gmoe_ref[...]
    xnh, xnl = _split_bf16(xn)
    for f in range(TOKEN_TILE_ROWS):
        xn_ref[pl.ds(f, xn.shape[0], stride=TOKEN_TILE_ROWS), :] = xn[:, f * LANES:(f + 1) * LANES]
    lg = _dot(xnh, wrh_ref[...]) + _dot(xnl, wrh_ref[...]) + _dot(xnh, wrl_ref[...])

    lane = lax.broadcasted_iota(I32, lg.shape, 1)
    lanef = lane.astype(F32)
    gl = jnp.where(lane < N_GROUPS, lg, NEG)
    gmax = jnp.max(gl, axis=-1, keepdims=True)
    g_w = 1.0 / jnp.sum(jnp.exp(gl - gmax), axis=-1, keepdims=True)
    gidx = jnp.min(jnp.where(gl == gmax, lanef, float(LANES)), axis=-1, keepdims=True)
    lane_group = ((lane - ROUTE_LANE0) >> 3).astype(F32)
    in_group = (lane >= ROUTE_LANE0) & (lane < ROUTE_LANE0 + N_EXPERTS) & (lane_group == gidx)
    el = jnp.where(in_group, lg, NEG)
    m1 = jnp.max(el, axis=-1, keepdims=True)
    i1 = jnp.min(jnp.where(el == m1, lanef, float(LANES)), axis=-1, keepdims=True)
    el2 = jnp.where(lanef == i1, NEG, el)
    m2 = jnp.max(el2, axis=-1, keepdims=True)
    i2 = jnp.min(jnp.where(el2 == m2, lanef, float(LANES)), axis=-1, keepdims=True)
    ratio = jnp.exp(m2 - m1)
    gate1 = g_w / (1.0 + ratio)
    gate2 = g_w * ratio / (1.0 + ratio)

    oh1 = lanef == i1
    oh2 = lanef == i2
    oh = jnp.where(oh1 | oh2, 1.0, 0.0).astype(F32)
    cum = _dot(ltri_ref[...], oh.astype(BF16)) + carry[...]
    rank1 = jnp.sum(jnp.where(oh1, cum, 0.0), axis=-1, keepdims=True)
    rank2 = jnp.sum(jnp.where(oh2, cum, 0.0), axis=-1, keepdims=True)
    carry[...] = carry[...] + jnp.sum(oh, axis=0, keepdims=True)

    routes = jnp.where(lane == 0, i1 - ROUTE_LANE0,
                       jnp.where(lane == 1, i2 - ROUTE_LANE0,
                                 jnp.where(lane == 2, rank1, jnp.where(lane == 3, rank2, 0.0))))
    ri_ref[...] = routes.astype(I32)
    rf_ref[...] = jnp.where(lane == 0, gate1, jnp.where(lane == 1, gate2, 0.0))
    cnt_ref[...] = jnp.broadcast_to(carry[...], cnt_ref.shape)


def _pos_kernel(ri_ref, cnt_ref, post_ref, blk_ref, ends_ref, *, tm):
    shift = EXPERT_BLOCK.bit_length() - 1
    cnt = cnt_ref[...].astype(I32)
    padded = ((cnt + (EXPERT_BLOCK - 1)) >> shift) << shift
    lane8 = lax.broadcasted_iota(I32, padded.shape, 1)
    pend = padded
    s = 1
    while s < LANES:
        pend = pend + jnp.where(lane8 >= s, pltpu.roll(pend, s, 1), 0)
        s *= 2
    pstart = (pend - padded)[0:1, :].astype(F32)

    ri = ri_ref[...]
    lane = lax.broadcasted_iota(I32, ri.shape, 1)
    e1 = ri[:, 0:1] + ROUTE_LANE0
    e2 = ri[:, 1:2] + ROUTE_LANE0
    ps1 = jnp.sum(jnp.where(lane == e1, pstart, 0.0), axis=-1, keepdims=True)
    ps2 = jnp.sum(jnp.where(lane == e2, pstart, 0.0), axis=-1, keepdims=True)
    pos1 = ps1 + ri[:, 2:3].astype(F32)
    pos2 = ps2 + ri[:, 3:4].astype(F32)
    posmat = jnp.where(lane == 0, pos1, jnp.where(lane == 1, pos2, 0.0))
    for j in range(tm // LANES):
        pt = posmat[j * LANES:(j + 1) * LANES, :].T
        post_ref[:, j * LANES:(j + 1) * LANES] = pt[0:8, :].astype(I32)

    pendf = pend.astype(F32)[0:1, :]
    rr = lax.broadcasted_iota(I32, (LANES, LANES), 0)
    cc = lax.broadcasted_iota(I32, (LANES, LANES), 1)
    pend_col = jnp.sum(jnp.where(rr == cc, pendf, 0.0), axis=1, keepdims=True)
    nbl = blk_ref.shape[1]
    er = lax.broadcasted_iota(I32, (LANES, nbl), 0)
    jb = (lax.broadcasted_iota(I32, (LANES, nbl), 1) * EXPERT_BLOCK).astype(F32)
    valid = (er >= ROUTE_LANE0) & (er < ROUTE_LANE0 + N_EXPERTS)
    be = jnp.sum(jnp.where(valid & (pend_col <= jb), 1.0, 0.0), axis=0, keepdims=True)
    be = jnp.minimum(be, float(N_EXPERTS - 1))
    nused = jnp.max(pendf, axis=-1, keepdims=True) * (1.0 / EXPERT_BLOCK)
    rowi = lax.broadcasted_iota(I32, blk_ref.shape, 0)
    blk_ref[...] = jnp.where(rowi == 0, be, nused).astype(I32)
    cend_row = pstart + cnt_ref[0:1, :]
    rowe = lax.broadcasted_iota(I32, ends_ref.shape, 0)
    ends_ref[...] = jnp.where(rowe == 0, cend_row, pendf).astype(I32)


def _slots_kernel(cend_ref, pend_ref, pos1_ref, pos2_ref, slot_ref, *, tm, n_tokens):
    i = pl.program_id(0)
    dump0 = 2 * n_tokens
    dump_mask = 2 * EXPERT_BLOCK - 1
    F = 8

    def fill_range(lo, hi):
        def body(c, carry):
            for u in range(F):
                p = jnp.minimum(lo + c * F + u, hi - 1)
                slot_ref[p] = dump0 + (p & dump_mask)
            return carry

        lax.fori_loop(0, (hi - lo + (F - 1)) // F, body, 0)

    @pl.when(i == 0)
    def _():
        def per_expert(e, carry):
            fill_range(cend_ref[e], pend_ref[e])
            return carry

        lax.fori_loop(0, N_EXPERTS, per_expert, 0)
        fill_range(pend_ref[N_EXPERTS - 1], slot_ref.shape[0])

    def body(r, carry):
        t = i * tm + r
        slot_ref[pos1_ref[0, 0, r]] = t
        slot_ref[pos2_ref[0, 0, r]] = n_tokens + t
        return carry

    lax.fori_loop(0, tm, body, 0, unroll=16)


def _expert_kernel(be_ref, nu_ref, slp_ref, slc_ref, sln_ref, xn_ref, wup_ref, wdn_ref, y2_ref,
                   wupb, wdnb, xbuf, ybuf, gsem, ssem, *, n_tokens):
    j = pl.program_id(0)
    cur = j % 2
    nxt = 1 - cur
    n_used = nu_ref[0]
    R = TOKEN_TILE_ROWS
    buf_rows = EXPERT_BLOCK * R
    tok_mask = n_tokens - 1

    def gather_row(slot_ref, b, r, prio):
        tok = slot_ref[0, 0, r] & tok_mask
        pltpu.make_async_copy(xn_ref.at[pl.ds(tok * R, R), :], xbuf.at[pl.ds(b * buf_rows + r * R, R), :],
                              gsem.at[b]).start(priority=prio)

    def scatter_row(slot_ref, b, r, prio):
        pltpu.make_async_copy(ybuf.at[pl.ds(b * buf_rows + r * R, R), :],
                              y2_ref.at[pl.ds(slot_ref[0, 0, r] * R, R), :], ssem.at[b]).start(priority=prio)

    def wait_buffer(sem):
        pltpu.make_async_copy(xn_ref.at[pl.ds(0, buf_rows), :], xbuf.at[pl.ds(0, buf_rows), :], sem).wait()

    def rows_loop(fn):
        def body(g, carry):
            for u in range(8):
                fn(g * 8 + u, u % 2)
            return carry

        lax.fori_loop(0, EXPERT_BLOCK // 8, body, 0)

    def dump_init(b, sync):
        cp = pltpu.make_async_copy(ybuf.at[pl.ds(b * buf_rows, buf_rows), :],
                                   y2_ref.at[pl.ds((2 * n_tokens + b * EXPERT_BLOCK) * R, buf_rows), :],
                                   ssem.at[b])
        cp.start()
        if sync:
            cp.wait()

    @pl.when(j == 0)
    def _():
        ybuf[...] = jnp.zeros(ybuf.shape, F32)
        dump_init(0, True)
        dump_init(1, False)
        rows_loop(lambda r, prio: gather_row(slc_ref, 0, r, prio))

    prev = be_ref[jnp.maximum(j - 1, 0)]

    @pl.when((j == 0) | (be_ref[j] != prev))
    def _():
        wupb[...] = wup_ref[...].astype(BF16)
        wdnb[...] = wdn_ref[...].astype(BF16)

    def block(first):
        wait_buffer(gsem.at[cur])
        base = cur * buf_rows
        x = jnp.concatenate([xbuf[pl.ds(base + f, EXPERT_BLOCK, stride=R), :] for f in range(R)], axis=1)
        ab = _dot(x.astype(BF16), wupb[...])
        if not first:
            for r in range(EXPERT_BLOCK):
                scatter_row(slp_ref, nxt, r, r % 2)
        for r in range(EXPERT_BLOCK):
            gather_row(sln_ref, nxt, r, r % 2)
        hmid = _silu(ab[:, :D_EXPERT]) * ab[:, D_EXPERT:]
        y = _dot(hmid.astype(BF16), wdnb[...])
        if not first:
            wait_buffer(ssem.at[cur])
        for f in range(R):
            ybuf[pl.ds(base + f, EXPERT_BLOCK, stride=R), :] = y[:, f * LANES:(f + 1) * LANES]

    @pl.when(j == 0)
    def _():
        block(True)

    @pl.when((j > 0) & (j < n_used))
    def _():
        block(False)

    @pl.when(j == n_used)
    def _():
        wait_buffer(gsem.at[cur])
        rows_loop(lambda r, prio: scatter_row(slp_ref, nxt, r, prio))
        wait_buffer(ssem.at[cur])
        wait_buffer(ssem.at[nxt])


def _ple_kernel(h1_ref, rf_ref, p_ref, ya_ref, yb_ref, gple_ref, wg_ref, wp_ref, gfin_ref, out_ref):
    tm = h1_ref.shape[0]
    R = TOKEN_TILE_ROWS
    rows = lambda ref: jnp.concatenate([ref[pl.ds(f, tm, stride=R), :] for f in range(R)], axis=1)
    rf = rf_ref[...]
    h2 = h1_ref[...] + (rf[:, 0:1] * rows(ya_ref) + rf[:, 1:2] * rows(yb_ref))
    hn = h2 * lax.rsqrt(jnp.mean(h2 * h2, axis=-1, keepdims=True) + EPS) * gple_ref[...]
    gate = jax.nn.sigmoid(_dot(hn.astype(BF16), wg_ref[...]))
    h3 = h2 + gate * _dot(p_ref[...].astype(BF16), wp_ref[...])
    out_ref[...] = h3 * lax.rsqrt(jnp.mean(h3 * h3, axis=-1, keepdims=True) + EPS) * gfin_ref[...]


def _params(sem):
    return pltpu.CompilerParams(dimension_semantics=sem, vmem_limit_bytes=VMEM_LIMIT)


def _full(shape):
    nd = len(shape)
    return pl.BlockSpec(shape, lambda *_: (0,) * nd)


def _pad_lanes(a, lane0=0):
    out = jnp.zeros((a.shape[0], LANES), a.dtype)
    return out.at[:, lane0:lane0 + a.shape[1]].set(a)


def _layer(x2, p2, B, T, g_mix, w_in, w_conv_qkv, a_log, dt_bias, g_gdn_out, b_glu, w_conv_dw,
           b_conv_dw, ln_conv_g, ln_conv_b, w_out, g_moe, w_group_router, w_expert_router, w_up,
           w_down, g_ple, w_ple_gate, w_ple_proj, g_out):
    N, D = x2.shape
    hw = GDN_HEADS * GDN_DK
    tm = min(512, T)
    nt = N // tm

    o1 = 3 * hw
    o2 = o1 + hw
    o4 = o2 + 2 * GDN_HEADS
    wqkv = w_in[:, :o1].astype(BF16)
    wz = w_in[:, o1:o2].astype(BF16)
    wbah, wbal = _split_bf16(_pad_lanes(w_in[:, o2:o4]))
    wglu = w_in[:, o4:].astype(BF16)
    pa = jnp.concatenate([_pad_lanes(a_log[None, :], GDN_HEADS), _pad_lanes(dt_bias[None, :], GDN_HEADS),
                          jnp.zeros((6, LANES), F32)], axis=0)
    cdw = jnp.concatenate([w_conv_dw, jnp.zeros((1, CONF_CH), F32)], axis=0)
    cvec = jnp.concatenate([b_conv_dw[None], ln_conv_g[None], ln_conv_b[None], jnp.zeros((5, CONF_CH), F32)], 0)

    row_spec = lambda w: pl.BlockSpec((tm, w), lambda i: (i, 0))
    q, k, v, bg, zs, yc = pl.pallas_call(
        functools.partial(_inproj_kernel, tiles_per_seq=T // tm, tm=tm),
        grid=(nt,),
        in_specs=[row_spec(D), _full((1, D)), _full(wqkv.shape), _full(wz.shape), _full(wbah.shape),
                  _full(wbal.shape), _full(wglu.shape), _full(w_conv_qkv.shape), _full(pa.shape),
                  _full((1, 2 * CONF_CH)), _full(cdw.shape), _full(cvec.shape)],
        out_specs=[row_spec(hw), row_spec(hw), row_spec(hw), row_spec(LANES), row_spec(hw), row_spec(CONF_CH)],
        out_shape=[jax.ShapeDtypeStruct((N, hw), F32)] * 3 + [jax.ShapeDtypeStruct((N, LANES), F32),
                   jax.ShapeDtypeStruct((N, hw), BF16), jax.ShapeDtypeStruct((N, CONF_CH), BF16)],
        scratch_shapes=[pltpu.VMEM((tm + QKV_HALO, 3 * hw), F32), pltpu.VMEM((tm + CONF_HALO, CONF_CH), F32)],
        compiler_params=_params(("arbitrary",)),
        name="inproj",
    )(x2, g_mix[None], wqkv, wz, wbah, wbal, wglu, w_conv_qkv, pa, b_glu[None], cdw, cvec)

    cb = 4
    rows = cb * GDN_CHUNK
    seq_spec = lambda w: pl.BlockSpec((B, rows, w), lambda n: (0, n, 0))
    r3 = lambda a: a.reshape(B, T, a.shape[-1])
    ygdn = pl.pallas_call(
        functools.partial(_gdn_kernel, cb=cb, nbatch=B, group=2 * GDN_HEADS),
        grid=(T // rows,),
        in_specs=[seq_spec(hw), seq_spec(hw), seq_spec(hw), seq_spec(LANES), seq_spec(hw),
                  pl.BlockSpec((1, GDN_DK), lambda n: (0, 0))],
        out_specs=seq_spec(hw),
        out_shape=jax.ShapeDtypeStruct((B, T, hw), BF16),
        scratch_shapes=[pltpu.VMEM((B * GDN_HEADS, GDN_DK, GDN_DK), F32)],
        compiler_params=_params(("arbitrary",)),
        name="gdn",
    )(r3(q), r3(k), r3(v), r3(bg), r3(zs), g_gdn_out[None]).reshape(N, hw)

    wo1 = w_out[:hw].astype(BF16)
    wo2 = w_out[hw:].astype(BF16)
    wrh, wrl = _split_bf16(_pad_lanes(jnp.concatenate([w_group_router, w_expert_router], axis=1)))
    ltri = jnp.tril(jnp.ones((tm, tm), F32), -1).astype(BF16)
    assert D == TOKEN_TILE_ROWS * LANES
    tile_spec = pl.BlockSpec((tm * TOKEN_TILE_ROWS, LANES), lambda i: (i, 0))
    h1, xn, ri, rf, cnt = pl.pallas_call(
        _router_kernel,
        grid=(nt,),
        in_specs=[row_spec(D), row_spec(hw), row_spec(CONF_CH), _full(wo1.shape), _full(wo2.shape),
                  _full((1, D)), _full(wrh.shape), _full(wrl.shape), _full(ltri.shape)],
        out_specs=[row_spec(D), tile_spec, row_spec(LANES), row_spec(LANES), _full((8, LANES))],
        out_shape=[jax.ShapeDtypeStruct((N, D), F32), jax.ShapeDtypeStruct((N * TOKEN_TILE_ROWS, LANES), F32),
                   jax.ShapeDtypeStruct((N, LANES), I32), jax.ShapeDtypeStruct((N, LANES), F32),
                   jax.ShapeDtypeStruct((8, LANES), F32)],
        scratch_shapes=[pltpu.VMEM((1, LANES), F32)],
        compiler_params=_params(("arbitrary",)),
        name="router",
    )(x2, ygdn, yc, wo1, wo2, g_moe[None], wrh, wrl, ltri)

    nb = (2 * N) // EXPERT_BLOCK + N_EXPERTS
    nbl = -(-nb // LANES) * LANES
    post, blk, ends = pl.pallas_call(
        functools.partial(_pos_kernel, tm=tm),
        grid=(nt,),
        in_specs=[row_spec(LANES), _full((8, LANES))],
        out_specs=[pl.BlockSpec((8, tm), lambda i: (0, i)), _full((8, nbl)), _full((8, LANES))],
        out_shape=[jax.ShapeDtypeStruct((8, N), I32), jax.ShapeDtypeStruct((8, nbl), I32),
                   jax.ShapeDtypeStruct((8, LANES), I32)],
        compiler_params=_params(("arbitrary",)),
        name="positions",
    )(ri, cnt)
    pos1 = post[0].reshape(nt, 1, tm)
    pos2 = post[1].reshape(nt, 1, tm)
    block_expert = blk[0, :nb]
    n_used = blk[1, :1]

    P = nb * EXPERT_BLOCK
    real_end = ends[0, ROUTE_LANE0:ROUTE_LANE0 + N_EXPERTS]
    padded_end = ends[1, ROUTE_LANE0:ROUTE_LANE0 + N_EXPERTS]
    smem_spec = pl.BlockSpec((1, 1, tm), lambda i, *_: (i, 0, 0), memory_space=pltpu.SMEM)
    flat_smem = pl.BlockSpec((P,), lambda i, *_: (0,), memory_space=pltpu.SMEM)
    assert N & (N - 1) == 0, "token index is recovered from a result row with a mask"
    slots = pl.pallas_call(
        functools.partial(_slots_kernel, tm=tm, n_tokens=N),
        grid_spec=pltpu.PrefetchScalarGridSpec(
            num_scalar_prefetch=2, grid=(nt,), in_specs=[smem_spec, smem_spec], out_specs=flat_smem),
        out_shape=jax.ShapeDtypeStruct((P,), I32),
        compiler_params=_params(("arbitrary",)),
        name="slots",
    )(real_end, padded_end, pos1, pos2).reshape(nb, 1, EXPERT_BLOCK)

    any_spec = pl.BlockSpec(memory_space=pl.ANY)
    blk_smem = lambda off: pl.BlockSpec((1, 1, EXPERT_BLOCK),
                                        lambda j, be, nu: (jnp.clip(j + off, 0, nb - 1), 0, 0),
                                        memory_space=pltpu.SMEM)
    buf = pltpu.VMEM((2 * EXPERT_BLOCK * TOKEN_TILE_ROWS, LANES), F32)
    y2 = pl.pallas_call(
        functools.partial(_expert_kernel, n_tokens=N),
        grid_spec=pltpu.PrefetchScalarGridSpec(
            num_scalar_prefetch=2,
            grid=(nb,),
            in_specs=[blk_smem(-1), blk_smem(0), blk_smem(1), any_spec,
                      pl.BlockSpec((None, D, 2 * D_EXPERT), lambda j, be, nu: (be[j], 0, 0)),
                      pl.BlockSpec((None, D_EXPERT, D), lambda j, be, nu: (be[j], 0, 0))],
            out_specs=any_spec,
            scratch_shapes=[pltpu.VMEM((D, 2 * D_EXPERT), BF16), pltpu.VMEM((D_EXPERT, D), BF16), buf, buf,
                            pltpu.SemaphoreType.DMA((2,)), pltpu.SemaphoreType.DMA((2,))]),
        out_shape=jax.ShapeDtypeStruct(((2 * N + 2 * EXPERT_BLOCK) * TOKEN_TILE_ROWS, LANES), F32),
        compiler_params=_params(("arbitrary",)),
        name="experts",
    )(block_expert, n_used, slots, slots, slots, xn, w_up, w_down)

    wg = w_ple_gate.astype(BF16)
    wp = w_ple_proj.astype(BF16)
    out = pl.pallas_call(
        _ple_kernel,
        grid=(nt,),
        in_specs=[row_spec(D), row_spec(LANES), row_spec(p2.shape[1]), tile_spec,
                  pl.BlockSpec((tm * TOKEN_TILE_ROWS, LANES), lambda i: (nt + i, 0)),
                  _full((1, D)), _full(wg.shape), _full(wp.shape), _full((1, D))],
        out_specs=row_spec(D),
        out_shape=jax.ShapeDtypeStruct((N, D), F32),
        compiler_params=_params(("arbitrary",)),
        name="ple",
    )(h1, rf, p2, y2, y2, g_ple[None], wg, wp, g_out[None])
    return out


def kernel(x, p, g_mix, w_in, w_conv_qkv, a_log, dt_bias, g_gdn_out, b_glu, w_conv_dw, b_conv_dw, ln_conv_g, ln_conv_b, w_out, g_moe, w_group_router, w_expert_router, w_up, w_down, g_ple, w_ple_gate, w_ple_proj, g_final):
    B, T, D = x.shape
    depth = p.shape[0]
    assert depth == 1, "the final RMSNorm is fused into the (single) layer's last kernel"
    h = _layer(x.reshape(B * T, D), p[0].reshape(B * T, -1), B, T, g_mix[0], w_in[0], w_conv_qkv[0],
               a_log[0], dt_bias[0], g_gdn_out[0], b_glu[0], w_conv_dw[0], b_conv_dw[0], ln_conv_g[0],
               ln_conv_b[0], w_out[0], g_moe[0], w_group_router[0], w_expert_router[0], w_up[0],
               w_down[0], g_ple[0], w_ple_gate[0], w_ple_proj[0], g_final)
    return h.reshape(B, T, D)
```

```python
import functools

import jax
import jax.numpy as jnp
from jax import lax
from jax.experimental import pallas as pl
from jax.experimental.pallas import tpu as pltpu

F32 = jnp.float32
BF16 = jnp.bfloat16
I32 = jnp.int32

EPS = 1e-6
GDN_HEADS = 4
GDN_DK = 128
GDN_CONV = 4
GDN_CHUNK = 64
CONF_CH = 512
CONF_WIDTH = 31
N_GROUPS = 8
EXPERTS_PER_GROUP = 8
N_EXPERTS = 64
D_EXPERT = 256
LANES = 128
SUBLANES = 8
TOKEN_TILE_ROWS = 8
ROUTE_LANE0 = N_GROUPS
EXPERT_BLOCK = 256
QKV_HALO = 8
CONF_HALO = 32
NEG = -1e30
VMEM_LIMIT = 56 * 1024 * 1024


def _dot(a, b):
    return jnp.dot(a, b, preferred_element_type=F32)


def _dot_nt(a, b):
    return lax.dot_general(a, b, (((1,), (1,)), ((), ())), preferred_element_type=F32)


def _dot_tn(a, b):
    return lax.dot_general(a, b, (((0,), (0,)), ((), ())), preferred_element_type=F32)


def _silu(x):
    return x * jax.nn.sigmoid(x)


def _split_bf16(x):
    hi = x.astype(BF16)
    lo = (x - hi.astype(F32)).astype(BF16)
    return hi, lo


def _causal_taps(ext_ref, w_ref, n_taps, halo, tm, cols):
    rows = ext_ref.shape[0]
    ext = ext_ref[:, cols]
    first = halo - (n_taps - 1)
    acc = None
    for res in range(SUBLANES):
        taps = [k for k in range(n_taps) if (first + k) % SUBLANES == res]
        if not taps:
            continue
        shifted = ext if res == 0 else pltpu.roll(ext, rows - res, 0)
        for k in taps:
            a = first + k - res
            term = w_ref[k:k + 1, cols] * shifted[a:a + tm, :]
            acc = term if acc is None else acc + term
    return acc


def _inproj_kernel(x_ref, gmix_ref, wqkv_ref, wz_ref, wbah_ref, wbal_ref, wglu_ref, cw_ref, pa_ref,
                   bglu_ref, cdw_ref, cvec_ref,
                   q_ref, k_ref, v_ref, bg_ref, zs_ref, yc_ref,
                   extq, extu, *, tiles_per_seq, tm):
    i = pl.program_id(0)
    first = (i % tiles_per_seq) == 0

    x = x_ref[...]
    n1 = x * lax.rsqrt(jnp.mean(x * x, axis=-1, keepdims=True) + EPS) * gmix_ref[...]
    n1h, n1l = _split_bf16(n1)

    @pl.when(first)
    def _():
        extq[0:QKV_HALO, :] = jnp.zeros((QKV_HALO, extq.shape[1]), F32)
        extu[0:CONF_HALO, :] = jnp.zeros((CONF_HALO, extu.shape[1]), F32)

    @pl.when(jnp.logical_not(first))
    def _():
        extq[0:QKV_HALO, :] = extq[tm:tm + QKV_HALO, :]
        extu[0:CONF_HALO, :] = extu[tm:tm + CONF_HALO, :]

    extq[QKV_HALO:QKV_HALO + tm, :] = _dot(n1h, wqkv_ref[...])
    hw = GDN_HEADS * GDN_DK
    qc = _silu(_causal_taps(extq, cw_ref, GDN_CONV, QKV_HALO, tm, slice(0, hw)))
    kc = _silu(_causal_taps(extq, cw_ref, GDN_CONV, QKV_HALO, tm, slice(hw, 2 * hw)))
    for h in range(GDN_HEADS):
        cs = slice(h * GDN_DK, (h + 1) * GDN_DK)
        qh = qc[:, cs]
        q_ref[:, cs] = qh * lax.rsqrt(jnp.sum(qh * qh, axis=-1, keepdims=True) + EPS) * (GDN_DK ** -0.5)
        kh = kc[:, cs]
        k_ref[:, cs] = kh * lax.rsqrt(jnp.sum(kh * kh, axis=-1, keepdims=True) + EPS)
    v_ref[...] = _silu(_causal_taps(extq, cw_ref, GDN_CONV, QKV_HALO, tm, slice(2 * hw, 3 * hw)))

    zs_ref[...] = _silu(_dot(n1h, wz_ref[...])).astype(BF16)

    ba = _dot(n1h, wbah_ref[...]) + _dot(n1l, wbah_ref[...]) + _dot(n1h, wbal_ref[...])
    lane = lax.broadcasted_iota(I32, ba.shape, 1)
    row = lax.broadcasted_iota(I32, ba.shape, 0)
    beta = jax.nn.sigmoid(ba)
    sp_in = ba + pa_ref[1:2, :]
    softplus = jnp.maximum(sp_in, 0.0) + jnp.log1p(jnp.exp(-jnp.abs(sp_in)))
    g = -jnp.exp(pa_ref[0:1, :]) * softplus
    rin = row & (GDN_CHUNK - 1)
    s = 1
    while s < GDN_CHUNK:
        g = g + jnp.where(rin >= s, pltpu.roll(g, s, 0), 0.0)
        s *= 2
    bg_ref[...] = jnp.where(lane < GDN_HEADS, beta, g)

    glu = _dot(n1h, wglu_ref[...]) + bglu_ref[...]
    extu[CONF_HALO:CONF_HALO + tm, :] = glu[:, :CONF_CH] * jax.nn.sigmoid(glu[:, CONF_CH:])
    c = _causal_taps(extu, cdw_ref, CONF_WIDTH, CONF_HALO, tm, slice(0, CONF_CH)) + cvec_ref[0:1, :]
    mu = jnp.mean(c, axis=-1, keepdims=True)
    cc = c - mu
    var = jnp.mean(cc * cc, axis=-1, keepdims=True)
    yln = cc * lax.rsqrt(var + EPS) * cvec_ref[1:2, :] + cvec_ref[2:3, :]
    yc_ref[...] = _silu(yln).astype(BF16)


def _gdn_kernel(q_ref, k_ref, v_ref, bg_ref, zs_ref, gout_ref, y_ref, s_ref, *, cb, nbatch, group):
    n = pl.program_id(0)

    @pl.when(n == 0)
    def _():
        s_ref[...] = jnp.zeros(s_ref.shape, F32)

    C = GDN_CHUNK
    r = lax.broadcasted_iota(I32, (C, C), 0)
    c = lax.broadcasted_iota(I32, (C, C), 1)
    causal = r >= c
    strict = r > c
    eye = r == c
    eyef = jnp.where(eye, 1.0, 0.0).astype(F32)
    gout = gout_ref[...]

    def front(b, ci, h):
        rows = slice(ci * C, (ci + 1) * C)
        cols = slice(h * GDN_DK, (h + 1) * GDN_DK)
        q = q_ref[b, rows, cols]
        k = k_ref[b, rows, cols]
        v = v_ref[b, rows, cols]
        beta = bg_ref[b, rows, h:h + 1]
        gc = bg_ref[b, rows, GDN_HEADS + h:GDN_HEADS + h + 1]
        gc_row = jnp.sum(jnp.where(eye, gc, 0.0), axis=0, keepdims=True)
        gc_last = gc[C - 1:C, :]
        eg = jnp.exp(gc)
        kb = k * beta
        return dict(
            decay=jnp.where(causal, jnp.exp(jnp.minimum(gc - gc_row, 0.0)), 0.0),
            lhs=jnp.concatenate([kb, q], axis=0).astype(BF16),
            kbf=k.astype(BF16),
            rhs=jnp.concatenate([v * beta, kb * eg], axis=1).astype(BF16),
            qd=q * eg,
            kd=(k * jnp.exp(gc_last - gc)).astype(BF16),
            gl=jnp.exp(gc_last))

    insts = [(b, ci, h) for ci in range(cb) for b in range(nbatch) for h in range(GDN_HEADS)]
    done = {}
    for g0 in range(0, len(insts), group):
        grp = insts[g0:g0 + group]
        st = [front(*key) for key in grp]
        kq = [_dot_nt(s["lhs"], s["kbf"]) for s in st]
        xp = [-jnp.where(strict, m[:C] * s["decay"], 0.0) for m, s in zip(kq, st)]
        qk = [(m[C:] * s["decay"]).astype(BF16) for m, s in zip(kq, st)]
        tinv = [eyef + x for x in xp]
        for _ in range(5):
            xpb = [x.astype(BF16) for x in xp]
            xp = [_dot(x, x) for x in xpb]
            tinv = [t + _dot(t.astype(BF16), x.astype(BF16)) for t, x in zip(tinv, xp)]
        uw = [_dot(t.astype(BF16), s["rhs"]) for t, s in zip(tinv, st)]
        for key, s, m, qkm in zip(grp, st, uw, qk):
            done[key] = dict(u=m[:, :GDN_DK],
                             wq=jnp.concatenate([m[:, GDN_DK:], s["qd"]], axis=0).astype(BF16),
                             qk=qkm, kd=s["kd"], gl=s["gl"])

    chains = [(b, h) for b in range(nbatch) for h in range(GDN_HEADS)]
    for ci in range(cb):
        rows = slice(ci * C, (ci + 1) * C)
        dd = [done[(b, ci, h)] for b, h in chains]
        s_old = [s_ref[b * GDN_HEADS + h] for b, h in chains]
        wq = [_dot(d["wq"], s.astype(BF16)) for d, s in zip(dd, s_old)]
        v_new = [(d["u"] - m[:C]).astype(BF16) for d, m in zip(dd, wq)]
        o = [m[C:] + _dot(d["qk"], vn) for d, m, vn in zip(dd, wq, v_new)]
        s_new = [s * d["gl"] + _dot_tn(d["kd"], vn) for d, s, vn in zip(dd, s_old, v_new)]
        for (b, h), sn, oo in zip(chains, s_new, o):
            cols = slice(h * GDN_DK, (h + 1) * GDN_DK)
            s_ref[b * GDN_HEADS + h] = sn
            on = oo * lax.rsqrt(jnp.mean(oo * oo, axis=-1, keepdims=True) + EPS) * gout
            y_ref[b, rows, cols] = (on * zs_ref[b, rows, cols].astype(F32)).astype(BF16)


def _router_kernel(x_ref, yg_ref, yc_ref, wo1_ref, wo2_ref, gmoe_ref, wrh_ref, wrl_ref, ltri_ref,
                   h1_ref, xn_ref, ri_ref, rf_ref, cnt_ref, carry, *, nsub):
    i = pl.program_id(0)

    @pl.when(i == 0)
    def _():
        carry[...] = jnp.zeros(carry.shape, F32)

    ts = x_ref.shape[0] // nsub
    subs = [slice(s * ts, (s + 1) * ts) for s in range(nsub)]
    each = lambda fn, *lists: [fn(*args) for args in zip(*lists)]
    lane_sum = lambda v: jnp.sum(v, axis=-1, keepdims=True)
    lane_max = lambda v: jnp.max(v, axis=-1, keepdims=True)
    lane_min = lambda v: jnp.min(v, axis=-1, keepdims=True)

    h1 = [x_ref[sl, :] + _dot(yg_ref[sl, :], wo1_ref[...]) + _dot(yc_ref[sl, :], wo2_ref[...]) for sl in subs]
    for sl, h in zip(subs, h1):
        h1_ref[sl, :] = h
    xn = each(lambda h: h * lax.rsqrt(jnp.mean(h * h, axis=-1, keepdims=True) + EPS) * gmoe_ref[...], h1)
    parts = each(_split_bf16, xn)
    for s, v in enumerate(xn):
        for f in range(TOKEN_TILE_ROWS):
            xn_ref[pl.ds(s * ts * TOKEN_TILE_ROWS + f, ts, stride=TOKEN_TILE_ROWS), :] = v[:, f * LANES:(f + 1) * LANES]
    lg = [_dot(hi, wrh_ref[...]) + _dot(lo, wrh_ref[...]) + _dot(hi, wrl_ref[...]) for hi, lo in parts]

    lane = lax.broadcasted_iota(I32, lg[0].shape, 1)
    lanef = lane.astype(F32)
    none = float(LANES)
    gl = each(lambda v: jnp.where(lane < N_GROUPS, v, NEG), lg)
    gmax = each(lane_max, gl)
    g_w = each(lambda v, m: 1.0 / lane_sum(jnp.exp(v - m)), gl, gmax)
    gidx = each(lambda v, m: lane_min(jnp.where(v == m, lanef, none)), gl, gmax)
    lane_group = ((lane - ROUTE_LANE0) >> 3).astype(F32)
    is_expert = (lane >= ROUTE_LANE0) & (lane < ROUTE_LANE0 + N_EXPERTS)
    el = each(lambda v, g: jnp.where(is_expert & (lane_group == g), v, NEG), lg, gidx)
    m1 = each(lane_max, el)
    i1 = each(lambda v, m: lane_min(jnp.where(v == m, lanef, none)), el, m1)
    el2 = each(lambda v, a: jnp.where(lanef == a, NEG, v), el, i1)
    m2 = each(lane_max, el2)
    i2 = each(lambda v, m: lane_min(jnp.where(v == m, lanef, none)), el2, m2)
    ratio = each(lambda a, b: jnp.exp(b - a), m1, m2)
    gate1 = each(lambda w, r: w / (1.0 + r), g_w, ratio)
    gate2 = each(lambda w, r: w * r / (1.0 + r), g_w, ratio)

    oh1 = each(lambda a: lanef == a, i1)
    oh2 = each(lambda a: lanef == a, i2)
    oh = each(lambda a, b: jnp.where(a | b, 1.0, 0.0).astype(F32), oh1, oh2)
    within = each(lambda v: _dot(ltri_ref[...], v.astype(BF16)), oh)
    base = carry[...]
    for s in range(nsub):
        cum = within[s] + base
        rank1 = lane_sum(jnp.where(oh1[s], cum, 0.0))
        rank2 = lane_sum(jnp.where(oh2[s], cum, 0.0))
        routes = jnp.where(lane == 0, i1[s] - ROUTE_LANE0,
                           jnp.where(lane == 1, i2[s] - ROUTE_LANE0,
                                     jnp.where(lane == 2, rank1, jnp.where(lane == 3, rank2, 0.0))))
        ri_ref[subs[s], :] = routes.astype(I32)
        rf_ref[subs[s], :] = jnp.where(lane == 0, gate1[s], jnp.where(lane == 1, gate2[s], 0.0))
        base = base + jnp.sum(oh[s], axis=0, keepdims=True)
    carry[...] = base
    cnt_ref[...] = jnp.broadcast_to(base, cnt_ref.shape)


def _pos_kernel(ri_ref, cnt_ref, post_ref, blk_ref, ends_ref, *, tm):
    shift = EXPERT_BLOCK.bit_length() - 1
    cnt = cnt_ref[...].astype(I32)
    padded = ((cnt + (EXPERT_BLOCK - 1)) >> shift) << shift
    lane8 = lax.broadcasted_iota(I32, padded.shape, 1)
    pend = padded
    s = 1
    while s < LANES:
        pend = pend + jnp.where(lane8 >= s, pltpu.roll(pend, s, 1), 0)
        s *= 2
    pstart = (pend - padded)[0:1, :].astype(F32)

    ri = ri_ref[...]
    lane = lax.broadcasted_iota(I32, ri.shape, 1)
    e1 = ri[:, 0:1] + ROUTE_LANE0
    e2 = ri[:, 1:2] + ROUTE_LANE0
    ps1 = jnp.sum(jnp.where(lane == e1, pstart, 0.0), axis=-1, keepdims=True)
    ps2 = jnp.sum(jnp.where(lane == e2, pstart, 0.0), axis=-1, keepdims=True)
    pos1 = ps1 + ri[:, 2:3].astype(F32)
    pos2 = ps2 + ri[:, 3:4].astype(F32)
    posmat = jnp.where(lane == 0, pos1, jnp.where(lane == 1, pos2, 0.0))
    for j in range(tm // LANES):
        pt = posmat[j * LANES:(j + 1) * LANES, :].T
        post_ref[:, j * LANES:(j + 1) * LANES] = pt[0:8, :].astype(I32)

    pendf = pend.astype(F32)[0:1, :]
    rr = lax.broadcasted_iota(I32, (LANES, LANES), 0)
    cc = lax.broadcasted_iota(I32, (LANES, LANES), 1)
    pend_col = jnp.sum(jnp.where(rr == cc, pendf, 0.0), axis=1, keepdims=True)
    nbl = blk_ref.shape[1]
    er = lax.broadcasted_iota(I32, (LANES, nbl), 0)
    jb = (lax.broadcasted_iota(I32, (LANES, nbl), 1) * EXPERT_BLOCK).astype(F32)
    valid = (er >= ROUTE_LANE0) & (er < ROUTE_LANE0 + N_EXPERTS)
    be = jnp.sum(jnp.where(valid & (pend_col <= jb), 1.0, 0.0), axis=0, keepdims=True)
    be = jnp.minimum(be, float(N_EXPERTS - 1))
    nused = jnp.max(pendf, axis=-1, keepdims=True) * (1.0 / EXPERT_BLOCK)
    rowi = lax.broadcasted_iota(I32, blk_ref.shape, 0)
    blk_ref[...] = jnp.where(rowi == 0, be, nused).astype(I32)
    cend_row = pstart + cnt_ref[0:1, :]
    rowe = lax.broadcasted_iota(I32, ends_ref.shape, 0)
    ends_ref[...] = jnp.where(rowe == 0, cend_row, pendf).astype(I32)


def _slots_kernel(cend_ref, pend_ref, pos1_ref, pos2_ref, slot_ref, *, tm, n_tokens):
    i = pl.program_id(0)
    dump0 = 2 * n_tokens
    dump_mask = 2 * EXPERT_BLOCK - 1
    F = 8

    def fill_range(lo, hi):
        def body(c, carry):
            for u in range(F):
                p = jnp.minimum(lo + c * F + u, hi - 1)
                slot_ref[p] = dump0 + (p & dump_mask)
            return carry

        lax.fori_loop(0, (hi - lo + (F - 1)) // F, body, 0)

    @pl.when(i == 0)
    def _():
        def per_expert(e, carry):
            fill_range(cend_ref[e], pend_ref[e])
            return carry

        lax.fori_loop(0, N_EXPERTS, per_expert, 0)
        fill_range(pend_ref[N_EXPERTS - 1], slot_ref.shape[0])

    def body(r, carry):
        t = i * tm + r
        slot_ref[pos1_ref[0, 0, r]] = t
        slot_ref[pos2_ref[0, 0, r]] = n_tokens + t
        return carry

    lax.fori_loop(0, tm, body, 0, unroll=16)


def _expert_kernel(be_ref, nu_ref, slp_ref, slc_ref, sln_ref, xn_ref, wup_ref, wdn_ref, y2_ref,
                   wupb, wdnb, xbuf, ybuf, gsem, ssem, *, n_tokens):
    j = pl.program_id(0)
    cur = j % 2
    nxt = 1 - cur
    n_used = nu_ref[0]
    R = TOKEN_TILE_ROWS
    buf_rows = EXPERT_BLOCK * R
    tok_mask = n_tokens - 1

    def gather_row(slot_ref, b, r, prio):
        tok = slot_ref[0, 0, r] & tok_mask
        pltpu.make_async_copy(xn_ref.at[pl.ds(tok * R, R), :], xbuf.at[pl.ds(b * buf_rows + r * R, R), :],
                              gsem.at[b]).start(priority=prio)

    def scatter_row(slot_ref, b, r, prio):
        pltpu.make_async_copy(ybuf.at[pl.ds(b * buf_rows + r * R, R), :],
                              y2_ref.at[pl.ds(slot_ref[0, 0, r] * R, R), :], ssem.at[b]).start(priority=prio)

    def wait_buffer(sem):
        pltpu.make_async_copy(xn_ref.at[pl.ds(0, buf_rows), :], xbuf.at[pl.ds(0, buf_rows), :], sem).wait()

    def rows_loop(fn):
        def body(g, carry):
            for u in range(8):
                fn(g * 8 + u, u % 2)
            return carry

        lax.fori_loop(0, EXPERT_BLOCK // 8, body, 0)

    def dump_init(b, sync):
        cp = pltpu.make_async_copy(ybuf.at[pl.ds(b * buf_rows, buf_rows), :],
                                   y2_ref.at[pl.ds((2 * n_tokens + b * EXPERT_BLOCK) * R, buf_rows), :],
                                   ssem.at[b])
        cp.start()
        if sync:
            cp.wait()

    @pl.when(j == 0)
    def _():
        ybuf[...] = jnp.zeros(ybuf.shape, F32)
        dump_init(0, True)
        dump_init(1, False)
        rows_loop(lambda r, prio: gather_row(slc_ref, 0, r, prio))

    prev = be_ref[jnp.maximum(j - 1, 0)]

    @pl.when((j == 0) | (be_ref[j] != prev))
    def _():
        wupb[...] = wup_ref[...].astype(BF16)
        wdnb[...] = wdn_ref[...].astype(BF16)

    def block(first):
        wait_buffer(gsem.at[cur])
        base = cur * buf_rows
        x = jnp.concatenate([xbuf[pl.ds(base + f, EXPERT_BLOCK, stride=R), :] for f in range(R)], axis=1)
        ab = _dot(x.astype(BF16), wupb[...])
        if not first:
            for r in range(EXPERT_BLOCK):
                scatter_row(slp_ref, nxt, r, r % 2)
        for r in range(EXPERT_BLOCK):
            gather_row(sln_ref, nxt, r, r % 2)
        hmid = _silu(ab[:, :D_EXPERT]) * ab[:, D_EXPERT:]
        y = _dot(hmid.astype(BF16), wdnb[...])
        if not first:
            wait_buffer(ssem.at[cur])
        for f in range(R):
            ybuf[pl.ds(base + f, EXPERT_BLOCK, stride=R), :] = y[:, f * LANES:(f + 1) * LANES]

    @pl.when(j == 0)
    def _():
        block(True)

    @pl.when((j > 0) & (j < n_used))
    def _():
        block(False)

    @pl.when(j == n_used)
    def _():
        wait_buffer(gsem.at[cur])
        rows_loop(lambda r, prio: scatter_row(slp_ref, nxt, r, prio))
        wait_buffer(ssem.at[cur])
        wait_buffer(ssem.at[nxt])


def _ple_kernel(h1_ref, rf_ref, p_ref, ya_ref, yb_ref, gple_ref, wg_ref, wp_ref, gfin_ref, out_ref):
    tm = h1_ref.shape[0]
    R = TOKEN_TILE_ROWS
    rows = lambda ref: jnp.concatenate([ref[pl.ds(f, tm, stride=R), :] for f in range(R)], axis=1)
    rf = rf_ref[...]
    h2 = h1_ref[...] + (rf[:, 0:1] * rows(ya_ref) + rf[:, 1:2] * rows(yb_ref))
    hn = h2 * lax.rsqrt(jnp.mean(h2 * h2, axis=-1, keepdims=True) + EPS) * gple_ref[...]
    gate = jax.nn.sigmoid(_dot(hn.astype(BF16), wg_ref[...]))
    h3 = h2 + gate * _dot(p_ref[...].astype(BF16), wp_ref[...])
    out_ref[...] = h3 * lax.rsqrt(jnp.mean(h3 * h3, axis=-1, keepdims=True) + EPS) * gfin_ref[...]


def _params(sem):
    return pltpu.CompilerParams(dimension_semantics=sem, vmem_limit_bytes=VMEM_LIMIT)


def _full(shape):
    nd = len(shape)
    return pl.BlockSpec(shape, lambda *_: (0,) * nd)


def _pad_lanes(a, lane0=0):
    out = jnp.zeros((a.shape[0], LANES), a.dtype)
    return out.at[:, lane0:lane0 + a.shape[1]].set(a)


def _layer(x2, p2, B, T, g_mix, w_in, w_conv_qkv, a_log, dt_bias, g_gdn_out, b_glu, w_conv_dw,
           b_conv_dw, ln_conv_g, ln_conv_b, w_out, g_moe, w_group_router, w_expert_router, w_up,
           w_down, g_ple, w_ple_gate, w_ple_proj, g_out):
    N, D = x2.shape
    hw = GDN_HEADS * GDN_DK
    tm = min(512, T)
    nt = N // tm

    o1 = 3 * hw
    o2 = o1 + hw
    o4 = o2 + 2 * GDN_HEADS
    wqkv = w_in[:, :o1].astype(BF16)
    wz = w_in[:, o1:o2].astype(BF16)
    wbah, wbal = _split_bf16(_pad_lanes(w_in[:, o2:o4]))
    wglu = w_in[:, o4:].astype(BF16)
    pa = jnp.concatenate([_pad_lanes(a_log[None, :], GDN_HEADS), _pad_lanes(dt_bias[None, :], GDN_HEADS),
                          jnp.zeros((6, LANES), F32)], axis=0)
    cdw = jnp.concatenate([w_conv_dw, jnp.zeros((1, CONF_CH), F32)], axis=0)
    cvec = jnp.concatenate([b_conv_dw[None], ln_conv_g[None], ln_conv_b[None], jnp.zeros((5, CONF_CH), F32)], 0)

    row_spec = lambda w: pl.BlockSpec((tm, w), lambda i: (i, 0))
    q, k, v, bg, zs, yc = pl.pallas_call(
        functools.partial(_inproj_kernel, tiles_per_seq=T // tm, tm=tm),
        grid=(nt,),
        in_specs=[row_spec(D), _full((1, D)), _full(wqkv.shape), _full(wz.shape), _full(wbah.shape),
                  _full(wbal.shape), _full(wglu.shape), _full(w_conv_qkv.shape), _full(pa.shape),
                  _full((1, 2 * CONF_CH)), _full(cdw.shape), _full(cvec.shape)],
        out_specs=[row_spec(hw), row_spec(hw), row_spec(hw), row_spec(LANES), row_spec(hw), row_spec(CONF_CH)],
        out_shape=[jax.ShapeDtypeStruct((N, hw), F32)] * 3 + [jax.ShapeDtypeStruct((N, LANES), F32),
                   jax.ShapeDtypeStruct((N, hw), BF16), jax.ShapeDtypeStruct((N, CONF_CH), BF16)],
        scratch_shapes=[pltpu.VMEM((tm + QKV_HALO, 3 * hw), F32), pltpu.VMEM((tm + CONF_HALO, CONF_CH), F32)],
        compiler_params=_params(("arbitrary",)),
        name="inproj",
    )(x2, g_mix[None], wqkv, wz, wbah, wbal, wglu, w_conv_qkv, pa, b_glu[None], cdw, cvec)

    cb = 8
    rows = cb * GDN_CHUNK
    seq_spec = lambda w: pl.BlockSpec((B, rows, w), lambda n: (0, n, 0))
    r3 = lambda a: a.reshape(B, T, a.shape[-1])
    ygdn = pl.pallas_call(
        functools.partial(_gdn_kernel, cb=cb, nbatch=B, group=8 * GDN_HEADS),
        grid=(T // rows,),
        in_specs=[seq_spec(hw), seq_spec(hw), seq_spec(hw), seq_spec(LANES), seq_spec(hw),
                  pl.BlockSpec((1, GDN_DK), lambda n: (0, 0))],
        out_specs=seq_spec(hw),
        out_shape=jax.ShapeDtypeStruct((B, T, hw), BF16),
        scratch_shapes=[pltpu.VMEM((B * GDN_HEADS, GDN_DK, GDN_DK), F32)],
        compiler_params=_params(("arbitrary",)),
        name="gdn",
    )(r3(q), r3(k), r3(v), r3(bg), r3(zs), g_gdn_out[None]).reshape(N, hw)

    wo1 = w_out[:hw].astype(BF16)
    wo2 = w_out[hw:].astype(BF16)
    wrh, wrl = _split_bf16(_pad_lanes(jnp.concatenate([w_group_router, w_expert_router], axis=1)))
    nsub = 2 if tm % 256 == 0 else 1
    ltri = jnp.tril(jnp.ones((tm // nsub, tm // nsub), F32), -1).astype(BF16)
    assert D == TOKEN_TILE_ROWS * LANES
    tile_spec = pl.BlockSpec((tm * TOKEN_TILE_ROWS, LANES), lambda i: (i, 0))
    h1, xn, ri, rf, cnt = pl.pallas_call(
        functools.partial(_router_kernel, nsub=nsub),
        grid=(nt,),
        in_specs=[row_spec(D), row_spec(hw), row_spec(CONF_CH), _full(wo1.shape), _full(wo2.shape),
                  _full((1, D)), _full(wrh.shape), _full(wrl.shape), _full(ltri.shape)],
        out_specs=[row_spec(D), tile_spec, row_spec(LANES), row_spec(LANES), _full((8, LANES))],
        out_shape=[jax.ShapeDtypeStruct((N, D), F32), jax.ShapeDtypeStruct((N * TOKEN_TILE_ROWS, LANES), F32),
                   jax.ShapeDtypeStruct((N, LANES), I32), jax.ShapeDtypeStruct((N, LANES), F32),
                   jax.ShapeDtypeStruct((8, LANES), F32)],
        scratch_shapes=[pltpu.VMEM((1, LANES), F32)],
        compiler_params=_params(("arbitrary",)),
        name="router",
    )(x2, ygdn, yc, wo1, wo2, g_moe[None], wrh, wrl, ltri)

    nb = (2 * N) // EXPERT_BLOCK + N_EXPERTS
    nbl = -(-nb // LANES) * LANES
    post, blk, ends = pl.pallas_call(
        functools.partial(_pos_kernel, tm=tm),
        grid=(nt,),
        in_specs=[row_spec(LANES), _full((8, LANES))],
        out_specs=[pl.BlockSpec((8, tm), lambda i: (0, i)), _full((8, nbl)), _full((8, LANES))],
        out_shape=[jax.ShapeDtypeStruct((8, N), I32), jax.ShapeDtypeStruct((8, nbl), I32),
                   jax.ShapeDtypeStruct((8, LANES), I32)],
        compiler_params=_params(("arbitrary",)),
        name="positions",
    )(ri, cnt)
    pos1 = post[0].reshape(nt, 1, tm)
    pos2 = post[1].reshape(nt, 1, tm)
    block_expert = blk[0, :nb]
    n_used = blk[1, :1]

    P = nb * EXPERT_BLOCK
    real_end = ends[0, ROUTE_LANE0:ROUTE_LANE0 + N_EXPERTS]
    padded_end = ends[1, ROUTE_LANE0:ROUTE_LANE0 + N_EXPERTS]
    smem_spec = pl.BlockSpec((1, 1, tm), lambda i, *_: (i, 0, 0), memory_space=pltpu.SMEM)
    flat_smem = pl.BlockSpec((P,), lambda i, *_: (0,), memory_space=pltpu.SMEM)
    assert N & (N - 1) == 0, "token index is recovered from a result row with a mask"
    slots = pl.pallas_call(
        functools.partial(_slots_kernel, tm=tm, n_tokens=N),
        grid_spec=pltpu.PrefetchScalarGridSpec(
            num_scalar_prefetch=2, grid=(nt,), in_specs=[smem_spec, smem_spec], out_specs=flat_smem),
        out_shape=jax.ShapeDtypeStruct((P,), I32),
        compiler_params=_params(("arbitrary",)),
        name="slots",
    )(real_end, padded_end, pos1, pos2).reshape(nb, 1, EXPERT_BLOCK)

    any_spec = pl.BlockSpec(memory_space=pl.ANY)
    blk_smem = lambda off: pl.BlockSpec((1, 1, EXPERT_BLOCK),
                                        lambda j, be, nu: (jnp.clip(j + off, 0, nb - 1), 0, 0),
                                        memory_space=pltpu.SMEM)
    buf = pltpu.VMEM((2 * EXPERT_BLOCK * TOKEN_TILE_ROWS, LANES), F32)
    y2 = pl.pallas_call(
        functools.partial(_expert_kernel, n_tokens=N),
        grid_spec=pltpu.PrefetchScalarGridSpec(
            num_scalar_prefetch=2,
            grid=(nb,),
            in_specs=[blk_smem(-1), blk_smem(0), blk_smem(1), any_spec,
                      pl.BlockSpec((None, D, 2 * D_EXPERT), lambda j, be, nu: (be[j], 0, 0)),
                      pl.BlockSpec((None, D_EXPERT, D), lambda j, be, nu: (be[j], 0, 0))],
            out_specs=any_spec,
            scratch_shapes=[pltpu.VMEM((D, 2 * D_EXPERT), BF16), pltpu.VMEM((D_EXPERT, D), BF16), buf, buf,
                            pltpu.SemaphoreType.DMA((2,)), pltpu.SemaphoreType.DMA((2,))]),
        out_shape=jax.ShapeDtypeStruct(((2 * N + 2 * EXPERT_BLOCK) * TOKEN_TILE_ROWS, LANES), F32),
        compiler_params=_params(("arbitrary",)),
        name="experts",
    )(block_expert, n_used, slots, slots, slots, xn, w_up, w_down)

    wg = w_ple_gate.astype(BF16)
    wp = w_ple_proj.astype(BF16)
    out = pl.pallas_call(
        _ple_kernel,
        grid=(nt,),
        in_specs=[row_spec(D), row_spec(LANES), row_spec(p2.shape[1]), tile_spec,
                  pl.BlockSpec((tm * TOKEN_TILE_ROWS, LANES), lambda i: (nt + i, 0)),
                  _full((1, D)), _full(wg.shape), _full(wp.shape), _full((1, D))],
        out_specs=row_spec(D),
        out_shape=jax.ShapeDtypeStruct((N, D), F32),
        compiler_params=_params(("arbitrary",)),
        name="ple",
    )(h1, rf, p2, y2, y2, g_ple[None], wg, wp, g_out[None])
    return out


def kernel(x, p, g_mix, w_in, w_conv_qkv, a_log, dt_bias, g_gdn_out, b_glu, w_conv_dw, b_conv_dw, ln_conv_g, ln_conv_b, w_out, g_moe, w_group_router, w_expert_router, w_up, w_down, g_ple, w_ple_gate, w_ple_proj, g_final):
    B, T, D = x.shape
    depth = p.shape[0]
    assert depth == 1, "the final RMSNorm is fused into the (single) layer's last kernel"
    h = _layer(x.reshape(B * T, D), p[0].reshape(B * T, -1), B, T, g_mix[0], w_in[0], w_conv_qkv[0],
               a_log[0], dt_bias[0], g_gdn_out[0], b_glu[0], w_conv_dw[0], b_conv_dw[0], ln_conv_g[0],
               ln_conv_b[0], w_out[0], g_moe[0], w_group_router[0], w_expert_router[0], w_up[0],
               w_down[0], g_ple[0], w_ple_gate[0], w_ple_proj[0], g_final)
    return h.reshape(B, T, D)
```

```python
import functools

import jax
import jax.numpy as jnp
from jax import lax
from jax.experimental import pallas as pl
from jax.experimental.pallas import tpu as pltpu

F32 = jnp.float32
BF16 = jnp.bfloat16
I32 = jnp.int32

EPS = 1e-6
GDN_HEADS = 4
GDN_DK = 128
GDN_CONV = 4
GDN_CHUNK = 64
CONF_CH = 512
CONF_WIDTH = 31
N_GROUPS = 8
EXPERTS_PER_GROUP = 8
N_EXPERTS = 64
D_EXPERT = 256
LANES = 128
SUBLANES = 8
TOKEN_TILE_ROWS = 8
ROUTE_LANE0 = N_GROUPS
EXPERT_BLOCK = 256
QKV_HALO = 8
CONF_HALO = 32
NEG = -1e30
VMEM_LIMIT = 56 * 1024 * 1024


def _dot(a, b):
    return jnp.dot(a, b, preferred_element_type=F32)


def _dot_nt(a, b):
    return lax.dot_general(a, b, (((1,), (1,)), ((), ())), preferred_element_type=F32)


def _dot_tn(a, b):
    return lax.dot_general(a, b, (((0,), (0,)), ((), ())), preferred_element_type=F32)


def _silu(x):
    return x * jax.nn.sigmoid(x)


def _split_bf16(x):
    hi = x.astype(BF16)
    lo = (x - hi.astype(F32)).astype(BF16)
    return hi, lo


def _causal_taps(ext_ref, w_ref, n_taps, halo, tm, cols):
    rows = ext_ref.shape[0]
    ext = ext_ref[:, cols]
    first = halo - (n_taps - 1)
    acc = None
    for res in range(SUBLANES):
        taps = [k for k in range(n_taps) if (first + k) % SUBLANES == res]
        if not taps:
            continue
        shifted = ext if res == 0 else pltpu.roll(ext, rows - res, 0)
        for k in taps:
            a = first + k - res
            term = w_ref[k:k + 1, cols] * shifted[a:a + tm, :]
            acc = term if acc is None else acc + term
    return acc


def _inproj_kernel(x_ref, gmix_ref, wqkv_ref, wz_ref, wbah_ref, wbal_ref, wglu_ref, cw_ref, pa_ref,
                   bglu_ref, cdw_ref, cvec_ref,
                   q_ref, k_ref, v_ref, bg_ref, zs_ref, yc_ref,
                   extq, extu, *, tiles_per_seq, tm):
    i = pl.program_id(0)
    first = (i % tiles_per_seq) == 0

    x = x_ref[...]
    n1 = x * lax.rsqrt(jnp.mean(x * x, axis=-1, keepdims=True) + EPS) * gmix_ref[...]
    n1h, n1l = _split_bf16(n1)

    @pl.when(first)
    def _():
        extq[0:QKV_HALO, :] = jnp.zeros((QKV_HALO, extq.shape[1]), F32)
        extu[0:CONF_HALO, :] = jnp.zeros((CONF_HALO, extu.shape[1]), F32)

    @pl.when(jnp.logical_not(first))
    def _():
        extq[0:QKV_HALO, :] = extq[tm:tm + QKV_HALO, :]
        extu[0:CONF_HALO, :] = extu[tm:tm + CONF_HALO, :]

    extq[QKV_HALO:QKV_HALO + tm, :] = _dot(n1h, wqkv_ref[...])
    hw = GDN_HEADS * GDN_DK
    qc = _silu(_causal_taps(extq, cw_ref, GDN_CONV, QKV_HALO, tm, slice(0, hw)))
    kc = _silu(_causal_taps(extq, cw_ref, GDN_CONV, QKV_HALO, tm, slice(hw, 2 * hw)))
    for h in range(GDN_HEADS):
        cs = slice(h * GDN_DK, (h + 1) * GDN_DK)
        qh = qc[:, cs]
        q_ref[:, cs] = qh * lax.rsqrt(jnp.sum(qh * qh, axis=-1, keepdims=True) + EPS) * (GDN_DK ** -0.5)
        kh = kc[:, cs]
        k_ref[:, cs] = kh * lax.rsqrt(jnp.sum(kh * kh, axis=-1, keepdims=True) + EPS)
    v_ref[...] = _silu(_causal_taps(extq, cw_ref, GDN_CONV, QKV_HALO, tm, slice(2 * hw, 3 * hw)))

    zs_ref[...] = _silu(_dot(n1h, wz_ref[...])).astype(BF16)

    ba = _dot(n1h, wbah_ref[...]) + _dot(n1l, wbah_ref[...]) + _dot(n1h, wbal_ref[...])
    lane = lax.broadcasted_iota(I32, ba.shape, 1)
    row = lax.broadcasted_iota(I32, ba.shape, 0)
    beta = jax.nn.sigmoid(ba)
    sp_in = ba + pa_ref[1:2, :]
    softplus = jnp.maximum(sp_in, 0.0) + jnp.log1p(jnp.exp(-jnp.abs(sp_in)))
    g = -jnp.exp(pa_ref[0:1, :]) * softplus
    rin = row & (GDN_CHUNK - 1)
    s = 1
    while s < GDN_CHUNK:
        g = g + jnp.where(rin >= s, pltpu.roll(g, s, 0), 0.0)
        s *= 2
    bg_ref[...] = jnp.where(lane < GDN_HEADS, beta, g)

    glu = _dot(n1h, wglu_ref[...]) + bglu_ref[...]
    extu[CONF_HALO:CONF_HALO + tm, :] = glu[:, :CONF_CH] * jax.nn.sigmoid(glu[:, CONF_CH:])
    c = _causal_taps(extu, cdw_ref, CONF_WIDTH, CONF_HALO, tm, slice(0, CONF_CH)) + cvec_ref[0:1, :]
    mu = jnp.mean(c, axis=-1, keepdims=True)
    cc = c - mu
    var = jnp.mean(cc * cc, axis=-1, keepdims=True)
    yln = cc * lax.rsqrt(var + EPS) * cvec_ref[1:2, :] + cvec_ref[2:3, :]
    yc_ref[...] = _silu(yln).astype(BF16)


def _gdn_kernel(q_ref, k_ref, v_ref, bg_ref, zs_ref, gout_ref, y_ref, s_ref, *, cb, nbatch, group):
    n = pl.program_id(0)

    @pl.when(n == 0)
    def _():
        s_ref[...] = jnp.zeros(s_ref.shape, F32)

    C = GDN_CHUNK
    r = lax.broadcasted_iota(I32, (C, C), 0)
    c = lax.broadcasted_iota(I32, (C, C), 1)
    causal = r >= c
    strict = r > c
    eye = r == c
    eyef = jnp.where(eye, 1.0, 0.0).astype(F32)
    gout = gout_ref[...]

    def front(b, ci, h):
        rows = slice(ci * C, (ci + 1) * C)
        cols = slice(h * GDN_DK, (h + 1) * GDN_DK)
        q = q_ref[b, rows, cols]
        k = k_ref[b, rows, cols]
        v = v_ref[b, rows, cols]
        beta = bg_ref[b, rows, h:h + 1]
        gc = bg_ref[b, rows, GDN_HEADS + h:GDN_HEADS + h + 1]
        gc_row = jnp.sum(jnp.where(eye, gc, 0.0), axis=0, keepdims=True)
        gc_last = gc[C - 1:C, :]
        eg = jnp.exp(gc)
        kb = k * beta
        return dict(
            decay=jnp.where(causal, jnp.exp(jnp.minimum(gc - gc_row, 0.0)), 0.0),
            lhs=jnp.concatenate([kb, q], axis=0).astype(BF16),
            kbf=k.astype(BF16),
            rhs=jnp.concatenate([v * beta, kb * eg], axis=1).astype(BF16),
            qd=q * eg,
            kd=(k * jnp.exp(gc_last - gc)).astype(BF16),
            gl=jnp.exp(gc_last))

    insts = [(b, ci, h) for ci in range(cb) for b in range(nbatch) for h in range(GDN_HEADS)]
    done = {}
    for g0 in range(0, len(insts), group):
        grp = insts[g0:g0 + group]
        st = [front(*key) for key in grp]
        kq = [_dot_nt(s["lhs"], s["kbf"]) for s in st]
        xp = [-jnp.where(strict, m[:C] * s["decay"], 0.0) for m, s in zip(kq, st)]
        qk = [(m[C:] * s["decay"]).astype(BF16) for m, s in zip(kq, st)]
        tinv = [eyef + x for x in xp]
        for _ in range(5):
            xpb = [x.astype(BF16) for x in xp]
            xp = [_dot(x, x) for x in xpb]
            tinv = [t + _dot(t.astype(BF16), x.astype(BF16)) for t, x in zip(tinv, xp)]
        uw = [_dot(t.astype(BF16), s["rhs"]) for t, s in zip(tinv, st)]
        for key, s, m, qkm in zip(grp, st, uw, qk):
            done[key] = dict(u=m[:, :GDN_DK],
                             wq=jnp.concatenate([m[:, GDN_DK:], s["qd"]], axis=0).astype(BF16),
                             qk=qkm, kd=s["kd"], gl=s["gl"])

    chains = [(b, h) for b in range(nbatch) for h in range(GDN_HEADS)]
    for ci in range(cb):
        rows = slice(ci * C, (ci + 1) * C)
        dd = [done[(b, ci, h)] for b, h in chains]
        s_old = [s_ref[b * GDN_HEADS + h] for b, h in chains]
        wq = [_dot(d["wq"], s.astype(BF16)) for d, s in zip(dd, s_old)]
        v_new = [(d["u"] - m[:C]).astype(BF16) for d, m in zip(dd, wq)]
        o = [m[C:] + _dot(d["qk"], vn) for d, m, vn in zip(dd, wq, v_new)]
        s_new = [s * d["gl"] + _dot_tn(d["kd"], vn) for d, s, vn in zip(dd, s_old, v_new)]
        for (b, h), sn, oo in zip(chains, s_new, o):
            cols = slice(h * GDN_DK, (h + 1) * GDN_DK)
            s_ref[b * GDN_HEADS + h] = sn
            on = oo * lax.rsqrt(jnp.mean(oo * oo, axis=-1, keepdims=True) + EPS) * gout
            y_ref[b, rows, cols] = (on * zs_ref[b, rows, cols].astype(F32)).astype(BF16)


def _router_kernel(x_ref, yg_ref, yc_ref, wo1_ref, wo2_ref, gmoe_ref, wrh_ref, wrl_ref, ltri_ref,
                   h1_ref, xn_ref, ri_ref, rf_ref, cnt_ref, carry, *, nsub):
    i = pl.program_id(0)

    @pl.when(i == 0)
    def _():
        carry[...] = jnp.zeros(carry.shape, F32)

    ts = x_ref.shape[0] // nsub
    subs = [slice(s * ts, (s + 1) * ts) for s in range(nsub)]
    each = lambda fn, *lists: [fn(*args) for args in zip(*lists)]
    lane_sum = lambda v: jnp.sum(v, axis=-1, keepdims=True)
    lane_max = lambda v: jnp.max(v, axis=-1, keepdims=True)
    lane_min = lambda v: jnp.min(v, axis=-1, keepdims=True)

    h1 = [x_ref[sl, :] + _dot(yg_ref[sl, :], wo1_ref[...]) + _dot(yc_ref[sl, :], wo2_ref[...]) for sl in subs]
    for sl, h in zip(subs, h1):
        h1_ref[sl, :] = h
    xn = each(lambda h: h * lax.rsqrt(jnp.mean(h * h, axis=-1, keepdims=True) + EPS) * gmoe_ref[...], h1)
    parts = each(_split_bf16, xn)
    for s, v in enumerate(xn):
        for f in range(TOKEN_TILE_ROWS):
            xn_ref[pl.ds(s * ts * TOKEN_TILE_ROWS + f, ts, stride=TOKEN_TILE_ROWS), :] = v[:, f * LANES:(f + 1) * LANES]
    lg = [_dot(hi, wrh_ref[...]) + _dot(lo, wrh_ref[...]) + _dot(hi, wrl_ref[...]) for hi, lo in parts]

    lane = lax.broadcasted_iota(I32, lg[0].shape, 1)
    lanef = lane.astype(F32)
    none = float(LANES)
    gl = each(lambda v: jnp.where(lane < N_GROUPS, v, NEG), lg)
    gmax = each(lane_max, gl)
    g_w = each(lambda v, m: 1.0 / lane_sum(jnp.exp(v - m)), gl, gmax)
    gidx = each(lambda v, m: lane_min(jnp.where(v == m, lanef, none)), gl, gmax)
    lane_group = ((lane - ROUTE_LANE0) >> 3).astype(F32)
    is_expert = (lane >= ROUTE_LANE0) & (lane < ROUTE_LANE0 + N_EXPERTS)
    el = each(lambda v, g: jnp.where(is_expert & (lane_group == g), v, NEG), lg, gidx)
    m1 = each(lane_max, el)
    i1 = each(lambda v, m: lane_min(jnp.where(v == m, lanef, none)), el, m1)
    el2 = each(lambda v, a: jnp.where(lanef == a, NEG, v), el, i1)
    m2 = each(lane_max, el2)
    i2 = each(lambda v, m: lane_min(jnp.where(v == m, lanef, none)), el2, m2)
    ratio = each(lambda a, b: jnp.exp(b - a), m1, m2)
    gate1 = each(lambda w, r: w / (1.0 + r), g_w, ratio)
    gate2 = each(lambda w, r: w * r / (1.0 + r), g_w, ratio)

    oh1 = each(lambda a: lanef == a, i1)
    oh2 = each(lambda a: lanef == a, i2)
    oh = each(lambda a, b: jnp.where(a | b, 1.0, 0.0).astype(F32), oh1, oh2)
    within = each(lambda v: _dot(ltri_ref[...], v.astype(BF16)), oh)
    base = carry[...]
    for s in range(nsub):
        cum = within[s] + base
        rank1 = lane_sum(jnp.where(oh1[s], cum, 0.0))
        rank2 = lane_sum(jnp.where(oh2[s], cum, 0.0))
        routes = jnp.where(lane == 0, i1[s] - ROUTE_LANE0,
                           jnp.where(lane == 1, i2[s] - ROUTE_LANE0,
                                     jnp.where(lane == 2, rank1, jnp.where(lane == 3, rank2, 0.0))))
        ri_ref[subs[s], :] = routes.astype(I32)
        rf_ref[subs[s], :] = jnp.where(lane == 0, gate1[s], jnp.where(lane == 1, gate2[s], 0.0))
        base = base + jnp.sum(oh[s], axis=0, keepdims=True)
    carry[...] = base
    cnt_ref[...] = jnp.broadcast_to(base, cnt_ref.shape)


def _pos_kernel(ri_ref, cnt_ref, post_ref, blk_ref, ends_ref, *, tm):
    shift = EXPERT_BLOCK.bit_length() - 1
    cnt = cnt_ref[...].astype(I32)
    padded = ((cnt + (EXPERT_BLOCK - 1)) >> shift) << shift
    lane8 = lax.broadcasted_iota(I32, padded.shape, 1)
    pend = padded
    s = 1
    while s < LANES:
        pend = pend + jnp.where(lane8 >= s, pltpu.roll(pend, s, 1), 0)
        s *= 2
    pstart = (pend - padded)[0:1, :].astype(F32)

    ri = ri_ref[...]
    lane = lax.broadcasted_iota(I32, ri.shape, 1)
    e1 = ri[:, 0:1] + ROUTE_LANE0
    e2 = ri[:, 1:2] + ROUTE_LANE0
    ps1 = jnp.sum(jnp.where(lane == e1, pstart, 0.0), axis=-1, keepdims=True)
    ps2 = jnp.sum(jnp.where(lane == e2, pstart, 0.0), axis=-1, keepdims=True)
    pos1 = ps1 + ri[:, 2:3].astype(F32)
    pos2 = ps2 + ri[:, 3:4].astype(F32)
    posmat = jnp.where(lane == 0, pos1, jnp.where(lane == 1, pos2, 0.0))
    for j in range(tm // LANES):
        pt = posmat[j * LANES:(j + 1) * LANES, :].T
        post_ref[:, j * LANES:(j + 1) * LANES] = pt[0:8, :].astype(I32)

    pendf = pend.astype(F32)[0:1, :]
    rr = lax.broadcasted_iota(I32, (LANES, LANES), 0)
    cc = lax.broadcasted_iota(I32, (LANES, LANES), 1)
    pend_col = jnp.sum(jnp.where(rr == cc, pendf, 0.0), axis=1, keepdims=True)
    nbl = blk_ref.shape[1]
    er = lax.broadcasted_iota(I32, (LANES, nbl), 0)
    jb = (lax.broadcasted_iota(I32, (LANES, nbl), 1) * EXPERT_BLOCK).astype(F32)
    valid = (er >= ROUTE_LANE0) & (er < ROUTE_LANE0 + N_EXPERTS)
    be = jnp.sum(jnp.where(valid & (pend_col <= jb), 1.0, 0.0), axis=0, keepdims=True)
    be = jnp.minimum(be, float(N_EXPERTS - 1))
    nused = jnp.max(pendf, axis=-1, keepdims=True) * (1.0 / EXPERT_BLOCK)
    rowi = lax.broadcasted_iota(I32, blk_ref.shape, 0)
    blk_ref[...] = jnp.where(rowi == 0, be, nused).astype(I32)
    cend_row = pstart + cnt_ref[0:1, :]
    rowe = lax.broadcasted_iota(I32, ends_ref.shape, 0)
    ends_ref[...] = jnp.where(rowe == 0, cend_row, pendf).astype(I32)


def _slots_kernel(cend_ref, pend_ref, pos1_ref, pos2_ref, slot_ref, *, tm):
    i = pl.program_id(0)
    F = 8

    def fill_range(lo, hi):
        def body(c, carry):
            for u in range(F):
                slot_ref[jnp.minimum(lo + c * F + u, hi - 1)] = 0
            return carry

        lax.fori_loop(0, (hi - lo + (F - 1)) // F, body, 0)

    @pl.when(i == 0)
    def _():
        def per_expert(e, carry):
            fill_range(cend_ref[e], pend_ref[e])
            return carry

        lax.fori_loop(0, N_EXPERTS, per_expert, 0)
        fill_range(pend_ref[N_EXPERTS - 1], slot_ref.shape[0])

    def body(r, carry):
        t = i * tm + r
        slot_ref[pos1_ref[0, 0, r]] = t
        slot_ref[pos2_ref[0, 0, r]] = t
        return carry

    lax.fori_loop(0, tm, body, 0, unroll=16)


def _expert_kernel(be_ref, nu_ref, slc_ref, sln_ref, xn_ref, wup_ref, wdn_ref, y_ref,
                   wupb, wdnb, xbuf, gsem):
    j = pl.program_id(0)
    cur = j % 2
    nxt = 1 - cur
    n_used = nu_ref[0]
    R = TOKEN_TILE_ROWS
    buf_rows = EXPERT_BLOCK * R

    def gather_row(slot_ref, b, r, prio):
        pltpu.make_async_copy(xn_ref.at[pl.ds(slot_ref[0, 0, r] * R, R), :],
                              xbuf.at[pl.ds(b * buf_rows + r * R, R), :], gsem.at[b]).start(priority=prio)

    def wait_buffer(sem):
        pltpu.make_async_copy(xn_ref.at[pl.ds(0, buf_rows), :], xbuf.at[pl.ds(0, buf_rows), :], sem).wait()

    @pl.when(j == 0)
    def _():
        def body(g, carry):
            for u in range(8):
                gather_row(slc_ref, 0, g * 8 + u, u % 2)
            return carry

        lax.fori_loop(0, EXPERT_BLOCK // 8, body, 0)

    prev = be_ref[jnp.maximum(j - 1, 0)]

    @pl.when((j == 0) | (be_ref[j] != prev))
    def _():
        wupb[...] = wup_ref[...].astype(BF16)
        wdnb[...] = wdn_ref[...].astype(BF16)

    @pl.when(j < n_used)
    def _():
        wait_buffer(gsem.at[cur])
        base = cur * buf_rows
        x = jnp.concatenate([xbuf[pl.ds(base + f, EXPERT_BLOCK, stride=R), :] for f in range(R)], axis=1)
        ab = _dot(x.astype(BF16), wupb[...])
        for r in range(EXPERT_BLOCK):
            gather_row(sln_ref, nxt, r, r % 2)
        hmid = _silu(ab[:, :D_EXPERT]) * ab[:, D_EXPERT:]
        y = _dot(hmid.astype(BF16), wdnb[...])
        for f in range(R):
            y_ref[pl.ds(f, EXPERT_BLOCK, stride=R), :] = y[:, f * LANES:(f + 1) * LANES]

    @pl.when(j == n_used)
    def _():
        wait_buffer(gsem.at[cur])

    @pl.when(j >= n_used)
    def _():
        y_ref[...] = jnp.zeros(y_ref.shape, F32)


def _ple_kernel(p1c_ref, p2c_ref, p1n_ref, p2n_ref, h1_ref, rf_ref, p_ref, ys_ref, gple_ref, wg_ref, wp_ref,
                gfin_ref, out_ref, ybuf, sem):
    i = pl.program_id(0)
    last = pl.num_programs(0) - 1
    cur = i % 2
    nxt = 1 - cur
    tm = h1_ref.shape[0]
    R = TOKEN_TILE_ROWS
    slot_rows = 2 * tm * R

    def gather_tile(pos1_ref, pos2_ref, b):
        def body(g, carry):
            for u in range(8):
                r = g * 8 + u
                for k, pos_ref in enumerate((pos1_ref, pos2_ref)):
                    pltpu.make_async_copy(ys_ref.at[pl.ds(pos_ref[0, 0, r] * R, R), :],
                                          ybuf.at[pl.ds(b * slot_rows + (k * tm + r) * R, R), :],
                                          sem.at[b]).start(priority=(u + k) % 2)
            return carry

        lax.fori_loop(0, tm // 8, body, 0)

    def wait_slot(b):
        pltpu.make_async_copy(ys_ref.at[pl.ds(0, slot_rows), :], ybuf.at[pl.ds(0, slot_rows), :], sem.at[b]).wait()

    @pl.when(i == 0)
    def _():
        gather_tile(p1c_ref, p2c_ref, 0)

    gather_tile(p1n_ref, p2n_ref, nxt)
    pp = _dot(p_ref[...].astype(BF16), wp_ref[...])
    wait_slot(cur)
    rows = lambda k: jnp.concatenate(
        [ybuf[pl.ds(cur * slot_rows + k * tm * R + f, tm, stride=R), :] for f in range(R)], axis=1)
    rf = rf_ref[...]
    h2 = h1_ref[...] + (rf[:, 0:1] * rows(0) + rf[:, 1:2] * rows(1))
    hn = h2 * lax.rsqrt(jnp.mean(h2 * h2, axis=-1, keepdims=True) + EPS) * gple_ref[...]
    gate = jax.nn.sigmoid(_dot(hn.astype(BF16), wg_ref[...]))
    h3 = h2 + gate * pp
    out_ref[...] = h3 * lax.rsqrt(jnp.mean(h3 * h3, axis=-1, keepdims=True) + EPS) * gfin_ref[...]

    @pl.when(i == last)
    def _():
        wait_slot(nxt)


def _params(sem):
    return pltpu.CompilerParams(dimension_semantics=sem, vmem_limit_bytes=VMEM_LIMIT)


def _full(shape):
    nd = len(shape)
    return pl.BlockSpec(shape, lambda *_: (0,) * nd)


def _pad_lanes(a, lane0=0):
    out = jnp.zeros((a.shape[0], LANES), a.dtype)
    return out.at[:, lane0:lane0 + a.shape[1]].set(a)


def _layer(x2, p2, B, T, g_mix, w_in, w_conv_qkv, a_log, dt_bias, g_gdn_out, b_glu, w_conv_dw,
           b_conv_dw, ln_conv_g, ln_conv_b, w_out, g_moe, w_group_router, w_expert_router, w_up,
           w_down, g_ple, w_ple_gate, w_ple_proj, g_out):
    N, D = x2.shape
    hw = GDN_HEADS * GDN_DK
    tm = min(512, T)
    nt = N // tm

    o1 = 3 * hw
    o2 = o1 + hw
    o4 = o2 + 2 * GDN_HEADS
    wqkv = w_in[:, :o1].astype(BF16)
    wz = w_in[:, o1:o2].astype(BF16)
    wbah, wbal = _split_bf16(_pad_lanes(w_in[:, o2:o4]))
    wglu = w_in[:, o4:].astype(BF16)
    pa = jnp.concatenate([_pad_lanes(a_log[None, :], GDN_HEADS), _pad_lanes(dt_bias[None, :], GDN_HEADS),
                          jnp.zeros((6, LANES), F32)], axis=0)
    cdw = jnp.concatenate([w_conv_dw, jnp.zeros((1, CONF_CH), F32)], axis=0)
    cvec = jnp.concatenate([b_conv_dw[None], ln_conv_g[None], ln_conv_b[None], jnp.zeros((5, CONF_CH), F32)], 0)

    row_spec = lambda w: pl.BlockSpec((tm, w), lambda i: (i, 0))
    q, k, v, bg, zs, yc = pl.pallas_call(
        functools.partial(_inproj_kernel, tiles_per_seq=T // tm, tm=tm),
        grid=(nt,),
        in_specs=[row_spec(D), _full((1, D)), _full(wqkv.shape), _full(wz.shape), _full(wbah.shape),
                  _full(wbal.shape), _full(wglu.shape), _full(w_conv_qkv.shape), _full(pa.shape),
                  _full((1, 2 * CONF_CH)), _full(cdw.shape), _full(cvec.shape)],
        out_specs=[row_spec(hw), row_spec(hw), row_spec(hw), row_spec(LANES), row_spec(hw), row_spec(CONF_CH)],
        out_shape=[jax.ShapeDtypeStruct((N, hw), F32)] * 3 + [jax.ShapeDtypeStruct((N, LANES), F32),
                   jax.ShapeDtypeStruct((N, hw), BF16), jax.ShapeDtypeStruct((N, CONF_CH), BF16)],
        scratch_shapes=[pltpu.VMEM((tm + QKV_HALO, 3 * hw), F32), pltpu.VMEM((tm + CONF_HALO, CONF_CH), F32)],
        compiler_params=_params(("arbitrary",)),
        name="inproj",
    )(x2, g_mix[None], wqkv, wz, wbah, wbal, wglu, w_conv_qkv, pa, b_glu[None], cdw, cvec)

    cb = 8
    rows = cb * GDN_CHUNK
    seq_spec = lambda w: pl.BlockSpec((B, rows, w), lambda n: (0, n, 0))
    r3 = lambda a: a.reshape(B, T, a.shape[-1])
    ygdn = pl.pallas_call(
        functools.partial(_gdn_kernel, cb=cb, nbatch=B, group=8 * GDN_HEADS),
        grid=(T // rows,),
        in_specs=[seq_spec(hw), seq_spec(hw), seq_spec(hw), seq_spec(LANES), seq_spec(hw),
                  pl.BlockSpec((1, GDN_DK), lambda n: (0, 0))],
        out_specs=seq_spec(hw),
        out_shape=jax.ShapeDtypeStruct((B, T, hw), BF16),
        scratch_shapes=[pltpu.VMEM((B * GDN_HEADS, GDN_DK, GDN_DK), F32)],
        compiler_params=_params(("arbitrary",)),
        name="gdn",
    )(r3(q), r3(k), r3(v), r3(bg), r3(zs), g_gdn_out[None]).reshape(N, hw)

    wo1 = w_out[:hw].astype(BF16)
    wo2 = w_out[hw:].astype(BF16)
    wrh, wrl = _split_bf16(_pad_lanes(jnp.concatenate([w_group_router, w_expert_router], axis=1)))
    nsub = 2 if tm % 256 == 0 else 1
    ltri = jnp.tril(jnp.ones((tm // nsub, tm // nsub), F32), -1).astype(BF16)
    assert D == TOKEN_TILE_ROWS * LANES
    tile_spec = pl.BlockSpec((tm * TOKEN_TILE_ROWS, LANES), lambda i: (i, 0))
    h1, xn, ri, rf, cnt = pl.pallas_call(
        functools.partial(_router_kernel, nsub=nsub),
        grid=(nt,),
        in_specs=[row_spec(D), row_spec(hw), row_spec(CONF_CH), _full(wo1.shape), _full(wo2.shape),
                  _full((1, D)), _full(wrh.shape), _full(wrl.shape), _full(ltri.shape)],
        out_specs=[row_spec(D), tile_spec, row_spec(LANES), row_spec(LANES), _full((8, LANES))],
        out_shape=[jax.ShapeDtypeStruct((N, D), F32), jax.ShapeDtypeStruct((N * TOKEN_TILE_ROWS, LANES), F32),
                   jax.ShapeDtypeStruct((N, LANES), I32), jax.ShapeDtypeStruct((N, LANES), F32),
                   jax.ShapeDtypeStruct((8, LANES), F32)],
        scratch_shapes=[pltpu.VMEM((1, LANES), F32)],
        compiler_params=_params(("arbitrary",)),
        name="router",
    )(x2, ygdn, yc, wo1, wo2, g_moe[None], wrh, wrl, ltri)

    nb = (2 * N) // EXPERT_BLOCK + N_EXPERTS
    nbl = -(-nb // LANES) * LANES
    post, blk, ends = pl.pallas_call(
        functools.partial(_pos_kernel, tm=tm),
        grid=(nt,),
        in_specs=[row_spec(LANES), _full((8, LANES))],
        out_specs=[pl.BlockSpec((8, tm), lambda i: (0, i)), _full((8, nbl)), _full((8, LANES))],
        out_shape=[jax.ShapeDtypeStruct((8, N), I32), jax.ShapeDtypeStruct((8, nbl), I32),
                   jax.ShapeDtypeStruct((8, LANES), I32)],
        compiler_params=_params(("arbitrary",)),
        name="positions",
    )(ri, cnt)
    pos1 = post[0].reshape(nt, 1, tm)
    pos2 = post[1].reshape(nt, 1, tm)
    block_expert = blk[0, :nb]
    n_used = blk[1, :1]

    P = nb * EXPERT_BLOCK
    real_end = ends[0, ROUTE_LANE0:ROUTE_LANE0 + N_EXPERTS]
    padded_end = ends[1, ROUTE_LANE0:ROUTE_LANE0 + N_EXPERTS]
    smem_spec = pl.BlockSpec((1, 1, tm), lambda i, *_: (i, 0, 0), memory_space=pltpu.SMEM)
    flat_smem = pl.BlockSpec((P,), lambda i, *_: (0,), memory_space=pltpu.SMEM)
    slots = pl.pallas_call(
        functools.partial(_slots_kernel, tm=tm),
        grid_spec=pltpu.PrefetchScalarGridSpec(
            num_scalar_prefetch=2, grid=(nt,), in_specs=[smem_spec, smem_spec], out_specs=flat_smem),
        out_shape=jax.ShapeDtypeStruct((P,), I32),
        compiler_params=_params(("arbitrary",)),
        name="slots",
    )(real_end, padded_end, pos1, pos2).reshape(nb, 1, EXPERT_BLOCK)

    any_spec = pl.BlockSpec(memory_space=pl.ANY)
    blk_smem = lambda off: pl.BlockSpec((1, 1, EXPERT_BLOCK),
                                        lambda j, be, nu: (jnp.minimum(j + off, nb - 1), 0, 0),
                                        memory_space=pltpu.SMEM)
    ys = pl.pallas_call(
        _expert_kernel,
        grid_spec=pltpu.PrefetchScalarGridSpec(
            num_scalar_prefetch=2,
            grid=(nb,),
            in_specs=[blk_smem(0), blk_smem(1), any_spec,
                      pl.BlockSpec((None, D, 2 * D_EXPERT), lambda j, be, nu: (be[j], 0, 0)),
                      pl.BlockSpec((None, D_EXPERT, D), lambda j, be, nu: (be[j], 0, 0))],
            out_specs=pl.BlockSpec((EXPERT_BLOCK * TOKEN_TILE_ROWS, LANES), lambda j, be, nu: (j, 0)),
            scratch_shapes=[pltpu.VMEM((D, 2 * D_EXPERT), BF16), pltpu.VMEM((D_EXPERT, D), BF16),
                            pltpu.VMEM((2 * EXPERT_BLOCK * TOKEN_TILE_ROWS, LANES), F32),
                            pltpu.SemaphoreType.DMA((2,))]),
        out_shape=jax.ShapeDtypeStruct((P * TOKEN_TILE_ROWS, LANES), F32),
        compiler_params=_params(("arbitrary",)),
        name="experts",
    )(block_expert, n_used, slots, slots, xn, w_up, w_down)

    wg = w_ple_gate.astype(BF16)
    wp = w_ple_proj.astype(BF16)
    pos_spec = lambda off: pl.BlockSpec((1, 1, tm), lambda i: (jnp.minimum(i + off, nt - 1), 0, 0),
                                        memory_space=pltpu.SMEM)
    out = pl.pallas_call(
        _ple_kernel,
        grid=(nt,),
        in_specs=[pos_spec(0), pos_spec(0), pos_spec(1), pos_spec(1),
                  row_spec(D), row_spec(LANES), row_spec(p2.shape[1]), any_spec,
                  _full((1, D)), _full(wg.shape), _full(wp.shape), _full((1, D))],
        out_specs=row_spec(D),
        out_shape=jax.ShapeDtypeStruct((N, D), F32),
        scratch_shapes=[pltpu.VMEM((2 * 2 * tm * TOKEN_TILE_ROWS, LANES), F32), pltpu.SemaphoreType.DMA((2,))],
        compiler_params=_params(("arbitrary",)),
        name="ple",
    )(pos1, pos2, pos1, pos2, h1, rf, p2, ys, g_ple[None], wg, wp, g_out[None])
    return out


def kernel(x, p, g_mix, w_in, w_conv_qkv, a_log, dt_bias, g_gdn_out, b_glu, w_conv_dw, b_conv_dw, ln_conv_g, ln_conv_b, w_out, g_moe, w_group_router, w_expert_router, w_up, w_down, g_ple, w_ple_gate, w_ple_proj, g_final):
    B, T, D = x.shape
    depth = p.shape[0]
    assert depth == 1, "the final RMSNorm is fused into the (single) layer's last kernel"
    h = _layer(x.reshape(B * T, D), p[0].reshape(B * T, -1), B, T, g_mix[0], w_in[0], w_conv_qkv[0],
               a_log[0], dt_bias[0], g_gdn_out[0], b_glu[0], w_conv_dw[0], b_conv_dw[0], ln_conv_g[0],
               ln_conv_b[0], w_out[0], g_moe[0], w_group_router[0], w_expert_router[0], w_up[0],
               w_down[0], g_ple[0], w_ple_gate[0], w_ple_proj[0], g_final)
    return h.reshape(B, T, D)
```

```python
import functools

import jax
import jax.numpy as jnp
from jax import lax
from jax.experimental import pallas as pl
from jax.experimental.pallas import tpu as pltpu

F32 = jnp.float32
BF16 = jnp.bfloat16
I32 = jnp.int32

EPS = 1e-6
GDN_HEADS = 4
GDN_DK = 128
GDN_CONV = 4
GDN_CHUNK = 64
CONF_CH = 512
CONF_WIDTH = 31
N_GROUPS = 8
EXPERTS_PER_GROUP = 8
N_EXPERTS = 64
D_EXPERT = 256
LANES = 128
SUBLANES = 8
TOKEN_TILE_ROWS = 8
ROUTE_LANE0 = N_GROUPS
EXPERT_BLOCK = 256
GATHER_RING = 3
QKV_HALO = 8
CONF_HALO = 32
NEG = -1e30
VMEM_LIMIT = 56 * 1024 * 1024


def _dot(a, b):
    return jnp.dot(a, b, preferred_element_type=F32)


def _dot_nt(a, b):
    return lax.dot_general(a, b, (((1,), (1,)), ((), ())), preferred_element_type=F32)


def _dot_tn(a, b):
    return lax.dot_general(a, b, (((0,), (0,)), ((), ())), preferred_element_type=F32)


def _silu(x):
    return x * jax.nn.sigmoid(x)


def _split_bf16(x):
    hi = x.astype(BF16)
    lo = (x - hi.astype(F32)).astype(BF16)
    return hi, lo


def _causal_taps(ext_ref, w_ref, n_taps, halo, tm, cols):
    rows = ext_ref.shape[0]
    ext = ext_ref[:, cols]
    first = halo - (n_taps - 1)
    acc = None
    for res in range(SUBLANES):
        taps = [k for k in range(n_taps) if (first + k) % SUBLANES == res]
        if not taps:
            continue
        shifted = ext if res == 0 else pltpu.roll(ext, rows - res, 0)
        for k in taps:
            a = first + k - res
            term = w_ref[k:k + 1, cols] * shifted[a:a + tm, :]
            acc = term if acc is None else acc + term
    return acc


def _inproj_kernel(x_ref, gmix_ref, wqkv_ref, wz_ref, wbah_ref, wbal_ref, wglu_ref, cw_ref, pa_ref,
                   bglu_ref, cdw_ref, cvec_ref,
                   q_ref, k_ref, v_ref, bg_ref, zs_ref, yc_ref,
                   extq, extu, *, tiles_per_seq, tm):
    i = pl.program_id(0)
    first = (i % tiles_per_seq) == 0

    x = x_ref[...]
    n1 = x * lax.rsqrt(jnp.mean(x * x, axis=-1, keepdims=True) + EPS) * gmix_ref[...]
    n1h, n1l = _split_bf16(n1)

    @pl.when(first)
    def _():
        extq[0:QKV_HALO, :] = jnp.zeros((QKV_HALO, extq.shape[1]), F32)
        extu[0:CONF_HALO, :] = jnp.zeros((CONF_HALO, extu.shape[1]), F32)

    @pl.when(jnp.logical_not(first))
    def _():
        extq[0:QKV_HALO, :] = extq[tm:tm + QKV_HALO, :]
        extu[0:CONF_HALO, :] = extu[tm:tm + CONF_HALO, :]

    extq[QKV_HALO:QKV_HALO + tm, :] = _dot(n1h, wqkv_ref[...])
    hw = GDN_HEADS * GDN_DK
    qc = _silu(_causal_taps(extq, cw_ref, GDN_CONV, QKV_HALO, tm, slice(0, hw)))
    kc = _silu(_causal_taps(extq, cw_ref, GDN_CONV, QKV_HALO, tm, slice(hw, 2 * hw)))
    for h in range(GDN_HEADS):
        cs = slice(h * GDN_DK, (h + 1) * GDN_DK)
        qh = qc[:, cs]
        q_ref[:, cs] = qh * lax.rsqrt(jnp.sum(qh * qh, axis=-1, keepdims=True) + EPS) * (GDN_DK ** -0.5)
        kh = kc[:, cs]
        k_ref[:, cs] = kh * lax.rsqrt(jnp.sum(kh * kh, axis=-1, keepdims=True) + EPS)
    v_ref[...] = _silu(_causal_taps(extq, cw_ref, GDN_CONV, QKV_HALO, tm, slice(2 * hw, 3 * hw)))

    zs_ref[...] = _silu(_dot(n1h, wz_ref[...])).astype(BF16)

    ba = _dot(n1h, wbah_ref[...]) + _dot(n1l, wbah_ref[...]) + _dot(n1h, wbal_ref[...])
    lane = lax.broadcasted_iota(I32, ba.shape, 1)
    row = lax.broadcasted_iota(I32, ba.shape, 0)
    beta = jax.nn.sigmoid(ba)
    sp_in = ba + pa_ref[1:2, :]
    softplus = jnp.maximum(sp_in, 0.0) + jnp.log1p(jnp.exp(-jnp.abs(sp_in)))
    g = -jnp.exp(pa_ref[0:1, :]) * softplus
    rin = row & (GDN_CHUNK - 1)
    s = 1
    while s < GDN_CHUNK:
        g = g + jnp.where(rin >= s, pltpu.roll(g, s, 0), 0.0)
        s *= 2
    bg_ref[...] = jnp.where(lane < GDN_HEADS, beta, g)

    glu = _dot(n1h, wglu_ref[...]) + bglu_ref[...]
    extu[CONF_HALO:CONF_HALO + tm, :] = glu[:, :CONF_CH] * jax.nn.sigmoid(glu[:, CONF_CH:])
    c = _causal_taps(extu, cdw_ref, CONF_WIDTH, CONF_HALO, tm, slice(0, CONF_CH)) + cvec_ref[0:1, :]
    mu = jnp.mean(c, axis=-1, keepdims=True)
    cc = c - mu
    var = jnp.mean(cc * cc, axis=-1, keepdims=True)
    yln = cc * lax.rsqrt(var + EPS) * cvec_ref[1:2, :] + cvec_ref[2:3, :]
    yc_ref[...] = _silu(yln).astype(BF16)


def _gdn_kernel(q_ref, k_ref, v_ref, bg_ref, zs_ref, gout_ref, y_ref, s_ref, *, cb, nbatch, group):
    n = pl.program_id(0)

    @pl.when(n == 0)
    def _():
        s_ref[...] = jnp.zeros(s_ref.shape, F32)

    C = GDN_CHUNK
    r = lax.broadcasted_iota(I32, (C, C), 0)
    c = lax.broadcasted_iota(I32, (C, C), 1)
    causal = r >= c
    strict = r > c
    eye = r == c
    eyef = jnp.where(eye, 1.0, 0.0).astype(F32)
    gout = gout_ref[...]

    def front(b, ci, h):
        rows = slice(ci * C, (ci + 1) * C)
        cols = slice(h * GDN_DK, (h + 1) * GDN_DK)
        q = q_ref[b, rows, cols]
        k = k_ref[b, rows, cols]
        v = v_ref[b, rows, cols]
        beta = bg_ref[b, rows, h:h + 1]
        gc = bg_ref[b, rows, GDN_HEADS + h:GDN_HEADS + h + 1]
        gc_row = jnp.sum(jnp.where(eye, gc, 0.0), axis=0, keepdims=True)
        gc_last = gc[C - 1:C, :]
        eg = jnp.exp(gc)
        kb = k * beta
        return dict(
            decay=jnp.where(causal, jnp.exp(jnp.minimum(gc - gc_row, 0.0)), 0.0),
            lhs=jnp.concatenate([kb, q], axis=0).astype(BF16),
            kbf=k.astype(BF16),
            rhs=jnp.concatenate([v * beta, kb * eg], axis=1).astype(BF16),
            qd=q * eg,
            kd=(k * jnp.exp(gc_last - gc)).astype(BF16),
            gl=jnp.exp(gc_last))

    insts = [(b, ci, h) for ci in range(cb) for b in range(nbatch) for h in range(GDN_HEADS)]
    done = {}
    for g0 in range(0, len(insts), group):
        grp = insts[g0:g0 + group]
        st = [front(*key) for key in grp]
        kq = [_dot_nt(s["lhs"], s["kbf"]) for s in st]
        xp = [-jnp.where(strict, m[:C] * s["decay"], 0.0) for m, s in zip(kq, st)]
        qk = [(m[C:] * s["decay"]).astype(BF16) for m, s in zip(kq, st)]
        tinv = [eyef + x for x in xp]
        for _ in range(5):
            xpb = [x.astype(BF16) for x in xp]
            xp = [_dot(x, x) for x in xpb]
            tinv = [t + _dot(t.astype(BF16), x.astype(BF16)) for t, x in zip(tinv, xp)]
        uw = [_dot(t.astype(BF16), s["rhs"]) for t, s in zip(tinv, st)]
        for key, s, m, qkm in zip(grp, st, uw, qk):
            done[key] = dict(u=m[:, :GDN_DK],
                             wq=jnp.concatenate([m[:, GDN_DK:], s["qd"]], axis=0).astype(BF16),
                             qk=qkm, kd=s["kd"], gl=s["gl"])

    chains = [(b, h) for b in range(nbatch) for h in range(GDN_HEADS)]
    for ci in range(cb):
        rows = slice(ci * C, (ci + 1) * C)
        dd = [done[(b, ci, h)] for b, h in chains]
        s_old = [s_ref[b * GDN_HEADS + h] for b, h in chains]
        wq = [_dot(d["wq"], s.astype(BF16)) for d, s in zip(dd, s_old)]
        v_new = [(d["u"] - m[:C]).astype(BF16) for d, m in zip(dd, wq)]
        o = [m[C:] + _dot(d["qk"], vn) for d, m, vn in zip(dd, wq, v_new)]
        s_new = [s * d["gl"] + _dot_tn(d["kd"], vn) for d, s, vn in zip(dd, s_old, v_new)]
        for (b, h), sn, oo in zip(chains, s_new, o):
            cols = slice(h * GDN_DK, (h + 1) * GDN_DK)
            s_ref[b * GDN_HEADS + h] = sn
            on = oo * lax.rsqrt(jnp.mean(oo * oo, axis=-1, keepdims=True) + EPS) * gout
            y_ref[b, rows, cols] = (on * zs_ref[b, rows, cols].astype(F32)).astype(BF16)


def _router_kernel(x_ref, yg_ref, yc_ref, wo1_ref, wo2_ref, gmoe_ref, wrh_ref, wrl_ref, ltri_ref,
                   h1_ref, xn_ref, ri_ref, rf_ref, cnt_ref, carry, *, nsub):
    i = pl.program_id(0)

    @pl.when(i == 0)
    def _():
        carry[...] = jnp.zeros(carry.shape, F32)

    ts = x_ref.shape[0] // nsub
    subs = [slice(s * ts, (s + 1) * ts) for s in range(nsub)]
    each = lambda fn, *lists: [fn(*args) for args in zip(*lists)]
    lane_sum = lambda v: jnp.sum(v, axis=-1, keepdims=True)
    lane_max = lambda v: jnp.max(v, axis=-1, keepdims=True)
    lane_min = lambda v: jnp.min(v, axis=-1, keepdims=True)

    h1 = [x_ref[sl, :] + _dot(yg_ref[sl, :], wo1_ref[...]) + _dot(yc_ref[sl, :], wo2_ref[...]) for sl in subs]
    for sl, h in zip(subs, h1):
        h1_ref[sl, :] = h
    xn = each(lambda h: h * lax.rsqrt(jnp.mean(h * h, axis=-1, keepdims=True) + EPS) * gmoe_ref[...], h1)
    parts = each(_split_bf16, xn)
    for s, v in enumerate(xn):
        for f in range(TOKEN_TILE_ROWS):
            xn_ref[pl.ds(s * ts * TOKEN_TILE_ROWS + f, ts, stride=TOKEN_TILE_ROWS), :] = v[:, f * LANES:(f + 1) * LANES]
    lg = [_dot(hi, wrh_ref[...]) + _dot(lo, wrh_ref[...]) + _dot(hi, wrl_ref[...]) for hi, lo in parts]

    lane = lax.broadcasted_iota(I32, lg[0].shape, 1)
    lanef = lane.astype(F32)
    none = float(LANES)
    gl = each(lambda v: jnp.where(lane < N_GROUPS, v, NEG), lg)
    gmax = each(lane_max, gl)
    g_w = each(lambda v, m: 1.0 / lane_sum(jnp.exp(v - m)), gl, gmax)
    gidx = each(lambda v, m: lane_min(jnp.where(v == m, lanef, none)), gl, gmax)
    lane_group = ((lane - ROUTE_LANE0) >> 3).astype(F32)
    is_expert = (lane >= ROUTE_LANE0) & (lane < ROUTE_LANE0 + N_EXPERTS)
    el = each(lambda v, g: jnp.where(is_expert & (lane_group == g), v, NEG), lg, gidx)
    m1 = each(lane_max, el)
    i1 = each(lambda v, m: lane_min(jnp.where(v == m, lanef, none)), el, m1)
    el2 = each(lambda v, a: jnp.where(lanef == a, NEG, v), el, i1)
    m2 = each(lane_max, el2)
    i2 = each(lambda v, m: lane_min(jnp.where(v == m, lanef, none)), el2, m2)
    ratio = each(lambda a, b: jnp.exp(b - a), m1, m2)
    gate1 = each(lambda w, r: w / (1.0 + r), g_w, ratio)
    gate2 = each(lambda w, r: w * r / (1.0 + r), g_w, ratio)

    oh1 = each(lambda a: lanef == a, i1)
    oh2 = each(lambda a: lanef == a, i2)
    oh = each(lambda a, b: jnp.where(a | b, 1.0, 0.0).astype(F32), oh1, oh2)
    within = each(lambda v: _dot(ltri_ref[...], v.astype(BF16)), oh)
    base = carry[...]
    for s in range(nsub):
        cum = within[s] + base
        rank1 = lane_sum(jnp.where(oh1[s], cum, 0.0))
        rank2 = lane_sum(jnp.where(oh2[s], cum, 0.0))
        routes = jnp.where(lane == 0, i1[s] - ROUTE_LANE0,
                           jnp.where(lane == 1, i2[s] - ROUTE_LANE0,
                                     jnp.where(lane == 2, rank1, jnp.where(lane == 3, rank2, 0.0))))
        ri_ref[subs[s], :] = routes.astype(I32)
        rf_ref[subs[s], :] = jnp.where(lane == 0, gate1[s], jnp.where(lane == 1, gate2[s], 0.0))
        base = base + jnp.sum(oh[s], axis=0, keepdims=True)
    carry[...] = base
    cnt_ref[...] = jnp.broadcast_to(base, cnt_ref.shape)


def _pos_kernel(ri_ref, cnt_ref, post_ref, blk_ref, ends_ref, *, tm):
    shift = EXPERT_BLOCK.bit_length() - 1
    cnt = cnt_ref[...].astype(I32)
    padded = ((cnt + (EXPERT_BLOCK - 1)) >> shift) << shift
    lane8 = lax.broadcasted_iota(I32, padded.shape, 1)
    pend = padded
    s = 1
    while s < LANES:
        pend = pend + jnp.where(lane8 >= s, pltpu.roll(pend, s, 1), 0)
        s *= 2
    pstart = (pend - padded)[0:1, :].astype(F32)

    ri = ri_ref[...]
    lane = lax.broadcasted_iota(I32, ri.shape, 1)
    e1 = ri[:, 0:1] + ROUTE_LANE0
    e2 = ri[:, 1:2] + ROUTE_LANE0
    ps1 = jnp.sum(jnp.where(lane == e1, pstart, 0.0), axis=-1, keepdims=True)
    ps2 = jnp.sum(jnp.where(lane == e2, pstart, 0.0), axis=-1, keepdims=True)
    pos1 = ps1 + ri[:, 2:3].astype(F32)
    pos2 = ps2 + ri[:, 3:4].astype(F32)
    posmat = jnp.where(lane == 0, pos1, jnp.where(lane == 1, pos2, 0.0))
    for j in range(tm // LANES):
        pt = posmat[j * LANES:(j + 1) * LANES, :].T
        post_ref[:, j * LANES:(j + 1) * LANES] = pt[0:8, :].astype(I32)

    pendf = pend.astype(F32)[0:1, :]
    rr = lax.broadcasted_iota(I32, (LANES, LANES), 0)
    cc = lax.broadcasted_iota(I32, (LANES, LANES), 1)
    pend_col = jnp.sum(jnp.where(rr == cc, pendf, 0.0), axis=1, keepdims=True)
    nbl = blk_ref.shape[1]
    er = lax.broadcasted_iota(I32, (LANES, nbl), 0)
    jb = (lax.broadcasted_iota(I32, (LANES, nbl), 1) * EXPERT_BLOCK).astype(F32)
    valid = (er >= ROUTE_LANE0) & (er < ROUTE_LANE0 + N_EXPERTS)
    be = jnp.sum(jnp.where(valid & (pend_col <= jb), 1.0, 0.0), axis=0, keepdims=True)
    be = jnp.minimum(be, float(N_EXPERTS - 1))
    nused = jnp.max(pendf, axis=-1, keepdims=True) * (1.0 / EXPERT_BLOCK)
    rowi = lax.broadcasted_iota(I32, blk_ref.shape, 0)
    blk_ref[...] = jnp.where(rowi == 0, be, nused).astype(I32)
    cend_row = pstart + cnt_ref[0:1, :]
    rowe = lax.broadcasted_iota(I32, ends_ref.shape, 0)
    ends_ref[...] = jnp.where(rowe == 0, cend_row, pendf).astype(I32)


def _slots_kernel(cend_ref, pend_ref, pos1_ref, pos2_ref, slot_ref, *, tm):
    i = pl.program_id(0)
    F = 8

    def fill_range(lo, hi):
        def body(c, carry):
            for u in range(F):
                p = jnp.minimum(lo + c * F + u, hi - 1)
                slot_ref[p] = p & (EXPERT_BLOCK - 1)
            return carry

        lax.fori_loop(0, (hi - lo + (F - 1)) // F, body, 0)

    @pl.when(i == 0)
    def _():
        def per_expert(e, carry):
            fill_range(cend_ref[e], pend_ref[e])
            return carry

        lax.fori_loop(0, N_EXPERTS, per_expert, 0)
        fill_range(pend_ref[N_EXPERTS - 1], slot_ref.shape[0])

    def body(r, carry):
        t = i * tm + r
        slot_ref[pos1_ref[0, 0, r]] = t
        slot_ref[pos2_ref[0, 0, r]] = t
        return carry

    lax.fori_loop(0, tm, body, 0, unroll=16)


def _expert_kernel(be_ref, nu_ref, sl0_ref, sl1_ref, sl2_ref, xn_ref, wup_ref, wdn_ref, y_ref,
                   wupb, wdnb, xbuf, gsem):
    j = pl.program_id(0)
    cur = lax.rem(j, GATHER_RING)
    n_used = nu_ref[0]
    R = TOKEN_TILE_ROWS
    buf_rows = EXPERT_BLOCK * R

    def gather_row(slot_ref, b, r, prio):
        pltpu.make_async_copy(xn_ref.at[pl.ds(slot_ref[0, 0, r] * R, R), :],
                              xbuf.at[pl.ds(b * buf_rows + r * R, R), :], gsem.at[b]).start(priority=prio)

    def wait_buffer(b):
        pltpu.make_async_copy(xn_ref.at[pl.ds(0, buf_rows), :], xbuf.at[pl.ds(0, buf_rows), :], gsem.at[b]).wait()

    @pl.when(j == 0)
    def _():
        def body(g, carry):
            for u in range(8):
                gather_row(sl0_ref, 0, g * 8 + u, u % 2)
                gather_row(sl1_ref, 1, g * 8 + u, (u + 1) % 2)
            return carry

        lax.fori_loop(0, EXPERT_BLOCK // 8, body, 0)

    prev = be_ref[jnp.maximum(j - 1, 0)]

    @pl.when((j == 0) | (be_ref[j] != prev))
    def _():
        wupb[...] = wup_ref[...].astype(BF16)
        wdnb[...] = wdn_ref[...].astype(BF16)

    @pl.when(j < n_used)
    def _():
        wait_buffer(cur)
        base = cur * buf_rows
        x = jnp.concatenate([xbuf[pl.ds(base + f, EXPERT_BLOCK, stride=R), :] for f in range(R)], axis=1)
        ab = _dot(x.astype(BF16), wupb[...])
        ahead = lax.rem(j + (GATHER_RING - 1), GATHER_RING)
        for r in range(EXPERT_BLOCK):
            gather_row(sl2_ref, ahead, r, r % 2)
        hmid = _silu(ab[:, :D_EXPERT]) * ab[:, D_EXPERT:]
        y = _dot(hmid.astype(BF16), wdnb[...])
        for f in range(R):
            y_ref[pl.ds(f, EXPERT_BLOCK, stride=R), :] = y[:, f * LANES:(f + 1) * LANES]

    @pl.when(j == n_used)
    def _():
        wait_buffer(cur)
        wait_buffer(lax.rem(j + 1, GATHER_RING))

    @pl.when(j >= n_used)
    def _():
        y_ref[...] = jnp.zeros(y_ref.shape, F32)


def _ple_kernel(p1c_ref, p2c_ref, p1n_ref, p2n_ref, h1_ref, rf_ref, p_ref, ys_ref, gple_ref, wg_ref, wp_ref,
                gfin_ref, out_ref, ybuf, sem):
    i = pl.program_id(0)
    last = pl.num_programs(0) - 1
    cur = i % 2
    nxt = 1 - cur
    tm = h1_ref.shape[0]
    R = TOKEN_TILE_ROWS
    slot_rows = 2 * tm * R

    def gather_tile(pos1_ref, pos2_ref, b):
        def body(g, carry):
            for u in range(8):
                r = g * 8 + u
                for k, pos_ref in enumerate((pos1_ref, pos2_ref)):
                    pltpu.make_async_copy(ys_ref.at[pl.ds(pos_ref[0, 0, r] * R, R), :],
                                          ybuf.at[pl.ds(b * slot_rows + (k * tm + r) * R, R), :],
                                          sem.at[b]).start(priority=(u + k) % 2)
            return carry

        lax.fori_loop(0, tm // 8, body, 0)

    def wait_slot(b):
        pltpu.make_async_copy(ys_ref.at[pl.ds(0, slot_rows), :], ybuf.at[pl.ds(0, slot_rows), :], sem.at[b]).wait()

    @pl.when(i == 0)
    def _():
        gather_tile(p1c_ref, p2c_ref, 0)

    gather_tile(p1n_ref, p2n_ref, nxt)
    pp = _dot(p_ref[...].astype(BF16), wp_ref[...])
    wait_slot(cur)
    rows = lambda k: jnp.concatenate(
        [ybuf[pl.ds(cur * slot_rows + k * tm * R + f, tm, stride=R), :] for f in range(R)], axis=1)
    rf = rf_ref[...]
    h2 = h1_ref[...] + (rf[:, 0:1] * rows(0) + rf[:, 1:2] * rows(1))
    hn = h2 * lax.rsqrt(jnp.mean(h2 * h2, axis=-1, keepdims=True) + EPS) * gple_ref[...]
    gate = jax.nn.sigmoid(_dot(hn.astype(BF16), wg_ref[...]))
    h3 = h2 + gate * pp
    out_ref[...] = h3 * lax.rsqrt(jnp.mean(h3 * h3, axis=-1, keepdims=True) + EPS) * gfin_ref[...]

    @pl.when(i == last)
    def _():
        wait_slot(nxt)


def _params(sem):
    return pltpu.CompilerParams(dimension_semantics=sem, vmem_limit_bytes=VMEM_LIMIT)


def _full(shape):
    nd = len(shape)
    return pl.BlockSpec(shape, lambda *_: (0,) * nd)


def _pad_lanes(a, lane0=0):
    out = jnp.zeros((a.shape[0], LANES), a.dtype)
    return out.at[:, lane0:lane0 + a.shape[1]].set(a)


def _layer(x2, p2, B, T, g_mix, w_in, w_conv_qkv, a_log, dt_bias, g_gdn_out, b_glu, w_conv_dw,
           b_conv_dw, ln_conv_g, ln_conv_b, w_out, g_moe, w_group_router, w_expert_router, w_up,
           w_down, g_ple, w_ple_gate, w_ple_proj, g_out):
    N, D = x2.shape
    hw = GDN_HEADS * GDN_DK
    tm = min(512, T)
    nt = N // tm

    o1 = 3 * hw
    o2 = o1 + hw
    o4 = o2 + 2 * GDN_HEADS
    wqkv = w_in[:, :o1].astype(BF16)
    wz = w_in[:, o1:o2].astype(BF16)
    wbah, wbal = _split_bf16(_pad_lanes(w_in[:, o2:o4]))
    wglu = w_in[:, o4:].astype(BF16)
    pa = jnp.concatenate([_pad_lanes(a_log[None, :], GDN_HEADS), _pad_lanes(dt_bias[None, :], GDN_HEADS),
                          jnp.zeros((6, LANES), F32)], axis=0)
    cdw = jnp.concatenate([w_conv_dw, jnp.zeros((1, CONF_CH), F32)], axis=0)
    cvec = jnp.concatenate([b_conv_dw[None], ln_conv_g[None], ln_conv_b[None], jnp.zeros((5, CONF_CH), F32)], 0)

    row_spec = lambda w: pl.BlockSpec((tm, w), lambda i: (i, 0))
    q, k, v, bg, zs, yc = pl.pallas_call(
        functools.partial(_inproj_kernel, tiles_per_seq=T // tm, tm=tm),
        grid=(nt,),
        in_specs=[row_spec(D), _full((1, D)), _full(wqkv.shape), _full(wz.shape), _full(wbah.shape),
                  _full(wbal.shape), _full(wglu.shape), _full(w_conv_qkv.shape), _full(pa.shape),
                  _full((1, 2 * CONF_CH)), _full(cdw.shape), _full(cvec.shape)],
        out_specs=[row_spec(hw), row_spec(hw), row_spec(hw), row_spec(LANES), row_spec(hw), row_spec(CONF_CH)],
        out_shape=[jax.ShapeDtypeStruct((N, hw), F32)] * 3 + [jax.ShapeDtypeStruct((N, LANES), F32),
                   jax.ShapeDtypeStruct((N, hw), BF16), jax.ShapeDtypeStruct((N, CONF_CH), BF16)],
        scratch_shapes=[pltpu.VMEM((tm + QKV_HALO, 3 * hw), F32), pltpu.VMEM((tm + CONF_HALO, CONF_CH), F32)],
        compiler_params=_params(("arbitrary",)),
        name="inproj",
    )(x2, g_mix[None], wqkv, wz, wbah, wbal, wglu, w_conv_qkv, pa, b_glu[None], cdw, cvec)

    cb = 8
    rows = cb * GDN_CHUNK
    seq_spec = lambda w: pl.BlockSpec((B, rows, w), lambda n: (0, n, 0))
    r3 = lambda a: a.reshape(B, T, a.shape[-1])
    ygdn = pl.pallas_call(
        functools.partial(_gdn_kernel, cb=cb, nbatch=B, group=8 * GDN_HEADS),
        grid=(T // rows,),
        in_specs=[seq_spec(hw), seq_spec(hw), seq_spec(hw), seq_spec(LANES), seq_spec(hw),
                  pl.BlockSpec((1, GDN_DK), lambda n: (0, 0))],
        out_specs=seq_spec(hw),
        out_shape=jax.ShapeDtypeStruct((B, T, hw), BF16),
        scratch_shapes=[pltpu.VMEM((B * GDN_HEADS, GDN_DK, GDN_DK), F32)],
        compiler_params=_params(("arbitrary",)),
        name="gdn",
    )(r3(q), r3(k), r3(v), r3(bg), r3(zs), g_gdn_out[None]).reshape(N, hw)

    wo1 = w_out[:hw].astype(BF16)
    wo2 = w_out[hw:].astype(BF16)
    wrh, wrl = _split_bf16(_pad_lanes(jnp.concatenate([w_group_router, w_expert_router], axis=1)))
    nsub = 2 if tm % 256 == 0 else 1
    ltri = jnp.tril(jnp.ones((tm // nsub, tm // nsub), F32), -1).astype(BF16)
    assert D == TOKEN_TILE_ROWS * LANES
    tile_spec = pl.BlockSpec((tm * TOKEN_TILE_ROWS, LANES), lambda i: (i, 0))
    h1, xn, ri, rf, cnt = pl.pallas_call(
        functools.partial(_router_kernel, nsub=nsub),
        grid=(nt,),
        in_specs=[row_spec(D), row_spec(hw), row_spec(CONF_CH), _full(wo1.shape), _full(wo2.shape),
                  _full((1, D)), _full(wrh.shape), _full(wrl.shape), _full(ltri.shape)],
        out_specs=[row_spec(D), tile_spec, row_spec(LANES), row_spec(LANES), _full((8, LANES))],
        out_shape=[jax.ShapeDtypeStruct((N, D), F32), jax.ShapeDtypeStruct((N * TOKEN_TILE_ROWS, LANES), F32),
                   jax.ShapeDtypeStruct((N, LANES), I32), jax.ShapeDtypeStruct((N, LANES), F32),
                   jax.ShapeDtypeStruct((8, LANES), F32)],
        scratch_shapes=[pltpu.VMEM((1, LANES), F32)],
        compiler_params=_params(("arbitrary",)),
        name="router",
    )(x2, ygdn, yc, wo1, wo2, g_moe[None], wrh, wrl, ltri)

    nb = (2 * N) // EXPERT_BLOCK + N_EXPERTS
    nbl = -(-nb // LANES) * LANES
    post, blk, ends = pl.pallas_call(
        functools.partial(_pos_kernel, tm=tm),
        grid=(nt,),
        in_specs=[row_spec(LANES), _full((8, LANES))],
        out_specs=[pl.BlockSpec((8, tm), lambda i: (0, i)), _full((8, nbl)), _full((8, LANES))],
        out_shape=[jax.ShapeDtypeStruct((8, N), I32), jax.ShapeDtypeStruct((8, nbl), I32),
                   jax.ShapeDtypeStruct((8, LANES), I32)],
        compiler_params=_params(("arbitrary",)),
        name="positions",
    )(ri, cnt)
    pos1 = post[0].reshape(nt, 1, tm)
    pos2 = post[1].reshape(nt, 1, tm)
    block_expert = blk[0, :nb]
    n_used = blk[1, :1]

    P = nb * EXPERT_BLOCK
    real_end = ends[0, ROUTE_LANE0:ROUTE_LANE0 + N_EXPERTS]
    padded_end = ends[1, ROUTE_LANE0:ROUTE_LANE0 + N_EXPERTS]
    smem_spec = pl.BlockSpec((1, 1, tm), lambda i, *_: (i, 0, 0), memory_space=pltpu.SMEM)
    flat_smem = pl.BlockSpec((P,), lambda i, *_: (0,), memory_space=pltpu.SMEM)
    slots = pl.pallas_call(
        functools.partial(_slots_kernel, tm=tm),
        grid_spec=pltpu.PrefetchScalarGridSpec(
            num_scalar_prefetch=2, grid=(nt,), in_specs=[smem_spec, smem_spec], out_specs=flat_smem),
        out_shape=jax.ShapeDtypeStruct((P,), I32),
        compiler_params=_params(("arbitrary",)),
        name="slots",
    )(real_end, padded_end, pos1, pos2).reshape(nb, 1, EXPERT_BLOCK)

    any_spec = pl.BlockSpec(memory_space=pl.ANY)
    blk_smem = lambda off: pl.BlockSpec((1, 1, EXPERT_BLOCK),
                                        lambda j, be, nu: (jnp.minimum(j + off, nb - 1), 0, 0),
                                        memory_space=pltpu.SMEM)
    ys = pl.pallas_call(
        _expert_kernel,
        grid_spec=pltpu.PrefetchScalarGridSpec(
            num_scalar_prefetch=2,
            grid=(nb,),
            in_specs=[blk_smem(0), blk_smem(1), blk_smem(2), any_spec,
                      pl.BlockSpec((None, D, 2 * D_EXPERT), lambda j, be, nu: (be[j], 0, 0)),
                      pl.BlockSpec((None, D_EXPERT, D), lambda j, be, nu: (be[j], 0, 0))],
            out_specs=pl.BlockSpec((EXPERT_BLOCK * TOKEN_TILE_ROWS, LANES), lambda j, be, nu: (j, 0)),
            scratch_shapes=[pltpu.VMEM((D, 2 * D_EXPERT), BF16), pltpu.VMEM((D_EXPERT, D), BF16),
                            pltpu.VMEM((GATHER_RING * EXPERT_BLOCK * TOKEN_TILE_ROWS, LANES), F32),
                            pltpu.SemaphoreType.DMA((GATHER_RING,))]),
        out_shape=jax.ShapeDtypeStruct((P * TOKEN_TILE_ROWS, LANES), F32),
        compiler_params=_params(("arbitrary",)),
        name="experts",
    )(block_expert, n_used, slots, slots, slots, xn, w_up, w_down)

    wg = w_ple_gate.astype(BF16)
    wp = w_ple_proj.astype(BF16)
    pos_spec = lambda off: pl.BlockSpec((1, 1, tm), lambda i: (jnp.minimum(i + off, nt - 1), 0, 0),
                                        memory_space=pltpu.SMEM)
    out = pl.pallas_call(
        _ple_kernel,
        grid=(nt,),
        in_specs=[pos_spec(0), pos_spec(0), pos_spec(1), pos_spec(1),
                  row_spec(D), row_spec(LANES), row_spec(p2.shape[1]), any_spec,
                  _full((1, D)), _full(wg.shape), _full(wp.shape), _full((1, D))],
        out_specs=row_spec(D),
        out_shape=jax.ShapeDtypeStruct((N, D), F32),
        scratch_shapes=[pltpu.VMEM((2 * 2 * tm * TOKEN_TILE_ROWS, LANES), F32), pltpu.SemaphoreType.DMA((2,))],
        compiler_params=_params(("arbitrary",)),
        name="ple",
    )(pos1, pos2, pos1, pos2, h1, rf, p2, ys, g_ple[None], wg, wp, g_out[None])
    return out


def kernel(x, p, g_mix, w_in, w_conv_qkv, a_log, dt_bias, g_gdn_out, b_glu, w_conv_dw, b_conv_dw, ln_conv_g, ln_conv_b, w_out, g_moe, w_group_router, w_expert_router, w_up, w_down, g_ple, w_ple_gate, w_ple_proj, g_final):
    B, T, D = x.shape
    depth = p.shape[0]
    assert depth == 1, "the final RMSNorm is fused into the (single) layer's last kernel"
    h = _layer(x.reshape(B * T, D), p[0].reshape(B * T, -1), B, T, g_mix[0], w_in[0], w_conv_qkv[0],
               a_log[0], dt_bias[0], g_gdn_out[0], b_glu[0], w_conv_dw[0], b_conv_dw[0], ln_conv_g[0],
               ln_conv_b[0], w_out[0], g_moe[0], w_group_router[0], w_expert_router[0], w_up[0],
               w_down[0], g_ple[0], w_ple_gate[0], w_ple_proj[0], g_final)
    return h.reshape(B, T, D)
```

```python
import functools

import jax
import jax.numpy as jnp
from jax import lax
from jax.experimental import pallas as pl
from jax.experimental.pallas import tpu as pltpu

F32 = jnp.float32
BF16 = jnp.bfloat16
I32 = jnp.int32

EPS = 1e-6
GDN_HEADS = 4
GDN_DK = 128
GDN_CONV = 4
GDN_CHUNK = 64
CONF_CH = 512
CONF_WIDTH = 31
N_GROUPS = 8
EXPERTS_PER_GROUP = 8
N_EXPERTS = 64
D_EXPERT = 256
LANES = 128
SUBLANES = 8
TOKEN_TILE_ROWS = 8
ROUTE_LANE0 = N_GROUPS
EXPERT_BLOCK = 256
GATHER_RING = 3
QKV_HALO = 8
CONF_HALO = 32
NEG = -1e30
VMEM_LIMIT = 56 * 1024 * 1024


def _dot(a, b):
    return jnp.dot(a, b, preferred_element_type=F32)


def _dot_nt(a, b):
    return lax.dot_general(a, b, (((1,), (1,)), ((), ())), preferred_element_type=F32)


def _dot_tn(a, b):
    return lax.dot_general(a, b, (((0,), (0,)), ((), ())), preferred_element_type=F32)


def _silu(x):
    return x * jax.nn.sigmoid(x)


def _split_bf16(x):
    hi = x.astype(BF16)
    lo = (x - hi.astype(F32)).astype(BF16)
    return hi, lo


def _causal_taps(ext_ref, w_ref, n_taps, halo, tm, cols):
    rows = ext_ref.shape[0]
    ext = ext_ref[:, cols]
    first = halo - (n_taps - 1)
    acc = None
    for res in range(SUBLANES):
        taps = [k for k in range(n_taps) if (first + k) % SUBLANES == res]
        if not taps:
            continue
        shifted = ext if res == 0 else pltpu.roll(ext, rows - res, 0)
        for k in taps:
            a = first + k - res
            term = w_ref[k:k + 1, cols] * shifted[a:a + tm, :]
            acc = term if acc is None else acc + term
    return acc


def _inproj_kernel(x_ref, gmix_ref, wqkv_ref, wz_ref, wbah_ref, wbal_ref, wglu_ref, cw_ref, pa_ref,
                   bglu_ref, cdw_ref, cvec_ref,
                   q_ref, k_ref, v_ref, bg_ref, zs_ref, yc_ref,
                   extq, extu, *, tiles_per_seq, tm):
    i = pl.program_id(0)
    first = (i % tiles_per_seq) == 0

    x = x_ref[...]
    n1 = x * lax.rsqrt(jnp.mean(x * x, axis=-1, keepdims=True) + EPS) * gmix_ref[...]
    n1h, n1l = _split_bf16(n1)

    @pl.when(first)
    def _():
        extq[0:QKV_HALO, :] = jnp.zeros((QKV_HALO, extq.shape[1]), F32)
        extu[0:CONF_HALO, :] = jnp.zeros((CONF_HALO, extu.shape[1]), F32)

    @pl.when(jnp.logical_not(first))
    def _():
        extq[0:QKV_HALO, :] = extq[tm:tm + QKV_HALO, :]
        extu[0:CONF_HALO, :] = extu[tm:tm + CONF_HALO, :]

    extq[QKV_HALO:QKV_HALO + tm, :] = _dot(n1h, wqkv_ref[...])
    hw = GDN_HEADS * GDN_DK
    qc = _silu(_causal_taps(extq, cw_ref, GDN_CONV, QKV_HALO, tm, slice(0, hw)))
    kc = _silu(_causal_taps(extq, cw_ref, GDN_CONV, QKV_HALO, tm, slice(hw, 2 * hw)))
    for h in range(GDN_HEADS):
        cs = slice(h * GDN_DK, (h + 1) * GDN_DK)
        qh = qc[:, cs]
        q_ref[:, cs] = qh * lax.rsqrt(jnp.sum(qh * qh, axis=-1, keepdims=True) + EPS) * (GDN_DK ** -0.5)
        kh = kc[:, cs]
        k_ref[:, cs] = kh * lax.rsqrt(jnp.sum(kh * kh, axis=-1, keepdims=True) + EPS)
    v_ref[...] = _silu(_causal_taps(extq, cw_ref, GDN_CONV, QKV_HALO, tm, slice(2 * hw, 3 * hw)))

    zs_ref[...] = _silu(_dot(n1h, wz_ref[...])).astype(BF16)

    ba = _dot(n1h, wbah_ref[...]) + _dot(n1l, wbah_ref[...]) + _dot(n1h, wbal_ref[...])
    lane = lax.broadcasted_iota(I32, ba.shape, 1)
    row = lax.broadcasted_iota(I32, ba.shape, 0)
    beta = jax.nn.sigmoid(ba)
    sp_in = ba + pa_ref[1:2, :]
    softplus = jnp.maximum(sp_in, 0.0) + jnp.log1p(jnp.exp(-jnp.abs(sp_in)))
    g = -jnp.exp(pa_ref[0:1, :]) * softplus
    rin = row & (GDN_CHUNK - 1)
    s = 1
    while s < GDN_CHUNK:
        g = g + jnp.where(rin >= s, pltpu.roll(g, s, 0), 0.0)
        s *= 2
    bg_ref[...] = jnp.where(lane < GDN_HEADS, beta, g)

    glu = _dot(n1h, wglu_ref[...]) + bglu_ref[...]
    extu[CONF_HALO:CONF_HALO + tm, :] = glu[:, :CONF_CH] * jax.nn.sigmoid(glu[:, CONF_CH:])
    c = _causal_taps(extu, cdw_ref, CONF_WIDTH, CONF_HALO, tm, slice(0, CONF_CH)) + cvec_ref[0:1, :]
    mu = jnp.mean(c, axis=-1, keepdims=True)
    cc = c - mu
    var = jnp.mean(cc * cc, axis=-1, keepdims=True)
    yln = cc * lax.rsqrt(var + EPS) * cvec_ref[1:2, :] + cvec_ref[2:3, :]
    yc_ref[...] = _silu(yln).astype(BF16)


def _gdn_kernel(q_ref, k_ref, v_ref, bg_ref, zs_ref, gout_ref, y_ref, s_ref, *, cb, nbatch, group):
    n = pl.program_id(0)

    @pl.when(n == 0)
    def _():
        s_ref[...] = jnp.zeros(s_ref.shape, F32)

    C = GDN_CHUNK
    r = lax.broadcasted_iota(I32, (C, C), 0)
    c = lax.broadcasted_iota(I32, (C, C), 1)
    causal = r >= c
    strict = r > c
    eye = r == c
    eyef = jnp.where(eye, 1.0, 0.0).astype(F32)
    gout = gout_ref[...]

    def front(b, ci, h):
        rows = slice(ci * C, (ci + 1) * C)
        cols = slice(h * GDN_DK, (h + 1) * GDN_DK)
        q = q_ref[b, rows, cols]
        k = k_ref[b, rows, cols]
        v = v_ref[b, rows, cols]
        beta = bg_ref[b, rows, h:h + 1]
        gc = bg_ref[b, rows, GDN_HEADS + h:GDN_HEADS + h + 1]
        gc_row = jnp.sum(jnp.where(eye, gc, 0.0), axis=0, keepdims=True)
        gc_last = gc[C - 1:C, :]
        eg = jnp.exp(gc)
        kb = k * beta
        return dict(
            decay=jnp.where(causal, jnp.exp(jnp.minimum(gc - gc_row, 0.0)), 0.0),
            lhs=jnp.concatenate([kb, q], axis=0).astype(BF16),
            kbf=k.astype(BF16),
            rhs=jnp.concatenate([v * beta, kb * eg], axis=1).astype(BF16),
            qd=q * eg,
            kd=(k * jnp.exp(gc_last - gc)).astype(BF16),
            gl=jnp.exp(gc_last))

    insts = [(b, ci, h) for ci in range(cb) for b in range(nbatch) for h in range(GDN_HEADS)]
    done = {}
    for g0 in range(0, len(insts), group):
        grp = insts[g0:g0 + group]
        st = [front(*key) for key in grp]
        kq = [_dot_nt(s["lhs"], s["kbf"]) for s in st]
        xp = [-jnp.where(strict, m[:C] * s["decay"], 0.0) for m, s in zip(kq, st)]
        qk = [(m[C:] * s["decay"]).astype(BF16) for m, s in zip(kq, st)]
        tinv = [eyef + x for x in xp]
        for _ in range(5):
            xpb = [x.astype(BF16) for x in xp]
            xp = [_dot(x, x) for x in xpb]
            tinv = [t + _dot(t.astype(BF16), x.astype(BF16)) for t, x in zip(tinv, xp)]
        uw = [_dot(t.astype(BF16), s["rhs"]) for t, s in zip(tinv, st)]
        for key, s, m, qkm in zip(grp, st, uw, qk):
            done[key] = dict(u=m[:, :GDN_DK],
                             wq=jnp.concatenate([m[:, GDN_DK:], s["qd"]], axis=0).astype(BF16),
                             qk=qkm, kd=s["kd"], gl=s["gl"])

    chains = [(b, h) for b in range(nbatch) for h in range(GDN_HEADS)]
    for ci in range(cb):
        rows = slice(ci * C, (ci + 1) * C)
        dd = [done[(b, ci, h)] for b, h in chains]
        s_old = [s_ref[b * GDN_HEADS + h] for b, h in chains]
        wq = [_dot(d["wq"], s.astype(BF16)) for d, s in zip(dd, s_old)]
        v_new = [(d["u"] - m[:C]).astype(BF16) for d, m in zip(dd, wq)]
        o = [m[C:] + _dot(d["qk"], vn) for d, m, vn in zip(dd, wq, v_new)]
        s_new = [s * d["gl"] + _dot_tn(d["kd"], vn) for d, s, vn in zip(dd, s_old, v_new)]
        for (b, h), sn, oo in zip(chains, s_new, o):
            cols = slice(h * GDN_DK, (h + 1) * GDN_DK)
            s_ref[b * GDN_HEADS + h] = sn
            on = oo * lax.rsqrt(jnp.mean(oo * oo, axis=-1, keepdims=True) + EPS) * gout
            y_ref[b, rows, cols] = (on * zs_ref[b, rows, cols].astype(F32)).astype(BF16)


def _router_kernel(x_ref, yg_ref, yc_ref, wo1_ref, wo2_ref, gmoe_ref, wrh_ref, wrl_ref, ltri_ref,
                   h1_ref, xn_ref, ri_ref, rf_ref, cnt_ref, carry, *, nsub):
    i = pl.program_id(0)

    @pl.when(i == 0)
    def _():
        carry[...] = jnp.zeros(carry.shape, F32)

    ts = x_ref.shape[0] // nsub
    subs = [slice(s * ts, (s + 1) * ts) for s in range(nsub)]
    each = lambda fn, *lists: [fn(*args) for args in zip(*lists)]
    lane_sum = lambda v: jnp.sum(v, axis=-1, keepdims=True)
    lane_max = lambda v: jnp.max(v, axis=-1, keepdims=True)
    lane_min = lambda v: jnp.min(v, axis=-1, keepdims=True)

    h1 = [x_ref[sl, :] + _dot(yg_ref[sl, :], wo1_ref[...]) + _dot(yc_ref[sl, :], wo2_ref[...]) for sl in subs]
    for sl, h in zip(subs, h1):
        h1_ref[sl, :] = h
    xn = each(lambda h: h * lax.rsqrt(jnp.mean(h * h, axis=-1, keepdims=True) + EPS) * gmoe_ref[...], h1)
    parts = each(_split_bf16, xn)
    for s, v in enumerate(xn):
        for f in range(TOKEN_TILE_ROWS):
            xn_ref[pl.ds(s * ts * TOKEN_TILE_ROWS + f, ts, stride=TOKEN_TILE_ROWS), :] = v[:, f * LANES:(f + 1) * LANES]
    lg = [_dot(hi, wrh_ref[...]) + _dot(lo, wrh_ref[...]) + _dot(hi, wrl_ref[...]) for hi, lo in parts]

    lane = lax.broadcasted_iota(I32, lg[0].shape, 1)
    lanef = lane.astype(F32)
    none = float(LANES)
    gl = each(lambda v: jnp.where(lane < N_GROUPS, v, NEG), lg)
    gmax = each(lane_max, gl)
    g_w = each(lambda v, m: 1.0 / lane_sum(jnp.exp(v - m)), gl, gmax)
    gidx = each(lambda v, m: lane_min(jnp.where(v == m, lanef, none)), gl, gmax)
    lane_group = ((lane - ROUTE_LANE0) >> 3).astype(F32)
    is_expert = (lane >= ROUTE_LANE0) & (lane < ROUTE_LANE0 + N_EXPERTS)
    el = each(lambda v, g: jnp.where(is_expert & (lane_group == g), v, NEG), lg, gidx)
    m1 = each(lane_max, el)
    i1 = each(lambda v, m: lane_min(jnp.where(v == m, lanef, none)), el, m1)
    el2 = each(lambda v, a: jnp.where(lanef == a, NEG, v), el, i1)
    m2 = each(lane_max, el2)
    i2 = each(lambda v, m: lane_min(jnp.where(v == m, lanef, none)), el2, m2)
    ratio = each(lambda a, b: jnp.exp(b - a), m1, m2)
    gate1 = each(lambda w, r: w / (1.0 + r), g_w, ratio)
    gate2 = each(lambda w, r: w * r / (1.0 + r), g_w, ratio)

    oh1 = each(lambda a: lanef == a, i1)
    oh2 = each(lambda a: lanef == a, i2)
    oh = each(lambda a, b: jnp.where(a | b, 1.0, 0.0).astype(F32), oh1, oh2)
    within = each(lambda v: _dot(ltri_ref[...], v.astype(BF16)), oh)
    base = carry[...]
    for s in range(nsub):
        cum = within[s] + base
        rank1 = lane_sum(jnp.where(oh1[s], cum, 0.0))
        rank2 = lane_sum(jnp.where(oh2[s], cum, 0.0))
        routes = jnp.where(lane == 0, i1[s] - ROUTE_LANE0,
                           jnp.where(lane == 1, i2[s] - ROUTE_LANE0,
                                     jnp.where(lane == 2, rank1, jnp.where(lane == 3, rank2, 0.0))))
        ri_ref[subs[s], :] = routes.astype(I32)
        rf_ref[subs[s], :] = jnp.where(lane == 0, gate1[s], jnp.where(lane == 1, gate2[s], 0.0))
        base = base + jnp.sum(oh[s], axis=0, keepdims=True)
    carry[...] = base
    cnt_ref[...] = jnp.broadcast_to(base, cnt_ref.shape)


def _pos_kernel(ri_ref, cnt_ref, post_ref, blk_ref, ends_ref, *, tm):
    shift = EXPERT_BLOCK.bit_length() - 1
    cnt = cnt_ref[...].astype(I32)
    padded = ((cnt + (EXPERT_BLOCK - 1)) >> shift) << shift
    lane8 = lax.broadcasted_iota(I32, padded.shape, 1)
    pend = padded
    s = 1
    while s < LANES:
        pend = pend + jnp.where(lane8 >= s, pltpu.roll(pend, s, 1), 0)
        s *= 2
    pstart = (pend - padded)[0:1, :].astype(F32)

    ri = ri_ref[...]
    lane = lax.broadcasted_iota(I32, ri.shape, 1)
    e1 = ri[:, 0:1] + ROUTE_LANE0
    e2 = ri[:, 1:2] + ROUTE_LANE0
    ps1 = jnp.sum(jnp.where(lane == e1, pstart, 0.0), axis=-1, keepdims=True)
    ps2 = jnp.sum(jnp.where(lane == e2, pstart, 0.0), axis=-1, keepdims=True)
    pos1 = ps1 + ri[:, 2:3].astype(F32)
    pos2 = ps2 + ri[:, 3:4].astype(F32)
    posmat = jnp.where(lane == 0, pos1, jnp.where(lane == 1, pos2, 0.0))
    for j in range(tm // LANES):
        pt = posmat[j * LANES:(j + 1) * LANES, :].T
        post_ref[:, j * LANES:(j + 1) * LANES] = pt[0:8, :].astype(I32)

    @pl.when(pl.program_id(0) == 0)
    def _():
        pendf = pend.astype(F32)[0:1, :]
        rr = lax.broadcasted_iota(I32, (LANES, LANES), 0)
        cc = lax.broadcasted_iota(I32, (LANES, LANES), 1)
        pend_col = jnp.sum(jnp.where(rr == cc, pendf, 0.0), axis=1, keepdims=True)
        nbl = blk_ref.shape[1]
        er = lax.broadcasted_iota(I32, (LANES, nbl), 0)
        jb = (lax.broadcasted_iota(I32, (LANES, nbl), 1) * EXPERT_BLOCK).astype(F32)
        valid = (er >= ROUTE_LANE0) & (er < ROUTE_LANE0 + N_EXPERTS)
        be = jnp.sum(jnp.where(valid & (pend_col <= jb), 1.0, 0.0), axis=0, keepdims=True)
        be = jnp.minimum(be, float(N_EXPERTS - 1))
        nused = jnp.max(pendf, axis=-1, keepdims=True) * (1.0 / EXPERT_BLOCK)
        rowi = lax.broadcasted_iota(I32, blk_ref.shape, 0)
        blk_ref[...] = jnp.where(rowi == 0, be, nused).astype(I32)
        cend_row = pstart + cnt_ref[0:1, :]
        rowe = lax.broadcasted_iota(I32, ends_ref.shape, 0)
        ends_ref[...] = jnp.where(rowe == 0, cend_row, pendf).astype(I32)


def _slots_kernel(cend_ref, pend_ref, pos1_ref, pos2_ref, slot_ref, *, tm):
    i = pl.program_id(0)
    F = 8

    def fill_range(lo, hi):
        def body(c, carry):
            for u in range(F):
                p = jnp.minimum(lo + c * F + u, hi - 1)
                slot_ref[p] = p & (EXPERT_BLOCK - 1)
            return carry

        lax.fori_loop(0, (hi - lo + (F - 1)) // F, body, 0)

    @pl.when(i == 0)
    def _():
        def per_expert(e, carry):
            fill_range(cend_ref[e], pend_ref[e])
            return carry

        lax.fori_loop(0, N_EXPERTS, per_expert, 0)
        fill_range(pend_ref[N_EXPERTS - 1], slot_ref.shape[0])

    def body(r, carry):
        t = i * tm + r
        slot_ref[pos1_ref[0, 0, r]] = t
        slot_ref[pos2_ref[0, 0, r]] = t
        return carry

    lax.fori_loop(0, tm, body, 0, unroll=16)


def _expert_kernel(be_ref, nu_ref, sl0_ref, sl1_ref, sl2_ref, xn_ref, wup_ref, wdn_ref, y_ref,
                   wupb, wdnb, wup_stage, wdn_stage, xbuf, gsem, wsem, wslot):
    j = pl.program_id(0)
    cur = lax.rem(j, GATHER_RING)
    n_used = nu_ref[0]
    R = TOKEN_TILE_ROWS
    buf_rows = EXPERT_BLOCK * R

    def gather_row(slot_ref, b, r, prio):
        pltpu.make_async_copy(xn_ref.at[pl.ds(slot_ref[0, 0, r] * R, R), :],
                              xbuf.at[pl.ds(b * buf_rows + r * R, R), :], gsem.at[b]).start(priority=prio)

    def wait_buffer(b):
        pltpu.make_async_copy(xn_ref.at[pl.ds(0, buf_rows), :], xbuf.at[pl.ds(0, buf_rows), :], gsem.at[b]).wait()

    @pl.when(j == 0)
    def _():
        def body(g, carry):
            for u in range(8):
                gather_row(sl0_ref, 0, g * 8 + u, u % 2)
                gather_row(sl1_ref, 1, g * 8 + u, (u + 1) % 2)
            return carry

        lax.fori_loop(0, EXPERT_BLOCK // 8, body, 0)

    def weight_copies(e, s):
        return (pltpu.make_async_copy(wup_ref.at[e], wup_stage.at[s], wsem.at[s]),
                pltpu.make_async_copy(wdn_ref.at[e], wdn_stage.at[s], wsem.at[s]))

    @pl.when(j == 0)
    def _():
        wslot[0] = 0
        for cp in weight_copies(be_ref[0], 0):
            cp.start()

    prev = be_ref[jnp.maximum(j - 1, 0)]

    @pl.when((j < n_used) & ((j == 0) | (be_ref[j] != prev)))
    def _():
        s = wslot[0]
        for cp in weight_copies(0, s):
            cp.wait()
        wupb[...] = wup_stage[s].astype(BF16)
        wdnb[...] = wdn_stage[s].astype(BF16)
        e = be_ref[j]
        k = lax.while_loop(lambda k: (k < n_used) & (be_ref[k] == e), lambda k: k + 1, j + 1)

        @pl.when(k < n_used)
        def _():
            for cp in weight_copies(be_ref[k], 1 - s):
                cp.start()

        wslot[0] = 1 - s

    @pl.when(j < n_used)
    def _():
        wait_buffer(cur)
        base = cur * buf_rows
        x = jnp.concatenate([xbuf[pl.ds(base + f, EXPERT_BLOCK, stride=R), :] for f in range(R)], axis=1)
        ab = _dot(x.astype(BF16), wupb[...])
        ahead = lax.rem(j + (GATHER_RING - 1), GATHER_RING)
        for r in range(EXPERT_BLOCK):
            gather_row(sl2_ref, ahead, r, r % 2)
        hmid = _silu(ab[:, :D_EXPERT]) * ab[:, D_EXPERT:]
        y = _dot(hmid.astype(BF16), wdnb[...])
        for f in range(R):
            y_ref[pl.ds(f, EXPERT_BLOCK, stride=R), :] = y[:, f * LANES:(f + 1) * LANES]

    @pl.when(j == n_used)
    def _():
        wait_buffer(cur)
        wait_buffer(lax.rem(j + 1, GATHER_RING))

    @pl.when(j >= n_used)
    def _():
        y_ref[...] = jnp.zeros(y_ref.shape, F32)


def _ple_kernel(p1c_ref, p2c_ref, p1n_ref, p2n_ref, h1_ref, rf_ref, p_ref, ys_ref, gple_ref, wg_ref, wp_ref,
                gfin_ref, out_ref, ybuf, sem):
    i = pl.program_id(0)
    last = pl.num_programs(0) - 1
    cur = i % 2
    nxt = 1 - cur
    tm = h1_ref.shape[0]
    R = TOKEN_TILE_ROWS
    slot_rows = 2 * tm * R

    def gather_row(pos_ref, b, k, r, prio):
        pltpu.make_async_copy(ys_ref.at[pl.ds(pos_ref[0, 0, r] * R, R), :],
                              ybuf.at[pl.ds(b * slot_rows + (k * tm + r) * R, R), :], sem.at[b]).start(priority=prio)

    def wait_slot(b):
        pltpu.make_async_copy(ys_ref.at[pl.ds(0, slot_rows), :], ybuf.at[pl.ds(0, slot_rows), :], sem.at[b]).wait()

    @pl.when(i == 0)
    def _():
        def body(g, carry):
            for u in range(8):
                gather_row(p1c_ref, 0, 0, g * 8 + u, u % 2)
                gather_row(p2c_ref, 0, 1, g * 8 + u, (u + 1) % 2)
            return carry

        lax.fori_loop(0, tm // 8, body, 0)

    pp = _dot(p_ref[...].astype(BF16), wp_ref[...])
    for r in range(tm):
        gather_row(p1n_ref, nxt, 0, r, r % 2)
        gather_row(p2n_ref, nxt, 1, r, (r + 1) % 2)
    wait_slot(cur)
    rows = lambda k: jnp.concatenate(
        [ybuf[pl.ds(cur * slot_rows + k * tm * R + f, tm, stride=R), :] for f in range(R)], axis=1)
    rf = rf_ref[...]
    h2 = h1_ref[...] + (rf[:, 0:1] * rows(0) + rf[:, 1:2] * rows(1))
    hn = h2 * lax.rsqrt(jnp.mean(h2 * h2, axis=-1, keepdims=True) + EPS) * gple_ref[...]
    gate = jax.nn.sigmoid(_dot(hn.astype(BF16), wg_ref[...]))
    h3 = h2 + gate * pp
    out_ref[...] = h3 * lax.rsqrt(jnp.mean(h3 * h3, axis=-1, keepdims=True) + EPS) * gfin_ref[...]

    @pl.when(i == last)
    def _():
        wait_slot(nxt)


def _params(sem):
    return pltpu.CompilerParams(dimension_semantics=sem, vmem_limit_bytes=VMEM_LIMIT)


def _full(shape):
    nd = len(shape)
    return pl.BlockSpec(shape, lambda *_: (0,) * nd)


def _pad_lanes(a, lane0=0):
    out = jnp.zeros((a.shape[0], LANES), a.dtype)
    return out.at[:, lane0:lane0 + a.shape[1]].set(a)


def _layer(x2, p2, B, T, g_mix, w_in, w_conv_qkv, a_log, dt_bias, g_gdn_out, b_glu, w_conv_dw,
           b_conv_dw, ln_conv_g, ln_conv_b, w_out, g_moe, w_group_router, w_expert_router, w_up,
           w_down, g_ple, w_ple_gate, w_ple_proj, g_out):
    N, D = x2.shape
    hw = GDN_HEADS * GDN_DK
    tm = min(512, T)
    nt = N // tm

    o1 = 3 * hw
    o2 = o1 + hw
    o4 = o2 + 2 * GDN_HEADS
    wqkv = w_in[:, :o1].astype(BF16)
    wz = w_in[:, o1:o2].astype(BF16)
    wbah, wbal = _split_bf16(_pad_lanes(w_in[:, o2:o4]))
    wglu = w_in[:, o4:].astype(BF16)
    pa = jnp.concatenate([_pad_lanes(a_log[None, :], GDN_HEADS), _pad_lanes(dt_bias[None, :], GDN_HEADS),
                          jnp.zeros((6, LANES), F32)], axis=0)
    cdw = jnp.concatenate([w_conv_dw, jnp.zeros((1, CONF_CH), F32)], axis=0)
    cvec = jnp.concatenate([b_conv_dw[None], ln_conv_g[None], ln_conv_b[None], jnp.zeros((5, CONF_CH), F32)], 0)

    row_spec = lambda w: pl.BlockSpec((tm, w), lambda i: (i, 0))
    q, k, v, bg, zs, yc = pl.pallas_call(
        functools.partial(_inproj_kernel, tiles_per_seq=T // tm, tm=tm),
        grid=(nt,),
        in_specs=[row_spec(D), _full((1, D)), _full(wqkv.shape), _full(wz.shape), _full(wbah.shape),
                  _full(wbal.shape), _full(wglu.shape), _full(w_conv_qkv.shape), _full(pa.shape),
                  _full((1, 2 * CONF_CH)), _full(cdw.shape), _full(cvec.shape)],
        out_specs=[row_spec(hw), row_spec(hw), row_spec(hw), row_spec(LANES), row_spec(hw), row_spec(CONF_CH)],
        out_shape=[jax.ShapeDtypeStruct((N, hw), F32)] * 3 + [jax.ShapeDtypeStruct((N, LANES), F32),
                   jax.ShapeDtypeStruct((N, hw), BF16), jax.ShapeDtypeStruct((N, CONF_CH), BF16)],
        scratch_shapes=[pltpu.VMEM((tm + QKV_HALO, 3 * hw), F32), pltpu.VMEM((tm + CONF_HALO, CONF_CH), F32)],
        compiler_params=_params(("arbitrary",)),
        name="inproj",
    )(x2, g_mix[None], wqkv, wz, wbah, wbal, wglu, w_conv_qkv, pa, b_glu[None], cdw, cvec)

    cb = 8
    rows = cb * GDN_CHUNK
    seq_spec = lambda w: pl.BlockSpec((B, rows, w), lambda n: (0, n, 0))
    r3 = lambda a: a.reshape(B, T, a.shape[-1])
    ygdn = pl.pallas_call(
        functools.partial(_gdn_kernel, cb=cb, nbatch=B, group=8 * GDN_HEADS),
        grid=(T // rows,),
        in_specs=[seq_spec(hw), seq_spec(hw), seq_spec(hw), seq_spec(LANES), seq_spec(hw),
                  pl.BlockSpec((1, GDN_DK), lambda n: (0, 0))],
        out_specs=seq_spec(hw),
        out_shape=jax.ShapeDtypeStruct((B, T, hw), BF16),
        scratch_shapes=[pltpu.VMEM((B * GDN_HEADS, GDN_DK, GDN_DK), F32)],
        compiler_params=_params(("arbitrary",)),
        name="gdn",
    )(r3(q), r3(k), r3(v), r3(bg), r3(zs), g_gdn_out[None]).reshape(N, hw)

    wo1 = w_out[:hw].astype(BF16)
    wo2 = w_out[hw:].astype(BF16)
    wrh, wrl = _split_bf16(_pad_lanes(jnp.concatenate([w_group_router, w_expert_router], axis=1)))
    nsub = 2 if tm % 256 == 0 else 1
    ltri = jnp.tril(jnp.ones((tm // nsub, tm // nsub), F32), -1).astype(BF16)
    assert D == TOKEN_TILE_ROWS * LANES
    tile_spec = pl.BlockSpec((tm * TOKEN_TILE_ROWS, LANES), lambda i: (i, 0))
    h1, xn, ri, rf, cnt = pl.pallas_call(
        functools.partial(_router_kernel, nsub=nsub),
        grid=(nt,),
        in_specs=[row_spec(D), row_spec(hw), row_spec(CONF_CH), _full(wo1.shape), _full(wo2.shape),
                  _full((1, D)), _full(wrh.shape), _full(wrl.shape), _full(ltri.shape)],
        out_specs=[row_spec(D), tile_spec, row_spec(LANES), row_spec(LANES), _full((8, LANES))],
        out_shape=[jax.ShapeDtypeStruct((N, D), F32), jax.ShapeDtypeStruct((N * TOKEN_TILE_ROWS, LANES), F32),
                   jax.ShapeDtypeStruct((N, LANES), I32), jax.ShapeDtypeStruct((N, LANES), F32),
                   jax.ShapeDtypeStruct((8, LANES), F32)],
        scratch_shapes=[pltpu.VMEM((1, LANES), F32)],
        compiler_params=_params(("arbitrary",)),
        name="router",
    )(x2, ygdn, yc, wo1, wo2, g_moe[None], wrh, wrl, ltri)

    nb = (2 * N) // EXPERT_BLOCK + N_EXPERTS
    nbl = -(-nb // LANES) * LANES
    post, blk, ends = pl.pallas_call(
        functools.partial(_pos_kernel, tm=tm),
        grid=(nt,),
        in_specs=[row_spec(LANES), _full((8, LANES))],
        out_specs=[pl.BlockSpec((8, tm), lambda i: (0, i)), _full((8, nbl)), _full((8, LANES))],
        out_shape=[jax.ShapeDtypeStruct((8, N), I32), jax.ShapeDtypeStruct((8, nbl), I32),
                   jax.ShapeDtypeStruct((8, LANES), I32)],
        compiler_params=_params(("arbitrary",)),
        name="positions",
    )(ri, cnt)
    pos1 = post[0].reshape(nt, 1, tm)
    pos2 = post[1].reshape(nt, 1, tm)
    block_expert = blk[0, :nb]
    n_used = blk[1, :1]

    P = nb * EXPERT_BLOCK
    real_end = ends[0, ROUTE_LANE0:ROUTE_LANE0 + N_EXPERTS]
    padded_end = ends[1, ROUTE_LANE0:ROUTE_LANE0 + N_EXPERTS]
    smem_spec = pl.BlockSpec((1, 1, tm), lambda i, *_: (i, 0, 0), memory_space=pltpu.SMEM)
    flat_smem = pl.BlockSpec((P,), lambda i, *_: (0,), memory_space=pltpu.SMEM)
    slots = pl.pallas_call(
        functools.partial(_slots_kernel, tm=tm),
        grid_spec=pltpu.PrefetchScalarGridSpec(
            num_scalar_prefetch=2, grid=(nt,), in_specs=[smem_spec, smem_spec], out_specs=flat_smem),
        out_shape=jax.ShapeDtypeStruct((P,), I32),
        compiler_params=_params(("arbitrary",)),
        name="slots",
    )(real_end, padded_end, pos1, pos2).reshape(nb, 1, EXPERT_BLOCK)

    any_spec = pl.BlockSpec(memory_space=pl.ANY)
    blk_smem = lambda off: pl.BlockSpec((1, 1, EXPERT_BLOCK),
                                        lambda j, be, nu: (jnp.minimum(j + off, nb - 1), 0, 0),
                                        memory_space=pltpu.SMEM)
    ys = pl.pallas_call(
        _expert_kernel,
        grid_spec=pltpu.PrefetchScalarGridSpec(
            num_scalar_prefetch=2,
            grid=(nb,),
            in_specs=[blk_smem(0), blk_smem(1), blk_smem(2), any_spec, any_spec, any_spec],
            out_specs=pl.BlockSpec((EXPERT_BLOCK * TOKEN_TILE_ROWS, LANES), lambda j, be, nu: (j, 0)),
            scratch_shapes=[pltpu.VMEM((D, 2 * D_EXPERT), BF16), pltpu.VMEM((D_EXPERT, D), BF16),
                            pltpu.VMEM((2, D, 2 * D_EXPERT), F32), pltpu.VMEM((2, D_EXPERT, D), F32),
                            pltpu.VMEM((GATHER_RING * EXPERT_BLOCK * TOKEN_TILE_ROWS, LANES), F32),
                            pltpu.SemaphoreType.DMA((GATHER_RING,)), pltpu.SemaphoreType.DMA((2,)),
                            pltpu.SMEM((1,), I32)]),
        out_shape=jax.ShapeDtypeStruct((P * TOKEN_TILE_ROWS, LANES), F32),
        compiler_params=_params(("arbitrary",)),
        name="experts",
    )(block_expert, n_used, slots, slots, slots, xn, w_up, w_down)

    wg = w_ple_gate.astype(BF16)
    wp = w_ple_proj.astype(BF16)
    pos_spec = lambda off: pl.BlockSpec((1, 1, tm), lambda i: (jnp.minimum(i + off, nt - 1), 0, 0),
                                        memory_space=pltpu.SMEM)
    out = pl.pallas_call(
        _ple_kernel,
        grid=(nt,),
        in_specs=[pos_spec(0), pos_spec(0), pos_spec(1), pos_spec(1),
                  row_spec(D), row_spec(LANES), row_spec(p2.shape[1]), any_spec,
                  _full((1, D)), _full(wg.shape), _full(wp.shape), _full((1, D))],
        out_specs=row_spec(D),
        out_shape=jax.ShapeDtypeStruct((N, D), F32),
        scratch_shapes=[pltpu.VMEM((2 * 2 * tm * TOKEN_TILE_ROWS, LANES), F32), pltpu.SemaphoreType.DMA((2,))],
        compiler_params=_params(("arbitrary",)),
        name="ple",
    )(pos1, pos2, pos1, pos2, h1, rf, p2, ys, g_ple[None], wg, wp, g_out[None])
    return out


def kernel(x, p, g_mix, w_in, w_conv_qkv, a_log, dt_bias, g_gdn_out, b_glu, w_conv_dw, b_conv_dw, ln_conv_g, ln_conv_b, w_out, g_moe, w_group_router, w_expert_router, w_up, w_down, g_ple, w_ple_gate, w_ple_proj, g_final):
    B, T, D = x.shape
    depth = p.shape[0]
    assert depth == 1, "the final RMSNorm is fused into the (single) layer's last kernel"
    h = _layer(x.reshape(B * T, D), p[0].reshape(B * T, -1), B, T, g_mix[0], w_in[0], w_conv_qkv[0],
               a_log[0], dt_bias[0], g_gdn_out[0], b_glu[0], w_conv_dw[0], b_conv_dw[0], ln_conv_g[0],
               ln_conv_b[0], w_out[0], g_moe[0], w_group_router[0], w_expert_router[0], w_up[0],
               w_down[0], g_ple[0], w_ple_gate[0], w_ple_proj[0], g_final)
    return h.reshape(B, T, D)
```

```python
import functools

import jax
import jax.numpy as jnp
from jax import lax
from jax.experimental import pallas as pl
from jax.experimental.pallas import tpu as pltpu

F32 = jnp.float32
BF16 = jnp.bfloat16
I32 = jnp.int32

EPS = 1e-6
GDN_HEADS = 4
GDN_DK = 128
GDN_CONV = 4
GDN_CHUNK = 64
CONF_CH = 512
CONF_WIDTH = 31
N_GROUPS = 8
EXPERTS_PER_GROUP = 8
N_EXPERTS = 64
D_EXPERT = 256
LANES = 128
SUBLANES = 8
TOKEN_TILE_ROWS = 8
ROUTE_LANE0 = N_GROUPS
EXPERT_BLOCK = 256
GATHER_RING = 3
ISSUE_UNROLL = 8
QKV_HALO = 8
CONF_HALO = 32
NEG = -1e30
VMEM_LIMIT = 56 * 1024 * 1024


def _dot(a, b):
    return jnp.dot(a, b, preferred_element_type=F32)


def _dot_nt(a, b):
    return lax.dot_general(a, b, (((1,), (1,)), ((), ())), preferred_element_type=F32)


def _dot_tn(a, b):
    return lax.dot_general(a, b, (((0,), (0,)), ((), ())), preferred_element_type=F32)


def _silu(x):
    return x * jax.nn.sigmoid(x)


def _split_bf16(x):
    hi = x.astype(BF16)
    lo = (x - hi.astype(F32)).astype(BF16)
    return hi, lo


def _causal_taps(ext_ref, w_ref, n_taps, halo, tm, cols):
    rows = ext_ref.shape[0]
    ext = ext_ref[:, cols]
    first = halo - (n_taps - 1)
    acc = None
    for res in range(SUBLANES):
        taps = [k for k in range(n_taps) if (first + k) % SUBLANES == res]
        if not taps:
            continue
        shifted = ext if res == 0 else pltpu.roll(ext, rows - res, 0)
        for k in taps:
            a = first + k - res
            term = w_ref[k:k + 1, cols] * shifted[a:a + tm, :]
            acc = term if acc is None else acc + term
    return acc


def _inproj_kernel(x_ref, gmix_ref, wqkv_ref, wz_ref, wbah_ref, wbal_ref, wglu_ref, cw_ref, pa_ref,
                   bglu_ref, cdw_ref, cvec_ref,
                   q_ref, k_ref, v_ref, bg_ref, zs_ref, yc_ref,
                   extq, extu, *, tiles_per_seq, tm):
    i = pl.program_id(0)
    first = (i % tiles_per_seq) == 0

    x = x_ref[...]
    n1 = x * lax.rsqrt(jnp.mean(x * x, axis=-1, keepdims=True) + EPS) * gmix_ref[...]
    n1h, n1l = _split_bf16(n1)

    @pl.when(first)
    def _():
        extq[0:QKV_HALO, :] = jnp.zeros((QKV_HALO, extq.shape[1]), F32)
        extu[0:CONF_HALO, :] = jnp.zeros((CONF_HALO, extu.shape[1]), F32)

    @pl.when(jnp.logical_not(first))
    def _():
        extq[0:QKV_HALO, :] = extq[tm:tm + QKV_HALO, :]
        extu[0:CONF_HALO, :] = extu[tm:tm + CONF_HALO, :]

    extq[QKV_HALO:QKV_HALO + tm, :] = _dot(n1h, wqkv_ref[...])
    hw = GDN_HEADS * GDN_DK
    qc = _silu(_causal_taps(extq, cw_ref, GDN_CONV, QKV_HALO, tm, slice(0, hw)))
    kc = _silu(_causal_taps(extq, cw_ref, GDN_CONV, QKV_HALO, tm, slice(hw, 2 * hw)))
    for h in range(GDN_HEADS):
        cs = slice(h * GDN_DK, (h + 1) * GDN_DK)
        qh = qc[:, cs]
        q_ref[:, cs] = qh * lax.rsqrt(jnp.sum(qh * qh, axis=-1, keepdims=True) + EPS) * (GDN_DK ** -0.5)
        kh = kc[:, cs]
        k_ref[:, cs] = kh * lax.rsqrt(jnp.sum(kh * kh, axis=-1, keepdims=True) + EPS)
    v_ref[...] = _silu(_causal_taps(extq, cw_ref, GDN_CONV, QKV_HALO, tm, slice(2 * hw, 3 * hw)))

    zs_ref[...] = _silu(_dot(n1h, wz_ref[...])).astype(BF16)

    ba = _dot(n1h, wbah_ref[...]) + _dot(n1l, wbah_ref[...]) + _dot(n1h, wbal_ref[...])
    lane = lax.broadcasted_iota(I32, ba.shape, 1)
    row = lax.broadcasted_iota(I32, ba.shape, 0)
    beta = jax.nn.sigmoid(ba)
    sp_in = ba + pa_ref[1:2, :]
    softplus = jnp.maximum(sp_in, 0.0) + jnp.log1p(jnp.exp(-jnp.abs(sp_in)))
    g = -jnp.exp(pa_ref[0:1, :]) * softplus
    rin = row & (GDN_CHUNK - 1)
    s = 1
    while s < GDN_CHUNK:
        g = g + jnp.where(rin >= s, pltpu.roll(g, s, 0), 0.0)
        s *= 2
    bg_ref[...] = jnp.where(lane < GDN_HEADS, beta, g)

    glu = _dot(n1h, wglu_ref[...]) + bglu_ref[...]
    extu[CONF_HALO:CONF_HALO + tm, :] = glu[:, :CONF_CH] * jax.nn.sigmoid(glu[:, CONF_CH:])
    c = _causal_taps(extu, cdw_ref, CONF_WIDTH, CONF_HALO, tm, slice(0, CONF_CH)) + cvec_ref[0:1, :]
    mu = jnp.mean(c, axis=-1, keepdims=True)
    cc = c - mu
    var = jnp.mean(cc * cc, axis=-1, keepdims=True)
    yln = cc * lax.rsqrt(var + EPS) * cvec_ref[1:2, :] + cvec_ref[2:3, :]
    yc_ref[...] = _silu(yln).astype(BF16)


def _gdn_kernel(q_ref, k_ref, v_ref, bg_ref, zs_ref, gout_ref, y_ref, s_ref, *, cb, nbatch, group):
    n = pl.program_id(0)

    @pl.when(n == 0)
    def _():
        s_ref[...] = jnp.zeros(s_ref.shape, F32)

    C = GDN_CHUNK
    r = lax.broadcasted_iota(I32, (C, C), 0)
    c = lax.broadcasted_iota(I32, (C, C), 1)
    causal = r >= c
    strict = r > c
    eye = r == c
    eyef = jnp.where(eye, 1.0, 0.0).astype(F32)
    gout = gout_ref[...]

    def front(b, ci, h):
        rows = slice(ci * C, (ci + 1) * C)
        cols = slice(h * GDN_DK, (h + 1) * GDN_DK)
        q = q_ref[b, rows, cols]
        k = k_ref[b, rows, cols]
        v = v_ref[b, rows, cols]
        beta = bg_ref[b, rows, h:h + 1]
        gc = bg_ref[b, rows, GDN_HEADS + h:GDN_HEADS + h + 1]
        gc_row = jnp.sum(jnp.where(eye, gc, 0.0), axis=0, keepdims=True)
        gc_last = gc[C - 1:C, :]
        eg = jnp.exp(gc)
        kb = k * beta
        return dict(
            decay=jnp.where(causal, jnp.exp(jnp.minimum(gc - gc_row, 0.0)), 0.0),
            lhs=jnp.concatenate([kb, q], axis=0).astype(BF16),
            kbf=k.astype(BF16),
            rhs=jnp.concatenate([v * beta, kb * eg], axis=1).astype(BF16),
            qd=q * eg,
            kd=(k * jnp.exp(gc_last - gc)).astype(BF16),
            gl=jnp.exp(gc_last))

    insts = [(b, ci, h) for ci in range(cb) for b in range(nbatch) for h in range(GDN_HEADS)]
    done = {}
    for g0 in range(0, len(insts), group):
        grp = insts[g0:g0 + group]
        st = [front(*key) for key in grp]
        kq = [_dot_nt(s["lhs"], s["kbf"]) for s in st]
        xp = [-jnp.where(strict, m[:C] * s["decay"], 0.0) for m, s in zip(kq, st)]
        qk = [(m[C:] * s["decay"]).astype(BF16) for m, s in zip(kq, st)]
        tinv = [eyef + x for x in xp]
        for _ in range(5):
            xpb = [x.astype(BF16) for x in xp]
            xp = [_dot(x, x) for x in xpb]
            tinv = [t + _dot(t.astype(BF16), x.astype(BF16)) for t, x in zip(tinv, xp)]
        uw = [_dot(t.astype(BF16), s["rhs"]) for t, s in zip(tinv, st)]
        for key, s, m, qkm in zip(grp, st, uw, qk):
            done[key] = dict(u=m[:, :GDN_DK],
                             wq=jnp.concatenate([m[:, GDN_DK:], s["qd"]], axis=0).astype(BF16),
                             qk=qkm, kd=s["kd"], gl=s["gl"])

    chains = [(b, h) for b in range(nbatch) for h in range(GDN_HEADS)]
    for ci in range(cb):
        rows = slice(ci * C, (ci + 1) * C)
        dd = [done[(b, ci, h)] for b, h in chains]
        s_old = [s_ref[b * GDN_HEADS + h] for b, h in chains]
        wq = [_dot(d["wq"], s.astype(BF16)) for d, s in zip(dd, s_old)]
        v_new = [(d["u"] - m[:C]).astype(BF16) for d, m in zip(dd, wq)]
        o = [m[C:] + _dot(d["qk"], vn) for d, m, vn in zip(dd, wq, v_new)]
        s_new = [s * d["gl"] + _dot_tn(d["kd"], vn) for d, s, vn in zip(dd, s_old, v_new)]
        for (b, h), sn, oo in zip(chains, s_new, o):
            cols = slice(h * GDN_DK, (h + 1) * GDN_DK)
            s_ref[b * GDN_HEADS + h] = sn
            on = oo * lax.rsqrt(jnp.mean(oo * oo, axis=-1, keepdims=True) + EPS) * gout
            y_ref[b, rows, cols] = (on * zs_ref[b, rows, cols].astype(F32)).astype(BF16)


def _router_kernel(x_ref, yg_ref, yc_ref, wo1_ref, wo2_ref, gmoe_ref, wrh_ref, wrl_ref, ltri_ref,
                   h1_ref, xn_ref, ri_ref, rf_ref, cnt_ref, carry, *, nsub):
    i = pl.program_id(0)

    @pl.when(i == 0)
    def _():
        carry[...] = jnp.zeros(carry.shape, F32)

    ts = x_ref.shape[0] // nsub
    subs = [slice(s * ts, (s + 1) * ts) for s in range(nsub)]
    each = lambda fn, *lists: [fn(*args) for args in zip(*lists)]
    lane_sum = lambda v: jnp.sum(v, axis=-1, keepdims=True)
    lane_max = lambda v: jnp.max(v, axis=-1, keepdims=True)
    lane_min = lambda v: jnp.min(v, axis=-1, keepdims=True)

    h1 = [x_ref[sl, :] + _dot(yg_ref[sl, :], wo1_ref[...]) + _dot(yc_ref[sl, :], wo2_ref[...]) for sl in subs]
    for sl, h in zip(subs, h1):
        h1_ref[sl, :] = h
    xn = each(lambda h: h * lax.rsqrt(jnp.mean(h * h, axis=-1, keepdims=True) + EPS) * gmoe_ref[...], h1)
    parts = each(_split_bf16, xn)
    for s, v in enumerate(xn):
        for f in range(TOKEN_TILE_ROWS):
            xn_ref[pl.ds(s * ts * TOKEN_TILE_ROWS + f, ts, stride=TOKEN_TILE_ROWS), :] = v[:, f * LANES:(f + 1) * LANES]
    lg = [_dot(hi, wrh_ref[...]) + _dot(lo, wrh_ref[...]) + _dot(hi, wrl_ref[...]) for hi, lo in parts]

    lane = lax.broadcasted_iota(I32, lg[0].shape, 1)
    lanef = lane.astype(F32)
    none = float(LANES)
    gl = each(lambda v: jnp.where(lane < N_GROUPS, v, NEG), lg)
    gmax = each(lane_max, gl)
    g_w = each(lambda v, m: 1.0 / lane_sum(jnp.exp(v - m)), gl, gmax)
    gidx = each(lambda v, m: lane_min(jnp.where(v == m, lanef, none)), gl, gmax)
    lane_group = ((lane - ROUTE_LANE0) >> (EXPERTS_PER_GROUP.bit_length() - 1)).astype(F32)
    is_expert = (lane >= ROUTE_LANE0) & (lane < ROUTE_LANE0 + N_EXPERTS)
    el = each(lambda v, g: jnp.where(is_expert & (lane_group == g), v, NEG), lg, gidx)
    m1 = each(lane_max, el)
    i1 = each(lambda v, m: lane_min(jnp.where(v == m, lanef, none)), el, m1)
    el2 = each(lambda v, a: jnp.where(lanef == a, NEG, v), el, i1)
    m2 = each(lane_max, el2)
    i2 = each(lambda v, m: lane_min(jnp.where(v == m, lanef, none)), el2, m2)
    ratio = each(lambda a, b: jnp.exp(b - a), m1, m2)
    gate1 = each(lambda w, r: w / (1.0 + r), g_w, ratio)
    gate2 = each(lambda w, r: w * r / (1.0 + r), g_w, ratio)

    oh1 = each(lambda a: lanef == a, i1)
    oh2 = each(lambda a: lanef == a, i2)
    oh = each(lambda a, b: jnp.where(a | b, 1.0, 0.0).astype(F32), oh1, oh2)
    within = each(lambda v: _dot(ltri_ref[...], v.astype(BF16)), oh)
    base = carry[...]
    for s in range(nsub):
        cum = within[s] + base
        rank1 = lane_sum(jnp.where(oh1[s], cum, 0.0))
        rank2 = lane_sum(jnp.where(oh2[s], cum, 0.0))
        routes = jnp.where(lane == 0, i1[s] - ROUTE_LANE0,
                           jnp.where(lane == 1, i2[s] - ROUTE_LANE0,
                                     jnp.where(lane == 2, rank1, jnp.where(lane == 3, rank2, 0.0))))
        ri_ref[subs[s], :] = routes.astype(I32)
        rf_ref[subs[s], :] = jnp.where(lane == 0, gate1[s], jnp.where(lane == 1, gate2[s], 0.0))
        base = base + jnp.sum(oh[s], axis=0, keepdims=True)
    carry[...] = base
    cnt_ref[...] = jnp.broadcast_to(base, cnt_ref.shape)


def _pos_kernel(ri_ref, cnt_ref, post_ref, blk_ref, ends_ref, *, tm):
    shift = EXPERT_BLOCK.bit_length() - 1
    cnt = cnt_ref[...].astype(I32)
    padded = ((cnt + (EXPERT_BLOCK - 1)) >> shift) << shift
    lane8 = lax.broadcasted_iota(I32, padded.shape, 1)
    pend = padded
    s = 1
    while s < LANES:
        pend = pend + jnp.where(lane8 >= s, pltpu.roll(pend, s, 1), 0)
        s *= 2
    pstart = (pend - padded)[0:1, :].astype(F32)

    ri = ri_ref[...]
    lane = lax.broadcasted_iota(I32, ri.shape, 1)
    e1 = ri[:, 0:1] + ROUTE_LANE0
    e2 = ri[:, 1:2] + ROUTE_LANE0
    ps1 = jnp.sum(jnp.where(lane == e1, pstart, 0.0), axis=-1, keepdims=True)
    ps2 = jnp.sum(jnp.where(lane == e2, pstart, 0.0), axis=-1, keepdims=True)
    pos1 = ps1 + ri[:, 2:3].astype(F32)
    pos2 = ps2 + ri[:, 3:4].astype(F32)
    posmat = jnp.where(lane == 0, pos1, jnp.where(lane == 1, pos2, 0.0))
    for j in range(tm // LANES):
        pt = posmat[j * LANES:(j + 1) * LANES, :].T
        post_ref[:, j * LANES:(j + 1) * LANES] = pt[0:8, :].astype(I32)

    @pl.when(pl.program_id(0) == 0)
    def _():
        pendf = pend.astype(F32)[0:1, :]
        rr = lax.broadcasted_iota(I32, (LANES, LANES), 0)
        cc = lax.broadcasted_iota(I32, (LANES, LANES), 1)
        pend_col = jnp.sum(jnp.where(rr == cc, pendf, 0.0), axis=1, keepdims=True)
        nbl = blk_ref.shape[1]
        er = lax.broadcasted_iota(I32, (LANES, nbl), 0)
        jb = (lax.broadcasted_iota(I32, (LANES, nbl), 1) * EXPERT_BLOCK).astype(F32)
        valid = (er >= ROUTE_LANE0) & (er < ROUTE_LANE0 + N_EXPERTS)
        be = jnp.sum(jnp.where(valid & (pend_col <= jb), 1.0, 0.0), axis=0, keepdims=True)
        be = jnp.minimum(be, float(N_EXPERTS - 1))
        nused = jnp.max(pendf, axis=-1, keepdims=True) * (1.0 / EXPERT_BLOCK)
        rowi = lax.broadcasted_iota(I32, blk_ref.shape, 0)
        blk_ref[...] = jnp.where(rowi == 0, be, nused).astype(I32)
        cend_row = pstart + cnt_ref[0:1, :]
        rowe = lax.broadcasted_iota(I32, ends_ref.shape, 0)
        ends_ref[...] = jnp.where(rowe == 0, cend_row, pendf).astype(I32)


def _slots_kernel(cend_ref, pend_ref, pos1_ref, pos2_ref, slot_ref, *, tm):
    i = pl.program_id(0)
    F = ISSUE_UNROLL

    def fill_range(lo, hi):
        def body(c, carry):
            for u in range(F):
                p = jnp.minimum(lo + c * F + u, hi - 1)
                slot_ref[p] = p & (EXPERT_BLOCK - 1)
            return carry

        lax.fori_loop(0, (hi - lo + (F - 1)) // F, body, 0)

    @pl.when(i == 0)
    def _():
        def per_expert(e, carry):
            fill_range(cend_ref[e], pend_ref[e])
            return carry

        lax.fori_loop(0, N_EXPERTS, per_expert, 0)
        fill_range(pend_ref[N_EXPERTS - 1], slot_ref.shape[0])

    def body(r, carry):
        t = i * tm + r
        slot_ref[pos1_ref[0, 0, r]] = t
        slot_ref[pos2_ref[0, 0, r]] = t
        return carry

    lax.fori_loop(0, tm, body, 0, unroll=16)


def _expert_kernel(be_ref, nu_ref, sl0_ref, sl1_ref, sl2_ref, xn_ref, wup_ref, wdn_ref, y_ref,
                   wupb, wdnb, wup_stage, wdn_stage, xbuf, gsem, wsem, wslot):
    assert GATHER_RING == 3, "the three slot BlockSpecs (blocks j, j+1, j+2) fix the ring depth"
    j = pl.program_id(0)
    cur = lax.rem(j, GATHER_RING)
    n_used = nu_ref[0]
    R = TOKEN_TILE_ROWS
    buf_rows = EXPERT_BLOCK * R

    def gather_row(slot_ref, b, r, prio):
        pltpu.make_async_copy(xn_ref.at[pl.ds(slot_ref[0, 0, r] * R, R), :],
                              xbuf.at[pl.ds(b * buf_rows + r * R, R), :], gsem.at[b]).start(priority=prio)

    def wait_buffer(b):
        pltpu.make_async_copy(xn_ref.at[pl.ds(0, buf_rows), :], xbuf.at[pl.ds(0, buf_rows), :], gsem.at[b]).wait()

    @pl.when(j == 0)
    def _():
        def body(g, carry):
            for u in range(ISSUE_UNROLL):
                gather_row(sl0_ref, 0, g * ISSUE_UNROLL + u, u % 2)
                gather_row(sl1_ref, 1, g * ISSUE_UNROLL + u, (u + 1) % 2)
            return carry

        lax.fori_loop(0, EXPERT_BLOCK // ISSUE_UNROLL, body, 0)

    def weight_copies(e, s):
        return (pltpu.make_async_copy(wup_ref.at[e], wup_stage.at[s], wsem.at[s]),
                pltpu.make_async_copy(wdn_ref.at[e], wdn_stage.at[s], wsem.at[s]))

    @pl.when(j == 0)
    def _():
        wslot[0] = 0
        for cp in weight_copies(be_ref[0], 0):
            cp.start()

    prev = be_ref[jnp.maximum(j - 1, 0)]

    @pl.when((j < n_used) & ((j == 0) | (be_ref[j] != prev)))
    def _():
        s = wslot[0]
        for cp in weight_copies(0, s):
            cp.wait()
        wupb[...] = wup_stage[s].astype(BF16)
        wdnb[...] = wdn_stage[s].astype(BF16)
        e = be_ref[j]
        k = lax.while_loop(lambda k: (k < n_used) & (be_ref[k] == e), lambda k: k + 1, j + 1)

        @pl.when(k < n_used)
        def _():
            for cp in weight_copies(be_ref[k], 1 - s):
                cp.start()

        wslot[0] = 1 - s

    @pl.when(j < n_used)
    def _():
        wait_buffer(cur)
        base = cur * buf_rows
        x = jnp.concatenate([xbuf[pl.ds(base + f, EXPERT_BLOCK, stride=R), :] for f in range(R)], axis=1)
        ab = _dot(x.astype(BF16), wupb[...])
        ahead = lax.rem(j + (GATHER_RING - 1), GATHER_RING)
        for r in range(EXPERT_BLOCK):
            gather_row(sl2_ref, ahead, r, r % 2)
        hmid = _silu(ab[:, :D_EXPERT]) * ab[:, D_EXPERT:]
        y = _dot(hmid.astype(BF16), wdnb[...])
        for f in range(R):
            y_ref[pl.ds(f, EXPERT_BLOCK, stride=R), :] = y[:, f * LANES:(f + 1) * LANES]

    @pl.when(j == n_used)
    def _():
        wait_buffer(cur)
        wait_buffer(lax.rem(j + 1, GATHER_RING))

    @pl.when(j >= n_used)
    def _():
        y_ref[...] = jnp.zeros(y_ref.shape, F32)


def _ple_kernel(p1c_ref, p2c_ref, p1n_ref, p2n_ref, h1_ref, rf_ref, p_ref, ys_ref, gple_ref, wg_ref, wp_ref,
                gfin_ref, out_ref, ybuf, sem):
    i = pl.program_id(0)
    last = pl.num_programs(0) - 1
    cur = i % 2
    nxt = 1 - cur
    tm = h1_ref.shape[0]
    R = TOKEN_TILE_ROWS
    slot_rows = 2 * tm * R

    def gather_row(pos_ref, b, k, r, prio):
        pltpu.make_async_copy(ys_ref.at[pl.ds(pos_ref[0, 0, r] * R, R), :],
                              ybuf.at[pl.ds(b * slot_rows + (k * tm + r) * R, R), :], sem.at[b]).start(priority=prio)

    def wait_slot(b):
        pltpu.make_async_copy(ys_ref.at[pl.ds(0, slot_rows), :], ybuf.at[pl.ds(0, slot_rows), :], sem.at[b]).wait()

    @pl.when(i == 0)
    def _():
        def body(g, carry):
            for u in range(ISSUE_UNROLL):
                gather_row(p1c_ref, 0, 0, g * ISSUE_UNROLL + u, u % 2)
                gather_row(p2c_ref, 0, 1, g * ISSUE_UNROLL + u, (u + 1) % 2)
            return carry

        lax.fori_loop(0, tm // ISSUE_UNROLL, body, 0)

    pp = _dot(p_ref[...].astype(BF16), wp_ref[...])
    for r in range(tm):
        gather_row(p1n_ref, nxt, 0, r, r % 2)
        gather_row(p2n_ref, nxt, 1, r, (r + 1) % 2)
    wait_slot(cur)
    rows = lambda k: jnp.concatenate(
        [ybuf[pl.ds(cur * slot_rows + k * tm * R + f, tm, stride=R), :] for f in range(R)], axis=1)
    rf = rf_ref[...]
    h2 = h1_ref[...] + (rf[:, 0:1] * rows(0) + rf[:, 1:2] * rows(1))
    hn = h2 * lax.rsqrt(jnp.mean(h2 * h2, axis=-1, keepdims=True) + EPS) * gple_ref[...]
    gate = jax.nn.sigmoid(_dot(hn.astype(BF16), wg_ref[...]))
    h3 = h2 + gate * pp
    out_ref[...] = h3 * lax.rsqrt(jnp.mean(h3 * h3, axis=-1, keepdims=True) + EPS) * gfin_ref[...]

    @pl.when(i == last)
    def _():
        wait_slot(nxt)


def _params(sem):
    return pltpu.CompilerParams(dimension_semantics=sem, vmem_limit_bytes=VMEM_LIMIT)


def _full(shape):
    nd = len(shape)
    return pl.BlockSpec(shape, lambda *_: (0,) * nd)


def _pad_lanes(a, lane0=0):
    out = jnp.zeros((a.shape[0], LANES), a.dtype)
    return out.at[:, lane0:lane0 + a.shape[1]].set(a)


def _layer(x2, p2, B, T, g_mix, w_in, w_conv_qkv, a_log, dt_bias, g_gdn_out, b_glu, w_conv_dw,
           b_conv_dw, ln_conv_g, ln_conv_b, w_out, g_moe, w_group_router, w_expert_router, w_up,
           w_down, g_ple, w_ple_gate, w_ple_proj, g_out):
    N, D = x2.shape
    hw = GDN_HEADS * GDN_DK
    tm = min(512, T)
    nt = N // tm

    o1 = 3 * hw
    o2 = o1 + hw
    o4 = o2 + 2 * GDN_HEADS
    wqkv = w_in[:, :o1].astype(BF16)
    wz = w_in[:, o1:o2].astype(BF16)
    wbah, wbal = _split_bf16(_pad_lanes(w_in[:, o2:o4]))
    wglu = w_in[:, o4:].astype(BF16)
    pa = jnp.concatenate([_pad_lanes(a_log[None, :], GDN_HEADS), _pad_lanes(dt_bias[None, :], GDN_HEADS),
                          jnp.zeros((6, LANES), F32)], axis=0)
    cdw = jnp.concatenate([w_conv_dw, jnp.zeros((1, CONF_CH), F32)], axis=0)
    cvec = jnp.concatenate([b_conv_dw[None], ln_conv_g[None], ln_conv_b[None], jnp.zeros((5, CONF_CH), F32)], 0)

    row_spec = lambda w: pl.BlockSpec((tm, w), lambda i: (i, 0))
    q, k, v, bg, zs, yc = pl.pallas_call(
        functools.partial(_inproj_kernel, tiles_per_seq=T // tm, tm=tm),
        grid=(nt,),
        in_specs=[row_spec(D), _full((1, D)), _full(wqkv.shape), _full(wz.shape), _full(wbah.shape),
                  _full(wbal.shape), _full(wglu.shape), _full(w_conv_qkv.shape), _full(pa.shape),
                  _full((1, 2 * CONF_CH)), _full(cdw.shape), _full(cvec.shape)],
        out_specs=[row_spec(hw), row_spec(hw), row_spec(hw), row_spec(LANES), row_spec(hw), row_spec(CONF_CH)],
        out_shape=[jax.ShapeDtypeStruct((N, hw), F32)] * 3 + [jax.ShapeDtypeStruct((N, LANES), F32),
                   jax.ShapeDtypeStruct((N, hw), BF16), jax.ShapeDtypeStruct((N, CONF_CH), BF16)],
        scratch_shapes=[pltpu.VMEM((tm + QKV_HALO, 3 * hw), F32), pltpu.VMEM((tm + CONF_HALO, CONF_CH), F32)],
        compiler_params=_params(("arbitrary",)),
        name="inproj",
    )(x2, g_mix[None], wqkv, wz, wbah, wbal, wglu, w_conv_qkv, pa, b_glu[None], cdw, cvec)

    cb = 8
    rows = cb * GDN_CHUNK
    seq_spec = lambda w: pl.BlockSpec((B, rows, w), lambda n: (0, n, 0))
    r3 = lambda a: a.reshape(B, T, a.shape[-1])
    ygdn = pl.pallas_call(
        functools.partial(_gdn_kernel, cb=cb, nbatch=B, group=8 * GDN_HEADS),
        grid=(T // rows,),
        in_specs=[seq_spec(hw), seq_spec(hw), seq_spec(hw), seq_spec(LANES), seq_spec(hw),
                  pl.BlockSpec((1, GDN_DK), lambda n: (0, 0))],
        out_specs=seq_spec(hw),
        out_shape=jax.ShapeDtypeStruct((B, T, hw), BF16),
        scratch_shapes=[pltpu.VMEM((B * GDN_HEADS, GDN_DK, GDN_DK), F32)],
        compiler_params=_params(("arbitrary",)),
        name="gdn",
    )(r3(q), r3(k), r3(v), r3(bg), r3(zs), g_gdn_out[None]).reshape(N, hw)

    wo1 = w_out[:hw].astype(BF16)
    wo2 = w_out[hw:].astype(BF16)
    wrh, wrl = _split_bf16(_pad_lanes(jnp.concatenate([w_group_router, w_expert_router], axis=1)))
    nsub = 2 if tm % 256 == 0 else 1
    ltri = jnp.tril(jnp.ones((tm // nsub, tm // nsub), F32), -1).astype(BF16)
    assert D == TOKEN_TILE_ROWS * LANES
    tile_spec = pl.BlockSpec((tm * TOKEN_TILE_ROWS, LANES), lambda i: (i, 0))
    h1, xn, ri, rf, cnt = pl.pallas_call(
        functools.partial(_router_kernel, nsub=nsub),
        grid=(nt,),
        in_specs=[row_spec(D), row_spec(hw), row_spec(CONF_CH), _full(wo1.shape), _full(wo2.shape),
                  _full((1, D)), _full(wrh.shape), _full(wrl.shape), _full(ltri.shape)],
        out_specs=[row_spec(D), tile_spec, row_spec(LANES), row_spec(LANES), _full((8, LANES))],
        out_shape=[jax.ShapeDtypeStruct((N, D), F32), jax.ShapeDtypeStruct((N * TOKEN_TILE_ROWS, LANES), F32),
                   jax.ShapeDtypeStruct((N, LANES), I32), jax.ShapeDtypeStruct((N, LANES), F32),
                   jax.ShapeDtypeStruct((8, LANES), F32)],
        scratch_shapes=[pltpu.VMEM((1, LANES), F32)],
        compiler_params=_params(("arbitrary",)),
        name="router",
    )(x2, ygdn, yc, wo1, wo2, g_moe[None], wrh, wrl, ltri)

    nb = (2 * N) // EXPERT_BLOCK + N_EXPERTS
    nbl = -(-nb // LANES) * LANES
    post, blk, ends = pl.pallas_call(
        functools.partial(_pos_kernel, tm=tm),
        grid=(nt,),
        in_specs=[row_spec(LANES), _full((8, LANES))],
        out_specs=[pl.BlockSpec((8, tm), lambda i: (0, i)), _full((8, nbl)), _full((8, LANES))],
        out_shape=[jax.ShapeDtypeStruct((8, N), I32), jax.ShapeDtypeStruct((8, nbl), I32),
                   jax.ShapeDtypeStruct((8, LANES), I32)],
        compiler_params=_params(("arbitrary",)),
        name="positions",
    )(ri, cnt)
    pos1 = post[0].reshape(nt, 1, tm)
    pos2 = post[1].reshape(nt, 1, tm)
    block_expert = blk[0, :nb]
    n_used = blk[1, :1]

    P = nb * EXPERT_BLOCK
    real_end = ends[0, ROUTE_LANE0:ROUTE_LANE0 + N_EXPERTS]
    padded_end = ends[1, ROUTE_LANE0:ROUTE_LANE0 + N_EXPERTS]
    smem_spec = pl.BlockSpec((1, 1, tm), lambda i, *_: (i, 0, 0), memory_space=pltpu.SMEM)
    flat_smem = pl.BlockSpec((P,), lambda i, *_: (0,), memory_space=pltpu.SMEM)
    slots = pl.pallas_call(
        functools.partial(_slots_kernel, tm=tm),
        grid_spec=pltpu.PrefetchScalarGridSpec(
            num_scalar_prefetch=2, grid=(nt,), in_specs=[smem_spec, smem_spec], out_specs=flat_smem),
        out_shape=jax.ShapeDtypeStruct((P,), I32),
        compiler_params=_params(("arbitrary",)),
        name="slots",
    )(real_end, padded_end, pos1, pos2).reshape(nb, 1, EXPERT_BLOCK)

    any_spec = pl.BlockSpec(memory_space=pl.ANY)
    blk_smem = lambda off: pl.BlockSpec((1, 1, EXPERT_BLOCK),
                                        lambda j, be, nu: (jnp.minimum(j + off, nb - 1), 0, 0),
                                        memory_space=pltpu.SMEM)
    ys = pl.pallas_call(
        _expert_kernel,
        grid_spec=pltpu.PrefetchScalarGridSpec(
            num_scalar_prefetch=2,
            grid=(nb,),
            in_specs=[blk_smem(0), blk_smem(1), blk_smem(2), any_spec, any_spec, any_spec],
            out_specs=pl.BlockSpec((EXPERT_BLOCK * TOKEN_TILE_ROWS, LANES), lambda j, be, nu: (j, 0)),
            scratch_shapes=[pltpu.VMEM((D, 2 * D_EXPERT), BF16), pltpu.VMEM((D_EXPERT, D), BF16),
                            pltpu.VMEM((2, D, 2 * D_EXPERT), F32), pltpu.VMEM((2, D_EXPERT, D), F32),
                            pltpu.VMEM((GATHER_RING * EXPERT_BLOCK * TOKEN_TILE_ROWS, LANES), F32),
                            pltpu.SemaphoreType.DMA((GATHER_RING,)), pltpu.SemaphoreType.DMA((2,)),
                            pltpu.SMEM((1,), I32)]),
        out_shape=jax.ShapeDtypeStruct((P * TOKEN_TILE_ROWS, LANES), F32),
        compiler_params=_params(("arbitrary",)),
        name="experts",
    )(block_expert, n_used, slots, slots, slots, xn, w_up, w_down)

    wg = w_ple_gate.astype(BF16)
    wp = w_ple_proj.astype(BF16)
    pos_spec = lambda off: pl.BlockSpec((1, 1, tm), lambda i: (jnp.minimum(i + off, nt - 1), 0, 0),
                                        memory_space=pltpu.SMEM)
    out = pl.pallas_call(
        _ple_kernel,
        grid=(nt,),
        in_specs=[pos_spec(0), pos_spec(0), pos_spec(1), pos_spec(1),
                  row_spec(D), row_spec(LANES), row_spec(p2.shape[1]), any_spec,
                  _full((1, D)), _full(wg.shape), _full(wp.shape), _full((1, D))],
        out_specs=row_spec(D),
        out_shape=jax.ShapeDtypeStruct((N, D), F32),
        scratch_shapes=[pltpu.VMEM((2 * 2 * tm * TOKEN_TILE_ROWS, LANES), F32), pltpu.SemaphoreType.DMA((2,))],
        compiler_params=_params(("arbitrary",)),
        name="ple",
    )(pos1, pos2, pos1, pos2, h1, rf, p2, ys, g_ple[None], wg, wp, g_out[None])
    return out


def kernel(x, p, g_mix, w_in, w_conv_qkv, a_log, dt_bias, g_gdn_out, b_glu, w_conv_dw, b_conv_dw, ln_conv_g, ln_conv_b, w_out, g_moe, w_group_router, w_expert_router, w_up, w_down, g_ple, w_ple_gate, w_ple_proj, g_final):
    B, T, D = x.shape
    depth = p.shape[0]
    assert depth == 1, "the final RMSNorm is fused into the (single) layer's last kernel"
    h = _layer(x.reshape(B * T, D), p[0].reshape(B * T, -1), B, T, g_mix[0], w_in[0], w_conv_qkv[0],
               a_log[0], dt_bias[0], g_gdn_out[0], b_glu[0], w_conv_dw[0], b_conv_dw[0], ln_conv_g[0],
               ln_conv_b[0], w_out[0], g_moe[0], w_group_router[0], w_expert_router[0], w_up[0],
               w_down[0], g_ple[0], w_ple_gate[0], w_ple_proj[0], g_final)
    return h.reshape(B, T, D)
```

```python
import functools

import jax
import jax.numpy as jnp
from jax import lax
from jax.experimental import pallas as pl
from jax.experimental.pallas import tpu as pltpu

F32 = jnp.float32
BF16 = jnp.bfloat16
I32 = jnp.int32

EPS = 1e-6
GDN_HEADS = 4
GDN_DK = 128
GDN_CONV = 4
GDN_CHUNK = 64
CONF_CH = 512
CONF_WIDTH = 31
N_GROUPS = 8
EXPERTS_PER_GROUP = 8
N_EXPERTS = 64
D_EXPERT = 256
LANES = 128
SUBLANES = 8
TOKEN_TILE_ROWS = 8
ROUTE_LANE0 = N_GROUPS
EXPERT_BLOCK = 256
GATHER_RING = 3
ISSUE_UNROLL = 8
QKV_HALO = 8
CONF_HALO = 32
NEG = -1e30
VMEM_LIMIT = 56 * 1024 * 1024


def _dot(a, b):
    return jnp.dot(a, b, preferred_element_type=F32)


def _dot_nt(a, b):
    return lax.dot_general(a, b, (((1,), (1,)), ((), ())), preferred_element_type=F32)


def _dot_tn(a, b):
    return lax.dot_general(a, b, (((0,), (0,)), ((), ())), preferred_element_type=F32)


def _silu(x):
    return x * jax.nn.sigmoid(x)


def _split_bf16(x):
    hi = x.astype(BF16)
    lo = (x - hi.astype(F32)).astype(BF16)
    return hi, lo


def _causal_taps(ext_ref, w_ref, n_taps, halo, tm, cols):
    rows = ext_ref.shape[0]
    ext = ext_ref[:, cols]
    first = halo - (n_taps - 1)
    acc = None
    for res in range(SUBLANES):
        taps = [k for k in range(n_taps) if (first + k) % SUBLANES == res]
        if not taps:
            continue
        shifted = ext if res == 0 else pltpu.roll(ext, rows - res, 0)
        for k in taps:
            a = first + k - res
            term = w_ref[k:k + 1, cols] * shifted[a:a + tm, :]
            acc = term if acc is None else acc + term
    return acc


def _inproj_kernel(x_ref, gmix_ref, wqkv_ref, wz_ref, wbah_ref, wbal_ref, wglu_ref, cw_ref, pa_ref,
                   bglu_ref, cdw_ref, cvec_ref,
                   q_ref, k_ref, v_ref, bg_ref, zs_ref, yc_ref,
                   extq, extu, *, tiles_per_seq, tm):
    i = pl.program_id(0)
    first = (i % tiles_per_seq) == 0

    x = x_ref[...]
    n1 = x * lax.rsqrt(jnp.mean(x * x, axis=-1, keepdims=True) + EPS) * gmix_ref[...]
    n1h, n1l = _split_bf16(n1)

    @pl.when(first)
    def _():
        extq[0:QKV_HALO, :] = jnp.zeros((QKV_HALO, extq.shape[1]), F32)
        extu[0:CONF_HALO, :] = jnp.zeros((CONF_HALO, extu.shape[1]), F32)

    @pl.when(jnp.logical_not(first))
    def _():
        extq[0:QKV_HALO, :] = extq[tm:tm + QKV_HALO, :]
        extu[0:CONF_HALO, :] = extu[tm:tm + CONF_HALO, :]

    extq[QKV_HALO:QKV_HALO + tm, :] = _dot(n1h, wqkv_ref[...])
    hw = GDN_HEADS * GDN_DK
    qc = _silu(_causal_taps(extq, cw_ref, GDN_CONV, QKV_HALO, tm, slice(0, hw)))
    kc = _silu(_causal_taps(extq, cw_ref, GDN_CONV, QKV_HALO, tm, slice(hw, 2 * hw)))
    for h in range(GDN_HEADS):
        cs = slice(h * GDN_DK, (h + 1) * GDN_DK)
        qh = qc[:, cs]
        q_ref[:, cs] = qh * lax.rsqrt(jnp.sum(qh * qh, axis=-1, keepdims=True) + EPS) * (GDN_DK ** -0.5)
        kh = kc[:, cs]
        k_ref[:, cs] = kh * lax.rsqrt(jnp.sum(kh * kh, axis=-1, keepdims=True) + EPS)
    v_ref[...] = _silu(_causal_taps(extq, cw_ref, GDN_CONV, QKV_HALO, tm, slice(2 * hw, 3 * hw)))

    zs_ref[...] = _silu(_dot(n1h, wz_ref[...])).astype(BF16)

    ba = _dot(n1h, wbah_ref[...]) + _dot(n1l, wbah_ref[...]) + _dot(n1h, wbal_ref[...])
    lane = lax.broadcasted_iota(I32, ba.shape, 1)
    row = lax.broadcasted_iota(I32, ba.shape, 0)
    beta = jax.nn.sigmoid(ba)
    sp_in = ba + pa_ref[1:2, :]
    softplus = jnp.maximum(sp_in, 0.0) + jnp.log1p(jnp.exp(-jnp.abs(sp_in)))
    g = -jnp.exp(pa_ref[0:1, :]) * softplus
    rin = row & (GDN_CHUNK - 1)
    s = 1
    while s < GDN_CHUNK:
        g = g + jnp.where(rin >= s, pltpu.roll(g, s, 0), 0.0)
        s *= 2
    bg_ref[...] = jnp.where(lane < GDN_HEADS, beta, g)

    glu = _dot(n1h, wglu_ref[...]) + bglu_ref[...]
    extu[CONF_HALO:CONF_HALO + tm, :] = glu[:, :CONF_CH] * jax.nn.sigmoid(glu[:, CONF_CH:])
    c = _causal_taps(extu, cdw_ref, CONF_WIDTH, CONF_HALO, tm, slice(0, CONF_CH)) + cvec_ref[0:1, :]
    mu = jnp.mean(c, axis=-1, keepdims=True)
    cc = c - mu
    var = jnp.mean(cc * cc, axis=-1, keepdims=True)
    yln = cc * lax.rsqrt(var + EPS) * cvec_ref[1:2, :] + cvec_ref[2:3, :]
    yc_ref[...] = _silu(yln).astype(BF16)


def _gdn_kernel(q_ref, k_ref, v_ref, bg_ref, zs_ref, gout_ref, y_ref, s_ref, *, cb, nbatch, group):
    n = pl.program_id(0)

    @pl.when(n == 0)
    def _():
        s_ref[...] = jnp.zeros(s_ref.shape, F32)

    C = GDN_CHUNK
    r = lax.broadcasted_iota(I32, (C, C), 0)
    c = lax.broadcasted_iota(I32, (C, C), 1)
    causal = r >= c
    strict = r > c
    eye = r == c
    eyef = jnp.where(eye, 1.0, 0.0).astype(F32)
    gout = gout_ref[...]

    def front(b, ci, h):
        rows = slice(ci * C, (ci + 1) * C)
        cols = slice(h * GDN_DK, (h + 1) * GDN_DK)
        q = q_ref[b, rows, cols]
        k = k_ref[b, rows, cols]
        v = v_ref[b, rows, cols]
        beta = bg_ref[b, rows, h:h + 1]
        gc = bg_ref[b, rows, GDN_HEADS + h:GDN_HEADS + h + 1]
        gc_row = jnp.sum(jnp.where(eye, gc, 0.0), axis=0, keepdims=True)
        gc_last = gc[C - 1:C, :]
        eg = jnp.exp(gc)
        kb = k * beta
        return dict(
            decay=jnp.where(causal, jnp.exp(jnp.minimum(gc - gc_row, 0.0)), 0.0),
            lhs=jnp.concatenate([kb, q], axis=0).astype(BF16),
            kbf=k.astype(BF16),
            rhs=jnp.concatenate([v * beta, kb * eg], axis=1).astype(BF16),
            qd=q * eg,
            kd=(k * jnp.exp(gc_last - gc)).astype(BF16),
            gl=jnp.exp(gc_last))

    insts = [(b, ci, h) for ci in range(cb) for b in range(nbatch) for h in range(GDN_HEADS)]
    done = {}
    for g0 in range(0, len(insts), group):
        grp = insts[g0:g0 + group]
        st = [front(*key) for key in grp]
        kq = [_dot_nt(s["lhs"], s["kbf"]) for s in st]
        xp = [-jnp.where(strict, m[:C] * s["decay"], 0.0) for m, s in zip(kq, st)]
        qk = [(m[C:] * s["decay"]).astype(BF16) for m, s in zip(kq, st)]
        tinv = [eyef + x for x in xp]
        for _ in range(5):
            xpb = [x.astype(BF16) for x in xp]
            xp = [_dot(x, x) for x in xpb]
            tinv = [t + _dot(t.astype(BF16), x.astype(BF16)) for t, x in zip(tinv, xp)]
        uw = [_dot(t.astype(BF16), s["rhs"]) for t, s in zip(tinv, st)]
        for key, s, m, qkm in zip(grp, st, uw, qk):
            done[key] = dict(u=m[:, :GDN_DK],
                             wq=jnp.concatenate([m[:, GDN_DK:], s["qd"]], axis=0).astype(BF16),
                             qk=qkm, kd=s["kd"], gl=s["gl"])

    chains = [(b, h) for b in range(nbatch) for h in range(GDN_HEADS)]
    for ci in range(cb):
        rows = slice(ci * C, (ci + 1) * C)
        dd = [done[(b, ci, h)] for b, h in chains]
        s_old = [s_ref[b * GDN_HEADS + h] for b, h in chains]
        wq = [_dot(d["wq"], s.astype(BF16)) for d, s in zip(dd, s_old)]
        v_new = [(d["u"] - m[:C]).astype(BF16) for d, m in zip(dd, wq)]
        o = [m[C:] + _dot(d["qk"], vn) for d, m, vn in zip(dd, wq, v_new)]
        s_new = [s * d["gl"] + _dot_tn(d["kd"], vn) for d, s, vn in zip(dd, s_old, v_new)]
        for (b, h), sn, oo in zip(chains, s_new, o):
            cols = slice(h * GDN_DK, (h + 1) * GDN_DK)
            s_ref[b * GDN_HEADS + h] = sn
            on = oo * lax.rsqrt(jnp.mean(oo * oo, axis=-1, keepdims=True) + EPS) * gout
            y_ref[b, rows, cols] = (on * zs_ref[b, rows, cols].astype(F32)).astype(BF16)


def _router_kernel(x_ref, yg_ref, yc_ref, wo1_ref, wo2_ref, gmoe_ref, wrh_ref, wrl_ref, ltri_ref,
                   h1_ref, xn_ref, ri_ref, rf_ref, cnt_ref, carry, *, nsub):
    i = pl.program_id(0)

    @pl.when(i == 0)
    def _():
        carry[...] = jnp.zeros(carry.shape, F32)

    ts = x_ref.shape[0] // nsub
    subs = [slice(s * ts, (s + 1) * ts) for s in range(nsub)]
    each = lambda fn, *lists: [fn(*args) for args in zip(*lists)]
    lane_sum = lambda v: jnp.sum(v, axis=-1, keepdims=True)
    lane_max = lambda v: jnp.max(v, axis=-1, keepdims=True)
    lane_min = lambda v: jnp.min(v, axis=-1, keepdims=True)

    h1 = [x_ref[sl, :] + _dot(yg_ref[sl, :], wo1_ref[...]) + _dot(yc_ref[sl, :], wo2_ref[...]) for sl in subs]
    for sl, h in zip(subs, h1):
        h1_ref[sl, :] = h
    xn = each(lambda h: h * lax.rsqrt(jnp.mean(h * h, axis=-1, keepdims=True) + EPS) * gmoe_ref[...], h1)
    parts = each(_split_bf16, xn)
    for s, v in enumerate(xn):
        for f in range(TOKEN_TILE_ROWS):
            xn_ref[pl.ds(s * ts * TOKEN_TILE_ROWS + f, ts, stride=TOKEN_TILE_ROWS), :] = v[:, f * LANES:(f + 1) * LANES]
    lg = [_dot(hi, wrh_ref[...]) + _dot(lo, wrh_ref[...]) + _dot(hi, wrl_ref[...]) for hi, lo in parts]

    lane = lax.broadcasted_iota(I32, lg[0].shape, 1)
    lanef = lane.astype(F32)
    none = float(LANES)
    gl = each(lambda v: jnp.where(lane < N_GROUPS, v, NEG), lg)
    gmax = each(lane_max, gl)
    g_w = each(lambda v, m: 1.0 / lane_sum(jnp.exp(v - m)), gl, gmax)
    gidx = each(lambda v, m: lane_min(jnp.where(v == m, lanef, none)), gl, gmax)
    lane_group = ((lane - ROUTE_LANE0) >> (EXPERTS_PER_GROUP.bit_length() - 1)).astype(F32)
    is_expert = (lane >= ROUTE_LANE0) & (lane < ROUTE_LANE0 + N_EXPERTS)
    el = each(lambda v, g: jnp.where(is_expert & (lane_group == g), v, NEG), lg, gidx)
    m1 = each(lane_max, el)
    i1 = each(lambda v, m: lane_min(jnp.where(v == m, lanef, none)), el, m1)
    el2 = each(lambda v, a: jnp.where(lanef == a, NEG, v), el, i1)
    m2 = each(lane_max, el2)
    i2 = each(lambda v, m: lane_min(jnp.where(v == m, lanef, none)), el2, m2)
    ratio = each(lambda a, b: jnp.exp(b - a), m1, m2)
    gate1 = each(lambda w, r: w / (1.0 + r), g_w, ratio)
    gate2 = each(lambda w, r: w * r / (1.0 + r), g_w, ratio)

    oh1 = each(lambda a: lanef == a, i1)
    oh2 = each(lambda a: lanef == a, i2)
    oh = each(lambda a, b: jnp.where(a | b, 1.0, 0.0).astype(F32), oh1, oh2)
    within = each(lambda v: _dot(ltri_ref[...], v.astype(BF16)), oh)
    base = carry[...]
    for s in range(nsub):
        cum = within[s] + base
        rank1 = lane_sum(jnp.where(oh1[s], cum, 0.0))
        rank2 = lane_sum(jnp.where(oh2[s], cum, 0.0))
        routes = jnp.where(lane == 0, i1[s] - ROUTE_LANE0,
                           jnp.where(lane == 1, i2[s] - ROUTE_LANE0,
                                     jnp.where(lane == 2, rank1, jnp.where(lane == 3, rank2, 0.0))))
        ri_ref[subs[s], :] = routes.astype(I32)
        rf_ref[subs[s], :] = jnp.where(lane == 0, gate1[s], jnp.where(lane == 1, gate2[s], 0.0))
        base = base + jnp.sum(oh[s], axis=0, keepdims=True)
    carry[...] = base
    cnt_ref[...] = jnp.broadcast_to(base, cnt_ref.shape)


def _pos_kernel(ri_ref, cnt_ref, post_ref, blk_ref, *, tm):
    shift = EXPERT_BLOCK.bit_length() - 1
    cnt = cnt_ref[...].astype(I32)
    padded = ((cnt + (EXPERT_BLOCK - 1)) >> shift) << shift
    lane8 = lax.broadcasted_iota(I32, padded.shape, 1)
    pend = padded
    s = 1
    while s < LANES:
        pend = pend + jnp.where(lane8 >= s, pltpu.roll(pend, s, 1), 0)
        s *= 2
    pstart = (pend - padded)[0:1, :].astype(F32)

    ri = ri_ref[...]
    lane = lax.broadcasted_iota(I32, ri.shape, 1)
    e1 = ri[:, 0:1] + ROUTE_LANE0
    e2 = ri[:, 1:2] + ROUTE_LANE0
    ps1 = jnp.sum(jnp.where(lane == e1, pstart, 0.0), axis=-1, keepdims=True)
    ps2 = jnp.sum(jnp.where(lane == e2, pstart, 0.0), axis=-1, keepdims=True)
    pos1 = ps1 + ri[:, 2:3].astype(F32)
    pos2 = ps2 + ri[:, 3:4].astype(F32)
    posmat = jnp.where(lane == 0, pos1, jnp.where(lane == 1, pos2, 0.0))
    for j in range(tm // LANES):
        pt = posmat[j * LANES:(j + 1) * LANES, :].T
        post_ref[:, j * LANES:(j + 1) * LANES] = pt[0:8, :].astype(I32)

    @pl.when(pl.program_id(0) == 0)
    def _():
        pendf = pend.astype(F32)[0:1, :]
        rr = lax.broadcasted_iota(I32, (LANES, LANES), 0)
        cc = lax.broadcasted_iota(I32, (LANES, LANES), 1)
        pend_col = jnp.sum(jnp.where(rr == cc, pendf, 0.0), axis=1, keepdims=True)
        nbl = blk_ref.shape[1]
        er = lax.broadcasted_iota(I32, (LANES, nbl), 0)
        jb = (lax.broadcasted_iota(I32, (LANES, nbl), 1) * EXPERT_BLOCK).astype(F32)
        valid = (er >= ROUTE_LANE0) & (er < ROUTE_LANE0 + N_EXPERTS)
        be = jnp.sum(jnp.where(valid & (pend_col <= jb), 1.0, 0.0), axis=0, keepdims=True)
        be = jnp.minimum(be, float(N_EXPERTS - 1))
        nused = jnp.max(pendf, axis=-1, keepdims=True) * (1.0 / EXPERT_BLOCK)
        rowi = lax.broadcasted_iota(I32, blk_ref.shape, 0)
        blk_ref[...] = jnp.where(rowi == 0, be, nused).astype(I32)


def _slots_kernel(pos1_ref, pos2_ref, slot_ref, fill, sem, *, tm):
    i = pl.program_id(0)

    @pl.when(i == 0)
    def _():
        fill[...] = lax.broadcasted_iota(I32, fill.shape, 0) & (EXPERT_BLOCK - 1)
        cp = pltpu.make_async_copy(fill, slot_ref, sem)
        cp.start()
        cp.wait()

    def body(r, carry):
        t = i * tm + r
        slot_ref[pos1_ref[0, 0, r]] = t
        slot_ref[pos2_ref[0, 0, r]] = t
        return carry

    lax.fori_loop(0, tm, body, 0, unroll=16)


def _expert_kernel(be_ref, nu_ref, sl0_ref, sl1_ref, sl2_ref, xn_ref, wup_ref, wdn_ref, y_ref,
                   wupb, wdnb, wup_stage, wdn_stage, xbuf, gsem, wsem, wslot):
    assert GATHER_RING == 3, "the three slot BlockSpecs (blocks j, j+1, j+2) fix the ring depth"
    j = pl.program_id(0)
    cur = lax.rem(j, GATHER_RING)
    n_used = nu_ref[0]
    R = TOKEN_TILE_ROWS
    buf_rows = EXPERT_BLOCK * R

    def gather_row(slot_ref, b, r, prio):
        pltpu.make_async_copy(xn_ref.at[pl.ds(slot_ref[0, 0, r] * R, R), :],
                              xbuf.at[pl.ds(b * buf_rows + r * R, R), :], gsem.at[b]).start(priority=prio)

    def wait_buffer(b):
        pltpu.make_async_copy(xn_ref.at[pl.ds(0, buf_rows), :], xbuf.at[pl.ds(0, buf_rows), :], gsem.at[b]).wait()

    @pl.when(j == 0)
    def _():
        def body(g, carry):
            for u in range(ISSUE_UNROLL):
                gather_row(sl0_ref, 0, g * ISSUE_UNROLL + u, u % 2)
                gather_row(sl1_ref, 1, g * ISSUE_UNROLL + u, (u + 1) % 2)
            return carry

        lax.fori_loop(0, EXPERT_BLOCK // ISSUE_UNROLL, body, 0)

    def weight_copies(e, s):
        return (pltpu.make_async_copy(wup_ref.at[e], wup_stage.at[s], wsem.at[s]),
                pltpu.make_async_copy(wdn_ref.at[e], wdn_stage.at[s], wsem.at[s]))

    @pl.when(j == 0)
    def _():
        wslot[0] = 0
        for cp in weight_copies(be_ref[0], 0):
            cp.start()

    prev = be_ref[jnp.maximum(j - 1, 0)]

    @pl.when((j < n_used) & ((j == 0) | (be_ref[j] != prev)))
    def _():
        s = wslot[0]
        for cp in weight_copies(0, s):
            cp.wait()
        wupb[...] = wup_stage[s].astype(BF16)
        wdnb[...] = wdn_stage[s].astype(BF16)
        e = be_ref[j]
        k = lax.while_loop(lambda k: (k < n_used) & (be_ref[k] == e), lambda k: k + 1, j + 1)

        @pl.when(k < n_used)
        def _():
            for cp in weight_copies(be_ref[k], 1 - s):
                cp.start()

        wslot[0] = 1 - s

    @pl.when(j < n_used)
    def _():
        wait_buffer(cur)
        base = cur * buf_rows
        x = jnp.concatenate([xbuf[pl.ds(base + f, EXPERT_BLOCK, stride=R), :] for f in range(R)], axis=1)
        ab = _dot(x.astype(BF16), wupb[...])
        ahead = lax.rem(j + (GATHER_RING - 1), GATHER_RING)
        for r in range(EXPERT_BLOCK):
            gather_row(sl2_ref, ahead, r, r % 2)
        hmid = _silu(ab[:, :D_EXPERT]) * ab[:, D_EXPERT:]
        y = _dot(hmid.astype(BF16), wdnb[...])
        for f in range(R):
            y_ref[pl.ds(f, EXPERT_BLOCK, stride=R), :] = y[:, f * LANES:(f + 1) * LANES]

    @pl.when(j == n_used)
    def _():
        wait_buffer(cur)
        wait_buffer(lax.rem(j + 1, GATHER_RING))

    @pl.when(j >= n_used)
    def _():
        y_ref[...] = jnp.zeros(y_ref.shape, F32)


def _ple_kernel(p1c_ref, p2c_ref, p1n_ref, p2n_ref, h1_ref, rf_ref, p_ref, ys_ref, gple_ref, wg_ref, wp_ref,
                gfin_ref, out_ref, ybuf, sem):
    i = pl.program_id(0)
    last = pl.num_programs(0) - 1
    cur = i % 2
    nxt = 1 - cur
    tm = h1_ref.shape[0]
    R = TOKEN_TILE_ROWS
    slot_rows = 2 * tm * R

    def gather_row(pos_ref, b, k, r, prio):
        pltpu.make_async_copy(ys_ref.at[pl.ds(pos_ref[0, 0, r] * R, R), :],
                              ybuf.at[pl.ds(b * slot_rows + (k * tm + r) * R, R), :], sem.at[b]).start(priority=prio)

    def wait_slot(b):
        pltpu.make_async_copy(ys_ref.at[pl.ds(0, slot_rows), :], ybuf.at[pl.ds(0, slot_rows), :], sem.at[b]).wait()

    @pl.when(i == 0)
    def _():
        def body(g, carry):
            for u in range(ISSUE_UNROLL):
                gather_row(p1c_ref, 0, 0, g * ISSUE_UNROLL + u, u % 2)
                gather_row(p2c_ref, 0, 1, g * ISSUE_UNROLL + u, (u + 1) % 2)
            return carry

        lax.fori_loop(0, tm // ISSUE_UNROLL, body, 0)

    pp = _dot(p_ref[...].astype(BF16), wp_ref[...])
    for r in range(tm):
        gather_row(p1n_ref, nxt, 0, r, r % 2)
        gather_row(p2n_ref, nxt, 1, r, (r + 1) % 2)
    wait_slot(cur)
    rows = lambda k: jnp.concatenate(
        [ybuf[pl.ds(cur * slot_rows + k * tm * R + f, tm, stride=R), :] for f in range(R)], axis=1)
    rf = rf_ref[...]
    h2 = h1_ref[...] + (rf[:, 0:1] * rows(0) + rf[:, 1:2] * rows(1))
    hn = h2 * lax.rsqrt(jnp.mean(h2 * h2, axis=-1, keepdims=True) + EPS) * gple_ref[...]
    gate = jax.nn.sigmoid(_dot(hn.astype(BF16), wg_ref[...]))
    h3 = h2 + gate * pp
    out_ref[...] = h3 * lax.rsqrt(jnp.mean(h3 * h3, axis=-1, keepdims=True) + EPS) * gfin_ref[...]

    @pl.when(i == last)
    def _():
        wait_slot(nxt)


def _params(sem):
    return pltpu.CompilerParams(dimension_semantics=sem, vmem_limit_bytes=VMEM_LIMIT)


def _full(shape):
    nd = len(shape)
    return pl.BlockSpec(shape, lambda *_: (0,) * nd)


def _pad_lanes(a, lane0=0):
    out = jnp.zeros((a.shape[0], LANES), a.dtype)
    return out.at[:, lane0:lane0 + a.shape[1]].set(a)


def _layer(x2, p2, B, T, g_mix, w_in, w_conv_qkv, a_log, dt_bias, g_gdn_out, b_glu, w_conv_dw,
           b_conv_dw, ln_conv_g, ln_conv_b, w_out, g_moe, w_group_router, w_expert_router, w_up,
           w_down, g_ple, w_ple_gate, w_ple_proj, g_out):
    N, D = x2.shape
    hw = GDN_HEADS * GDN_DK
    tm = min(512, T)
    nt = N // tm

    o1 = 3 * hw
    o2 = o1 + hw
    o4 = o2 + 2 * GDN_HEADS
    wqkv = w_in[:, :o1].astype(BF16)
    wz = w_in[:, o1:o2].astype(BF16)
    wbah, wbal = _split_bf16(_pad_lanes(w_in[:, o2:o4]))
    wglu = w_in[:, o4:].astype(BF16)
    pa = jnp.concatenate([_pad_lanes(a_log[None, :], GDN_HEADS), _pad_lanes(dt_bias[None, :], GDN_HEADS),
                          jnp.zeros((6, LANES), F32)], axis=0)
    cdw = jnp.concatenate([w_conv_dw, jnp.zeros((1, CONF_CH), F32)], axis=0)
    cvec = jnp.concatenate([b_conv_dw[None], ln_conv_g[None], ln_conv_b[None], jnp.zeros((5, CONF_CH), F32)], 0)

    row_spec = lambda w: pl.BlockSpec((tm, w), lambda i: (i, 0))
    q, k, v, bg, zs, yc = pl.pallas_call(
        functools.partial(_inproj_kernel, tiles_per_seq=T // tm, tm=tm),
        grid=(nt,),
        in_specs=[row_spec(D), _full((1, D)), _full(wqkv.shape), _full(wz.shape), _full(wbah.shape),
                  _full(wbal.shape), _full(wglu.shape), _full(w_conv_qkv.shape), _full(pa.shape),
                  _full((1, 2 * CONF_CH)), _full(cdw.shape), _full(cvec.shape)],
        out_specs=[row_spec(hw), row_spec(hw), row_spec(hw), row_spec(LANES), row_spec(hw), row_spec(CONF_CH)],
        out_shape=[jax.ShapeDtypeStruct((N, hw), F32)] * 3 + [jax.ShapeDtypeStruct((N, LANES), F32),
                   jax.ShapeDtypeStruct((N, hw), BF16), jax.ShapeDtypeStruct((N, CONF_CH), BF16)],
        scratch_shapes=[pltpu.VMEM((tm + QKV_HALO, 3 * hw), F32), pltpu.VMEM((tm + CONF_HALO, CONF_CH), F32)],
        compiler_params=_params(("arbitrary",)),
        name="inproj",
    )(x2, g_mix[None], wqkv, wz, wbah, wbal, wglu, w_conv_qkv, pa, b_glu[None], cdw, cvec)

    cb = 8
    rows = cb * GDN_CHUNK
    seq_spec = lambda w: pl.BlockSpec((B, rows, w), lambda n: (0, n, 0))
    r3 = lambda a: a.reshape(B, T, a.shape[-1])
    ygdn = pl.pallas_call(
        functools.partial(_gdn_kernel, cb=cb, nbatch=B, group=8 * GDN_HEADS),
        grid=(T // rows,),
        in_specs=[seq_spec(hw), seq_spec(hw), seq_spec(hw), seq_spec(LANES), seq_spec(hw),
                  pl.BlockSpec((1, GDN_DK), lambda n: (0, 0))],
        out_specs=seq_spec(hw),
        out_shape=jax.ShapeDtypeStruct((B, T, hw), BF16),
        scratch_shapes=[pltpu.VMEM((B * GDN_HEADS, GDN_DK, GDN_DK), F32)],
        compiler_params=_params(("arbitrary",)),
        name="gdn",
    )(r3(q), r3(k), r3(v), r3(bg), r3(zs), g_gdn_out[None]).reshape(N, hw)

    wo1 = w_out[:hw].astype(BF16)
    wo2 = w_out[hw:].astype(BF16)
    wrh, wrl = _split_bf16(_pad_lanes(jnp.concatenate([w_group_router, w_expert_router], axis=1)))
    nsub = 2 if tm % 256 == 0 else 1
    ltri = jnp.tril(jnp.ones((tm // nsub, tm // nsub), F32), -1).astype(BF16)
    assert D == TOKEN_TILE_ROWS * LANES
    tile_spec = pl.BlockSpec((tm * TOKEN_TILE_ROWS, LANES), lambda i: (i, 0))
    h1, xn, ri, rf, cnt = pl.pallas_call(
        functools.partial(_router_kernel, nsub=nsub),
        grid=(nt,),
        in_specs=[row_spec(D), row_spec(hw), row_spec(CONF_CH), _full(wo1.shape), _full(wo2.shape),
                  _full((1, D)), _full(wrh.shape), _full(wrl.shape), _full(ltri.shape)],
        out_specs=[row_spec(D), tile_spec, row_spec(LANES), row_spec(LANES), _full((8, LANES))],
        out_shape=[jax.ShapeDtypeStruct((N, D), F32), jax.ShapeDtypeStruct((N * TOKEN_TILE_ROWS, LANES), F32),
                   jax.ShapeDtypeStruct((N, LANES), I32), jax.ShapeDtypeStruct((N, LANES), F32),
                   jax.ShapeDtypeStruct((8, LANES), F32)],
        scratch_shapes=[pltpu.VMEM((1, LANES), F32)],
        compiler_params=_params(("arbitrary",)),
        name="router",
    )(x2, ygdn, yc, wo1, wo2, g_moe[None], wrh, wrl, ltri)

    nb = (2 * N) // EXPERT_BLOCK + N_EXPERTS
    nbl = -(-nb // LANES) * LANES
    post, blk = pl.pallas_call(
        functools.partial(_pos_kernel, tm=tm),
        grid=(nt,),
        in_specs=[row_spec(LANES), _full((8, LANES))],
        out_specs=[pl.BlockSpec((8, tm), lambda i: (0, i)), _full((8, nbl))],
        out_shape=[jax.ShapeDtypeStruct((8, N), I32), jax.ShapeDtypeStruct((8, nbl), I32)],
        compiler_params=_params(("arbitrary",)),
        name="positions",
    )(ri, cnt)
    pos1 = post[0].reshape(nt, 1, tm)
    pos2 = post[1].reshape(nt, 1, tm)
    block_expert = blk[0, :nb]
    n_used = blk[1, :1]

    P = nb * EXPERT_BLOCK
    smem_spec = pl.BlockSpec((1, 1, tm), lambda i: (i, 0, 0), memory_space=pltpu.SMEM)
    flat_smem = pl.BlockSpec((P,), lambda i: (0,), memory_space=pltpu.SMEM)
    slots = pl.pallas_call(
        functools.partial(_slots_kernel, tm=tm),
        grid=(nt,),
        in_specs=[smem_spec, smem_spec],
        out_specs=flat_smem,
        out_shape=jax.ShapeDtypeStruct((P,), I32),
        scratch_shapes=[pltpu.VMEM((P,), I32), pltpu.SemaphoreType.DMA],
        compiler_params=_params(("arbitrary",)),
        name="slots",
    )(pos1, pos2).reshape(nb, 1, EXPERT_BLOCK)

    any_spec = pl.BlockSpec(memory_space=pl.ANY)
    blk_smem = lambda off: pl.BlockSpec((1, 1, EXPERT_BLOCK),
                                        lambda j, be, nu: (jnp.minimum(j + off, nb - 1), 0, 0),
                                        memory_space=pltpu.SMEM)
    ys = pl.pallas_call(
        _expert_kernel,
        grid_spec=pltpu.PrefetchScalarGridSpec(
            num_scalar_prefetch=2,
            grid=(nb,),
            in_specs=[blk_smem(0), blk_smem(1), blk_smem(2), any_spec, any_spec, any_spec],
            out_specs=pl.BlockSpec((EXPERT_BLOCK * TOKEN_TILE_ROWS, LANES), lambda j, be, nu: (j, 0)),
            scratch_shapes=[pltpu.VMEM((D, 2 * D_EXPERT), BF16), pltpu.VMEM((D_EXPERT, D), BF16),
                            pltpu.VMEM((2, D, 2 * D_EXPERT), F32), pltpu.VMEM((2, D_EXPERT, D), F32),
                            pltpu.VMEM((GATHER_RING * EXPERT_BLOCK * TOKEN_TILE_ROWS, LANES), F32),
                            pltpu.SemaphoreType.DMA((GATHER_RING,)), pltpu.SemaphoreType.DMA((2,)),
                            pltpu.SMEM((1,), I32)]),
        out_shape=jax.ShapeDtypeStruct((P * TOKEN_TILE_ROWS, LANES), F32),
        compiler_params=_params(("arbitrary",)),
        name="experts",
    )(block_expert, n_used, slots, slots, slots, xn, w_up, w_down)

    wg = w_ple_gate.astype(BF16)
    wp = w_ple_proj.astype(BF16)
    pos_spec = lambda off: pl.BlockSpec((1, 1, tm), lambda i: (jnp.minimum(i + off, nt - 1), 0, 0),
                                        memory_space=pltpu.SMEM)
    out = pl.pallas_call(
        _ple_kernel,
        grid=(nt,),
        in_specs=[pos_spec(0), pos_spec(0), pos_spec(1), pos_spec(1),
                  row_spec(D), row_spec(LANES), row_spec(p2.shape[1]), any_spec,
                  _full((1, D)), _full(wg.shape), _full(wp.shape), _full((1, D))],
        out_specs=row_spec(D),
        out_shape=jax.ShapeDtypeStruct((N, D), F32),
        scratch_shapes=[pltpu.VMEM((2 * 2 * tm * TOKEN_TILE_ROWS, LANES), F32), pltpu.SemaphoreType.DMA((2,))],
        compiler_params=_params(("arbitrary",)),
        name="ple",
    )(pos1, pos2, pos1, pos2, h1, rf, p2, ys, g_ple[None], wg, wp, g_out[None])
    return out


def kernel(x, p, g_mix, w_in, w_conv_qkv, a_log, dt_bias, g_gdn_out, b_glu, w_conv_dw, b_conv_dw, ln_conv_g, ln_conv_b, w_out, g_moe, w_group_router, w_expert_router, w_up, w_down, g_ple, w_ple_gate, w_ple_proj, g_final):
    B, T, D = x.shape
    depth = p.shape[0]
    assert depth == 1, "the final RMSNorm is fused into the (single) layer's last kernel"
    h = _layer(x.reshape(B * T, D), p[0].reshape(B * T, -1), B, T, g_mix[0], w_in[0], w_conv_qkv[0],
               a_log[0], dt_bias[0], g_gdn_out[0], b_glu[0], w_conv_dw[0], b_conv_dw[0], ln_conv_g[0],
               ln_conv_b[0], w_out[0], g_moe[0], w_group_router[0], w_expert_router[0], w_up[0],
               w_down[0], g_ple[0], w_ple_gate[0], w_ple_proj[0], g_final)
    return h.reshape(B, T, D)
```

```python
import functools

import jax
import jax.numpy as jnp
from jax import lax
from jax.experimental import pallas as pl
from jax.experimental.pallas import tpu as pltpu

F32 = jnp.float32
BF16 = jnp.bfloat16
I32 = jnp.int32

EPS = 1e-6
GDN_HEADS = 4
GDN_DK = 128
GDN_CONV = 4
GDN_CHUNK = 64
CONF_CH = 512
CONF_WIDTH = 31
N_GROUPS = 8
EXPERTS_PER_GROUP = 8
N_EXPERTS = 64
D_EXPERT = 256
LANES = 128
SUBLANES = 8
TOKEN_TILE_ROWS = 8
ROUTE_LANE0 = N_GROUPS
EXPERT_BLOCK = 256
GATHER_RING = 3
ISSUE_UNROLL = 8
QKV_HALO = 8
CONF_HALO = 32
NEG = -1e30
VMEM_LIMIT = 56 * 1024 * 1024


def _dot(a, b):
    return jnp.dot(a, b, preferred_element_type=F32)


def _dot_nt(a, b):
    return lax.dot_general(a, b, (((1,), (1,)), ((), ())), preferred_element_type=F32)


def _dot_tn(a, b):
    return lax.dot_general(a, b, (((0,), (0,)), ((), ())), preferred_element_type=F32)


def _silu(x):
    return x * jax.nn.sigmoid(x)


def _split_bf16(x):
    hi = x.astype(BF16)
    lo = (x - hi.astype(F32)).astype(BF16)
    return hi, lo


def _causal_taps(ext_ref, w_ref, n_taps, halo, tm, cols):
    rows = ext_ref.shape[0]
    ext = ext_ref[:, cols]
    first = halo - (n_taps - 1)
    acc = None
    for res in range(SUBLANES):
        taps = [k for k in range(n_taps) if (first + k) % SUBLANES == res]
        if not taps:
            continue
        shifted = ext if res == 0 else pltpu.roll(ext, rows - res, 0)
        for k in taps:
            a = first + k - res
            term = w_ref[k:k + 1, cols] * shifted[a:a + tm, :]
            acc = term if acc is None else acc + term
    return acc


def _inproj_kernel(x_ref, gmix_ref, wqkv_ref, wz_ref, wbah_ref, wbal_ref, wglu_ref, cw_ref, pa_ref,
                   bglu_ref, cdw_ref, cvec_ref,
                   q_ref, k_ref, v_ref, bg_ref, zs_ref, yc_ref,
                   extq, extu, *, tiles_per_seq, tm):
    i = pl.program_id(0)
    first = (i % tiles_per_seq) == 0

    x = x_ref[...]
    n1 = x * lax.rsqrt(jnp.mean(x * x, axis=-1, keepdims=True) + EPS) * gmix_ref[...]
    n1h, n1l = _split_bf16(n1)

    @pl.when(first)
    def _():
        extq[0:QKV_HALO, :] = jnp.zeros((QKV_HALO, extq.shape[1]), F32)
        extu[0:CONF_HALO, :] = jnp.zeros((CONF_HALO, extu.shape[1]), F32)

    @pl.when(jnp.logical_not(first))
    def _():
        extq[0:QKV_HALO, :] = extq[tm:tm + QKV_HALO, :]
        extu[0:CONF_HALO, :] = extu[tm:tm + CONF_HALO, :]

    extq[QKV_HALO:QKV_HALO + tm, :] = _dot(n1h, wqkv_ref[...])
    hw = GDN_HEADS * GDN_DK
    qc = _silu(_causal_taps(extq, cw_ref, GDN_CONV, QKV_HALO, tm, slice(0, hw)))
    kc = _silu(_causal_taps(extq, cw_ref, GDN_CONV, QKV_HALO, tm, slice(hw, 2 * hw)))
    for h in range(GDN_HEADS):
        cs = slice(h * GDN_DK, (h + 1) * GDN_DK)
        qh = qc[:, cs]
        q_ref[:, cs] = qh * lax.rsqrt(jnp.sum(qh * qh, axis=-1, keepdims=True) + EPS) * (GDN_DK ** -0.5)
        kh = kc[:, cs]
        k_ref[:, cs] = kh * lax.rsqrt(jnp.sum(kh * kh, axis=-1, keepdims=True) + EPS)
    v_ref[...] = _silu(_causal_taps(extq, cw_ref, GDN_CONV, QKV_HALO, tm, slice(2 * hw, 3 * hw)))

    zs_ref[...] = _silu(_dot(n1h, wz_ref[...])).astype(BF16)

    ba = _dot(n1h, wbah_ref[...]) + _dot(n1l, wbah_ref[...]) + _dot(n1h, wbal_ref[...])
    lane = lax.broadcasted_iota(I32, ba.shape, 1)
    row = lax.broadcasted_iota(I32, ba.shape, 0)
    beta = jax.nn.sigmoid(ba)
    sp_in = ba + pa_ref[1:2, :]
    softplus = jnp.maximum(sp_in, 0.0) + jnp.log1p(jnp.exp(-jnp.abs(sp_in)))
    g = -jnp.exp(pa_ref[0:1, :]) * softplus
    rin = row & (GDN_CHUNK - 1)
    s = 1
    while s < GDN_CHUNK:
        g = g + jnp.where(rin >= s, pltpu.roll(g, s, 0), 0.0)
        s *= 2
    bg_ref[...] = jnp.where(lane < GDN_HEADS, beta, g)

    glu = _dot(n1h, wglu_ref[...]) + bglu_ref[...]
    extu[CONF_HALO:CONF_HALO + tm, :] = glu[:, :CONF_CH] * jax.nn.sigmoid(glu[:, CONF_CH:])
    c = _causal_taps(extu, cdw_ref, CONF_WIDTH, CONF_HALO, tm, slice(0, CONF_CH)) + cvec_ref[0:1, :]
    mu = jnp.mean(c, axis=-1, keepdims=True)
    cc = c - mu
    var = jnp.mean(cc * cc, axis=-1, keepdims=True)
    yln = cc * lax.rsqrt(var + EPS) * cvec_ref[1:2, :] + cvec_ref[2:3, :]
    yc_ref[...] = _silu(yln).astype(BF16)


def _gdn_kernel(q_ref, k_ref, v_ref, bg_ref, zs_ref, gout_ref, y_ref, s_ref, *, cb, nbatch, group):
    n = pl.program_id(0)

    @pl.when(n == 0)
    def _():
        s_ref[...] = jnp.zeros(s_ref.shape, F32)

    C = GDN_CHUNK
    r = lax.broadcasted_iota(I32, (C, C), 0)
    c = lax.broadcasted_iota(I32, (C, C), 1)
    causal = r >= c
    strict = r > c
    eye = r == c
    eyef = jnp.where(eye, 1.0, 0.0).astype(F32)
    gout = gout_ref[...]

    def front(b, ci, h):
        rows = slice(ci * C, (ci + 1) * C)
        cols = slice(h * GDN_DK, (h + 1) * GDN_DK)
        q = q_ref[b, rows, cols]
        k = k_ref[b, rows, cols]
        v = v_ref[b, rows, cols]
        beta = bg_ref[b, rows, h:h + 1]
        gc = bg_ref[b, rows, GDN_HEADS + h:GDN_HEADS + h + 1]
        gc_row = jnp.sum(jnp.where(eye, gc, 0.0), axis=0, keepdims=True)
        gc_last = gc[C - 1:C, :]
        eg = jnp.exp(gc)
        kb = k * beta
        return dict(
            decay=jnp.where(causal, jnp.exp(jnp.minimum(gc - gc_row, 0.0)), 0.0),
            lhs=jnp.concatenate([kb, q], axis=0).astype(BF16),
            kbf=k.astype(BF16),
            rhs=jnp.concatenate([v * beta, kb * eg], axis=1).astype(BF16),
            qd=q * eg,
            kd=(k * jnp.exp(gc_last - gc)).astype(BF16),
            gl=jnp.exp(gc_last))

    insts = [(b, ci, h) for ci in range(cb) for b in range(nbatch) for h in range(GDN_HEADS)]
    done = {}
    for g0 in range(0, len(insts), group):
        grp = insts[g0:g0 + group]
        st = [front(*key) for key in grp]
        kq = [_dot_nt(s["lhs"], s["kbf"]) for s in st]
        xp = [-jnp.where(strict, m[:C] * s["decay"], 0.0) for m, s in zip(kq, st)]
        qk = [(m[C:] * s["decay"]).astype(BF16) for m, s in zip(kq, st)]
        tinv = [eyef + x for x in xp]
        for _ in range(5):
            xpb = [x.astype(BF16) for x in xp]
            xp = [_dot(x, x) for x in xpb]
            tinv = [t + _dot(t.astype(BF16), x.astype(BF16)) for t, x in zip(tinv, xp)]
        uw = [_dot(t.astype(BF16), s["rhs"]) for t, s in zip(tinv, st)]
        for key, s, m, qkm in zip(grp, st, uw, qk):
            done[key] = dict(u=m[:, :GDN_DK],
                             wq=jnp.concatenate([m[:, GDN_DK:], s["qd"]], axis=0).astype(BF16),
                             qk=qkm, kd=s["kd"], gl=s["gl"])

    chains = [(b, h) for b in range(nbatch) for h in range(GDN_HEADS)]
    for ci in range(cb):
        rows = slice(ci * C, (ci + 1) * C)
        dd = [done[(b, ci, h)] for b, h in chains]
        s_old = [s_ref[b * GDN_HEADS + h] for b, h in chains]
        wq = [_dot(d["wq"], s.astype(BF16)) for d, s in zip(dd, s_old)]
        v_new = [(d["u"] - m[:C]).astype(BF16) for d, m in zip(dd, wq)]
        o = [m[C:] + _dot(d["qk"], vn) for d, m, vn in zip(dd, wq, v_new)]
        s_new = [s * d["gl"] + _dot_tn(d["kd"], vn) for d, s, vn in zip(dd, s_old, v_new)]
        for (b, h), sn, oo in zip(chains, s_new, o):
            cols = slice(h * GDN_DK, (h + 1) * GDN_DK)
            s_ref[b * GDN_HEADS + h] = sn
            on = oo * lax.rsqrt(jnp.mean(oo * oo, axis=-1, keepdims=True) + EPS) * gout
            y_ref[b, rows, cols] = (on * zs_ref[b, rows, cols].astype(F32)).astype(BF16)


def _router_kernel(x_ref, yg_ref, yc_ref, wo1_ref, wo2_ref, gmoe_ref, wrh_ref, wrl_ref, ltri_ref,
                   h1_ref, xn_ref, ri_ref, rf_ref, cnt_ref, carry, *, nsub):
    i = pl.program_id(0)

    @pl.when(i == 0)
    def _():
        carry[...] = jnp.zeros(carry.shape, F32)

    ts = x_ref.shape[0] // nsub
    subs = [slice(s * ts, (s + 1) * ts) for s in range(nsub)]
    each = lambda fn, *lists: [fn(*args) for args in zip(*lists)]
    lane_sum = lambda v: jnp.sum(v, axis=-1, keepdims=True)
    lane_max = lambda v: jnp.max(v, axis=-1, keepdims=True)
    lane_min = lambda v: jnp.min(v, axis=-1, keepdims=True)

    h1 = [x_ref[sl, :] + _dot(yg_ref[sl, :], wo1_ref[...]) + _dot(yc_ref[sl, :], wo2_ref[...]) for sl in subs]
    for sl, h in zip(subs, h1):
        h1_ref[sl, :] = h
    xn = each(lambda h: h * lax.rsqrt(jnp.mean(h * h, axis=-1, keepdims=True) + EPS) * gmoe_ref[...], h1)
    parts = each(_split_bf16, xn)
    for s, v in enumerate(xn):
        for f in range(TOKEN_TILE_ROWS):
            xn_ref[pl.ds(s * ts * TOKEN_TILE_ROWS + f, ts, stride=TOKEN_TILE_ROWS), :] = v[:, f * LANES:(f + 1) * LANES]
    lg = [_dot(hi, wrh_ref[...]) + _dot(lo, wrh_ref[...]) + _dot(hi, wrl_ref[...]) for hi, lo in parts]

    lane = lax.broadcasted_iota(I32, lg[0].shape, 1)
    lanef = lane.astype(F32)
    none = float(LANES)
    gl = each(lambda v: jnp.where(lane < N_GROUPS, v, NEG), lg)
    gmax = each(lane_max, gl)
    g_w = each(lambda v, m: 1.0 / lane_sum(jnp.exp(v - m)), gl, gmax)
    gidx = each(lambda v, m: lane_min(jnp.where(v == m, lanef, none)), gl, gmax)
    lane_group = ((lane - ROUTE_LANE0) >> (EXPERTS_PER_GROUP.bit_length() - 1)).astype(F32)
    is_expert = (lane >= ROUTE_LANE0) & (lane < ROUTE_LANE0 + N_EXPERTS)
    el = each(lambda v, g: jnp.where(is_expert & (lane_group == g), v, NEG), lg, gidx)
    m1 = each(lane_max, el)
    i1 = each(lambda v, m: lane_min(jnp.where(v == m, lanef, none)), el, m1)
    el2 = each(lambda v, a: jnp.where(lanef == a, NEG, v), el, i1)
    m2 = each(lane_max, el2)
    i2 = each(lambda v, m: lane_min(jnp.where(v == m, lanef, none)), el2, m2)
    ratio = each(lambda a, b: jnp.exp(b - a), m1, m2)
    gate1 = each(lambda w, r: w / (1.0 + r), g_w, ratio)
    gate2 = each(lambda w, r: w * r / (1.0 + r), g_w, ratio)

    oh1 = each(lambda a: lanef == a, i1)
    oh2 = each(lambda a: lanef == a, i2)
    oh = each(lambda a, b: jnp.where(a | b, 1.0, 0.0).astype(F32), oh1, oh2)
    within = each(lambda v: _dot(ltri_ref[...], v.astype(BF16)), oh)
    base = carry[...]
    for s in range(nsub):
        cum = within[s] + base
        rank1 = lane_sum(jnp.where(oh1[s], cum, 0.0))
        rank2 = lane_sum(jnp.where(oh2[s], cum, 0.0))
        routes = jnp.where(lane == 0, i1[s] - ROUTE_LANE0,
                           jnp.where(lane == 1, i2[s] - ROUTE_LANE0,
                                     jnp.where(lane == 2, rank1, jnp.where(lane == 3, rank2, 0.0))))
        ri_ref[subs[s], :] = routes.astype(I32)
        rf_ref[subs[s], :] = jnp.where(lane == 0, gate1[s], jnp.where(lane == 1, gate2[s], 0.0))
        base = base + jnp.sum(oh[s], axis=0, keepdims=True)
    carry[...] = base
    cnt_ref[...] = jnp.broadcast_to(base, cnt_ref.shape)


def _pos_kernel(ri_ref, cnt_ref, post_ref, blk_ref, *, tm):
    shift = EXPERT_BLOCK.bit_length() - 1
    cnt = cnt_ref[...].astype(I32)
    padded = ((cnt + (EXPERT_BLOCK - 1)) >> shift) << shift
    lane8 = lax.broadcasted_iota(I32, padded.shape, 1)
    pend = padded
    s = 1
    while s < LANES:
        pend = pend + jnp.where(lane8 >= s, pltpu.roll(pend, s, 1), 0)
        s *= 2
    pstart = (pend - padded)[0:1, :].astype(F32)

    ri = ri_ref[...]
    lane = lax.broadcasted_iota(I32, ri.shape, 1)
    e1 = ri[:, 0:1] + ROUTE_LANE0
    e2 = ri[:, 1:2] + ROUTE_LANE0
    ps1 = jnp.sum(jnp.where(lane == e1, pstart, 0.0), axis=-1, keepdims=True)
    ps2 = jnp.sum(jnp.where(lane == e2, pstart, 0.0), axis=-1, keepdims=True)
    pos1 = ps1 + ri[:, 2:3].astype(F32)
    pos2 = ps2 + ri[:, 3:4].astype(F32)
    posmat = jnp.where(lane == 0, pos1, jnp.where(lane == 1, pos2, 0.0))
    for j in range(tm // LANES):
        pt = posmat[j * LANES:(j + 1) * LANES, :].T
        post_ref[:, j * LANES:(j + 1) * LANES] = pt[0:8, :].astype(I32)

    @pl.when(pl.program_id(0) == 0)
    def _():
        pendf = pend.astype(F32)[0:1, :]
        rr = lax.broadcasted_iota(I32, (LANES, LANES), 0)
        cc = lax.broadcasted_iota(I32, (LANES, LANES), 1)
        pend_col = jnp.sum(jnp.where(rr == cc, pendf, 0.0), axis=1, keepdims=True)
        nbl = blk_ref.shape[1]
        er = lax.broadcasted_iota(I32, (LANES, nbl), 0)
        jb = (lax.broadcasted_iota(I32, (LANES, nbl), 1) * EXPERT_BLOCK).astype(F32)
        valid = (er >= ROUTE_LANE0) & (er < ROUTE_LANE0 + N_EXPERTS)
        be = jnp.sum(jnp.where(valid & (pend_col <= jb), 1.0, 0.0), axis=0, keepdims=True)
        be = jnp.minimum(be, float(N_EXPERTS - 1))
        nused = jnp.max(pendf, axis=-1, keepdims=True) * (1.0 / EXPERT_BLOCK)
        rowi = lax.broadcasted_iota(I32, blk_ref.shape, 0)
        blk_ref[...] = jnp.where(rowi == 0, be, nused).astype(I32)


def _slots_kernel(pos1_ref, pos2_ref, slot_ref, fill, sem, *, tm):
    i = pl.program_id(0)

    @pl.when(i == 0)
    def _():
        fill[...] = lax.broadcasted_iota(I32, fill.shape, 0) & (EXPERT_BLOCK - 1)
        cp = pltpu.make_async_copy(fill, slot_ref, sem)
        cp.start()
        cp.wait()

    def body(r, carry):
        t = i * tm + r
        slot_ref[pos1_ref[0, 0, r]] = t
        slot_ref[pos2_ref[0, 0, r]] = t
        return carry

    lax.fori_loop(0, tm, body, 0, unroll=16)


def _expert_kernel(be_ref, nu_ref, sl0_ref, sl1_ref, sl2_ref, xn_ref, wup_ref, wdn_ref, y_ref,
                   wupb, wdnb, wup_stage, wdn_stage, xbuf, gsem, wsem, wslot):
    assert GATHER_RING == 3, "the three slot BlockSpecs (blocks j, j+1, j+2) fix the ring depth"
    j = pl.program_id(0)
    cur = lax.rem(j, GATHER_RING)
    n_used = nu_ref[0]
    R = TOKEN_TILE_ROWS
    buf_rows = EXPERT_BLOCK * R

    def gather_row(slot_ref, b, r, prio):
        pltpu.make_async_copy(xn_ref.at[pl.ds(slot_ref[0, 0, r] * R, R), :],
                              xbuf.at[pl.ds(b * buf_rows + r * R, R), :], gsem.at[b]).start(priority=prio)

    def wait_buffer(b):
        pltpu.make_async_copy(xn_ref.at[pl.ds(0, buf_rows), :], xbuf.at[pl.ds(0, buf_rows), :], gsem.at[b]).wait()

    @pl.when(j == 0)
    def _():
        def body(g, carry):
            for u in range(ISSUE_UNROLL):
                gather_row(sl0_ref, 0, g * ISSUE_UNROLL + u, u % 2)
                gather_row(sl1_ref, 1, g * ISSUE_UNROLL + u, (u + 1) % 2)
            return carry

        lax.fori_loop(0, EXPERT_BLOCK // ISSUE_UNROLL, body, 0)

    def weight_copies(e, s):
        return (pltpu.make_async_copy(wup_ref.at[e], wup_stage.at[s], wsem.at[s]),
                pltpu.make_async_copy(wdn_ref.at[e], wdn_stage.at[s], wsem.at[s]))

    @pl.when(j == 0)
    def _():
        wslot[0] = 0
        for cp in weight_copies(be_ref[0], 0):
            cp.start()

    prev = be_ref[jnp.maximum(j - 1, 0)]

    @pl.when((j < n_used) & ((j == 0) | (be_ref[j] != prev)))
    def _():
        s = wslot[0]
        for cp in weight_copies(0, s):
            cp.wait()
        wupb[...] = wup_stage[s].astype(BF16)
        wdnb[...] = wdn_stage[s].astype(BF16)
        e = be_ref[j]
        k = lax.while_loop(lambda k: (k < n_used) & (be_ref[k] == e), lambda k: k + 1, j + 1)

        @pl.when(k < n_used)
        def _():
            for cp in weight_copies(be_ref[k], 1 - s):
                cp.start()

        wslot[0] = 1 - s

    @pl.when(j < n_used)
    def _():
        wait_buffer(cur)
        base = cur * buf_rows
        x = jnp.concatenate([xbuf[pl.ds(base + f, EXPERT_BLOCK, stride=R), :] for f in range(R)], axis=1)
        ab = _dot(x.astype(BF16), wupb[...])
        ahead = lax.rem(j + (GATHER_RING - 1), GATHER_RING)
        for r in range(EXPERT_BLOCK):
            gather_row(sl2_ref, ahead, r, r % 2)
        hmid = _silu(ab[:, :D_EXPERT]) * ab[:, D_EXPERT:]
        y = _dot(hmid.astype(BF16), wdnb[...])
        for f in range(R):
            y_ref[pl.ds(f, EXPERT_BLOCK, stride=R), :] = y[:, f * LANES:(f + 1) * LANES]

    @pl.when(j == n_used)
    def _():
        wait_buffer(cur)
        wait_buffer(lax.rem(j + 1, GATHER_RING))

    @pl.when(j >= n_used)
    def _():
        y_ref[...] = jnp.zeros(y_ref.shape, F32)


def _ple_kernel(p1c_ref, p2c_ref, p1n_ref, p2n_ref, h1_ref, rf_ref, p_ref, ys_ref, gple_ref, wg_ref, wp_ref,
                gfin_ref, out_ref, ybuf, sem):
    i = pl.program_id(0)
    last = pl.num_programs(0) - 1
    cur = i % 2
    nxt = 1 - cur
    tm = h1_ref.shape[0]
    R = TOKEN_TILE_ROWS
    slot_rows = 2 * tm * R

    def gather_row(pos_ref, b, k, r, prio):
        pltpu.make_async_copy(ys_ref.at[pl.ds(pos_ref[0, 0, r] * R, R), :],
                              ybuf.at[pl.ds(b * slot_rows + (k * tm + r) * R, R), :], sem.at[b]).start(priority=prio)

    def wait_slot(b):
        pltpu.make_async_copy(ys_ref.at[pl.ds(0, slot_rows), :], ybuf.at[pl.ds(0, slot_rows), :], sem.at[b]).wait()

    @pl.when(i == 0)
    def _():
        def body(g, carry):
            for u in range(ISSUE_UNROLL):
                gather_row(p1c_ref, 0, 0, g * ISSUE_UNROLL + u, u % 2)
                gather_row(p2c_ref, 0, 1, g * ISSUE_UNROLL + u, (u + 1) % 2)
            return carry

        lax.fori_loop(0, tm // ISSUE_UNROLL, body, 0)

    pp = _dot(p_ref[...].astype(BF16), wp_ref[...])
    for r in range(tm):
        gather_row(p1n_ref, nxt, 0, r, r % 2)
        gather_row(p2n_ref, nxt, 1, r, (r + 1) % 2)
    wait_slot(cur)
    rows = lambda k: jnp.concatenate(
        [ybuf[pl.ds(cur * slot_rows + k * tm * R + f, tm, stride=R), :] for f in range(R)], axis=1)
    rf = rf_ref[...]
    h2 = h1_ref[...] + (rf[:, 0:1] * rows(0) + rf[:, 1:2] * rows(1))
    hn = h2 * lax.rsqrt(jnp.mean(h2 * h2, axis=-1, keepdims=True) + EPS) * gple_ref[...]
    gate = jax.nn.sigmoid(_dot(hn.astype(BF16), wg_ref[...]))
    h3 = h2 + gate * pp
    out_ref[...] = h3 * lax.rsqrt(jnp.mean(h3 * h3, axis=-1, keepdims=True) + EPS) * gfin_ref[...]

    @pl.when(i == last)
    def _():
        wait_slot(nxt)


def _params(sem):
    return pltpu.CompilerParams(dimension_semantics=sem, vmem_limit_bytes=VMEM_LIMIT)


def _full(shape):
    nd = len(shape)
    return pl.BlockSpec(shape, lambda *_: (0,) * nd)


def _pad_lanes(a, lane0=0):
    out = jnp.zeros((a.shape[0], LANES), a.dtype)
    return out.at[:, lane0:lane0 + a.shape[1]].set(a)


def _layer(x2, p2, B, T, g_mix, w_in, w_conv_qkv, a_log, dt_bias, g_gdn_out, b_glu, w_conv_dw,
           b_conv_dw, ln_conv_g, ln_conv_b, w_out, g_moe, w_group_router, w_expert_router, w_up,
           w_down, g_ple, w_ple_gate, w_ple_proj, g_out):
    N, D = x2.shape
    hw = GDN_HEADS * GDN_DK
    tm = min(512, T)
    nt = N // tm

    o1 = 3 * hw
    o2 = o1 + hw
    o4 = o2 + 2 * GDN_HEADS
    wqkv = w_in[:, :o1].astype(BF16)
    wz = w_in[:, o1:o2].astype(BF16)
    wbah, wbal = _split_bf16(_pad_lanes(w_in[:, o2:o4]))
    wglu = w_in[:, o4:].astype(BF16)
    pa = jnp.concatenate([_pad_lanes(a_log[None, :], GDN_HEADS), _pad_lanes(dt_bias[None, :], GDN_HEADS),
                          jnp.zeros((6, LANES), F32)], axis=0)
    cdw = jnp.concatenate([w_conv_dw, jnp.zeros((1, CONF_CH), F32)], axis=0)
    cvec = jnp.concatenate([b_conv_dw[None], ln_conv_g[None], ln_conv_b[None], jnp.zeros((5, CONF_CH), F32)], 0)

    row_spec = lambda w: pl.BlockSpec((tm, w), lambda i: (i, 0))
    q, k, v, bg, zs, yc = pl.pallas_call(
        functools.partial(_inproj_kernel, tiles_per_seq=T // tm, tm=tm),
        grid=(nt,),
        in_specs=[row_spec(D), _full((1, D)), _full(wqkv.shape), _full(wz.shape), _full(wbah.shape),
                  _full(wbal.shape), _full(wglu.shape), _full(w_conv_qkv.shape), _full(pa.shape),
                  _full((1, 2 * CONF_CH)), _full(cdw.shape), _full(cvec.shape)],
        out_specs=[row_spec(hw), row_spec(hw), row_spec(hw), row_spec(LANES), row_spec(hw), row_spec(CONF_CH)],
        out_shape=[jax.ShapeDtypeStruct((N, hw), F32)] * 3 + [jax.ShapeDtypeStruct((N, LANES), F32),
                   jax.ShapeDtypeStruct((N, hw), BF16), jax.ShapeDtypeStruct((N, CONF_CH), BF16)],
        scratch_shapes=[pltpu.VMEM((tm + QKV_HALO, 3 * hw), F32), pltpu.VMEM((tm + CONF_HALO, CONF_CH), F32)],
        compiler_params=_params(("arbitrary",)),
        name="inproj",
    )(x2, g_mix[None], wqkv, wz, wbah, wbal, wglu, w_conv_qkv, pa, b_glu[None], cdw, cvec)

    cb = 8
    rows = cb * GDN_CHUNK
    seq_spec = lambda w: pl.BlockSpec((B, rows, w), lambda n: (0, n, 0))
    r3 = lambda a: a.reshape(B, T, a.shape[-1])
    ygdn = pl.pallas_call(
        functools.partial(_gdn_kernel, cb=cb, nbatch=B, group=8 * GDN_HEADS),
        grid=(T // rows,),
        in_specs=[seq_spec(hw), seq_spec(hw), seq_spec(hw), seq_spec(LANES), seq_spec(hw),
                  pl.BlockSpec((1, GDN_DK), lambda n: (0, 0))],
        out_specs=seq_spec(hw),
        out_shape=jax.ShapeDtypeStruct((B, T, hw), BF16),
        scratch_shapes=[pltpu.VMEM((B * GDN_HEADS, GDN_DK, GDN_DK), F32)],
        compiler_params=_params(("arbitrary",)),
        name="gdn",
    )(r3(q), r3(k), r3(v), r3(bg), r3(zs), g_gdn_out[None]).reshape(N, hw)

    wo1 = w_out[:hw].astype(BF16)
    wo2 = w_out[hw:].astype(BF16)
    wrh, wrl = _split_bf16(_pad_lanes(jnp.concatenate([w_group_router, w_expert_router], axis=1)))
    nsub = 2 if tm % 256 == 0 else 1
    ltri = jnp.tril(jnp.ones((tm // nsub, tm // nsub), F32), -1).astype(BF16)
    assert D == TOKEN_TILE_ROWS * LANES
    tile_spec = pl.BlockSpec((tm * TOKEN_TILE_ROWS, LANES), lambda i: (i, 0))
    h1, xn, ri, rf, cnt = pl.pallas_call(
        functools.partial(_router_kernel, nsub=nsub),
        grid=(nt,),
        in_specs=[row_spec(D), row_spec(hw), row_spec(CONF_CH), _full(wo1.shape), _full(wo2.shape),
                  _full((1, D)), _full(wrh.shape), _full(wrl.shape), _full(ltri.shape)],
        out_specs=[row_spec(D), tile_spec, row_spec(LANES), row_spec(LANES), _full((8, LANES))],
        out_shape=[jax.ShapeDtypeStruct((N, D), F32), jax.ShapeDtypeStruct((N * TOKEN_TILE_ROWS, LANES), F32),
                   jax.ShapeDtypeStruct((N, LANES), I32), jax.ShapeDtypeStruct((N, LANES), F32),
                   jax.ShapeDtypeStruct((8, LANES), F32)],
        scratch_shapes=[pltpu.VMEM((1, LANES), F32)],
        compiler_params=_params(("arbitrary",)),
        name="router",
    )(x2, ygdn, yc, wo1, wo2, g_moe[None], wrh, wrl, ltri)

    nb = (2 * N) // EXPERT_BLOCK + N_EXPERTS
    nbl = -(-nb // LANES) * LANES
    tp = min(4 * tm, N)
    post, blk = pl.pallas_call(
        functools.partial(_pos_kernel, tm=tp),
        grid=(N // tp,),
        in_specs=[pl.BlockSpec((tp, LANES), lambda i: (i, 0)), _full((8, LANES))],
        out_specs=[pl.BlockSpec((8, tp), lambda i: (0, i)), _full((8, nbl))],
        out_shape=[jax.ShapeDtypeStruct((8, N), I32), jax.ShapeDtypeStruct((8, nbl), I32)],
        compiler_params=_params(("arbitrary",)),
        name="positions",
    )(ri, cnt)
    pos1 = post[0].reshape(nt, 1, tm)
    pos2 = post[1].reshape(nt, 1, tm)
    block_expert = blk[0, :nb]
    n_used = blk[1, :1]

    P = nb * EXPERT_BLOCK
    smem_spec = pl.BlockSpec((1, 1, tp), lambda i: (i, 0, 0), memory_space=pltpu.SMEM)
    flat_smem = pl.BlockSpec((P,), lambda i: (0,), memory_space=pltpu.SMEM)
    slots = pl.pallas_call(
        functools.partial(_slots_kernel, tm=tp),
        grid=(N // tp,),
        in_specs=[smem_spec, smem_spec],
        out_specs=flat_smem,
        out_shape=jax.ShapeDtypeStruct((P,), I32),
        scratch_shapes=[pltpu.VMEM((P,), I32), pltpu.SemaphoreType.DMA],
        compiler_params=_params(("arbitrary",)),
        name="slots",
    )(post[0].reshape(N // tp, 1, tp), post[1].reshape(N // tp, 1, tp)).reshape(nb, 1, EXPERT_BLOCK)

    any_spec = pl.BlockSpec(memory_space=pl.ANY)
    blk_smem = lambda off: pl.BlockSpec((1, 1, EXPERT_BLOCK),
                                        lambda j, be, nu: (jnp.minimum(j + off, nb - 1), 0, 0),
                                        memory_space=pltpu.SMEM)
    ys = pl.pallas_call(
        _expert_kernel,
        grid_spec=pltpu.PrefetchScalarGridSpec(
            num_scalar_prefetch=2,
            grid=(nb,),
            in_specs=[blk_smem(0), blk_smem(1), blk_smem(2), any_spec, any_spec, any_spec],
            out_specs=pl.BlockSpec((EXPERT_BLOCK * TOKEN_TILE_ROWS, LANES), lambda j, be, nu: (j, 0)),
            scratch_shapes=[pltpu.VMEM((D, 2 * D_EXPERT), BF16), pltpu.VMEM((D_EXPERT, D), BF16),
                            pltpu.VMEM((2, D, 2 * D_EXPERT), F32), pltpu.VMEM((2, D_EXPERT, D), F32),
                            pltpu.VMEM((GATHER_RING * EXPERT_BLOCK * TOKEN_TILE_ROWS, LANES), F32),
                            pltpu.SemaphoreType.DMA((GATHER_RING,)), pltpu.SemaphoreType.DMA((2,)),
                            pltpu.SMEM((1,), I32)]),
        out_shape=jax.ShapeDtypeStruct((P * TOKEN_TILE_ROWS, LANES), F32),
        compiler_params=_params(("arbitrary",)),
        name="experts",
    )(block_expert, n_used, slots, slots, slots, xn, w_up, w_down)

    wg = w_ple_gate.astype(BF16)
    wp = w_ple_proj.astype(BF16)
    pos_spec = lambda off: pl.BlockSpec((1, 1, tm), lambda i: (jnp.minimum(i + off, nt - 1), 0, 0),
                                        memory_space=pltpu.SMEM)
    out = pl.pallas_call(
        _ple_kernel,
        grid=(nt,),
        in_specs=[pos_spec(0), pos_spec(0), pos_spec(1), pos_spec(1),
                  row_spec(D), row_spec(LANES), row_spec(p2.shape[1]), any_spec,
                  _full((1, D)), _full(wg.shape), _full(wp.shape), _full((1, D))],
        out_specs=row_spec(D),
        out_shape=jax.ShapeDtypeStruct((N, D), F32),
        scratch_shapes=[pltpu.VMEM((2 * 2 * tm * TOKEN_TILE_ROWS, LANES), F32), pltpu.SemaphoreType.DMA((2,))],
        compiler_params=_params(("arbitrary",)),
        name="ple",
    )(pos1, pos2, pos1, pos2, h1, rf, p2, ys, g_ple[None], wg, wp, g_out[None])
    return out


def kernel(x, p, g_mix, w_in, w_conv_qkv, a_log, dt_bias, g_gdn_out, b_glu, w_conv_dw, b_conv_dw, ln_conv_g, ln_conv_b, w_out, g_moe, w_group_router, w_expert_router, w_up, w_down, g_ple, w_ple_gate, w_ple_proj, g_final):
    B, T, D = x.shape
    depth = p.shape[0]
    assert depth == 1, "the final RMSNorm is fused into the (single) layer's last kernel"
    h = _layer(x.reshape(B * T, D), p[0].reshape(B * T, -1), B, T, g_mix[0], w_in[0], w_conv_qkv[0],
               a_log[0], dt_bias[0], g_gdn_out[0], b_glu[0], w_conv_dw[0], b_conv_dw[0], ln_conv_g[0],
               ln_conv_b[0], w_out[0], g_moe[0], w_group_router[0], w_expert_router[0], w_up[0],
               w_down[0], g_ple[0], w_ple_gate[0], w_ple_proj[0], g_final)
    return h.reshape(B, T, D)
```

```python
import functools

import jax
import jax.numpy as jnp
from jax import lax
from jax.experimental import pallas as pl
from jax.experimental.pallas import tpu as pltpu

F32 = jnp.float32
BF16 = jnp.bfloat16
I32 = jnp.int32

EPS = 1e-6
GDN_HEADS = 4
GDN_DK = 128
GDN_CONV = 4
GDN_CHUNK = 64
CONF_CH = 512
CONF_WIDTH = 31
N_GROUPS = 8
EXPERTS_PER_GROUP = 8
N_EXPERTS = 64
D_EXPERT = 256
LANES = 128
SUBLANES = 8
TOKEN_TILE_ROWS = 8
ROUTE_LANE0 = N_GROUPS
EXPERT_BLOCK = 256
GATHER_RING = 3
ISSUE_UNROLL = 8
QKV_HALO = 8
CONF_HALO = 32
NEG = -1e30
VMEM_LIMIT = 56 * 1024 * 1024


def _dot(a, b):
    return jnp.dot(a, b, preferred_element_type=F32)


def _dot_nt(a, b):
    return lax.dot_general(a, b, (((1,), (1,)), ((), ())), preferred_element_type=F32)


def _dot_tn(a, b):
    return lax.dot_general(a, b, (((0,), (0,)), ((), ())), preferred_element_type=F32)


def _silu(x):
    return x * jax.nn.sigmoid(x)


def _split_bf16(x):
    hi = x.astype(BF16)
    lo = (x - hi.astype(F32)).astype(BF16)
    return hi, lo


def _causal_taps(ext_ref, w_ref, n_taps, halo, tm, cols):
    rows = ext_ref.shape[0]
    ext = ext_ref[:, cols]
    first = halo - (n_taps - 1)
    acc = None
    for res in range(SUBLANES):
        taps = [k for k in range(n_taps) if (first + k) % SUBLANES == res]
        if not taps:
            continue
        shifted = ext if res == 0 else pltpu.roll(ext, rows - res, 0)
        for k in taps:
            a = first + k - res
            term = w_ref[k:k + 1, cols] * shifted[a:a + tm, :]
            acc = term if acc is None else acc + term
    return acc


def _wsplit_kernel(w_ref, wqkv_ref, wz_ref, wbah_ref, wbal_ref, wglu_ref, *, o1, o2, nba):
    wqkv_ref[...] = w_ref[:, 0:o1].astype(BF16)
    wz_ref[...] = w_ref[:, o1:o2].astype(BF16)
    tail = w_ref[:, o2:]
    lane = lax.broadcasted_iota(I32, (tail.shape[0], LANES), 1)
    ba = jnp.where(lane < nba, tail[:, :LANES], 0.0)
    hi, lo = _split_bf16(ba)
    wbah_ref[...] = hi
    wbal_ref[...] = lo
    wglu_ref[...] = tail[:, nba:].astype(BF16)


def _inproj_kernel(x_ref, gmix_ref, wqkv_ref, wz_ref, wbah_ref, wbal_ref, wglu_ref, cw_ref, pa_ref,
                   bglu_ref, cdw_ref, cvec_ref,
                   q_ref, k_ref, v_ref, bg_ref, zs_ref, yc_ref,
                   extq, extu, *, tiles_per_seq, tm):
    i = pl.program_id(0)
    first = (i % tiles_per_seq) == 0

    x = x_ref[...]
    n1 = x * lax.rsqrt(jnp.mean(x * x, axis=-1, keepdims=True) + EPS) * gmix_ref[...]
    n1h, n1l = _split_bf16(n1)

    @pl.when(first)
    def _():
        extq[0:QKV_HALO, :] = jnp.zeros((QKV_HALO, extq.shape[1]), F32)
        extu[0:CONF_HALO, :] = jnp.zeros((CONF_HALO, extu.shape[1]), F32)

    @pl.when(jnp.logical_not(first))
    def _():
        extq[0:QKV_HALO, :] = extq[tm:tm + QKV_HALO, :]
        extu[0:CONF_HALO, :] = extu[tm:tm + CONF_HALO, :]

    extq[QKV_HALO:QKV_HALO + tm, :] = _dot(n1h, wqkv_ref[...])
    hw = GDN_HEADS * GDN_DK
    qc = _silu(_causal_taps(extq, cw_ref, GDN_CONV, QKV_HALO, tm, slice(0, hw)))
    kc = _silu(_causal_taps(extq, cw_ref, GDN_CONV, QKV_HALO, tm, slice(hw, 2 * hw)))
    for h in range(GDN_HEADS):
        cs = slice(h * GDN_DK, (h + 1) * GDN_DK)
        qh = qc[:, cs]
        q_ref[:, cs] = qh * lax.rsqrt(jnp.sum(qh * qh, axis=-1, keepdims=True) + EPS) * (GDN_DK ** -0.5)
        kh = kc[:, cs]
        k_ref[:, cs] = kh * lax.rsqrt(jnp.sum(kh * kh, axis=-1, keepdims=True) + EPS)
    v_ref[...] = _silu(_causal_taps(extq, cw_ref, GDN_CONV, QKV_HALO, tm, slice(2 * hw, 3 * hw)))

    zs_ref[...] = _silu(_dot(n1h, wz_ref[...])).astype(BF16)

    ba = _dot(n1h, wbah_ref[...]) + _dot(n1l, wbah_ref[...]) + _dot(n1h, wbal_ref[...])
    lane = lax.broadcasted_iota(I32, ba.shape, 1)
    row = lax.broadcasted_iota(I32, ba.shape, 0)
    beta = jax.nn.sigmoid(ba)
    sp_in = ba + pa_ref[1:2, :]
    softplus = jnp.maximum(sp_in, 0.0) + jnp.log1p(jnp.exp(-jnp.abs(sp_in)))
    g = -jnp.exp(pa_ref[0:1, :]) * softplus
    rin = row & (GDN_CHUNK - 1)
    s = 1
    while s < GDN_CHUNK:
        g = g + jnp.where(rin >= s, pltpu.roll(g, s, 0), 0.0)
        s *= 2
    bg_ref[...] = jnp.where(lane < GDN_HEADS, beta, g)

    glu = _dot(n1h, wglu_ref[...]) + bglu_ref[...]
    extu[CONF_HALO:CONF_HALO + tm, :] = glu[:, :CONF_CH] * jax.nn.sigmoid(glu[:, CONF_CH:])
    c = _causal_taps(extu, cdw_ref, CONF_WIDTH, CONF_HALO, tm, slice(0, CONF_CH)) + cvec_ref[0:1, :]
    mu = jnp.mean(c, axis=-1, keepdims=True)
    cc = c - mu
    var = jnp.mean(cc * cc, axis=-1, keepdims=True)
    yln = cc * lax.rsqrt(var + EPS) * cvec_ref[1:2, :] + cvec_ref[2:3, :]
    yc_ref[...] = _silu(yln).astype(BF16)


def _gdn_kernel(q_ref, k_ref, v_ref, bg_ref, zs_ref, gout_ref, y_ref, s_ref, *, cb, nbatch, group):
    n = pl.program_id(0)

    @pl.when(n == 0)
    def _():
        s_ref[...] = jnp.zeros(s_ref.shape, F32)

    C = GDN_CHUNK
    r = lax.broadcasted_iota(I32, (C, C), 0)
    c = lax.broadcasted_iota(I32, (C, C), 1)
    causal = r >= c
    strict = r > c
    eye = r == c
    eyef = jnp.where(eye, 1.0, 0.0).astype(F32)
    gout = gout_ref[...]

    def front(b, ci, h):
        rows = slice(ci * C, (ci + 1) * C)
        cols = slice(h * GDN_DK, (h + 1) * GDN_DK)
        q = q_ref[b, rows, cols]
        k = k_ref[b, rows, cols]
        v = v_ref[b, rows, cols]
        beta = bg_ref[b, rows, h:h + 1]
        gc = bg_ref[b, rows, GDN_HEADS + h:GDN_HEADS + h + 1]
        gc_row = jnp.sum(jnp.where(eye, gc, 0.0), axis=0, keepdims=True)
        gc_last = gc[C - 1:C, :]
        eg = jnp.exp(gc)
        kb = k * beta
        return dict(
            decay=jnp.where(causal, jnp.exp(jnp.minimum(gc - gc_row, 0.0)), 0.0),
            lhs=jnp.concatenate([kb, q], axis=0).astype(BF16),
            kbf=k.astype(BF16),
            rhs=jnp.concatenate([v * beta, kb * eg], axis=1).astype(BF16),
            qd=q * eg,
            kd=(k * jnp.exp(gc_last - gc)).astype(BF16),
            gl=jnp.exp(gc_last))

    insts = [(b, ci, h) for ci in range(cb) for b in range(nbatch) for h in range(GDN_HEADS)]
    done = {}
    for g0 in range(0, len(insts), group):
        grp = insts[g0:g0 + group]
        st = [front(*key) for key in grp]
        kq = [_dot_nt(s["lhs"], s["kbf"]) for s in st]
        xp = [-jnp.where(strict, m[:C] * s["decay"], 0.0) for m, s in zip(kq, st)]
        qk = [(m[C:] * s["decay"]).astype(BF16) for m, s in zip(kq, st)]
        tinv = [eyef + x for x in xp]
        for _ in range(5):
            xpb = [x.astype(BF16) for x in xp]
            xp = [_dot(x, x) for x in xpb]
            tinv = [t + _dot(t.astype(BF16), x.astype(BF16)) for t, x in zip(tinv, xp)]
        uw = [_dot(t.astype(BF16), s["rhs"]) for t, s in zip(tinv, st)]
        for key, s, m, qkm in zip(grp, st, uw, qk):
            done[key] = dict(u=m[:, :GDN_DK],
                             wq=jnp.concatenate([m[:, GDN_DK:], s["qd"]], axis=0).astype(BF16),
                             qk=qkm, kd=s["kd"], gl=s["gl"])

    chains = [(b, h) for b in range(nbatch) for h in range(GDN_HEADS)]
    for ci in range(cb):
        rows = slice(ci * C, (ci + 1) * C)
        dd = [done[(b, ci, h)] for b, h in chains]
        s_old = [s_ref[b * GDN_HEADS + h] for b, h in chains]
        wq = [_dot(d["wq"], s.astype(BF16)) for d, s in zip(dd, s_old)]
        v_new = [(d["u"] - m[:C]).astype(BF16) for d, m in zip(dd, wq)]
        o = [m[C:] + _dot(d["qk"], vn) for d, m, vn in zip(dd, wq, v_new)]
        s_new = [s * d["gl"] + _dot_tn(d["kd"], vn) for d, s, vn in zip(dd, s_old, v_new)]
        for (b, h), sn, oo in zip(chains, s_new, o):
            cols = slice(h * GDN_DK, (h + 1) * GDN_DK)
            s_ref[b * GDN_HEADS + h] = sn
            on = oo * lax.rsqrt(jnp.mean(oo * oo, axis=-1, keepdims=True) + EPS) * gout
            y_ref[b, rows, cols] = (on * zs_ref[b, rows, cols].astype(F32)).astype(BF16)


def _router_kernel(x_ref, yg_ref, yc_ref, wo1_ref, wo2_ref, gmoe_ref, wrh_ref, wrl_ref, ltri_ref,
                   h1_ref, xn_ref, ri_ref, rf_ref, cnt_ref, carry, *, nsub):
    i = pl.program_id(0)

    @pl.when(i == 0)
    def _():
        carry[...] = jnp.zeros(carry.shape, F32)

    ts = x_ref.shape[0] // nsub
    subs = [slice(s * ts, (s + 1) * ts) for s in range(nsub)]
    each = lambda fn, *lists: [fn(*args) for args in zip(*lists)]
    lane_sum = lambda v: jnp.sum(v, axis=-1, keepdims=True)
    lane_max = lambda v: jnp.max(v, axis=-1, keepdims=True)
    lane_min = lambda v: jnp.min(v, axis=-1, keepdims=True)

    h1 = [x_ref[sl, :] + _dot(yg_ref[sl, :], wo1_ref[...]) + _dot(yc_ref[sl, :], wo2_ref[...]) for sl in subs]
    for sl, h in zip(subs, h1):
        h1_ref[sl, :] = h
    xn = each(lambda h: h * lax.rsqrt(jnp.mean(h * h, axis=-1, keepdims=True) + EPS) * gmoe_ref[...], h1)
    parts = each(_split_bf16, xn)
    for s, v in enumerate(xn):
        for f in range(TOKEN_TILE_ROWS):
            xn_ref[pl.ds(s * ts * TOKEN_TILE_ROWS + f, ts, stride=TOKEN_TILE_ROWS), :] = v[:, f * LANES:(f + 1) * LANES]
    lg = [_dot(hi, wrh_ref[...]) + _dot(lo, wrh_ref[...]) + _dot(hi, wrl_ref[...]) for hi, lo in parts]

    lane = lax.broadcasted_iota(I32, lg[0].shape, 1)
    lanef = lane.astype(F32)
    none = float(LANES)
    gl = each(lambda v: jnp.where(lane < N_GROUPS, v, NEG), lg)
    gmax = each(lane_max, gl)
    g_w = each(lambda v, m: 1.0 / lane_sum(jnp.exp(v - m)), gl, gmax)
    gidx = each(lambda v, m: lane_min(jnp.where(v == m, lanef, none)), gl, gmax)
    lane_group = ((lane - ROUTE_LANE0) >> (EXPERTS_PER_GROUP.bit_length() - 1)).astype(F32)
    is_expert = (lane >= ROUTE_LANE0) & (lane < ROUTE_LANE0 + N_EXPERTS)
    el = each(lambda v, g: jnp.where(is_expert & (lane_group == g), v, NEG), lg, gidx)
    m1 = each(lane_max, el)
    i1 = each(lambda v, m: lane_min(jnp.where(v == m, lanef, none)), el, m1)
    el2 = each(lambda v, a: jnp.where(lanef == a, NEG, v), el, i1)
    m2 = each(lane_max, el2)
    i2 = each(lambda v, m: lane_min(jnp.where(v == m, lanef, none)), el2, m2)
    ratio = each(lambda a, b: jnp.exp(b - a), m1, m2)
    gate1 = each(lambda w, r: w / (1.0 + r), g_w, ratio)
    gate2 = each(lambda w, r: w * r / (1.0 + r), g_w, ratio)

    oh1 = each(lambda a: lanef == a, i1)
    oh2 = each(lambda a: lanef == a, i2)
    oh = each(lambda a, b: jnp.where(a | b, 1.0, 0.0).astype(F32), oh1, oh2)
    within = each(lambda v: _dot(ltri_ref[...], v.astype(BF16)), oh)
    base = carry[...]
    for s in range(nsub):
        cum = within[s] + base
        rank1 = lane_sum(jnp.where(oh1[s], cum, 0.0))
        rank2 = lane_sum(jnp.where(oh2[s], cum, 0.0))
        routes = jnp.where(lane == 0, i1[s] - ROUTE_LANE0,
                           jnp.where(lane == 1, i2[s] - ROUTE_LANE0,
                                     jnp.where(lane == 2, rank1, jnp.where(lane == 3, rank2, 0.0))))
        ri_ref[subs[s], :] = routes.astype(I32)
        rf_ref[subs[s], :] = jnp.where(lane == 0, gate1[s], jnp.where(lane == 1, gate2[s], 0.0))
        base = base + jnp.sum(oh[s], axis=0, keepdims=True)
    carry[...] = base
    cnt_ref[...] = jnp.broadcast_to(base, cnt_ref.shape)


def _pos_kernel(ri_ref, cnt_ref, post_ref, blk_ref, *, tm):
    shift = EXPERT_BLOCK.bit_length() - 1
    cnt = cnt_ref[...].astype(I32)
    padded = ((cnt + (EXPERT_BLOCK - 1)) >> shift) << shift
    lane8 = lax.broadcasted_iota(I32, padded.shape, 1)
    pend = padded
    s = 1
    while s < LANES:
        pend = pend + jnp.where(lane8 >= s, pltpu.roll(pend, s, 1), 0)
        s *= 2
    pstart = (pend - padded)[0:1, :].astype(F32)

    ri = ri_ref[...]
    lane = lax.broadcasted_iota(I32, ri.shape, 1)
    e1 = ri[:, 0:1] + ROUTE_LANE0
    e2 = ri[:, 1:2] + ROUTE_LANE0
    ps1 = jnp.sum(jnp.where(lane == e1, pstart, 0.0), axis=-1, keepdims=True)
    ps2 = jnp.sum(jnp.where(lane == e2, pstart, 0.0), axis=-1, keepdims=True)
    pos1 = ps1 + ri[:, 2:3].astype(F32)
    pos2 = ps2 + ri[:, 3:4].astype(F32)
    posmat = jnp.where(lane == 0, pos1, jnp.where(lane == 1, pos2, 0.0))
    for j in range(tm // LANES):
        pt = posmat[j * LANES:(j + 1) * LANES, :].T
        post_ref[:, j * LANES:(j + 1) * LANES] = pt[0:8, :].astype(I32)

    @pl.when(pl.program_id(0) == 0)
    def _():
        pendf = pend.astype(F32)[0:1, :]
        rr = lax.broadcasted_iota(I32, (LANES, LANES), 0)
        cc = lax.broadcasted_iota(I32, (LANES, LANES), 1)
        pend_col = jnp.sum(jnp.where(rr == cc, pendf, 0.0), axis=1, keepdims=True)
        nbl = blk_ref.shape[1]
        er = lax.broadcasted_iota(I32, (LANES, nbl), 0)
        jb = (lax.broadcasted_iota(I32, (LANES, nbl), 1) * EXPERT_BLOCK).astype(F32)
        valid = (er >= ROUTE_LANE0) & (er < ROUTE_LANE0 + N_EXPERTS)
        be = jnp.sum(jnp.where(valid & (pend_col <= jb), 1.0, 0.0), axis=0, keepdims=True)
        be = jnp.minimum(be, float(N_EXPERTS - 1))
        nused = jnp.max(pendf, axis=-1, keepdims=True) * (1.0 / EXPERT_BLOCK)
        rowi = lax.broadcasted_iota(I32, blk_ref.shape, 0)
        blk_ref[...] = jnp.where(rowi == 0, be, nused).astype(I32)


def _slots_kernel(pos1_ref, pos2_ref, slot_ref, fill, sem, *, tm):
    i = pl.program_id(0)

    @pl.when(i == 0)
    def _():
        fill[...] = lax.broadcasted_iota(I32, fill.shape, 0) & (EXPERT_BLOCK - 1)
        cp = pltpu.make_async_copy(fill, slot_ref, sem)
        cp.start()
        cp.wait()

    def body(r, carry):
        t = i * tm + r
        slot_ref[pos1_ref[0, 0, r]] = t
        slot_ref[pos2_ref[0, 0, r]] = t
        return carry

    lax.fori_loop(0, tm, body, 0, unroll=16)


def _expert_kernel(be_ref, nu_ref, sl0_ref, sl1_ref, sl2_ref, xn_ref, wup_ref, wdn_ref, y_ref,
                   wupb, wdnb, wup_stage, wdn_stage, xbuf, gsem, wsem, wslot):
    assert GATHER_RING == 3, "the three slot BlockSpecs (blocks j, j+1, j+2) fix the ring depth"
    j = pl.program_id(0)
    cur = lax.rem(j, GATHER_RING)
    n_used = nu_ref[0]
    R = TOKEN_TILE_ROWS
    buf_rows = EXPERT_BLOCK * R

    def gather_row(slot_ref, b, r, prio):
        pltpu.make_async_copy(xn_ref.at[pl.ds(slot_ref[0, 0, r] * R, R), :],
                              xbuf.at[pl.ds(b * buf_rows + r * R, R), :], gsem.at[b]).start(priority=prio)

    def wait_buffer(b):
        pltpu.make_async_copy(xn_ref.at[pl.ds(0, buf_rows), :], xbuf.at[pl.ds(0, buf_rows), :], gsem.at[b]).wait()

    @pl.when(j == 0)
    def _():
        def body(g, carry):
            for u in range(ISSUE_UNROLL):
                gather_row(sl0_ref, 0, g * ISSUE_UNROLL + u, u % 2)
                gather_row(sl1_ref, 1, g * ISSUE_UNROLL + u, (u + 1) % 2)
            return carry

        lax.fori_loop(0, EXPERT_BLOCK // ISSUE_UNROLL, body, 0)

    def weight_copies(e, s):
        return (pltpu.make_async_copy(wup_ref.at[e], wup_stage.at[s], wsem.at[s]),
                pltpu.make_async_copy(wdn_ref.at[e], wdn_stage.at[s], wsem.at[s]))

    @pl.when(j == 0)
    def _():
        wslot[0] = 0
        for cp in weight_copies(be_ref[0], 0):
            cp.start()

    prev = be_ref[jnp.maximum(j - 1, 0)]

    @pl.when((j < n_used) & ((j == 0) | (be_ref[j] != prev)))
    def _():
        s = wslot[0]
        for cp in weight_copies(0, s):
            cp.wait()
        wupb[...] = wup_stage[s].astype(BF16)
        wdnb[...] = wdn_stage[s].astype(BF16)
        e = be_ref[j]
        k = lax.while_loop(lambda k: (k < n_used) & (be_ref[k] == e), lambda k: k + 1, j + 1)

        @pl.when(k < n_used)
        def _():
            for cp in weight_copies(be_ref[k], 1 - s):
                cp.start()

        wslot[0] = 1 - s

    @pl.when(j < n_used)
    def _():
        wait_buffer(cur)
        base = cur * buf_rows
        x = jnp.concatenate([xbuf[pl.ds(base + f, EXPERT_BLOCK, stride=R), :] for f in range(R)], axis=1)
        ab = _dot(x.astype(BF16), wupb[...])
        ahead = lax.rem(j + (GATHER_RING - 1), GATHER_RING)
        for r in range(EXPERT_BLOCK):
            gather_row(sl2_ref, ahead, r, r % 2)
        hmid = _silu(ab[:, :D_EXPERT]) * ab[:, D_EXPERT:]
        y = _dot(hmid.astype(BF16), wdnb[...])
        for f in range(R):
            y_ref[pl.ds(f, EXPERT_BLOCK, stride=R), :] = y[:, f * LANES:(f + 1) * LANES]

    @pl.when(j == n_used)
    def _():
        wait_buffer(cur)
        wait_buffer(lax.rem(j + 1, GATHER_RING))

    @pl.when(j >= n_used)
    def _():
        y_ref[...] = jnp.zeros(y_ref.shape, F32)


def _ple_kernel(p1c_ref, p2c_ref, p1n_ref, p2n_ref, h1_ref, rf_ref, p_ref, ys_ref, gple_ref, wg_ref, wp_ref,
                gfin_ref, out_ref, ybuf, sem):
    i = pl.program_id(0)
    last = pl.num_programs(0) - 1
    cur = i % 2
    nxt = 1 - cur
    tm = h1_ref.shape[0]
    R = TOKEN_TILE_ROWS
    slot_rows = 2 * tm * R

    def gather_row(pos_ref, b, k, r, prio):
        pltpu.make_async_copy(ys_ref.at[pl.ds(pos_ref[0, 0, r] * R, R), :],
                              ybuf.at[pl.ds(b * slot_rows + (k * tm + r) * R, R), :], sem.at[b]).start(priority=prio)

    def wait_slot(b):
        pltpu.make_async_copy(ys_ref.at[pl.ds(0, slot_rows), :], ybuf.at[pl.ds(0, slot_rows), :], sem.at[b]).wait()

    @pl.when(i == 0)
    def _():
        def body(g, carry):
            for u in range(ISSUE_UNROLL):
                gather_row(p1c_ref, 0, 0, g * ISSUE_UNROLL + u, u % 2)
                gather_row(p2c_ref, 0, 1, g * ISSUE_UNROLL + u, (u + 1) % 2)
            return carry

        lax.fori_loop(0, tm // ISSUE_UNROLL, body, 0)

    pp = _dot(p_ref[...].astype(BF16), wp_ref[...])
    for r in range(tm):
        gather_row(p1n_ref, nxt, 0, r, r % 2)
        gather_row(p2n_ref, nxt, 1, r, (r + 1) % 2)
    wait_slot(cur)
    rows = lambda k: jnp.concatenate(
        [ybuf[pl.ds(cur * slot_rows + k * tm * R + f, tm, stride=R), :] for f in range(R)], axis=1)
    rf = rf_ref[...]
    h2 = h1_ref[...] + (rf[:, 0:1] * rows(0) + rf[:, 1:2] * rows(1))
    hn = h2 * lax.rsqrt(jnp.mean(h2 * h2, axis=-1, keepdims=True) + EPS) * gple_ref[...]
    gate = jax.nn.sigmoid(_dot(hn.astype(BF16), wg_ref[...]))
    h3 = h2 + gate * pp
    out_ref[...] = h3 * lax.rsqrt(jnp.mean(h3 * h3, axis=-1, keepdims=True) + EPS) * gfin_ref[...]

    @pl.when(i == last)
    def _():
        wait_slot(nxt)


def _params(sem):
    return pltpu.CompilerParams(dimension_semantics=sem, vmem_limit_bytes=VMEM_LIMIT)


def _full(shape):
    nd = len(shape)
    return pl.BlockSpec(shape, lambda *_: (0,) * nd)


def _pad_lanes(a, lane0=0):
    out = jnp.zeros((a.shape[0], LANES), a.dtype)
    return out.at[:, lane0:lane0 + a.shape[1]].set(a)


def _layer(x2, p2, B, T, g_mix, w_in, w_conv_qkv, a_log, dt_bias, g_gdn_out, b_glu, w_conv_dw,
           b_conv_dw, ln_conv_g, ln_conv_b, w_out, g_moe, w_group_router, w_expert_router, w_up,
           w_down, g_ple, w_ple_gate, w_ple_proj, g_out):
    N, D = x2.shape
    hw = GDN_HEADS * GDN_DK
    tm = min(512, T)
    nt = N // tm

    o1 = 3 * hw
    o2 = o1 + hw
    nba = 2 * GDN_HEADS
    wqkv, wz, wbah, wbal, wglu = pl.pallas_call(
        functools.partial(_wsplit_kernel, o1=o1, o2=o2, nba=nba),
        out_shape=[jax.ShapeDtypeStruct((D, o1), BF16), jax.ShapeDtypeStruct((D, hw), BF16),
                   jax.ShapeDtypeStruct((D, LANES), BF16), jax.ShapeDtypeStruct((D, LANES), BF16),
                   jax.ShapeDtypeStruct((D, 2 * CONF_CH), BF16)],
        compiler_params=pltpu.CompilerParams(vmem_limit_bytes=VMEM_LIMIT),
        name="wsplit",
    )(w_in)
    pa = jnp.concatenate([_pad_lanes(a_log[None, :], GDN_HEADS), _pad_lanes(dt_bias[None, :], GDN_HEADS),
                          jnp.zeros((6, LANES), F32)], axis=0)
    cdw = jnp.concatenate([w_conv_dw, jnp.zeros((1, CONF_CH), F32)], axis=0)
    cvec = jnp.concatenate([b_conv_dw[None], ln_conv_g[None], ln_conv_b[None], jnp.zeros((5, CONF_CH), F32)], 0)

    row_spec = lambda w: pl.BlockSpec((tm, w), lambda i: (i, 0))
    q, k, v, bg, zs, yc = pl.pallas_call(
        functools.partial(_inproj_kernel, tiles_per_seq=T // tm, tm=tm),
        grid=(nt,),
        in_specs=[row_spec(D), _full((1, D)), _full(wqkv.shape), _full(wz.shape), _full(wbah.shape),
                  _full(wbal.shape), _full(wglu.shape), _full(w_conv_qkv.shape), _full(pa.shape),
                  _full((1, 2 * CONF_CH)), _full(cdw.shape), _full(cvec.shape)],
        out_specs=[row_spec(hw), row_spec(hw), row_spec(hw), row_spec(LANES), row_spec(hw), row_spec(CONF_CH)],
        out_shape=[jax.ShapeDtypeStruct((N, hw), F32)] * 3 + [jax.ShapeDtypeStruct((N, LANES), F32),
                   jax.ShapeDtypeStruct((N, hw), BF16), jax.ShapeDtypeStruct((N, CONF_CH), BF16)],
        scratch_shapes=[pltpu.VMEM((tm + QKV_HALO, 3 * hw), F32), pltpu.VMEM((tm + CONF_HALO, CONF_CH), F32)],
        compiler_params=_params(("arbitrary",)),
        name="inproj",
    )(x2, g_mix[None], wqkv, wz, wbah, wbal, wglu, w_conv_qkv, pa, b_glu[None], cdw, cvec)

    cb = 8
    rows = cb * GDN_CHUNK
    seq_spec = lambda w: pl.BlockSpec((B, rows, w), lambda n: (0, n, 0))
    r3 = lambda a: a.reshape(B, T, a.shape[-1])
    ygdn = pl.pallas_call(
        functools.partial(_gdn_kernel, cb=cb, nbatch=B, group=8 * GDN_HEADS),
        grid=(T // rows,),
        in_specs=[seq_spec(hw), seq_spec(hw), seq_spec(hw), seq_spec(LANES), seq_spec(hw),
                  pl.BlockSpec((1, GDN_DK), lambda n: (0, 0))],
        out_specs=seq_spec(hw),
        out_shape=jax.ShapeDtypeStruct((B, T, hw), BF16),
        scratch_shapes=[pltpu.VMEM((B * GDN_HEADS, GDN_DK, GDN_DK), F32)],
        compiler_params=_params(("arbitrary",)),
        name="gdn",
    )(r3(q), r3(k), r3(v), r3(bg), r3(zs), g_gdn_out[None]).reshape(N, hw)

    wo1 = w_out[:hw].astype(BF16)
    wo2 = w_out[hw:].astype(BF16)
    wrh, wrl = _split_bf16(_pad_lanes(jnp.concatenate([w_group_router, w_expert_router], axis=1)))
    nsub = 2 if tm % 256 == 0 else 1
    ltri = jnp.tril(jnp.ones((tm // nsub, tm // nsub), F32), -1).astype(BF16)
    assert D == TOKEN_TILE_ROWS * LANES
    tile_spec = pl.BlockSpec((tm * TOKEN_TILE_ROWS, LANES), lambda i: (i, 0))
    h1, xn, ri, rf, cnt = pl.pallas_call(
        functools.partial(_router_kernel, nsub=nsub),
        grid=(nt,),
        in_specs=[row_spec(D), row_spec(hw), row_spec(CONF_CH), _full(wo1.shape), _full(wo2.shape),
                  _full((1, D)), _full(wrh.shape), _full(wrl.shape), _full(ltri.shape)],
        out_specs=[row_spec(D), tile_spec, row_spec(LANES), row_spec(LANES), _full((8, LANES))],
        out_shape=[jax.ShapeDtypeStruct((N, D), F32), jax.ShapeDtypeStruct((N * TOKEN_TILE_ROWS, LANES), F32),
                   jax.ShapeDtypeStruct((N, LANES), I32), jax.ShapeDtypeStruct((N, LANES), F32),
                   jax.ShapeDtypeStruct((8, LANES), F32)],
        scratch_shapes=[pltpu.VMEM((1, LANES), F32)],
        compiler_params=_params(("arbitrary",)),
        name="router",
    )(x2, ygdn, yc, wo1, wo2, g_moe[None], wrh, wrl, ltri)

    nb = (2 * N) // EXPERT_BLOCK + N_EXPERTS
    nbl = -(-nb // LANES) * LANES
    tp = min(4 * tm, N)
    post, blk = pl.pallas_call(
        functools.partial(_pos_kernel, tm=tp),
        grid=(N // tp,),
        in_specs=[pl.BlockSpec((tp, LANES), lambda i: (i, 0)), _full((8, LANES))],
        out_specs=[pl.BlockSpec((8, tp), lambda i: (0, i)), _full((8, nbl))],
        out_shape=[jax.ShapeDtypeStruct((8, N), I32), jax.ShapeDtypeStruct((8, nbl), I32)],
        compiler_params=_params(("arbitrary",)),
        name="positions",
    )(ri, cnt)
    pos1 = post[0].reshape(nt, 1, tm)
    pos2 = post[1].reshape(nt, 1, tm)
    block_expert = blk[0, :nb]
    n_used = blk[1, :1]

    P = nb * EXPERT_BLOCK
    smem_spec = pl.BlockSpec((1, 1, tp), lambda i: (i, 0, 0), memory_space=pltpu.SMEM)
    flat_smem = pl.BlockSpec((P,), lambda i: (0,), memory_space=pltpu.SMEM)
    slots = pl.pallas_call(
        functools.partial(_slots_kernel, tm=tp),
        grid=(N // tp,),
        in_specs=[smem_spec, smem_spec],
        out_specs=flat_smem,
        out_shape=jax.ShapeDtypeStruct((P,), I32),
        scratch_shapes=[pltpu.VMEM((P,), I32), pltpu.SemaphoreType.DMA],
        compiler_params=_params(("arbitrary",)),
        name="slots",
    )(post[0].reshape(N // tp, 1, tp), post[1].reshape(N // tp, 1, tp)).reshape(nb, 1, EXPERT_BLOCK)

    any_spec = pl.BlockSpec(memory_space=pl.ANY)
    blk_smem = lambda off: pl.BlockSpec((1, 1, EXPERT_BLOCK),
                                        lambda j, be, nu: (jnp.minimum(j + off, nb - 1), 0, 0),
                                        memory_space=pltpu.SMEM)
    ys = pl.pallas_call(
        _expert_kernel,
        grid_spec=pltpu.PrefetchScalarGridSpec(
            num_scalar_prefetch=2,
            grid=(nb,),
            in_specs=[blk_smem(0), blk_smem(1), blk_smem(2), any_spec, any_spec, any_spec],
            out_specs=pl.BlockSpec((EXPERT_BLOCK * TOKEN_TILE_ROWS, LANES), lambda j, be, nu: (j, 0)),
            scratch_shapes=[pltpu.VMEM((D, 2 * D_EXPERT), BF16), pltpu.VMEM((D_EXPERT, D), BF16),
                            pltpu.VMEM((2, D, 2 * D_EXPERT), F32), pltpu.VMEM((2, D_EXPERT, D), F32),
                            pltpu.VMEM((GATHER_RING * EXPERT_BLOCK * TOKEN_TILE_ROWS, LANES), F32),
                            pltpu.SemaphoreType.DMA((GATHER_RING,)), pltpu.SemaphoreType.DMA((2,)),
                            pltpu.SMEM((1,), I32)]),
        out_shape=jax.ShapeDtypeStruct((P * TOKEN_TILE_ROWS, LANES), F32),
        compiler_params=_params(("arbitrary",)),
        name="experts",
    )(block_expert, n_used, slots, slots, slots, xn, w_up, w_down)

    wg = w_ple_gate.astype(BF16)
    wp = w_ple_proj.astype(BF16)
    pos_spec = lambda off: pl.BlockSpec((1, 1, tm), lambda i: (jnp.minimum(i + off, nt - 1), 0, 0),
                                        memory_space=pltpu.SMEM)
    out = pl.pallas_call(
        _ple_kernel,
        grid=(nt,),
        in_specs=[pos_spec(0), pos_spec(0), pos_spec(1), pos_spec(1),
                  row_spec(D), row_spec(LANES), row_spec(p2.shape[1]), any_spec,
                  _full((1, D)), _full(wg.shape), _full(wp.shape), _full((1, D))],
        out_specs=row_spec(D),
        out_shape=jax.ShapeDtypeStruct((N, D), F32),
        scratch_shapes=[pltpu.VMEM((2 * 2 * tm * TOKEN_TILE_ROWS, LANES), F32), pltpu.SemaphoreType.DMA((2,))],
        compiler_params=_params(("arbitrary",)),
        name="ple",
    )(pos1, pos2, pos1, pos2, h1, rf, p2, ys, g_ple[None], wg, wp, g_out[None])
    return out


def kernel(x, p, g_mix, w_in, w_conv_qkv, a_log, dt_bias, g_gdn_out, b_glu, w_conv_dw, b_conv_dw, ln_conv_g, ln_conv_b, w_out, g_moe, w_group_router, w_expert_router, w_up, w_down, g_ple, w_ple_gate, w_ple_proj, g_final):
    B, T, D = x.shape
    depth = p.shape[0]
    assert depth == 1, "the final RMSNorm is fused into the (single) layer's last kernel"
    h = _layer(x.reshape(B * T, D), p[0].reshape(B * T, -1), B, T, g_mix[0], w_in[0], w_conv_qkv[0],
               a_log[0], dt_bias[0], g_gdn_out[0], b_glu[0], w_conv_dw[0], b_conv_dw[0], ln_conv_g[0],
               ln_conv_b[0], w_out[0], g_moe[0], w_group_router[0], w_expert_router[0], w_up[0],
               w_down[0], g_ple[0], w_ple_gate[0], w_ple_proj[0], g_final)
    return h.reshape(B, T, D)
```

```python
import functools

import jax
import jax.numpy as jnp
from jax import lax
from jax.experimental import pallas as pl
from jax.experimental.pallas import tpu as pltpu

F32 = jnp.float32
BF16 = jnp.bfloat16
I32 = jnp.int32

EPS = 1e-6
GDN_HEADS = 4
GDN_DK = 128
GDN_CONV = 4
GDN_CHUNK = 64
CONF_CH = 512
CONF_WIDTH = 31
N_GROUPS = 8
EXPERTS_PER_GROUP = 8
N_EXPERTS = 64
D_EXPERT = 256
LANES = 128
SUBLANES = 8
TOKEN_TILE_ROWS = 8
ROUTE_LANE0 = N_GROUPS
EXPERT_BLOCK = 256
GATHER_RING = 3
ISSUE_UNROLL = 8
QKV_HALO = 8
CONF_HALO = 32
NEG = -1e30
VMEM_LIMIT = 56 * 1024 * 1024


def _dot(a, b):
    return jnp.dot(a, b, preferred_element_type=F32)


def _dot_nt(a, b):
    return lax.dot_general(a, b, (((1,), (1,)), ((), ())), preferred_element_type=F32)


def _dot_tn(a, b):
    return lax.dot_general(a, b, (((0,), (0,)), ((), ())), preferred_element_type=F32)


def _silu(x):
    return x * jax.nn.sigmoid(x)


def _split_bf16(x):
    hi = x.astype(BF16)
    lo = (x - hi.astype(F32)).astype(BF16)
    return hi, lo


def _causal_taps(ext_ref, w_ref, n_taps, halo, tm, cols):
    rows = ext_ref.shape[0]
    ext = ext_ref[:, cols]
    first = halo - (n_taps - 1)
    acc = None
    for res in range(SUBLANES):
        taps = [k for k in range(n_taps) if (first + k) % SUBLANES == res]
        if not taps:
            continue
        shifted = ext if res == 0 else pltpu.roll(ext, rows - res, 0)
        for k in taps:
            a = first + k - res
            term = w_ref[k:k + 1, cols] * shifted[a:a + tm, :]
            acc = term if acc is None else acc + term
    return acc


def _wsplit_kernel(wt_ref, wqkv_ref, wz_ref, wbah_ref, wbal_ref, wglu_ref, *, o1, o2, nba):
    d = wt_ref.shape[1]

    def cols(start, out_ref):
        for c in range(out_ref.shape[1] // LANES):
            for r in range(d // LANES):
                blk = wt_ref[start + c * LANES:start + (c + 1) * LANES, r * LANES:(r + 1) * LANES]
                out_ref[r * LANES:(r + 1) * LANES, c * LANES:(c + 1) * LANES] = blk.T.astype(BF16)

    cols(0, wqkv_ref)
    cols(o1, wz_ref)
    cols(o2 + nba, wglu_ref)
    lane = lax.broadcasted_iota(I32, (LANES, LANES), 1)
    for r in range(d // LANES):
        ba = jnp.where(lane < nba, wt_ref[o2:o2 + LANES, r * LANES:(r + 1) * LANES].T, 0.0)
        hi, lo = _split_bf16(ba)
        wbah_ref[r * LANES:(r + 1) * LANES, :] = hi
        wbal_ref[r * LANES:(r + 1) * LANES, :] = lo


def _inproj_kernel(x_ref, gmix_ref, wqkv_ref, wz_ref, wbah_ref, wbal_ref, wglu_ref, cw_ref, pa_ref,
                   bglu_ref, cdw_ref, cvec_ref,
                   q_ref, k_ref, v_ref, bg_ref, zs_ref, yc_ref,
                   extq, extu, *, tiles_per_seq, tm):
    i = pl.program_id(0)
    first = (i % tiles_per_seq) == 0

    x = x_ref[...]
    n1 = x * lax.rsqrt(jnp.mean(x * x, axis=-1, keepdims=True) + EPS) * gmix_ref[...]
    n1h, n1l = _split_bf16(n1)

    @pl.when(first)
    def _():
        extq[0:QKV_HALO, :] = jnp.zeros((QKV_HALO, extq.shape[1]), F32)
        extu[0:CONF_HALO, :] = jnp.zeros((CONF_HALO, extu.shape[1]), F32)

    @pl.when(jnp.logical_not(first))
    def _():
        extq[0:QKV_HALO, :] = extq[tm:tm + QKV_HALO, :]
        extu[0:CONF_HALO, :] = extu[tm:tm + CONF_HALO, :]

    extq[QKV_HALO:QKV_HALO + tm, :] = _dot(n1h, wqkv_ref[...])
    hw = GDN_HEADS * GDN_DK
    qc = _silu(_causal_taps(extq, cw_ref, GDN_CONV, QKV_HALO, tm, slice(0, hw)))
    kc = _silu(_causal_taps(extq, cw_ref, GDN_CONV, QKV_HALO, tm, slice(hw, 2 * hw)))
    for h in range(GDN_HEADS):
        cs = slice(h * GDN_DK, (h + 1) * GDN_DK)
        qh = qc[:, cs]
        q_ref[:, cs] = qh * lax.rsqrt(jnp.sum(qh * qh, axis=-1, keepdims=True) + EPS) * (GDN_DK ** -0.5)
        kh = kc[:, cs]
        k_ref[:, cs] = kh * lax.rsqrt(jnp.sum(kh * kh, axis=-1, keepdims=True) + EPS)
    v_ref[...] = _silu(_causal_taps(extq, cw_ref, GDN_CONV, QKV_HALO, tm, slice(2 * hw, 3 * hw)))

    zs_ref[...] = _silu(_dot(n1h, wz_ref[...])).astype(BF16)

    ba = _dot(n1h, wbah_ref[...]) + _dot(n1l, wbah_ref[...]) + _dot(n1h, wbal_ref[...])
    lane = lax.broadcasted_iota(I32, ba.shape, 1)
    row = lax.broadcasted_iota(I32, ba.shape, 0)
    beta = jax.nn.sigmoid(ba)
    sp_in = ba + pa_ref[1:2, :]
    softplus = jnp.maximum(sp_in, 0.0) + jnp.log1p(jnp.exp(-jnp.abs(sp_in)))
    g = -jnp.exp(pa_ref[0:1, :]) * softplus
    rin = row & (GDN_CHUNK - 1)
    s = 1
    while s < GDN_CHUNK:
        g = g + jnp.where(rin >= s, pltpu.roll(g, s, 0), 0.0)
        s *= 2
    bg_ref[...] = jnp.where(lane < GDN_HEADS, beta, g)

    glu = _dot(n1h, wglu_ref[...]) + bglu_ref[...]
    extu[CONF_HALO:CONF_HALO + tm, :] = glu[:, :CONF_CH] * jax.nn.sigmoid(glu[:, CONF_CH:])
    c = _causal_taps(extu, cdw_ref, CONF_WIDTH, CONF_HALO, tm, slice(0, CONF_CH)) + cvec_ref[0:1, :]
    mu = jnp.mean(c, axis=-1, keepdims=True)
    cc = c - mu
    var = jnp.mean(cc * cc, axis=-1, keepdims=True)
    yln = cc * lax.rsqrt(var + EPS) * cvec_ref[1:2, :] + cvec_ref[2:3, :]
    yc_ref[...] = _silu(yln).astype(BF16)


def _gdn_kernel(q_ref, k_ref, v_ref, bg_ref, zs_ref, gout_ref, y_ref, s_ref, *, cb, nbatch, group):
    n = pl.program_id(0)

    @pl.when(n == 0)
    def _():
        s_ref[...] = jnp.zeros(s_ref.shape, F32)

    C = GDN_CHUNK
    r = lax.broadcasted_iota(I32, (C, C), 0)
    c = lax.broadcasted_iota(I32, (C, C), 1)
    causal = r >= c
    strict = r > c
    eye = r == c
    eyef = jnp.where(eye, 1.0, 0.0).astype(F32)
    gout = gout_ref[...]

    def front(b, ci, h):
        rows = slice(ci * C, (ci + 1) * C)
        cols = slice(h * GDN_DK, (h + 1) * GDN_DK)
        q = q_ref[b, rows, cols]
        k = k_ref[b, rows, cols]
        v = v_ref[b, rows, cols]
        beta = bg_ref[b, rows, h:h + 1]
        gc = bg_ref[b, rows, GDN_HEADS + h:GDN_HEADS + h + 1]
        gc_row = jnp.sum(jnp.where(eye, gc, 0.0), axis=0, keepdims=True)
        gc_last = gc[C - 1:C, :]
        eg = jnp.exp(gc)
        kb = k * beta
        return dict(
            decay=jnp.where(causal, jnp.exp(jnp.minimum(gc - gc_row, 0.0)), 0.0),
            lhs=jnp.concatenate([kb, q], axis=0).astype(BF16),
            kbf=k.astype(BF16),
            rhs=jnp.concatenate([v * beta, kb * eg], axis=1).astype(BF16),
            qd=q * eg,
            kd=(k * jnp.exp(gc_last - gc)).astype(BF16),
            gl=jnp.exp(gc_last))

    insts = [(b, ci, h) for ci in range(cb) for b in range(nbatch) for h in range(GDN_HEADS)]
    done = {}
    for g0 in range(0, len(insts), group):
        grp = insts[g0:g0 + group]
        st = [front(*key) for key in grp]
        kq = [_dot_nt(s["lhs"], s["kbf"]) for s in st]
        xp = [-jnp.where(strict, m[:C] * s["decay"], 0.0) for m, s in zip(kq, st)]
        qk = [(m[C:] * s["decay"]).astype(BF16) for m, s in zip(kq, st)]
        tinv = [eyef + x for x in xp]
        for _ in range(5):
            xpb = [x.astype(BF16) for x in xp]
            xp = [_dot(x, x) for x in xpb]
            tinv = [t + _dot(t.astype(BF16), x.astype(BF16)) for t, x in zip(tinv, xp)]
        uw = [_dot(t.astype(BF16), s["rhs"]) for t, s in zip(tinv, st)]
        for key, s, m, qkm in zip(grp, st, uw, qk):
            done[key] = dict(u=m[:, :GDN_DK],
                             wq=jnp.concatenate([m[:, GDN_DK:], s["qd"]], axis=0).astype(BF16),
                             qk=qkm, kd=s["kd"], gl=s["gl"])

    chains = [(b, h) for b in range(nbatch) for h in range(GDN_HEADS)]
    for ci in range(cb):
        rows = slice(ci * C, (ci + 1) * C)
        dd = [done[(b, ci, h)] for b, h in chains]
        s_old = [s_ref[b * GDN_HEADS + h] for b, h in chains]
        wq = [_dot(d["wq"], s.astype(BF16)) for d, s in zip(dd, s_old)]
        v_new = [(d["u"] - m[:C]).astype(BF16) for d, m in zip(dd, wq)]
        o = [m[C:] + _dot(d["qk"], vn) for d, m, vn in zip(dd, wq, v_new)]
        s_new = [s * d["gl"] + _dot_tn(d["kd"], vn) for d, s, vn in zip(dd, s_old, v_new)]
        for (b, h), sn, oo in zip(chains, s_new, o):
            cols = slice(h * GDN_DK, (h + 1) * GDN_DK)
            s_ref[b * GDN_HEADS + h] = sn
            on = oo * lax.rsqrt(jnp.mean(oo * oo, axis=-1, keepdims=True) + EPS) * gout
            y_ref[b, rows, cols] = (on * zs_ref[b, rows, cols].astype(F32)).astype(BF16)


def _router_kernel(x_ref, yg_ref, yc_ref, wo1_ref, wo2_ref, gmoe_ref, wrh_ref, wrl_ref, ltri_ref,
                   h1_ref, xn_ref, ri_ref, rf_ref, cnt_ref, carry, *, nsub):
    i = pl.program_id(0)

    @pl.when(i == 0)
    def _():
        carry[...] = jnp.zeros(carry.shape, F32)

    ts = x_ref.shape[0] // nsub
    subs = [slice(s * ts, (s + 1) * ts) for s in range(nsub)]
    each = lambda fn, *lists: [fn(*args) for args in zip(*lists)]
    lane_sum = lambda v: jnp.sum(v, axis=-1, keepdims=True)
    lane_max = lambda v: jnp.max(v, axis=-1, keepdims=True)
    lane_min = lambda v: jnp.min(v, axis=-1, keepdims=True)

    h1 = [x_ref[sl, :] + _dot(yg_ref[sl, :], wo1_ref[...]) + _dot(yc_ref[sl, :], wo2_ref[...]) for sl in subs]
    for sl, h in zip(subs, h1):
        h1_ref[sl, :] = h
    xn = each(lambda h: h * lax.rsqrt(jnp.mean(h * h, axis=-1, keepdims=True) + EPS) * gmoe_ref[...], h1)
    parts = each(_split_bf16, xn)
    for s, v in enumerate(xn):
        for f in range(TOKEN_TILE_ROWS):
            xn_ref[pl.ds(s * ts * TOKEN_TILE_ROWS + f, ts, stride=TOKEN_TILE_ROWS), :] = v[:, f * LANES:(f + 1) * LANES]
    lg = [_dot(hi, wrh_ref[...]) + _dot(lo, wrh_ref[...]) + _dot(hi, wrl_ref[...]) for hi, lo in parts]

    lane = lax.broadcasted_iota(I32, lg[0].shape, 1)
    lanef = lane.astype(F32)
    none = float(LANES)
    gl = each(lambda v: jnp.where(lane < N_GROUPS, v, NEG), lg)
    gmax = each(lane_max, gl)
    g_w = each(lambda v, m: 1.0 / lane_sum(jnp.exp(v - m)), gl, gmax)
    gidx = each(lambda v, m: lane_min(jnp.where(v == m, lanef, none)), gl, gmax)
    lane_group = ((lane - ROUTE_LANE0) >> (EXPERTS_PER_GROUP.bit_length() - 1)).astype(F32)
    is_expert = (lane >= ROUTE_LANE0) & (lane < ROUTE_LANE0 + N_EXPERTS)
    el = each(lambda v, g: jnp.where(is_expert & (lane_group == g), v, NEG), lg, gidx)
    m1 = each(lane_max, el)
    i1 = each(lambda v, m: lane_min(jnp.where(v == m, lanef, none)), el, m1)
    el2 = each(lambda v, a: jnp.where(lanef == a, NEG, v), el, i1)
    m2 = each(lane_max, el2)
    i2 = each(lambda v, m: lane_min(jnp.where(v == m, lanef, none)), el2, m2)
    ratio = each(lambda a, b: jnp.exp(b - a), m1, m2)
    gate1 = each(lambda w, r: w / (1.0 + r), g_w, ratio)
    gate2 = each(lambda w, r: w * r / (1.0 + r), g_w, ratio)

    oh1 = each(lambda a: lanef == a, i1)
    oh2 = each(lambda a: lanef == a, i2)
    oh = each(lambda a, b: jnp.where(a | b, 1.0, 0.0).astype(F32), oh1, oh2)
    within = each(lambda v: _dot(ltri_ref[...], v.astype(BF16)), oh)
    base = carry[...]
    for s in range(nsub):
        cum = within[s] + base
        rank1 = lane_sum(jnp.where(oh1[s], cum, 0.0))
        rank2 = lane_sum(jnp.where(oh2[s], cum, 0.0))
        routes = jnp.where(lane == 0, i1[s] - ROUTE_LANE0,
                           jnp.where(lane == 1, i2[s] - ROUTE_LANE0,
                                     jnp.where(lane == 2, rank1, jnp.where(lane == 3, rank2, 0.0))))
        ri_ref[subs[s], :] = routes.astype(I32)
        rf_ref[subs[s], :] = jnp.where(lane == 0, gate1[s], jnp.where(lane == 1, gate2[s], 0.0))
        base = base + jnp.sum(oh[s], axis=0, keepdims=True)
    carry[...] = base
    cnt_ref[...] = jnp.broadcast_to(base, cnt_ref.shape)


def _pos_kernel(ri_ref, cnt_ref, post_ref, blk_ref, *, tm):
    shift = EXPERT_BLOCK.bit_length() - 1
    cnt = cnt_ref[...].astype(I32)
    padded = ((cnt + (EXPERT_BLOCK - 1)) >> shift) << shift
    lane8 = lax.broadcasted_iota(I32, padded.shape, 1)
    pend = padded
    s = 1
    while s < LANES:
        pend = pend + jnp.where(lane8 >= s, pltpu.roll(pend, s, 1), 0)
        s *= 2
    pstart = (pend - padded)[0:1, :].astype(F32)

    ri = ri_ref[...]
    lane = lax.broadcasted_iota(I32, ri.shape, 1)
    e1 = ri[:, 0:1] + ROUTE_LANE0
    e2 = ri[:, 1:2] + ROUTE_LANE0
    ps1 = jnp.sum(jnp.where(lane == e1, pstart, 0.0), axis=-1, keepdims=True)
    ps2 = jnp.sum(jnp.where(lane == e2, pstart, 0.0), axis=-1, keepdims=True)
    pos1 = ps1 + ri[:, 2:3].astype(F32)
    pos2 = ps2 + ri[:, 3:4].astype(F32)
    posmat = jnp.where(lane == 0, pos1, jnp.where(lane == 1, pos2, 0.0))
    for j in range(tm // LANES):
        pt = posmat[j * LANES:(j + 1) * LANES, :].T
        post_ref[:, j * LANES:(j + 1) * LANES] = pt[0:8, :].astype(I32)

    @pl.when(pl.program_id(0) == 0)
    def _():
        pendf = pend.astype(F32)[0:1, :]
        rr = lax.broadcasted_iota(I32, (LANES, LANES), 0)
        cc = lax.broadcasted_iota(I32, (LANES, LANES), 1)
        pend_col = jnp.sum(jnp.where(rr == cc, pendf, 0.0), axis=1, keepdims=True)
        nbl = blk_ref.shape[1]
        er = lax.broadcasted_iota(I32, (LANES, nbl), 0)
        jb = (lax.broadcasted_iota(I32, (LANES, nbl), 1) * EXPERT_BLOCK).astype(F32)
        valid = (er >= ROUTE_LANE0) & (er < ROUTE_LANE0 + N_EXPERTS)
        be = jnp.sum(jnp.where(valid & (pend_col <= jb), 1.0, 0.0), axis=0, keepdims=True)
        be = jnp.minimum(be, float(N_EXPERTS - 1))
        nused = jnp.max(pendf, axis=-1, keepdims=True) * (1.0 / EXPERT_BLOCK)
        rowi = lax.broadcasted_iota(I32, blk_ref.shape, 0)
        blk_ref[...] = jnp.where(rowi == 0, be, nused).astype(I32)


def _slots_kernel(pos1_ref, pos2_ref, slot_ref, fill, sem, *, tm):
    i = pl.program_id(0)

    @pl.when(i == 0)
    def _():
        fill[...] = lax.broadcasted_iota(I32, fill.shape, 0) & (EXPERT_BLOCK - 1)
        cp = pltpu.make_async_copy(fill, slot_ref, sem)
        cp.start()
        cp.wait()

    def body(r, carry):
        t = i * tm + r
        slot_ref[pos1_ref[0, 0, r]] = t
        slot_ref[pos2_ref[0, 0, r]] = t
        return carry

    lax.fori_loop(0, tm, body, 0, unroll=16)


def _expert_kernel(be_ref, nu_ref, sl0_ref, sl1_ref, sl2_ref, xn_ref, wup_ref, wdn_ref, y_ref,
                   wupb, wdnb, wup_stage, wdn_stage, xbuf, gsem, wsem, wslot):
    assert GATHER_RING == 3, "the three slot BlockSpecs (blocks j, j+1, j+2) fix the ring depth"
    j = pl.program_id(0)
    cur = lax.rem(j, GATHER_RING)
    n_used = nu_ref[0]
    R = TOKEN_TILE_ROWS
    buf_rows = EXPERT_BLOCK * R

    def gather_row(slot_ref, b, r, prio):
        pltpu.make_async_copy(xn_ref.at[pl.ds(slot_ref[0, 0, r] * R, R), :],
                              xbuf.at[pl.ds(b * buf_rows + r * R, R), :], gsem.at[b]).start(priority=prio)

    def wait_buffer(b):
        pltpu.make_async_copy(xn_ref.at[pl.ds(0, buf_rows), :], xbuf.at[pl.ds(0, buf_rows), :], gsem.at[b]).wait()

    @pl.when(j == 0)
    def _():
        def body(g, carry):
            for u in range(ISSUE_UNROLL):
                gather_row(sl0_ref, 0, g * ISSUE_UNROLL + u, u % 2)
                gather_row(sl1_ref, 1, g * ISSUE_UNROLL + u, (u + 1) % 2)
            return carry

        lax.fori_loop(0, EXPERT_BLOCK // ISSUE_UNROLL, body, 0)

    def weight_copies(e, s):
        return (pltpu.make_async_copy(wup_ref.at[e], wup_stage.at[s], wsem.at[s]),
                pltpu.make_async_copy(wdn_ref.at[e], wdn_stage.at[s], wsem.at[s]))

    @pl.when(j == 0)
    def _():
        wslot[0] = 0
        for cp in weight_copies(be_ref[0], 0):
            cp.start()

    prev = be_ref[jnp.maximum(j - 1, 0)]

    @pl.when((j < n_used) & ((j == 0) | (be_ref[j] != prev)))
    def _():
        s = wslot[0]
        for cp in weight_copies(0, s):
            cp.wait()
        wupb[...] = wup_stage[s].astype(BF16)
        wdnb[...] = wdn_stage[s].astype(BF16)
        e = be_ref[j]
        k = lax.while_loop(lambda k: (k < n_used) & (be_ref[k] == e), lambda k: k + 1, j + 1)

        @pl.when(k < n_used)
        def _():
            for cp in weight_copies(be_ref[k], 1 - s):
                cp.start()

        wslot[0] = 1 - s

    @pl.when(j < n_used)
    def _():
        wait_buffer(cur)
        base = cur * buf_rows
        x = jnp.concatenate([xbuf[pl.ds(base + f, EXPERT_BLOCK, stride=R), :] for f in range(R)], axis=1)
        ab = _dot(x.astype(BF16), wupb[...])
        ahead = lax.rem(j + (GATHER_RING - 1), GATHER_RING)
        for r in range(EXPERT_BLOCK):
            gather_row(sl2_ref, ahead, r, r % 2)
        hmid = _silu(ab[:, :D_EXPERT]) * ab[:, D_EXPERT:]
        y = _dot(hmid.astype(BF16), wdnb[...])
        for f in range(R):
            y_ref[pl.ds(f, EXPERT_BLOCK, stride=R), :] = y[:, f * LANES:(f + 1) * LANES]

    @pl.when(j == n_used)
    def _():
        wait_buffer(cur)
        wait_buffer(lax.rem(j + 1, GATHER_RING))

    @pl.when(j >= n_used)
    def _():
        y_ref[...] = jnp.zeros(y_ref.shape, F32)


def _ple_kernel(p1c_ref, p2c_ref, p1n_ref, p2n_ref, h1_ref, rf_ref, p_ref, ys_ref, gple_ref, wg_ref, wp_ref,
                gfin_ref, out_ref, ybuf, sem):
    i = pl.program_id(0)
    last = pl.num_programs(0) - 1
    cur = i % 2
    nxt = 1 - cur
    tm = h1_ref.shape[0]
    R = TOKEN_TILE_ROWS
    slot_rows = 2 * tm * R

    def gather_row(pos_ref, b, k, r, prio):
        pltpu.make_async_copy(ys_ref.at[pl.ds(pos_ref[0, 0, r] * R, R), :],
                              ybuf.at[pl.ds(b * slot_rows + (k * tm + r) * R, R), :], sem.at[b]).start(priority=prio)

    def wait_slot(b):
        pltpu.make_async_copy(ys_ref.at[pl.ds(0, slot_rows), :], ybuf.at[pl.ds(0, slot_rows), :], sem.at[b]).wait()

    @pl.when(i == 0)
    def _():
        def body(g, carry):
            for u in range(ISSUE_UNROLL):
                gather_row(p1c_ref, 0, 0, g * ISSUE_UNROLL + u, u % 2)
                gather_row(p2c_ref, 0, 1, g * ISSUE_UNROLL + u, (u + 1) % 2)
            return carry

        lax.fori_loop(0, tm // ISSUE_UNROLL, body, 0)

    pp = _dot(p_ref[...].astype(BF16), wp_ref[...])
    for r in range(tm):
        gather_row(p1n_ref, nxt, 0, r, r % 2)
        gather_row(p2n_ref, nxt, 1, r, (r + 1) % 2)
    wait_slot(cur)
    rows = lambda k: jnp.concatenate(
        [ybuf[pl.ds(cur * slot_rows + k * tm * R + f, tm, stride=R), :] for f in range(R)], axis=1)
    rf = rf_ref[...]
    h2 = h1_ref[...] + (rf[:, 0:1] * rows(0) + rf[:, 1:2] * rows(1))
    hn = h2 * lax.rsqrt(jnp.mean(h2 * h2, axis=-1, keepdims=True) + EPS) * gple_ref[...]
    gate = jax.nn.sigmoid(_dot(hn.astype(BF16), wg_ref[...]))
    h3 = h2 + gate * pp
    out_ref[...] = h3 * lax.rsqrt(jnp.mean(h3 * h3, axis=-1, keepdims=True) + EPS) * gfin_ref[...]

    @pl.when(i == last)
    def _():
        wait_slot(nxt)


def _params(sem):
    return pltpu.CompilerParams(dimension_semantics=sem, vmem_limit_bytes=VMEM_LIMIT)


def _full(shape):
    nd = len(shape)
    return pl.BlockSpec(shape, lambda *_: (0,) * nd)


def _pad_lanes(a, lane0=0):
    out = jnp.zeros((a.shape[0], LANES), a.dtype)
    return out.at[:, lane0:lane0 + a.shape[1]].set(a)


def _layer(x2, p2, B, T, g_mix, w_in, w_conv_qkv, a_log, dt_bias, g_gdn_out, b_glu, w_conv_dw,
           b_conv_dw, ln_conv_g, ln_conv_b, w_out, g_moe, w_group_router, w_expert_router, w_up,
           w_down, g_ple, w_ple_gate, w_ple_proj, g_out):
    N, D = x2.shape
    hw = GDN_HEADS * GDN_DK
    tm = min(512, T)
    nt = N // tm

    o1 = 3 * hw
    o2 = o1 + hw
    nba = 2 * GDN_HEADS
    wqkv, wz, wbah, wbal, wglu = pl.pallas_call(
        functools.partial(_wsplit_kernel, o1=o1, o2=o2, nba=nba),
        out_shape=[jax.ShapeDtypeStruct((D, o1), BF16), jax.ShapeDtypeStruct((D, hw), BF16),
                   jax.ShapeDtypeStruct((D, LANES), BF16), jax.ShapeDtypeStruct((D, LANES), BF16),
                   jax.ShapeDtypeStruct((D, 2 * CONF_CH), BF16)],
        compiler_params=pltpu.CompilerParams(vmem_limit_bytes=VMEM_LIMIT),
        name="wsplit",
    )(jnp.swapaxes(w_in, 0, 1))
    pa = jnp.concatenate([_pad_lanes(a_log[None, :], GDN_HEADS), _pad_lanes(dt_bias[None, :], GDN_HEADS),
                          jnp.zeros((6, LANES), F32)], axis=0)
    cdw = jnp.concatenate([w_conv_dw, jnp.zeros((1, CONF_CH), F32)], axis=0)
    cvec = jnp.concatenate([b_conv_dw[None], ln_conv_g[None], ln_conv_b[None], jnp.zeros((5, CONF_CH), F32)], 0)

    row_spec = lambda w: pl.BlockSpec((tm, w), lambda i: (i, 0))
    q, k, v, bg, zs, yc = pl.pallas_call(
        functools.partial(_inproj_kernel, tiles_per_seq=T // tm, tm=tm),
        grid=(nt,),
        in_specs=[row_spec(D), _full((1, D)), _full(wqkv.shape), _full(wz.shape), _full(wbah.shape),
                  _full(wbal.shape), _full(wglu.shape), _full(w_conv_qkv.shape), _full(pa.shape),
                  _full((1, 2 * CONF_CH)), _full(cdw.shape), _full(cvec.shape)],
        out_specs=[row_spec(hw), row_spec(hw), row_spec(hw), row_spec(LANES), row_spec(hw), row_spec(CONF_CH)],
        out_shape=[jax.ShapeDtypeStruct((N, hw), F32)] * 3 + [jax.ShapeDtypeStruct((N, LANES), F32),
                   jax.ShapeDtypeStruct((N, hw), BF16), jax.ShapeDtypeStruct((N, CONF_CH), BF16)],
        scratch_shapes=[pltpu.VMEM((tm + QKV_HALO, 3 * hw), F32), pltpu.VMEM((tm + CONF_HALO, CONF_CH), F32)],
        compiler_params=_params(("arbitrary",)),
        name="inproj",
    )(x2, g_mix[None], wqkv, wz, wbah, wbal, wglu, w_conv_qkv, pa, b_glu[None], cdw, cvec)

    cb = 8
    rows = cb * GDN_CHUNK
    seq_spec = lambda w: pl.BlockSpec((B, rows, w), lambda n: (0, n, 0))
    r3 = lambda a: a.reshape(B, T, a.shape[-1])
    ygdn = pl.pallas_call(
        functools.partial(_gdn_kernel, cb=cb, nbatch=B, group=8 * GDN_HEADS),
        grid=(T // rows,),
        in_specs=[seq_spec(hw), seq_spec(hw), seq_spec(hw), seq_spec(LANES), seq_spec(hw),
                  pl.BlockSpec((1, GDN_DK), lambda n: (0, 0))],
        out_specs=seq_spec(hw),
        out_shape=jax.ShapeDtypeStruct((B, T, hw), BF16),
        scratch_shapes=[pltpu.VMEM((B * GDN_HEADS, GDN_DK, GDN_DK), F32)],
        compiler_params=_params(("arbitrary",)),
        name="gdn",
    )(r3(q), r3(k), r3(v), r3(bg), r3(zs), g_gdn_out[None]).reshape(N, hw)

    wo1 = w_out[:hw].astype(BF16)
    wo2 = w_out[hw:].astype(BF16)
    wrh, wrl = _split_bf16(_pad_lanes(jnp.concatenate([w_group_router, w_expert_router], axis=1)))
    nsub = 2 if tm % 256 == 0 else 1
    ltri = jnp.tril(jnp.ones((tm // nsub, tm // nsub), F32), -1).astype(BF16)
    assert D == TOKEN_TILE_ROWS * LANES
    tile_spec = pl.BlockSpec((tm * TOKEN_TILE_ROWS, LANES), lambda i: (i, 0))
    h1, xn, ri, rf, cnt = pl.pallas_call(
        functools.partial(_router_kernel, nsub=nsub),
        grid=(nt,),
        in_specs=[row_spec(D), row_spec(hw), row_spec(CONF_CH), _full(wo1.shape), _full(wo2.shape),
                  _full((1, D)), _full(wrh.shape), _full(wrl.shape), _full(ltri.shape)],
        out_specs=[row_spec(D), tile_spec, row_spec(LANES), row_spec(LANES), _full((8, LANES))],
        out_shape=[jax.ShapeDtypeStruct((N, D), F32), jax.ShapeDtypeStruct((N * TOKEN_TILE_ROWS, LANES), F32),
                   jax.ShapeDtypeStruct((N, LANES), I32), jax.ShapeDtypeStruct((N, LANES), F32),
                   jax.ShapeDtypeStruct((8, LANES), F32)],
        scratch_shapes=[pltpu.VMEM((1, LANES), F32)],
        compiler_params=_params(("arbitrary",)),
        name="router",
    )(x2, ygdn, yc, wo1, wo2, g_moe[None], wrh, wrl, ltri)

    nb = (2 * N) // EXPERT_BLOCK + N_EXPERTS
    nbl = -(-nb // LANES) * LANES
    tp = min(4 * tm, N)
    post, blk = pl.pallas_call(
        functools.partial(_pos_kernel, tm=tp),
        grid=(N // tp,),
        in_specs=[pl.BlockSpec((tp, LANES), lambda i: (i, 0)), _full((8, LANES))],
        out_specs=[pl.BlockSpec((8, tp), lambda i: (0, i)), _full((8, nbl))],
        out_shape=[jax.ShapeDtypeStruct((8, N), I32), jax.ShapeDtypeStruct((8, nbl), I32)],
        compiler_params=_params(("arbitrary",)),
        name="positions",
    )(ri, cnt)
    pos1 = post[0].reshape(nt, 1, tm)
    pos2 = post[1].reshape(nt, 1, tm)
    block_expert = blk[0, :nb]
    n_used = blk[1, :1]

    P = nb * EXPERT_BLOCK
    smem_spec = pl.BlockSpec((1, 1, tp), lambda i: (i, 0, 0), memory_space=pltpu.SMEM)
    flat_smem = pl.BlockSpec((P,), lambda i: (0,), memory_space=pltpu.SMEM)
    slots = pl.pallas_call(
        functools.partial(_slots_kernel, tm=tp),
        grid=(N // tp,),
        in_specs=[smem_spec, smem_spec],
        out_specs=flat_smem,
        out_shape=jax.ShapeDtypeStruct((P,), I32),
        scratch_shapes=[pltpu.VMEM((P,), I32), pltpu.SemaphoreType.DMA],
        compiler_params=_params(("arbitrary",)),
        name="slots",
    )(post[0].reshape(N // tp, 1, tp), post[1].reshape(N // tp, 1, tp)).reshape(nb, 1, EXPERT_BLOCK)

    any_spec = pl.BlockSpec(memory_space=pl.ANY)
    blk_smem = lambda off: pl.BlockSpec((1, 1, EXPERT_BLOCK),
                                        lambda j, be, nu: (jnp.minimum(j + off, nb - 1), 0, 0),
                                        memory_space=pltpu.SMEM)
    ys = pl.pallas_call(
        _expert_kernel,
        grid_spec=pltpu.PrefetchScalarGridSpec(
            num_scalar_prefetch=2,
            grid=(nb,),
            in_specs=[blk_smem(0), blk_smem(1), blk_smem(2), any_spec, any_spec, any_spec],
            out_specs=pl.BlockSpec((EXPERT_BLOCK * TOKEN_TILE_ROWS, LANES), lambda j, be, nu: (j, 0)),
            scratch_shapes=[pltpu.VMEM((D, 2 * D_EXPERT), BF16), pltpu.VMEM((D_EXPERT, D), BF16),
                            pltpu.VMEM((2, D, 2 * D_EXPERT), F32), pltpu.VMEM((2, D_EXPERT, D), F32),
                            pltpu.VMEM((GATHER_RING * EXPERT_BLOCK * TOKEN_TILE_ROWS, LANES), F32),
                            pltpu.SemaphoreType.DMA((GATHER_RING,)), pltpu.SemaphoreType.DMA((2,)),
                            pltpu.SMEM((1,), I32)]),
        out_shape=jax.ShapeDtypeStruct((P * TOKEN_TILE_ROWS, LANES), F32),
        compiler_params=_params(("arbitrary",)),
        name="experts",
    )(block_expert, n_used, slots, slots, slots, xn, w_up, w_down)

    wg = w_ple_gate.astype(BF16)
    wp = w_ple_proj.astype(BF16)
    pos_spec = lambda off: pl.BlockSpec((1, 1, tm), lambda i: (jnp.minimum(i + off, nt - 1), 0, 0),
                                        memory_space=pltpu.SMEM)
    out = pl.pallas_call(
        _ple_kernel,
        grid=(nt,),
        in_specs=[pos_spec(0), pos_spec(0), pos_spec(1), pos_spec(1),
                  row_spec(D), row_spec(LANES), row_spec(p2.shape[1]), any_spec,
                  _full((1, D)), _full(wg.shape), _full(wp.shape), _full((1, D))],
        out_specs=row_spec(D),
        out_shape=jax.ShapeDtypeStruct((N, D), F32),
        scratch_shapes=[pltpu.VMEM((2 * 2 * tm * TOKEN_TILE_ROWS, LANES), F32), pltpu.SemaphoreType.DMA((2,))],
        compiler_params=_params(("arbitrary",)),
        name="ple",
    )(pos1, pos2, pos1, pos2, h1, rf, p2, ys, g_ple[None], wg, wp, g_out[None])
    return out


def kernel(x, p, g_mix, w_in, w_conv_qkv, a_log, dt_bias, g_gdn_out, b_glu, w_conv_dw, b_conv_dw, ln_conv_g, ln_conv_b, w_out, g_moe, w_group_router, w_expert_router, w_up, w_down, g_ple, w_ple_gate, w_ple_proj, g_final):
    B, T, D = x.shape
    depth = p.shape[0]
    assert depth == 1, "the final RMSNorm is fused into the (single) layer's last kernel"
    h = _layer(x.reshape(B * T, D), p[0].reshape(B * T, -1), B, T, g_mix[0], w_in[0], w_conv_qkv[0],
               a_log[0], dt_bias[0], g_gdn_out[0], b_glu[0], w_conv_dw[0], b_conv_dw[0], ln_conv_g[0],
               ln_conv_b[0], w_out[0], g_moe[0], w_group_router[0], w_expert_router[0], w_up[0],
               w_down[0], g_ple[0], w_ple_gate[0], w_ple_proj[0], g_final)
    return h.reshape(B, T, D)
```

```python
import functools

import jax
import jax.numpy as jnp
from jax import lax
from jax.experimental import pallas as pl
from jax.experimental.pallas import tpu as pltpu

F32 = jnp.float32
BF16 = jnp.bfloat16
I32 = jnp.int32

EPS = 1e-6
GDN_HEADS = 4
GDN_DK = 128
GDN_CONV = 4
GDN_CHUNK = 64
CONF_CH = 512
CONF_WIDTH = 31
N_GROUPS = 8
EXPERTS_PER_GROUP = 8
N_EXPERTS = 64
D_EXPERT = 256
LANES = 128
SUBLANES = 8
TOKEN_TILE_ROWS = 8
ROUTE_LANE0 = N_GROUPS
EXPERT_BLOCK = 256
GATHER_RING = 3
ISSUE_UNROLL = 8
QKV_HALO = 8
CONF_HALO = 32
NEG = -1e30
VMEM_LIMIT = 56 * 1024 * 1024


def _dot(a, b):
    return jnp.dot(a, b, preferred_element_type=F32)


def _dot_nt(a, b):
    return lax.dot_general(a, b, (((1,), (1,)), ((), ())), preferred_element_type=F32)


def _dot_tn(a, b):
    return lax.dot_general(a, b, (((0,), (0,)), ((), ())), preferred_element_type=F32)


def _silu(x):
    return x * jax.nn.sigmoid(x)


def _split_bf16(x):
    hi = x.astype(BF16)
    lo = (x - hi.astype(F32)).astype(BF16)
    return hi, lo


def _causal_taps(ext_ref, w_ref, n_taps, halo, tm, cols):
    rows = ext_ref.shape[0]
    ext = ext_ref[:, cols]
    first = halo - (n_taps - 1)
    acc = None
    for res in range(SUBLANES):
        taps = [k for k in range(n_taps) if (first + k) % SUBLANES == res]
        if not taps:
            continue
        shifted = ext if res == 0 else pltpu.roll(ext, rows - res, 0)
        for k in taps:
            a = first + k - res
            term = w_ref[k:k + 1, cols] * shifted[a:a + tm, :]
            acc = term if acc is None else acc + term
    return acc


def _wsplit_kernel(wt_ref, wqkv_ref, wz_ref, wbah_ref, wbal_ref, wglu_ref, *, o1, o2, nba):
    d = wt_ref.shape[1]

    def cols(start, out_ref):
        for c in range(out_ref.shape[1] // LANES):
            for r in range(d // LANES):
                blk = wt_ref[start + c * LANES:start + (c + 1) * LANES, r * LANES:(r + 1) * LANES]
                out_ref[r * LANES:(r + 1) * LANES, c * LANES:(c + 1) * LANES] = blk.T.astype(BF16)

    cols(0, wqkv_ref)
    cols(o1, wz_ref)
    cols(o2 + nba, wglu_ref)
    lane = lax.broadcasted_iota(I32, (LANES, LANES), 1)
    for r in range(d // LANES):
        ba = jnp.where(lane < nba, wt_ref[o2:o2 + LANES, r * LANES:(r + 1) * LANES].T, 0.0)
        hi, lo = _split_bf16(ba)
        wbah_ref[r * LANES:(r + 1) * LANES, :] = hi
        wbal_ref[r * LANES:(r + 1) * LANES, :] = lo


def _inproj_kernel(x_ref, gmix_ref, wqkv_ref, wz_ref, wbah_ref, wbal_ref, wglu_ref, cw_ref, pa_ref,
                   bglu_ref, cdw_ref, cvec_ref,
                   q_ref, k_ref, v_ref, bg_ref, zs_ref, yc_ref,
                   extq, extu, *, tiles_per_seq, tm):
    i = pl.program_id(0)
    first = (i % tiles_per_seq) == 0

    x = x_ref[...]
    n1 = x * lax.rsqrt(jnp.mean(x * x, axis=-1, keepdims=True) + EPS) * gmix_ref[...]
    n1h, n1l = _split_bf16(n1)

    @pl.when(first)
    def _():
        extq[0:QKV_HALO, :] = jnp.zeros((QKV_HALO, extq.shape[1]), F32)
        extu[0:CONF_HALO, :] = jnp.zeros((CONF_HALO, extu.shape[1]), F32)

    @pl.when(jnp.logical_not(first))
    def _():
        extq[0:QKV_HALO, :] = extq[tm:tm + QKV_HALO, :]
        extu[0:CONF_HALO, :] = extu[tm:tm + CONF_HALO, :]

    extq[QKV_HALO:QKV_HALO + tm, :] = _dot(n1h, wqkv_ref[...])
    hw = GDN_HEADS * GDN_DK
    qc = _silu(_causal_taps(extq, cw_ref, GDN_CONV, QKV_HALO, tm, slice(0, hw)))
    kc = _silu(_causal_taps(extq, cw_ref, GDN_CONV, QKV_HALO, tm, slice(hw, 2 * hw)))
    for h in range(GDN_HEADS):
        cs = slice(h * GDN_DK, (h + 1) * GDN_DK)
        qh = qc[:, cs]
        q_ref[:, cs] = qh * lax.rsqrt(jnp.sum(qh * qh, axis=-1, keepdims=True) + EPS) * (GDN_DK ** -0.5)
        kh = kc[:, cs]
        k_ref[:, cs] = kh * lax.rsqrt(jnp.sum(kh * kh, axis=-1, keepdims=True) + EPS)
    v_ref[...] = _silu(_causal_taps(extq, cw_ref, GDN_CONV, QKV_HALO, tm, slice(2 * hw, 3 * hw)))

    zs_ref[...] = _silu(_dot(n1h, wz_ref[...])).astype(BF16)

    ba = _dot(n1h, wbah_ref[...]) + _dot(n1l, wbah_ref[...]) + _dot(n1h, wbal_ref[...])
    lane = lax.broadcasted_iota(I32, ba.shape, 1)
    row = lax.broadcasted_iota(I32, ba.shape, 0)
    beta = jax.nn.sigmoid(ba)
    sp_in = ba + pa_ref[1:2, :]
    softplus = jnp.maximum(sp_in, 0.0) + jnp.log1p(jnp.exp(-jnp.abs(sp_in)))
    g = -jnp.exp(pa_ref[0:1, :]) * softplus
    rin = row & (GDN_CHUNK - 1)
    s = 1
    while s < GDN_CHUNK:
        g = g + jnp.where(rin >= s, pltpu.roll(g, s, 0), 0.0)
        s *= 2
    bg_ref[...] = jnp.where(lane < GDN_HEADS, beta, g)

    glu = _dot(n1h, wglu_ref[...]) + bglu_ref[...]
    extu[CONF_HALO:CONF_HALO + tm, :] = glu[:, :CONF_CH] * jax.nn.sigmoid(glu[:, CONF_CH:])
    c = _causal_taps(extu, cdw_ref, CONF_WIDTH, CONF_HALO, tm, slice(0, CONF_CH)) + cvec_ref[0:1, :]
    mu = jnp.mean(c, axis=-1, keepdims=True)
    cc = c - mu
    var = jnp.mean(cc * cc, axis=-1, keepdims=True)
    yln = cc * lax.rsqrt(var + EPS) * cvec_ref[1:2, :] + cvec_ref[2:3, :]
    yc_ref[...] = _silu(yln).astype(BF16)


def _gdn_kernel(q_ref, k_ref, v_ref, bg_ref, zs_ref, gout_ref, y_ref, s_ref, *, cb, nbatch, group):
    n = pl.program_id(0)

    @pl.when(n == 0)
    def _():
        s_ref[...] = jnp.zeros(s_ref.shape, F32)

    C = GDN_CHUNK
    r = lax.broadcasted_iota(I32, (C, C), 0)
    c = lax.broadcasted_iota(I32, (C, C), 1)
    causal = r >= c
    strict = r > c
    eye = r == c
    eyef = jnp.where(eye, 1.0, 0.0).astype(F32)
    gout = gout_ref[...]

    def front(b, ci, h):
        rows = slice(ci * C, (ci + 1) * C)
        cols = slice(h * GDN_DK, (h + 1) * GDN_DK)
        q = q_ref[b, rows, cols]
        k = k_ref[b, rows, cols]
        v = v_ref[b, rows, cols]
        beta = bg_ref[b, rows, h:h + 1]
        gc = bg_ref[b, rows, GDN_HEADS + h:GDN_HEADS + h + 1]
        gc_row = jnp.sum(jnp.where(eye, gc, 0.0), axis=0, keepdims=True)
        gc_last = gc[C - 1:C, :]
        eg = jnp.exp(gc)
        kb = k * beta
        return dict(
            decay=jnp.where(causal, jnp.exp(jnp.minimum(gc - gc_row, 0.0)), 0.0),
            lhs=jnp.concatenate([kb, q], axis=0).astype(BF16),
            kbf=k.astype(BF16),
            rhs=jnp.concatenate([v * beta, kb * eg], axis=1).astype(BF16),
            qd=q * eg,
            kd=(k * jnp.exp(gc_last - gc)).astype(BF16),
            gl=jnp.exp(gc_last))

    insts = [(b, ci, h) for ci in range(cb) for b in range(nbatch) for h in range(GDN_HEADS)]
    done = {}
    for g0 in range(0, len(insts), group):
        grp = insts[g0:g0 + group]
        st = [front(*key) for key in grp]
        kq = [_dot_nt(s["lhs"], s["kbf"]) for s in st]
        xp = [-jnp.where(strict, m[:C] * s["decay"], 0.0) for m, s in zip(kq, st)]
        qk = [(m[C:] * s["decay"]).astype(BF16) for m, s in zip(kq, st)]
        tinv = [eyef + x for x in xp]
        for _ in range(5):
            xpb = [x.astype(BF16) for x in xp]
            xp = [_dot(x, x) for x in xpb]
            tinv = [t + _dot(t.astype(BF16), x.astype(BF16)) for t, x in zip(tinv, xp)]
        uw = [_dot(t.astype(BF16), s["rhs"]) for t, s in zip(tinv, st)]
        for key, s, m, qkm in zip(grp, st, uw, qk):
            done[key] = dict(u=m[:, :GDN_DK],
                             wq=jnp.concatenate([m[:, GDN_DK:], s["qd"]], axis=0).astype(BF16),
                             qk=qkm, kd=s["kd"], gl=s["gl"])

    chains = [(b, h) for b in range(nbatch) for h in range(GDN_HEADS)]
    for ci in range(cb):
        rows = slice(ci * C, (ci + 1) * C)
        dd = [done[(b, ci, h)] for b, h in chains]
        s_old = [s_ref[b * GDN_HEADS + h] for b, h in chains]
        wq = [_dot(d["wq"], s.astype(BF16)) for d, s in zip(dd, s_old)]
        v_new = [(d["u"] - m[:C]).astype(BF16) for d, m in zip(dd, wq)]
        o = [m[C:] + _dot(d["qk"], vn) for d, m, vn in zip(dd, wq, v_new)]
        s_new = [s * d["gl"] + _dot_tn(d["kd"], vn) for d, s, vn in zip(dd, s_old, v_new)]
        for (b, h), sn, oo in zip(chains, s_new, o):
            cols = slice(h * GDN_DK, (h + 1) * GDN_DK)
            s_ref[b * GDN_HEADS + h] = sn
            on = oo * lax.rsqrt(jnp.mean(oo * oo, axis=-1, keepdims=True) + EPS) * gout
            y_ref[b, rows, cols] = (on * zs_ref[b, rows, cols].astype(F32)).astype(BF16)


def _router_kernel(x_ref, yg_ref, yc_ref, wo1_ref, wo2_ref, gmoe_ref, wrh_ref, wrl_ref, ltri_ref,
                   h1_ref, xn_ref, ri_ref, rf_ref, cnt_ref, carry, *, nsub):
    i = pl.program_id(0)

    @pl.when(i == 0)
    def _():
        carry[...] = jnp.zeros(carry.shape, F32)

    ts = x_ref.shape[0] // nsub
    subs = [slice(s * ts, (s + 1) * ts) for s in range(nsub)]
    each = lambda fn, *lists: [fn(*args) for args in zip(*lists)]
    lane_sum = lambda v: jnp.sum(v, axis=-1, keepdims=True)
    lane_max = lambda v: jnp.max(v, axis=-1, keepdims=True)
    lane_min = lambda v: jnp.min(v, axis=-1, keepdims=True)

    h1 = [x_ref[sl, :] + _dot(yg_ref[sl, :], wo1_ref[...]) + _dot(yc_ref[sl, :], wo2_ref[...]) for sl in subs]
    for sl, h in zip(subs, h1):
        h1_ref[sl, :] = h
    xn = each(lambda h: h * lax.rsqrt(jnp.mean(h * h, axis=-1, keepdims=True) + EPS) * gmoe_ref[...], h1)
    parts = each(_split_bf16, xn)
    for s, v in enumerate(xn):
        for f in range(TOKEN_TILE_ROWS):
            xn_ref[pl.ds(s * ts * TOKEN_TILE_ROWS + f, ts, stride=TOKEN_TILE_ROWS), :] = v[:, f * LANES:(f + 1) * LANES]
    lg = [_dot(hi, wrh_ref[...]) + _dot(lo, wrh_ref[...]) + _dot(hi, wrl_ref[...]) for hi, lo in parts]

    lane = lax.broadcasted_iota(I32, lg[0].shape, 1)
    lanef = lane.astype(F32)
    none = float(LANES)
    gl = each(lambda v: jnp.where(lane < N_GROUPS, v, NEG), lg)
    gmax = each(lane_max, gl)
    g_w = each(lambda v, m: 1.0 / lane_sum(jnp.exp(v - m)), gl, gmax)
    gidx = each(lambda v, m: lane_min(jnp.where(v == m, lanef, none)), gl, gmax)
    lane_group = ((lane - ROUTE_LANE0) >> (EXPERTS_PER_GROUP.bit_length() - 1)).astype(F32)
    is_expert = (lane >= ROUTE_LANE0) & (lane < ROUTE_LANE0 + N_EXPERTS)
    el = each(lambda v, g: jnp.where(is_expert & (lane_group == g), v, NEG), lg, gidx)
    m1 = each(lane_max, el)
    i1 = each(lambda v, m: lane_min(jnp.where(v == m, lanef, none)), el, m1)
    el2 = each(lambda v, a: jnp.where(lanef == a, NEG, v), el, i1)
    m2 = each(lane_max, el2)
    i2 = each(lambda v, m: lane_min(jnp.where(v == m, lanef, none)), el2, m2)
    ratio = each(lambda a, b: jnp.exp(b - a), m1, m2)
    gate1 = each(lambda w, r: w / (1.0 + r), g_w, ratio)
    gate2 = each(lambda w, r: w * r / (1.0 + r), g_w, ratio)

    oh1 = each(lambda a: lanef == a, i1)
    oh2 = each(lambda a: lanef == a, i2)
    oh = each(lambda a, b: jnp.where(a | b, 1.0, 0.0).astype(F32), oh1, oh2)
    within = each(lambda v: _dot(ltri_ref[...], v.astype(BF16)), oh)
    base = carry[...]
    for s in range(nsub):
        cum = within[s] + base
        rank1 = lane_sum(jnp.where(oh1[s], cum, 0.0))
        rank2 = lane_sum(jnp.where(oh2[s], cum, 0.0))
        routes = jnp.where(lane == 0, i1[s] - ROUTE_LANE0,
                           jnp.where(lane == 1, i2[s] - ROUTE_LANE0,
                                     jnp.where(lane == 2, rank1, jnp.where(lane == 3, rank2, 0.0))))
        ri_ref[subs[s], :] = routes.astype(I32)
        rf_ref[subs[s], :] = jnp.where(lane == 0, gate1[s], jnp.where(lane == 1, gate2[s], 0.0))
        base = base + jnp.sum(oh[s], axis=0, keepdims=True)
    carry[...] = base
    cnt_ref[...] = jnp.broadcast_to(base, cnt_ref.shape)


def _pos_kernel(ri_ref, cnt_ref, post_ref, blk_ref, *, tm):
    shift = EXPERT_BLOCK.bit_length() - 1
    cnt = cnt_ref[...].astype(I32)
    padded = ((cnt + (EXPERT_BLOCK - 1)) >> shift) << shift
    lane8 = lax.broadcasted_iota(I32, padded.shape, 1)
    pend = padded
    s = 1
    while s < LANES:
        pend = pend + jnp.where(lane8 >= s, pltpu.roll(pend, s, 1), 0)
        s *= 2
    pstart = (pend - padded)[0:1, :].astype(F32)

    ri = ri_ref[...]
    lane = lax.broadcasted_iota(I32, ri.shape, 1)
    e1 = ri[:, 0:1] + ROUTE_LANE0
    e2 = ri[:, 1:2] + ROUTE_LANE0
    ps1 = jnp.sum(jnp.where(lane == e1, pstart, 0.0), axis=-1, keepdims=True)
    ps2 = jnp.sum(jnp.where(lane == e2, pstart, 0.0), axis=-1, keepdims=True)
    pos1 = ps1 + ri[:, 2:3].astype(F32)
    pos2 = ps2 + ri[:, 3:4].astype(F32)
    posmat = jnp.where(lane == 0, pos1, jnp.where(lane == 1, pos2, 0.0))
    for j in range(tm // LANES):
        pt = posmat[j * LANES:(j + 1) * LANES, :].T
        post_ref[:, j * LANES:(j + 1) * LANES] = pt[0:8, :].astype(I32)

    @pl.when(pl.program_id(0) == 0)
    def _():
        pendf = pend.astype(F32)[0:1, :]
        rr = lax.broadcasted_iota(I32, (LANES, LANES), 0)
        cc = lax.broadcasted_iota(I32, (LANES, LANES), 1)
        pend_col = jnp.sum(jnp.where(rr == cc, pendf, 0.0), axis=1, keepdims=True)
        nbl = blk_ref.shape[1]
        er = lax.broadcasted_iota(I32, (LANES, nbl), 0)
        jb = (lax.broadcasted_iota(I32, (LANES, nbl), 1) * EXPERT_BLOCK).astype(F32)
        valid = (er >= ROUTE_LANE0) & (er < ROUTE_LANE0 + N_EXPERTS)
        be = jnp.sum(jnp.where(valid & (pend_col <= jb), 1.0, 0.0), axis=0, keepdims=True)
        be = jnp.minimum(be, float(N_EXPERTS - 1))
        nused = jnp.max(pendf, axis=-1, keepdims=True) * (1.0 / EXPERT_BLOCK)
        rowi = lax.broadcasted_iota(I32, blk_ref.shape, 0)
        blk_ref[...] = jnp.where(rowi == 0, be, nused).astype(I32)


def _slots_kernel(pos1_ref, pos2_ref, slot_ref, fill, sem, *, tm):
    i = pl.program_id(0)

    @pl.when(i == 0)
    def _():
        fill[...] = lax.broadcasted_iota(I32, fill.shape, 0) & (EXPERT_BLOCK - 1)
        cp = pltpu.make_async_copy(fill, slot_ref, sem)
        cp.start()
        cp.wait()

    def body(r, carry):
        t = i * tm + r
        slot_ref[pos1_ref[0, 0, r]] = t
        slot_ref[pos2_ref[0, 0, r]] = t
        return carry

    lax.fori_loop(0, tm, body, 0, unroll=16)


def _expert_kernel(be_ref, nu_ref, sl0_ref, sl1_ref, sl2_ref, xn_ref, wup_ref, wdn_ref, y_ref,
                   wupb, wdnb, wup_stage, wdn_stage, xbuf, gsem, wsem, wslot):
    assert GATHER_RING == 3, "the three slot BlockSpecs (blocks j, j+1, j+2) fix the ring depth"
    j = pl.program_id(0)
    cur = lax.rem(j, GATHER_RING)
    n_used = nu_ref[0]
    R = TOKEN_TILE_ROWS
    buf_rows = EXPERT_BLOCK * R

    def gather_row(slot_ref, b, r, prio):
        pltpu.make_async_copy(xn_ref.at[pl.ds(slot_ref[0, 0, r] * R, R), :],
                              xbuf.at[pl.ds(b * buf_rows + r * R, R), :], gsem.at[b]).start(priority=prio)

    def wait_buffer(b):
        pltpu.make_async_copy(xn_ref.at[pl.ds(0, buf_rows), :], xbuf.at[pl.ds(0, buf_rows), :], gsem.at[b]).wait()

    @pl.when(j == 0)
    def _():
        def body(g, carry):
            for u in range(ISSUE_UNROLL):
                gather_row(sl0_ref, 0, g * ISSUE_UNROLL + u, u % 2)
                gather_row(sl1_ref, 1, g * ISSUE_UNROLL + u, (u + 1) % 2)
            return carry

        lax.fori_loop(0, EXPERT_BLOCK // ISSUE_UNROLL, body, 0)

    def weight_copies(e, s):
        return (pltpu.make_async_copy(wup_ref.at[e], wup_stage.at[s], wsem.at[s]),
                pltpu.make_async_copy(wdn_ref.at[e], wdn_stage.at[s], wsem.at[s]))

    @pl.when(j == 0)
    def _():
        wslot[0] = 0
        for cp in weight_copies(be_ref[0], 0):
            cp.start()

    prev = be_ref[jnp.maximum(j - 1, 0)]

    @pl.when((j < n_used) & ((j == 0) | (be_ref[j] != prev)))
    def _():
        s = wslot[0]
        for cp in weight_copies(0, s):
            cp.wait()
        wupb[...] = wup_stage[s].astype(BF16)
        wdnb[...] = wdn_stage[s].astype(BF16)
        e = be_ref[j]
        k = lax.while_loop(lambda k: (k < n_used) & (be_ref[k] == e), lambda k: k + 1, j + 1)

        @pl.when(k < n_used)
        def _():
            for cp in weight_copies(be_ref[k], 1 - s):
                cp.start()

        wslot[0] = 1 - s

    @pl.when(j < n_used)
    def _():
        wait_buffer(cur)
        base = cur * buf_rows
        x = jnp.concatenate([xbuf[pl.ds(base + f, EXPERT_BLOCK, stride=R), :] for f in range(R)], axis=1)
        ab = _dot(x.astype(BF16), wupb[...])
        ahead = lax.rem(j + (GATHER_RING - 1), GATHER_RING)
        for r in range(EXPERT_BLOCK):
            gather_row(sl2_ref, ahead, r, r % 2)
        hmid = _silu(ab[:, :D_EXPERT]) * ab[:, D_EXPERT:]
        y = _dot(hmid.astype(BF16), wdnb[...])
        for f in range(R):
            y_ref[pl.ds(f, EXPERT_BLOCK, stride=R), :] = y[:, f * LANES:(f + 1) * LANES]

    @pl.when(j == n_used)
    def _():
        wait_buffer(cur)
        wait_buffer(lax.rem(j + 1, GATHER_RING))

    @pl.when(j >= n_used)
    def _():
        y_ref[...] = jnp.zeros(y_ref.shape, F32)


def _ple_kernel(p1c_ref, p2c_ref, p1n_ref, p2n_ref, h1_ref, rf_ref, p_ref, ys_ref, gple_ref, wg_ref, wp_ref,
                gfin_ref, out_ref, ybuf, sem):
    i = pl.program_id(0)
    last = pl.num_programs(0) - 1
    cur = i % 2
    nxt = 1 - cur
    tm = h1_ref.shape[0]
    R = TOKEN_TILE_ROWS
    slot_rows = 2 * tm * R

    def gather_row(pos_ref, b, k, r, prio):
        pltpu.make_async_copy(ys_ref.at[pl.ds(pos_ref[0, 0, r] * R, R), :],
                              ybuf.at[pl.ds(b * slot_rows + (k * tm + r) * R, R), :], sem.at[b]).start(priority=prio)

    def wait_slot(b):
        pltpu.make_async_copy(ys_ref.at[pl.ds(0, slot_rows), :], ybuf.at[pl.ds(0, slot_rows), :], sem.at[b]).wait()

    @pl.when(i == 0)
    def _():
        def body(g, carry):
            for u in range(ISSUE_UNROLL):
                gather_row(p1c_ref, 0, 0, g * ISSUE_UNROLL + u, u % 2)
                gather_row(p2c_ref, 0, 1, g * ISSUE_UNROLL + u, (u + 1) % 2)
            return carry

        lax.fori_loop(0, tm // ISSUE_UNROLL, body, 0)

    bursts = 4
    per_burst = tm // bursts

    def start_rows(burst):
        for r in range(burst * per_burst, (burst + 1) * per_burst):
            gather_row(p1n_ref, nxt, 0, r, r % 2)
            gather_row(p2n_ref, nxt, 1, r, (r + 1) % 2)

    pp = _dot(p_ref[...].astype(BF16), wp_ref[...])
    start_rows(0)
    wait_slot(cur)
    rows = lambda k: jnp.concatenate(
        [ybuf[pl.ds(cur * slot_rows + k * tm * R + f, tm, stride=R), :] for f in range(R)], axis=1)
    rf = rf_ref[...]
    h2 = h1_ref[...] + (rf[:, 0:1] * rows(0) + rf[:, 1:2] * rows(1))
    hn = h2 * lax.rsqrt(jnp.mean(h2 * h2, axis=-1, keepdims=True) + EPS) * gple_ref[...]
    start_rows(1)
    gate = jax.nn.sigmoid(_dot(hn.astype(BF16), wg_ref[...]))
    start_rows(2)
    h3 = h2 + gate * pp
    out_ref[...] = h3 * lax.rsqrt(jnp.mean(h3 * h3, axis=-1, keepdims=True) + EPS) * gfin_ref[...]
    start_rows(3)

    @pl.when(i == last)
    def _():
        wait_slot(nxt)


def _params(sem):
    return pltpu.CompilerParams(dimension_semantics=sem, vmem_limit_bytes=VMEM_LIMIT)


def _full(shape):
    nd = len(shape)
    return pl.BlockSpec(shape, lambda *_: (0,) * nd)


def _pad_lanes(a, lane0=0):
    out = jnp.zeros((a.shape[0], LANES), a.dtype)
    return out.at[:, lane0:lane0 + a.shape[1]].set(a)


def _layer(x2, p2, B, T, g_mix, w_in, w_conv_qkv, a_log, dt_bias, g_gdn_out, b_glu, w_conv_dw,
           b_conv_dw, ln_conv_g, ln_conv_b, w_out, g_moe, w_group_router, w_expert_router, w_up,
           w_down, g_ple, w_ple_gate, w_ple_proj, g_out):
    N, D = x2.shape
    hw = GDN_HEADS * GDN_DK
    tm = min(512, T)
    nt = N // tm

    o1 = 3 * hw
    o2 = o1 + hw
    nba = 2 * GDN_HEADS
    wqkv, wz, wbah, wbal, wglu = pl.pallas_call(
        functools.partial(_wsplit_kernel, o1=o1, o2=o2, nba=nba),
        out_shape=[jax.ShapeDtypeStruct((D, o1), BF16), jax.ShapeDtypeStruct((D, hw), BF16),
                   jax.ShapeDtypeStruct((D, LANES), BF16), jax.ShapeDtypeStruct((D, LANES), BF16),
                   jax.ShapeDtypeStruct((D, 2 * CONF_CH), BF16)],
        compiler_params=pltpu.CompilerParams(vmem_limit_bytes=VMEM_LIMIT),
        name="wsplit",
    )(jnp.swapaxes(w_in, 0, 1))
    pa = jnp.concatenate([_pad_lanes(a_log[None, :], GDN_HEADS), _pad_lanes(dt_bias[None, :], GDN_HEADS),
                          jnp.zeros((6, LANES), F32)], axis=0)
    cdw = jnp.concatenate([w_conv_dw, jnp.zeros((1, CONF_CH), F32)], axis=0)
    cvec = jnp.concatenate([b_conv_dw[None], ln_conv_g[None], ln_conv_b[None], jnp.zeros((5, CONF_CH), F32)], 0)

    row_spec = lambda w: pl.BlockSpec((tm, w), lambda i: (i, 0))
    q, k, v, bg, zs, yc = pl.pallas_call(
        functools.partial(_inproj_kernel, tiles_per_seq=T // tm, tm=tm),
        grid=(nt,),
        in_specs=[row_spec(D), _full((1, D)), _full(wqkv.shape), _full(wz.shape), _full(wbah.shape),
                  _full(wbal.shape), _full(wglu.shape), _full(w_conv_qkv.shape), _full(pa.shape),
                  _full((1, 2 * CONF_CH)), _full(cdw.shape), _full(cvec.shape)],
        out_specs=[row_spec(hw), row_spec(hw), row_spec(hw), row_spec(LANES), row_spec(hw), row_spec(CONF_CH)],
        out_shape=[jax.ShapeDtypeStruct((N, hw), F32)] * 3 + [jax.ShapeDtypeStruct((N, LANES), F32),
                   jax.ShapeDtypeStruct((N, hw), BF16), jax.ShapeDtypeStruct((N, CONF_CH), BF16)],
        scratch_shapes=[pltpu.VMEM((tm + QKV_HALO, 3 * hw), F32), pltpu.VMEM((tm + CONF_HALO, CONF_CH), F32)],
        compiler_params=_params(("arbitrary",)),
        name="inproj",
    )(x2, g_mix[None], wqkv, wz, wbah, wbal, wglu, w_conv_qkv, pa, b_glu[None], cdw, cvec)

    cb = 8
    rows = cb * GDN_CHUNK
    seq_spec = lambda w: pl.BlockSpec((B, rows, w), lambda n: (0, n, 0))
    r3 = lambda a: a.reshape(B, T, a.shape[-1])
    ygdn = pl.pallas_call(
        functools.partial(_gdn_kernel, cb=cb, nbatch=B, group=8 * GDN_HEADS),
        grid=(T // rows,),
        in_specs=[seq_spec(hw), seq_spec(hw), seq_spec(hw), seq_spec(LANES), seq_spec(hw),
                  pl.BlockSpec((1, GDN_DK), lambda n: (0, 0))],
        out_specs=seq_spec(hw),
        out_shape=jax.ShapeDtypeStruct((B, T, hw), BF16),
        scratch_shapes=[pltpu.VMEM((B * GDN_HEADS, GDN_DK, GDN_DK), F32)],
        compiler_params=_params(("arbitrary",)),
        name="gdn",
    )(r3(q), r3(k), r3(v), r3(bg), r3(zs), g_gdn_out[None]).reshape(N, hw)

    wo1 = w_out[:hw].astype(BF16)
    wo2 = w_out[hw:].astype(BF16)
    wrh, wrl = _split_bf16(_pad_lanes(jnp.concatenate([w_group_router, w_expert_router], axis=1)))
    nsub = 2 if tm % 256 == 0 else 1
    ltri = jnp.tril(jnp.ones((tm // nsub, tm // nsub), F32), -1).astype(BF16)
    assert D == TOKEN_TILE_ROWS * LANES
    tile_spec = pl.BlockSpec((tm * TOKEN_TILE_ROWS, LANES), lambda i: (i, 0))
    h1, xn, ri, rf, cnt = pl.pallas_call(
        functools.partial(_router_kernel, nsub=nsub),
        grid=(nt,),
        in_specs=[row_spec(D), row_spec(hw), row_spec(CONF_CH), _full(wo1.shape), _full(wo2.shape),
                  _full((1, D)), _full(wrh.shape), _full(wrl.shape), _full(ltri.shape)],
        out_specs=[row_spec(D), tile_spec, row_spec(LANES), row_spec(LANES), _full((8, LANES))],
        out_shape=[jax.ShapeDtypeStruct((N, D), F32), jax.ShapeDtypeStruct((N * TOKEN_TILE_ROWS, LANES), F32),
                   jax.ShapeDtypeStruct((N, LANES), I32), jax.ShapeDtypeStruct((N, LANES), F32),
                   jax.ShapeDtypeStruct((8, LANES), F32)],
        scratch_shapes=[pltpu.VMEM((1, LANES), F32)],
        compiler_params=_params(("arbitrary",)),
        name="router",
    )(x2, ygdn, yc, wo1, wo2, g_moe[None], wrh, wrl, ltri)

    nb = (2 * N) // EXPERT_BLOCK + N_EXPERTS
    nbl = -(-nb // LANES) * LANES
    tp = min(4 * tm, N)
    post, blk = pl.pallas_call(
        functools.partial(_pos_kernel, tm=tp),
        grid=(N // tp,),
        in_specs=[pl.BlockSpec((tp, LANES), lambda i: (i, 0)), _full((8, LANES))],
        out_specs=[pl.BlockSpec((8, tp), lambda i: (0, i)), _full((8, nbl))],
        out_shape=[jax.ShapeDtypeStruct((8, N), I32), jax.ShapeDtypeStruct((8, nbl), I32)],
        compiler_params=_params(("arbitrary",)),
        name="positions",
    )(ri, cnt)
    pos1 = post[0].reshape(nt, 1, tm)
    pos2 = post[1].reshape(nt, 1, tm)
    block_expert = blk[0, :nb]
    n_used = blk[1, :1]

    P = nb * EXPERT_BLOCK
    smem_spec = pl.BlockSpec((1, 1, tp), lambda i: (i, 0, 0), memory_space=pltpu.SMEM)
    flat_smem = pl.BlockSpec((P,), lambda i: (0,), memory_space=pltpu.SMEM)
    slots = pl.pallas_call(
        functools.partial(_slots_kernel, tm=tp),
        grid=(N // tp,),
        in_specs=[smem_spec, smem_spec],
        out_specs=flat_smem,
        out_shape=jax.ShapeDtypeStruct((P,), I32),
        scratch_shapes=[pltpu.VMEM((P,), I32), pltpu.SemaphoreType.DMA],
        compiler_params=_params(("arbitrary",)),
        name="slots",
    )(post[0].reshape(N // tp, 1, tp), post[1].reshape(N // tp, 1, tp)).reshape(nb, 1, EXPERT_BLOCK)

    any_spec = pl.BlockSpec(memory_space=pl.ANY)
    blk_smem = lambda off: pl.BlockSpec((1, 1, EXPERT_BLOCK),
                                        lambda j, be, nu: (jnp.minimum(j + off, nb - 1), 0, 0),
                                        memory_space=pltpu.SMEM)
    ys = pl.pallas_call(
        _expert_kernel,
        grid_spec=pltpu.PrefetchScalarGridSpec(
            num_scalar_prefetch=2,
            grid=(nb,),
            in_specs=[blk_smem(0), blk_smem(1), blk_smem(2), any_spec, any_spec, any_spec],
            out_specs=pl.BlockSpec((EXPERT_BLOCK * TOKEN_TILE_ROWS, LANES), lambda j, be, nu: (j, 0)),
            scratch_shapes=[pltpu.VMEM((D, 2 * D_EXPERT), BF16), pltpu.VMEM((D_EXPERT, D), BF16),
                            pltpu.VMEM((2, D, 2 * D_EXPERT), F32), pltpu.VMEM((2, D_EXPERT, D), F32),
                            pltpu.VMEM((GATHER_RING * EXPERT_BLOCK * TOKEN_TILE_ROWS, LANES), F32),
                            pltpu.SemaphoreType.DMA((GATHER_RING,)), pltpu.SemaphoreType.DMA((2,)),
                            pltpu.SMEM((1,), I32)]),
        out_shape=jax.ShapeDtypeStruct((P * TOKEN_TILE_ROWS, LANES), F32),
        compiler_params=_params(("arbitrary",)),
        name="experts",
    )(block_expert, n_used, slots, slots, slots, xn, w_up, w_down)

    wg = w_ple_gate.astype(BF16)
    wp = w_ple_proj.astype(BF16)
    pos_spec = lambda off: pl.BlockSpec((1, 1, tm), lambda i: (jnp.minimum(i + off, nt - 1), 0, 0),
                                        memory_space=pltpu.SMEM)
    out = pl.pallas_call(
        _ple_kernel,
        grid=(nt,),
        in_specs=[pos_spec(0), pos_spec(0), pos_spec(1), pos_spec(1),
                  row_spec(D), row_spec(LANES), row_spec(p2.shape[1]), any_spec,
                  _full((1, D)), _full(wg.shape), _full(wp.shape), _full((1, D))],
        out_specs=row_spec(D),
        out_shape=jax.ShapeDtypeStruct((N, D), F32),
        scratch_shapes=[pltpu.VMEM((2 * 2 * tm * TOKEN_TILE_ROWS, LANES), F32), pltpu.SemaphoreType.DMA((2,))],
        compiler_params=_params(("arbitrary",)),
        name="ple",
    )(pos1, pos2, pos1, pos2, h1, rf, p2, ys, g_ple[None], wg, wp, g_out[None])
    return out


def kernel(x, p, g_mix, w_in, w_conv_qkv, a_log, dt_bias, g_gdn_out, b_glu, w_conv_dw, b_conv_dw, ln_conv_g, ln_conv_b, w_out, g_moe, w_group_router, w_expert_router, w_up, w_down, g_ple, w_ple_gate, w_ple_proj, g_final):
    B, T, D = x.shape
    depth = p.shape[0]
    assert depth == 1, "the final RMSNorm is fused into the (single) layer's last kernel"
    h = _layer(x.reshape(B * T, D), p[0].reshape(B * T, -1), B, T, g_mix[0], w_in[0], w_conv_qkv[0],
               a_log[0], dt_bias[0], g_gdn_out[0], b_glu[0], w_conv_dw[0], b_conv_dw[0], ln_conv_g[0],
               ln_conv_b[0], w_out[0], g_moe[0], w_group_router[0], w_expert_router[0], w_up[0],
               w_down[0], g_ple[0], w_ple_gate[0], w_ple_proj[0], g_final)
    return h.reshape(B, T, D)
```

```python
import functools

import jax
import jax.numpy as jnp
from jax import lax
from jax.experimental import pallas as pl
from jax.experimental.pallas import tpu as pltpu

F32 = jnp.float32
BF16 = jnp.bfloat16
I32 = jnp.int32

EPS = 1e-6
GDN_HEADS = 4
GDN_DK = 128
GDN_CONV = 4
GDN_CHUNK = 64
CONF_CH = 512
CONF_WIDTH = 31
N_GROUPS = 8
EXPERTS_PER_GROUP = 8
N_EXPERTS = 64
D_EXPERT = 256
LANES = 128
SUBLANES = 8
TOKEN_TILE_ROWS = 8
ROUTE_LANE0 = N_GROUPS
EXPERT_BLOCK = 256
GATHER_RING = 3
ISSUE_UNROLL = 8
QKV_HALO = 8
CONF_HALO = 32
NEG = -1e30
VMEM_LIMIT = 56 * 1024 * 1024


def _dot(a, b):
    return jnp.dot(a, b, preferred_element_type=F32)


def _dot_nt(a, b):
    return lax.dot_general(a, b, (((1,), (1,)), ((), ())), preferred_element_type=F32)


def _dot_tn(a, b):
    return lax.dot_general(a, b, (((0,), (0,)), ((), ())), preferred_element_type=F32)


def _silu(x):
    return x * jax.nn.sigmoid(x)


def _split_bf16(x):
    hi = x.astype(BF16)
    lo = (x - hi.astype(F32)).astype(BF16)
    return hi, lo


def _causal_taps(ext_ref, w_ref, n_taps, halo, tm, cols):
    rows = ext_ref.shape[0]
    ext = ext_ref[:, cols]
    first = halo - (n_taps - 1)
    acc = None
    for res in range(SUBLANES):
        taps = [k for k in range(n_taps) if (first + k) % SUBLANES == res]
        if not taps:
            continue
        shifted = ext if res == 0 else pltpu.roll(ext, rows - res, 0)
        for k in taps:
            a = first + k - res
            term = w_ref[k:k + 1, cols] * shifted[a:a + tm, :]
            acc = term if acc is None else acc + term
    return acc


def _wsplit_kernel(wt_ref, wqkv_ref, wz_ref, wbah_ref, wbal_ref, wglu_ref, *, o1, o2, nba):
    d = wt_ref.shape[1]

    def cols(start, out_ref):
        for c in range(out_ref.shape[1] // LANES):
            for r in range(d // LANES):
                blk = wt_ref[start + c * LANES:start + (c + 1) * LANES, r * LANES:(r + 1) * LANES]
                out_ref[r * LANES:(r + 1) * LANES, c * LANES:(c + 1) * LANES] = blk.T.astype(BF16)

    cols(0, wqkv_ref)
    cols(o1, wz_ref)
    cols(o2 + nba, wglu_ref)
    lane = lax.broadcasted_iota(I32, (LANES, LANES), 1)
    for r in range(d // LANES):
        ba = jnp.where(lane < nba, wt_ref[o2:o2 + LANES, r * LANES:(r + 1) * LANES].T, 0.0)
        hi, lo = _split_bf16(ba)
        wbah_ref[r * LANES:(r + 1) * LANES, :] = hi
        wbal_ref[r * LANES:(r + 1) * LANES, :] = lo


def _inproj_kernel(x_ref, gmix_ref, wqkv_ref, wz_ref, wbah_ref, wbal_ref, wglu_ref, cw_ref, pa_ref,
                   bglu_ref, cdw_ref, cvec_ref,
                   q_ref, k_ref, v_ref, bg_ref, zs_ref, yc_ref,
                   extq, extu, n1h_s, n1l_s, *, tiles_per_seq, tm):
    i = pl.program_id(0)
    phase = pl.program_id(1)
    first = (i % tiles_per_seq) == 0
    hw = GDN_HEADS * GDN_DK

    @pl.when(phase == 0)
    def _():
        x = x_ref[...]
        n1 = x * lax.rsqrt(jnp.mean(x * x, axis=-1, keepdims=True) + EPS) * gmix_ref[...]
        n1h, n1l = _split_bf16(n1)
        n1h_s[...] = n1h
        n1l_s[...] = n1l

        @pl.when(first)
        def _():
            extq[0:QKV_HALO, :] = jnp.zeros((QKV_HALO, extq.shape[1]), F32)

        @pl.when(jnp.logical_not(first))
        def _():
            extq[0:QKV_HALO, :] = extq[tm:tm + QKV_HALO, :]

        extq[QKV_HALO:QKV_HALO + tm, :] = _dot(n1h, wqkv_ref[...])
        qc = _silu(_causal_taps(extq, cw_ref, GDN_CONV, QKV_HALO, tm, slice(0, hw)))
        kc = _silu(_causal_taps(extq, cw_ref, GDN_CONV, QKV_HALO, tm, slice(hw, 2 * hw)))
        for h in range(GDN_HEADS):
            cs = slice(h * GDN_DK, (h + 1) * GDN_DK)
            qh = qc[:, cs]
            q_ref[:, cs] = qh * lax.rsqrt(jnp.sum(qh * qh, axis=-1, keepdims=True) + EPS) * (GDN_DK ** -0.5)
            kh = kc[:, cs]
            k_ref[:, cs] = kh * lax.rsqrt(jnp.sum(kh * kh, axis=-1, keepdims=True) + EPS)

    @pl.when(phase == 0)
    def _():
        n1h = n1h_s[...]
        n1l = n1l_s[...]
        v_ref[...] = _silu(_causal_taps(extq, cw_ref, GDN_CONV, QKV_HALO, tm, slice(2 * hw, 3 * hw)))
        zs_ref[...] = _silu(_dot(n1h, wz_ref[...])).astype(BF16)

        ba = _dot(n1h, wbah_ref[...]) + _dot(n1l, wbah_ref[...]) + _dot(n1h, wbal_ref[...])
        lane = lax.broadcasted_iota(I32, ba.shape, 1)
        row = lax.broadcasted_iota(I32, ba.shape, 0)
        beta = jax.nn.sigmoid(ba)
        sp_in = ba + pa_ref[1:2, :]
        softplus = jnp.maximum(sp_in, 0.0) + jnp.log1p(jnp.exp(-jnp.abs(sp_in)))
        g = -jnp.exp(pa_ref[0:1, :]) * softplus
        rin = row & (GDN_CHUNK - 1)
        s = 1
        while s < GDN_CHUNK:
            g = g + jnp.where(rin >= s, pltpu.roll(g, s, 0), 0.0)
            s *= 2
        bg_ref[...] = jnp.where(lane < GDN_HEADS, beta, g)

    @pl.when(phase == 1)
    def _():
        @pl.when(first)
        def _():
            extu[0:CONF_HALO, :] = jnp.zeros((CONF_HALO, extu.shape[1]), F32)

        @pl.when(jnp.logical_not(first))
        def _():
            extu[0:CONF_HALO, :] = extu[tm:tm + CONF_HALO, :]

        glu = _dot(n1h_s[...], wglu_ref[...]) + bglu_ref[...]
        extu[CONF_HALO:CONF_HALO + tm, :] = glu[:, :CONF_CH] * jax.nn.sigmoid(glu[:, CONF_CH:])
        c = _causal_taps(extu, cdw_ref, CONF_WIDTH, CONF_HALO, tm, slice(0, CONF_CH)) + cvec_ref[0:1, :]
        mu = jnp.mean(c, axis=-1, keepdims=True)
        cc = c - mu
        var = jnp.mean(cc * cc, axis=-1, keepdims=True)
        yln = cc * lax.rsqrt(var + EPS) * cvec_ref[1:2, :] + cvec_ref[2:3, :]
        yc_ref[...] = _silu(yln).astype(BF16)


def _gdn_kernel(q_ref, k_ref, v_ref, bg_ref, zs_ref, gout_ref, y_ref, s_ref, *, cb, nbatch, group):
    n = pl.program_id(0)

    @pl.when(n == 0)
    def _():
        s_ref[...] = jnp.zeros(s_ref.shape, F32)

    C = GDN_CHUNK
    r = lax.broadcasted_iota(I32, (C, C), 0)
    c = lax.broadcasted_iota(I32, (C, C), 1)
    causal = r >= c
    strict = r > c
    eye = r == c
    eyef = jnp.where(eye, 1.0, 0.0).astype(F32)
    gout = gout_ref[...]

    def front(b, ci, h):
        rows = slice(ci * C, (ci + 1) * C)
        cols = slice(h * GDN_DK, (h + 1) * GDN_DK)
        q = q_ref[b, rows, cols]
        k = k_ref[b, rows, cols]
        v = v_ref[b, rows, cols]
        beta = bg_ref[b, rows, h:h + 1]
        gc = bg_ref[b, rows, GDN_HEADS + h:GDN_HEADS + h + 1]
        gc_row = jnp.sum(jnp.where(eye, gc, 0.0), axis=0, keepdims=True)
        gc_last = gc[C - 1:C, :]
        eg = jnp.exp(gc)
        kb = k * beta
        return dict(
            decay=jnp.where(causal, jnp.exp(jnp.minimum(gc - gc_row, 0.0)), 0.0),
            lhs=jnp.concatenate([kb, q], axis=0).astype(BF16),
            kbf=k.astype(BF16),
            rhs=jnp.concatenate([v * beta, kb * eg], axis=1).astype(BF16),
            qd=q * eg,
            kd=(k * jnp.exp(gc_last - gc)).astype(BF16),
            gl=jnp.exp(gc_last))

    insts = [(b, ci, h) for ci in range(cb) for b in range(nbatch) for h in range(GDN_HEADS)]
    done = {}
    for g0 in range(0, len(insts), group):
        grp = insts[g0:g0 + group]
        st = [front(*key) for key in grp]
        kq = [_dot_nt(s["lhs"], s["kbf"]) for s in st]
        xp = [-jnp.where(strict, m[:C] * s["decay"], 0.0) for m, s in zip(kq, st)]
        qk = [(m[C:] * s["decay"]).astype(BF16) for m, s in zip(kq, st)]
        tinv = [eyef + x for x in xp]
        for _ in range(5):
            xpb = [x.astype(BF16) for x in xp]
            xp = [_dot(x, x) for x in xpb]
            tinv = [t + _dot(t.astype(BF16), x.astype(BF16)) for t, x in zip(tinv, xp)]
        uw = [_dot(t.astype(BF16), s["rhs"]) for t, s in zip(tinv, st)]
        for key, s, m, qkm in zip(grp, st, uw, qk):
            done[key] = dict(u=m[:, :GDN_DK],
                             wq=jnp.concatenate([m[:, GDN_DK:], s["qd"]], axis=0).astype(BF16),
                             qk=qkm, kd=s["kd"], gl=s["gl"])

    chains = [(b, h) for b in range(nbatch) for h in range(GDN_HEADS)]
    for ci in range(cb):
        rows = slice(ci * C, (ci + 1) * C)
        dd = [done[(b, ci, h)] for b, h in chains]
        s_old = [s_ref[b * GDN_HEADS + h] for b, h in chains]
        wq = [_dot(d["wq"], s.astype(BF16)) for d, s in zip(dd, s_old)]
        v_new = [(d["u"] - m[:C]).astype(BF16) for d, m in zip(dd, wq)]
        o = [m[C:] + _dot(d["qk"], vn) for d, m, vn in zip(dd, wq, v_new)]
        s_new = [s * d["gl"] + _dot_tn(d["kd"], vn) for d, s, vn in zip(dd, s_old, v_new)]
        for (b, h), sn, oo in zip(chains, s_new, o):
            cols = slice(h * GDN_DK, (h + 1) * GDN_DK)
            s_ref[b * GDN_HEADS + h] = sn
            on = oo * lax.rsqrt(jnp.mean(oo * oo, axis=-1, keepdims=True) + EPS) * gout
            y_ref[b, rows, cols] = (on * zs_ref[b, rows, cols].astype(F32)).astype(BF16)


def _router_kernel(x_ref, yg_ref, yc_ref, wo1_ref, wo2_ref, gmoe_ref, wrh_ref, wrl_ref, ltri_ref,
                   h1_ref, xn_ref, ri_ref, rf_ref, cnt_ref, carry, *, nsub):
    i = pl.program_id(0)

    @pl.when(i == 0)
    def _():
        carry[...] = jnp.zeros(carry.shape, F32)

    ts = x_ref.shape[0] // nsub
    subs = [slice(s * ts, (s + 1) * ts) for s in range(nsub)]
    each = lambda fn, *lists: [fn(*args) for args in zip(*lists)]
    lane_sum = lambda v: jnp.sum(v, axis=-1, keepdims=True)
    lane_max = lambda v: jnp.max(v, axis=-1, keepdims=True)
    lane_min = lambda v: jnp.min(v, axis=-1, keepdims=True)

    h1 = [x_ref[sl, :] + _dot(yg_ref[sl, :], wo1_ref[...]) + _dot(yc_ref[sl, :], wo2_ref[...]) for sl in subs]
    for sl, h in zip(subs, h1):
        h1_ref[sl, :] = h
    xn = each(lambda h: h * lax.rsqrt(jnp.mean(h * h, axis=-1, keepdims=True) + EPS) * gmoe_ref[...], h1)
    parts = each(_split_bf16, xn)
    for s, v in enumerate(xn):
        for f in range(TOKEN_TILE_ROWS):
            xn_ref[pl.ds(s * ts * TOKEN_TILE_ROWS + f, ts, stride=TOKEN_TILE_ROWS), :] = v[:, f * LANES:(f + 1) * LANES]
    lg = [_dot(hi, wrh_ref[...]) + _dot(lo, wrh_ref[...]) + _dot(hi, wrl_ref[...]) for hi, lo in parts]

    lane = lax.broadcasted_iota(I32, lg[0].shape, 1)
    lanef = lane.astype(F32)
    none = float(LANES)
    gl = each(lambda v: jnp.where(lane < N_GROUPS, v, NEG), lg)
    gmax = each(lane_max, gl)
    g_w = each(lambda v, m: 1.0 / lane_sum(jnp.exp(v - m)), gl, gmax)
    gidx = each(lambda v, m: lane_min(jnp.where(v == m, lanef, none)), gl, gmax)
    lane_group = ((lane - ROUTE_LANE0) >> (EXPERTS_PER_GROUP.bit_length() - 1)).astype(F32)
    is_expert = (lane >= ROUTE_LANE0) & (lane < ROUTE_LANE0 + N_EXPERTS)
    el = each(lambda v, g: jnp.where(is_expert & (lane_group == g), v, NEG), lg, gidx)
    m1 = each(lane_max, el)
    i1 = each(lambda v, m: lane_min(jnp.where(v == m, lanef, none)), el, m1)
    el2 = each(lambda v, a: jnp.where(lanef == a, NEG, v), el, i1)
    m2 = each(lane_max, el2)
    i2 = each(lambda v, m: lane_min(jnp.where(v == m, lanef, none)), el2, m2)
    ratio = each(lambda a, b: jnp.exp(b - a), m1, m2)
    gate1 = each(lambda w, r: w / (1.0 + r), g_w, ratio)
    gate2 = each(lambda w, r: w * r / (1.0 + r), g_w, ratio)

    oh1 = each(lambda a: lanef == a, i1)
    oh2 = each(lambda a: lanef == a, i2)
    oh = each(lambda a, b: jnp.where(a | b, 1.0, 0.0).astype(F32), oh1, oh2)
    within = each(lambda v: _dot(ltri_ref[...], v.astype(BF16)), oh)
    base = carry[...]
    for s in range(nsub):
        cum = within[s] + base
        rank1 = lane_sum(jnp.where(oh1[s], cum, 0.0))
        rank2 = lane_sum(jnp.where(oh2[s], cum, 0.0))
        routes = jnp.where(lane == 0, i1[s] - ROUTE_LANE0,
                           jnp.where(lane == 1, i2[s] - ROUTE_LANE0,
                                     jnp.where(lane == 2, rank1, jnp.where(lane == 3, rank2, 0.0))))
        ri_ref[subs[s], :] = routes.astype(I32)
        rf_ref[subs[s], :] = jnp.where(lane == 0, gate1[s], jnp.where(lane == 1, gate2[s], 0.0))
        base = base + jnp.sum(oh[s], axis=0, keepdims=True)
    carry[...] = base
    cnt_ref[...] = jnp.broadcast_to(base, cnt_ref.shape)


def _pos_kernel(ri_ref, cnt_ref, post_ref, blk_ref, *, tm):
    shift = EXPERT_BLOCK.bit_length() - 1
    cnt = cnt_ref[...].astype(I32)
    padded = ((cnt + (EXPERT_BLOCK - 1)) >> shift) << shift
    lane8 = lax.broadcasted_iota(I32, padded.shape, 1)
    pend = padded
    s = 1
    while s < LANES:
        pend = pend + jnp.where(lane8 >= s, pltpu.roll(pend, s, 1), 0)
        s *= 2
    pstart = (pend - padded)[0:1, :].astype(F32)

    ri = ri_ref[...]
    lane = lax.broadcasted_iota(I32, ri.shape, 1)
    e1 = ri[:, 0:1] + ROUTE_LANE0
    e2 = ri[:, 1:2] + ROUTE_LANE0
    ps1 = jnp.sum(jnp.where(lane == e1, pstart, 0.0), axis=-1, keepdims=True)
    ps2 = jnp.sum(jnp.where(lane == e2, pstart, 0.0), axis=-1, keepdims=True)
    pos1 = ps1 + ri[:, 2:3].astype(F32)
    pos2 = ps2 + ri[:, 3:4].astype(F32)
    posmat = jnp.where(lane == 0, pos1, jnp.where(lane == 1, pos2, 0.0))
    for j in range(tm // LANES):
        pt = posmat[j * LANES:(j + 1) * LANES, :].T
        post_ref[:, j * LANES:(j + 1) * LANES] = pt[0:8, :].astype(I32)

    @pl.when(pl.program_id(0) == 0)
    def _():
        pendf = pend.astype(F32)[0:1, :]
        rr = lax.broadcasted_iota(I32, (LANES, LANES), 0)
        cc = lax.broadcasted_iota(I32, (LANES, LANES), 1)
        pend_col = jnp.sum(jnp.where(rr == cc, pendf, 0.0), axis=1, keepdims=True)
        nbl = blk_ref.shape[1]
        er = lax.broadcasted_iota(I32, (LANES, nbl), 0)
        jb = (lax.broadcasted_iota(I32, (LANES, nbl), 1) * EXPERT_BLOCK).astype(F32)
        valid = (er >= ROUTE_LANE0) & (er < ROUTE_LANE0 + N_EXPERTS)
        be = jnp.sum(jnp.where(valid & (pend_col <= jb), 1.0, 0.0), axis=0, keepdims=True)
        be = jnp.minimum(be, float(N_EXPERTS - 1))
        nused = jnp.max(pendf, axis=-1, keepdims=True) * (1.0 / EXPERT_BLOCK)
        rowi = lax.broadcasted_iota(I32, blk_ref.shape, 0)
        blk_ref[...] = jnp.where(rowi == 0, be, nused).astype(I32)


def _slots_kernel(pos1_ref, pos2_ref, slot_ref, fill, sem, *, tm):
    i = pl.program_id(0)

    @pl.when(i == 0)
    def _():
        fill[...] = lax.broadcasted_iota(I32, fill.shape, 0) & (EXPERT_BLOCK - 1)
        cp = pltpu.make_async_copy(fill, slot_ref, sem)
        cp.start()
        cp.wait()

    def body(r, carry):
        t = i * tm + r
        slot_ref[pos1_ref[0, 0, r]] = t
        slot_ref[pos2_ref[0, 0, r]] = t
        return carry

    lax.fori_loop(0, tm, body, 0, unroll=16)


def _expert_kernel(be_ref, nu_ref, sl0_ref, sl1_ref, sl2_ref, xn_ref, wup_ref, wdn_ref, y_ref,
                   wupb, wdnb, wup_stage, wdn_stage, xbuf, gsem, wsem, wslot):
    assert GATHER_RING == 3, "the three slot BlockSpecs (blocks j, j+1, j+2) fix the ring depth"
    j = pl.program_id(0)
    cur = lax.rem(j, GATHER_RING)
    n_used = nu_ref[0]
    R = TOKEN_TILE_ROWS
    buf_rows = EXPERT_BLOCK * R

    def gather_row(slot_ref, b, r, prio):
        pltpu.make_async_copy(xn_ref.at[pl.ds(slot_ref[0, 0, r] * R, R), :],
                              xbuf.at[pl.ds(b * buf_rows + r * R, R), :], gsem.at[b]).start(priority=prio)

    def wait_buffer(b):
        pltpu.make_async_copy(xn_ref.at[pl.ds(0, buf_rows), :], xbuf.at[pl.ds(0, buf_rows), :], gsem.at[b]).wait()

    @pl.when(j == 0)
    def _():
        def body(g, carry):
            for u in range(ISSUE_UNROLL):
                gather_row(sl0_ref, 0, g * ISSUE_UNROLL + u, u % 2)
                gather_row(sl1_ref, 1, g * ISSUE_UNROLL + u, (u + 1) % 2)
            return carry

        lax.fori_loop(0, EXPERT_BLOCK // ISSUE_UNROLL, body, 0)

    def weight_copies(e, s):
        return (pltpu.make_async_copy(wup_ref.at[e], wup_stage.at[s], wsem.at[s]),
                pltpu.make_async_copy(wdn_ref.at[e], wdn_stage.at[s], wsem.at[s]))

    @pl.when(j == 0)
    def _():
        wslot[0] = 0
        for cp in weight_copies(be_ref[0], 0):
            cp.start()

    prev = be_ref[jnp.maximum(j - 1, 0)]

    @pl.when((j < n_used) & ((j == 0) | (be_ref[j] != prev)))
    def _():
        s = wslot[0]
        for cp in weight_copies(0, s):
            cp.wait()
        wupb[...] = wup_stage[s].astype(BF16)
        wdnb[...] = wdn_stage[s].astype(BF16)
        e = be_ref[j]
        k = lax.while_loop(lambda k: (k < n_used) & (be_ref[k] == e), lambda k: k + 1, j + 1)

        @pl.when(k < n_used)
        def _():
            for cp in weight_copies(be_ref[k], 1 - s):
                cp.start()

        wslot[0] = 1 - s

    @pl.when(j < n_used)
    def _():
        wait_buffer(cur)
        base = cur * buf_rows
        x = jnp.concatenate([xbuf[pl.ds(base + f, EXPERT_BLOCK, stride=R), :] for f in range(R)], axis=1)
        ab = _dot(x.astype(BF16), wupb[...])
        ahead = lax.rem(j + (GATHER_RING - 1), GATHER_RING)
        for r in range(EXPERT_BLOCK):
            gather_row(sl2_ref, ahead, r, r % 2)
        hmid = _silu(ab[:, :D_EXPERT]) * ab[:, D_EXPERT:]
        y = _dot(hmid.astype(BF16), wdnb[...])
        for f in range(R):
            y_ref[pl.ds(f, EXPERT_BLOCK, stride=R), :] = y[:, f * LANES:(f + 1) * LANES]

    @pl.when(j == n_used)
    def _():
        wait_buffer(cur)
        wait_buffer(lax.rem(j + 1, GATHER_RING))

    @pl.when(j >= n_used)
    def _():
        y_ref[...] = jnp.zeros(y_ref.shape, F32)


def _ple_kernel(p1c_ref, p2c_ref, p1n_ref, p2n_ref, h1_ref, rf_ref, p_ref, ys_ref, gple_ref, wg_ref, wp_ref,
                gfin_ref, out_ref, ybuf, sem):
    i = pl.program_id(0)
    last = pl.num_programs(0) - 1
    cur = i % 2
    nxt = 1 - cur
    tm = h1_ref.shape[0]
    R = TOKEN_TILE_ROWS
    slot_rows = 2 * tm * R

    def gather_row(pos_ref, b, k, r, prio):
        pltpu.make_async_copy(ys_ref.at[pl.ds(pos_ref[0, 0, r] * R, R), :],
                              ybuf.at[pl.ds(b * slot_rows + (k * tm + r) * R, R), :], sem.at[b]).start(priority=prio)

    def wait_slot(b):
        pltpu.make_async_copy(ys_ref.at[pl.ds(0, slot_rows), :], ybuf.at[pl.ds(0, slot_rows), :], sem.at[b]).wait()

    @pl.when(i == 0)
    def _():
        def body(g, carry):
            for u in range(ISSUE_UNROLL):
                gather_row(p1c_ref, 0, 0, g * ISSUE_UNROLL + u, u % 2)
                gather_row(p2c_ref, 0, 1, g * ISSUE_UNROLL + u, (u + 1) % 2)
            return carry

        lax.fori_loop(0, tm // ISSUE_UNROLL, body, 0)

    bursts = 4
    per_burst = tm // bursts

    def start_rows(burst):
        for r in range(burst * per_burst, (burst + 1) * per_burst):
            gather_row(p1n_ref, nxt, 0, r, r % 2)
            gather_row(p2n_ref, nxt, 1, r, (r + 1) % 2)

    pp = _dot(p_ref[...].astype(BF16), wp_ref[...])
    start_rows(0)
    wait_slot(cur)
    rows = lambda k: jnp.concatenate(
        [ybuf[pl.ds(cur * slot_rows + k * tm * R + f, tm, stride=R), :] for f in range(R)], axis=1)
    rf = rf_ref[...]
    h2 = h1_ref[...] + (rf[:, 0:1] * rows(0) + rf[:, 1:2] * rows(1))
    hn = h2 * lax.rsqrt(jnp.mean(h2 * h2, axis=-1, keepdims=True) + EPS) * gple_ref[...]
    start_rows(1)
    gate = jax.nn.sigmoid(_dot(hn.astype(BF16), wg_ref[...]))
    start_rows(2)
    h3 = h2 + gate * pp
    out_ref[...] = h3 * lax.rsqrt(jnp.mean(h3 * h3, axis=-1, keepdims=True) + EPS) * gfin_ref[...]
    start_rows(3)

    @pl.when(i == last)
    def _():
        wait_slot(nxt)


def _params(sem):
    return pltpu.CompilerParams(dimension_semantics=sem, vmem_limit_bytes=VMEM_LIMIT)


def _full(shape):
    nd = len(shape)
    return pl.BlockSpec(shape, lambda *_: (0,) * nd)


def _pad_lanes(a, lane0=0):
    out = jnp.zeros((a.shape[0], LANES), a.dtype)
    return out.at[:, lane0:lane0 + a.shape[1]].set(a)


def _layer(x2, p2, B, T, g_mix, w_in, w_conv_qkv, a_log, dt_bias, g_gdn_out, b_glu, w_conv_dw,
           b_conv_dw, ln_conv_g, ln_conv_b, w_out, g_moe, w_group_router, w_expert_router, w_up,
           w_down, g_ple, w_ple_gate, w_ple_proj, g_out):
    N, D = x2.shape
    hw = GDN_HEADS * GDN_DK
    tm = min(512, T)
    nt = N // tm

    o1 = 3 * hw
    o2 = o1 + hw
    nba = 2 * GDN_HEADS
    wqkv, wz, wbah, wbal, wglu = pl.pallas_call(
        functools.partial(_wsplit_kernel, o1=o1, o2=o2, nba=nba),
        out_shape=[jax.ShapeDtypeStruct((D, o1), BF16), jax.ShapeDtypeStruct((D, hw), BF16),
                   jax.ShapeDtypeStruct((D, LANES), BF16), jax.ShapeDtypeStruct((D, LANES), BF16),
                   jax.ShapeDtypeStruct((D, 2 * CONF_CH), BF16)],
        compiler_params=pltpu.CompilerParams(vmem_limit_bytes=VMEM_LIMIT),
        name="wsplit",
    )(jnp.swapaxes(w_in, 0, 1))
    pa = jnp.concatenate([_pad_lanes(a_log[None, :], GDN_HEADS), _pad_lanes(dt_bias[None, :], GDN_HEADS),
                          jnp.zeros((6, LANES), F32)], axis=0)
    cdw = jnp.concatenate([w_conv_dw, jnp.zeros((1, CONF_CH), F32)], axis=0)
    cvec = jnp.concatenate([b_conv_dw[None], ln_conv_g[None], ln_conv_b[None], jnp.zeros((5, CONF_CH), F32)], 0)

    row_spec = lambda w: pl.BlockSpec((tm, w), lambda i, *_: (i, 0))
    q, k, v, bg, zs, yc = pl.pallas_call(
        functools.partial(_inproj_kernel, tiles_per_seq=T // tm, tm=tm),
        grid=(nt, 2),
        in_specs=[row_spec(D), _full((1, D)), _full(wqkv.shape), _full(wz.shape), _full(wbah.shape),
                  _full(wbal.shape), _full(wglu.shape), _full(w_conv_qkv.shape), _full(pa.shape),
                  _full((1, 2 * CONF_CH)), _full(cdw.shape), _full(cvec.shape)],
        out_specs=[row_spec(hw), row_spec(hw), row_spec(hw), row_spec(LANES), row_spec(hw), row_spec(CONF_CH)],
        out_shape=[jax.ShapeDtypeStruct((N, hw), F32)] * 3 + [jax.ShapeDtypeStruct((N, LANES), F32),
                   jax.ShapeDtypeStruct((N, hw), BF16), jax.ShapeDtypeStruct((N, CONF_CH), BF16)],
        scratch_shapes=[pltpu.VMEM((tm + QKV_HALO, 3 * hw), F32), pltpu.VMEM((tm + CONF_HALO, CONF_CH), F32),
                        pltpu.VMEM((tm, D), BF16), pltpu.VMEM((tm, D), BF16)],
        compiler_params=_params(("arbitrary", "arbitrary")),
        name="inproj",
    )(x2, g_mix[None], wqkv, wz, wbah, wbal, wglu, w_conv_qkv, pa, b_glu[None], cdw, cvec)

    cb = 8
    rows = cb * GDN_CHUNK
    seq_spec = lambda w: pl.BlockSpec((B, rows, w), lambda n: (0, n, 0))
    r3 = lambda a: a.reshape(B, T, a.shape[-1])
    ygdn = pl.pallas_call(
        functools.partial(_gdn_kernel, cb=cb, nbatch=B, group=8 * GDN_HEADS),
        grid=(T // rows,),
        in_specs=[seq_spec(hw), seq_spec(hw), seq_spec(hw), seq_spec(LANES), seq_spec(hw),
                  pl.BlockSpec((1, GDN_DK), lambda n: (0, 0))],
        out_specs=seq_spec(hw),
        out_shape=jax.ShapeDtypeStruct((B, T, hw), BF16),
        scratch_shapes=[pltpu.VMEM((B * GDN_HEADS, GDN_DK, GDN_DK), F32)],
        compiler_params=_params(("arbitrary",)),
        name="gdn",
    )(r3(q), r3(k), r3(v), r3(bg), r3(zs), g_gdn_out[None]).reshape(N, hw)

    wo1 = w_out[:hw].astype(BF16)
    wo2 = w_out[hw:].astype(BF16)
    wrh, wrl = _split_bf16(_pad_lanes(jnp.concatenate([w_group_router, w_expert_router], axis=1)))
    nsub = 2 if tm % 256 == 0 else 1
    ltri = jnp.tril(jnp.ones((tm // nsub, tm // nsub), F32), -1).astype(BF16)
    assert D == TOKEN_TILE_ROWS * LANES
    tile_spec = pl.BlockSpec((tm * TOKEN_TILE_ROWS, LANES), lambda i: (i, 0))
    h1, xn, ri, rf, cnt = pl.pallas_call(
        functools.partial(_router_kernel, nsub=nsub),
        grid=(nt,),
        in_specs=[row_spec(D), row_spec(hw), row_spec(CONF_CH), _full(wo1.shape), _full(wo2.shape),
                  _full((1, D)), _full(wrh.shape), _full(wrl.shape), _full(ltri.shape)],
        out_specs=[row_spec(D), tile_spec, row_spec(LANES), row_spec(LANES), _full((8, LANES))],
        out_shape=[jax.ShapeDtypeStruct((N, D), F32), jax.ShapeDtypeStruct((N * TOKEN_TILE_ROWS, LANES), F32),
                   jax.ShapeDtypeStruct((N, LANES), I32), jax.ShapeDtypeStruct((N, LANES), F32),
                   jax.ShapeDtypeStruct((8, LANES), F32)],
        scratch_shapes=[pltpu.VMEM((1, LANES), F32)],
        compiler_params=_params(("arbitrary",)),
        name="router",
    )(x2, ygdn, yc, wo1, wo2, g_moe[None], wrh, wrl, ltri)

    nb = (2 * N) // EXPERT_BLOCK + N_EXPERTS
    nbl = -(-nb // LANES) * LANES
    tp = min(4 * tm, N)
    post, blk = pl.pallas_call(
        functools.partial(_pos_kernel, tm=tp),
        grid=(N // tp,),
        in_specs=[pl.BlockSpec((tp, LANES), lambda i: (i, 0)), _full((8, LANES))],
        out_specs=[pl.BlockSpec((8, tp), lambda i: (0, i)), _full((8, nbl))],
        out_shape=[jax.ShapeDtypeStruct((8, N), I32), jax.ShapeDtypeStruct((8, nbl), I32)],
        compiler_params=_params(("arbitrary",)),
        name="positions",
    )(ri, cnt)
    pos1 = post[0].reshape(nt, 1, tm)
    pos2 = post[1].reshape(nt, 1, tm)
    block_expert = blk[0, :nb]
    n_used = blk[1, :1]

    P = nb * EXPERT_BLOCK
    smem_spec = pl.BlockSpec((1, 1, tp), lambda i: (i, 0, 0), memory_space=pltpu.SMEM)
    flat_smem = pl.BlockSpec((P,), lambda i: (0,), memory_space=pltpu.SMEM)
    slots = pl.pallas_call(
        functools.partial(_slots_kernel, tm=tp),
        grid=(N // tp,),
        in_specs=[smem_spec, smem_spec],
        out_specs=flat_smem,
        out_shape=jax.ShapeDtypeStruct((P,), I32),
        scratch_shapes=[pltpu.VMEM((P,), I32), pltpu.SemaphoreType.DMA],
        compiler_params=_params(("arbitrary",)),
        name="slots",
    )(post[0].reshape(N // tp, 1, tp), post[1].reshape(N // tp, 1, tp)).reshape(nb, 1, EXPERT_BLOCK)

    any_spec = pl.BlockSpec(memory_space=pl.ANY)
    blk_smem = lambda off: pl.BlockSpec((1, 1, EXPERT_BLOCK),
                                        lambda j, be, nu: (jnp.minimum(j + off, nb - 1), 0, 0),
                                        memory_space=pltpu.SMEM)
    ys = pl.pallas_call(
        _expert_kernel,
        grid_spec=pltpu.PrefetchScalarGridSpec(
            num_scalar_prefetch=2,
            grid=(nb,),
            in_specs=[blk_smem(0), blk_smem(1), blk_smem(2), any_spec, any_spec, any_spec],
            out_specs=pl.BlockSpec((EXPERT_BLOCK * TOKEN_TILE_ROWS, LANES), lambda j, be, nu: (j, 0)),
            scratch_shapes=[pltpu.VMEM((D, 2 * D_EXPERT), BF16), pltpu.VMEM((D_EXPERT, D), BF16),
                            pltpu.VMEM((2, D, 2 * D_EXPERT), F32), pltpu.VMEM((2, D_EXPERT, D), F32),
                            pltpu.VMEM((GATHER_RING * EXPERT_BLOCK * TOKEN_TILE_ROWS, LANES), F32),
                            pltpu.SemaphoreType.DMA((GATHER_RING,)), pltpu.SemaphoreType.DMA((2,)),
                            pltpu.SMEM((1,), I32)]),
        out_shape=jax.ShapeDtypeStruct((P * TOKEN_TILE_ROWS, LANES), F32),
        compiler_params=_params(("arbitrary",)),
        name="experts",
    )(block_expert, n_used, slots, slots, slots, xn, w_up, w_down)

    wg = w_ple_gate.astype(BF16)
    wp = w_ple_proj.astype(BF16)
    pos_spec = lambda off: pl.BlockSpec((1, 1, tm), lambda i: (jnp.minimum(i + off, nt - 1), 0, 0),
                                        memory_space=pltpu.SMEM)
    out = pl.pallas_call(
        _ple_kernel,
        grid=(nt,),
        in_specs=[pos_spec(0), pos_spec(0), pos_spec(1), pos_spec(1),
                  row_spec(D), row_spec(LANES), row_spec(p2.shape[1]), any_spec,
                  _full((1, D)), _full(wg.shape), _full(wp.shape), _full((1, D))],
        out_specs=row_spec(D),
        out_shape=jax.ShapeDtypeStruct((N, D), F32),
        scratch_shapes=[pltpu.VMEM((2 * 2 * tm * TOKEN_TILE_ROWS, LANES), F32), pltpu.SemaphoreType.DMA((2,))],
        compiler_params=_params(("arbitrary",)),
        name="ple",
    )(pos1, pos2, pos1, pos2, h1, rf, p2, ys, g_ple[None], wg, wp, g_out[None])
    return out


def kernel(x, p, g_mix, w_in, w_conv_qkv, a_log, dt_bias, g_gdn_out, b_glu, w_conv_dw, b_conv_dw, ln_conv_g, ln_conv_b, w_out, g_moe, w_group_router, w_expert_router, w_up, w_down, g_ple, w_ple_gate, w_ple_proj, g_final):
    B, T, D = x.shape
    depth = p.shape[0]
    assert depth == 1, "the final RMSNorm is fused into the (single) layer's last kernel"
    h = _layer(x.reshape(B * T, D), p[0].reshape(B * T, -1), B, T, g_mix[0], w_in[0], w_conv_qkv[0],
               a_log[0], dt_bias[0], g_gdn_out[0], b_glu[0], w_conv_dw[0], b_conv_dw[0], ln_conv_g[0],
               ln_conv_b[0], w_out[0], g_moe[0], w_group_router[0], w_expert_router[0], w_up[0],
               w_down[0], g_ple[0], w_ple_gate[0], w_ple_proj[0], g_final)
    return h.reshape(B, T, D)
```

```python
import functools

import jax
import jax.numpy as jnp
from jax import lax
from jax.experimental import pallas as pl
from jax.experimental.pallas import tpu as pltpu

F32 = jnp.float32
BF16 = jnp.bfloat16
I32 = jnp.int32

EPS = 1e-6
GDN_HEADS = 4
GDN_DK = 128
GDN_CONV = 4
GDN_CHUNK = 64
CONF_CH = 512
CONF_WIDTH = 31
N_GROUPS = 8
EXPERTS_PER_GROUP = 8
N_EXPERTS = 64
D_EXPERT = 256
LANES = 128
SUBLANES = 8
TOKEN_TILE_ROWS = 8
ROUTE_LANE0 = N_GROUPS
EXPERT_BLOCK = 256
GATHER_RING = 3
ISSUE_UNROLL = 8
QKV_HALO = 8
CONF_HALO = 32
NEG = -1e30
VMEM_LIMIT = 56 * 1024 * 1024


def _dot(a, b):
    return jnp.dot(a, b, preferred_element_type=F32)


def _dot_nt(a, b):
    return lax.dot_general(a, b, (((1,), (1,)), ((), ())), preferred_element_type=F32)


def _dot_tn(a, b):
    return lax.dot_general(a, b, (((0,), (0,)), ((), ())), preferred_element_type=F32)


def _silu(x):
    return x * jax.nn.sigmoid(x)


def _split_bf16(x):
    hi = x.astype(BF16)
    lo = (x - hi.astype(F32)).astype(BF16)
    return hi, lo


def _causal_taps(ext_ref, w_ref, n_taps, halo, tm, cols):
    rows = ext_ref.shape[0]
    ext = ext_ref[:, cols]
    first = halo - (n_taps - 1)
    acc = None
    for res in range(SUBLANES):
        taps = [k for k in range(n_taps) if (first + k) % SUBLANES == res]
        if not taps:
            continue
        shifted = ext if res == 0 else pltpu.roll(ext, rows - res, 0)
        for k in taps:
            a = first + k - res
            term = w_ref[k:k + 1, cols] * shifted[a:a + tm, :]
            acc = term if acc is None else acc + term
    return acc


def _wsplit_kernel(wt_ref, wqkv_ref, wz_ref, wbah_ref, wbal_ref, wglu_ref, *, o1, o2, nba):
    d = wt_ref.shape[1]

    def cols(start, out_ref):
        for c in range(out_ref.shape[1] // LANES):
            for r in range(d // LANES):
                blk = wt_ref[start + c * LANES:start + (c + 1) * LANES, r * LANES:(r + 1) * LANES]
                out_ref[r * LANES:(r + 1) * LANES, c * LANES:(c + 1) * LANES] = blk.T.astype(BF16)

    cols(0, wqkv_ref)
    cols(o1, wz_ref)
    cols(o2 + nba, wglu_ref)
    lane = lax.broadcasted_iota(I32, (LANES, LANES), 1)
    for r in range(d // LANES):
        ba = jnp.where(lane < nba, wt_ref[o2:o2 + LANES, r * LANES:(r + 1) * LANES].T, 0.0)
        hi, lo = _split_bf16(ba)
        wbah_ref[r * LANES:(r + 1) * LANES, :] = hi
        wbal_ref[r * LANES:(r + 1) * LANES, :] = lo


def _inproj_kernel(x_ref, gmix_ref, wqkv_ref, wz_ref, wbah_ref, wbal_ref, wglu_ref, cw_ref, pa_ref,
                   bglu_ref, cdw_ref, cvec_ref,
                   q_ref, k_ref, v_ref, bg_ref, zs_ref, yc_ref,
                   extq, extu, *, tiles_per_seq, tm):
    i = pl.program_id(0)
    first = (i % tiles_per_seq) == 0

    x = x_ref[...]
    n1 = x * lax.rsqrt(jnp.mean(x * x, axis=-1, keepdims=True) + EPS) * gmix_ref[...]
    n1h, n1l = _split_bf16(n1)

    @pl.when(first)
    def _():
        extq[0:QKV_HALO, :] = jnp.zeros((QKV_HALO, extq.shape[1]), F32)
        extu[0:CONF_HALO, :] = jnp.zeros((CONF_HALO, extu.shape[1]), F32)

    @pl.when(jnp.logical_not(first))
    def _():
        extq[0:QKV_HALO, :] = extq[tm:tm + QKV_HALO, :]
        extu[0:CONF_HALO, :] = extu[tm:tm + CONF_HALO, :]

    extq[QKV_HALO:QKV_HALO + tm, :] = _dot(n1h, wqkv_ref[...])
    glu = _dot(n1h, wglu_ref[...]) + bglu_ref[...]
    extu[CONF_HALO:CONF_HALO + tm, :] = glu[:, :CONF_CH] * jax.nn.sigmoid(glu[:, CONF_CH:])
    zs_ref[...] = _silu(_dot(n1h, wz_ref[...])).astype(BF16)
    hw = GDN_HEADS * GDN_DK
    qc = _silu(_causal_taps(extq, cw_ref, GDN_CONV, QKV_HALO, tm, slice(0, hw)))
    kc = _silu(_causal_taps(extq, cw_ref, GDN_CONV, QKV_HALO, tm, slice(hw, 2 * hw)))
    for h in range(GDN_HEADS):
        cs = slice(h * GDN_DK, (h + 1) * GDN_DK)
        qh = qc[:, cs]
        q_ref[:, cs] = qh * lax.rsqrt(jnp.sum(qh * qh, axis=-1, keepdims=True) + EPS) * (GDN_DK ** -0.5)
        kh = kc[:, cs]
        k_ref[:, cs] = kh * lax.rsqrt(jnp.sum(kh * kh, axis=-1, keepdims=True) + EPS)
    v_ref[...] = _silu(_causal_taps(extq, cw_ref, GDN_CONV, QKV_HALO, tm, slice(2 * hw, 3 * hw)))

    ba = _dot(n1h, wbah_ref[...]) + _dot(n1l, wbah_ref[...]) + _dot(n1h, wbal_ref[...])
    lane = lax.broadcasted_iota(I32, ba.shape, 1)
    row = lax.broadcasted_iota(I32, ba.shape, 0)
    beta = jax.nn.sigmoid(ba)
    sp_in = ba + pa_ref[1:2, :]
    softplus = jnp.maximum(sp_in, 0.0) + jnp.log1p(jnp.exp(-jnp.abs(sp_in)))
    g = -jnp.exp(pa_ref[0:1, :]) * softplus
    rin = row & (GDN_CHUNK - 1)
    s = 1
    while s < GDN_CHUNK:
        g = g + jnp.where(rin >= s, pltpu.roll(g, s, 0), 0.0)
        s *= 2
    bg_ref[...] = jnp.where(lane < GDN_HEADS, beta, g)

    c = _causal_taps(extu, cdw_ref, CONF_WIDTH, CONF_HALO, tm, slice(0, CONF_CH)) + cvec_ref[0:1, :]
    mu = jnp.mean(c, axis=-1, keepdims=True)
    cc = c - mu
    var = jnp.mean(cc * cc, axis=-1, keepdims=True)
    yln = cc * lax.rsqrt(var + EPS) * cvec_ref[1:2, :] + cvec_ref[2:3, :]
    yc_ref[...] = _silu(yln).astype(BF16)


def _gdn_kernel(q_ref, k_ref, v_ref, bg_ref, zs_ref, gout_ref, y_ref, s_ref, *, cb, nbatch, group):
    n = pl.program_id(0)

    @pl.when(n == 0)
    def _():
        s_ref[...] = jnp.zeros(s_ref.shape, F32)

    C = GDN_CHUNK
    r = lax.broadcasted_iota(I32, (C, C), 0)
    c = lax.broadcasted_iota(I32, (C, C), 1)
    causal = r >= c
    strict = r > c
    eye = r == c
    eyef = jnp.where(eye, 1.0, 0.0).astype(F32)
    gout = gout_ref[...]

    def front(b, ci, h):
        rows = slice(ci * C, (ci + 1) * C)
        cols = slice(h * GDN_DK, (h + 1) * GDN_DK)
        q = q_ref[b, rows, cols]
        k = k_ref[b, rows, cols]
        v = v_ref[b, rows, cols]
        beta = bg_ref[b, rows, h:h + 1]
        gc = bg_ref[b, rows, GDN_HEADS + h:GDN_HEADS + h + 1]
        gc_row = jnp.sum(jnp.where(eye, gc, 0.0), axis=0, keepdims=True)
        gc_last = gc[C - 1:C, :]
        eg = jnp.exp(gc)
        kb = k * beta
        return dict(
            decay=jnp.where(causal, jnp.exp(jnp.minimum(gc - gc_row, 0.0)), 0.0),
            lhs=jnp.concatenate([kb, q], axis=0).astype(BF16),
            kbf=k.astype(BF16),
            rhs=jnp.concatenate([v * beta, kb * eg], axis=1).astype(BF16),
            qd=q * eg,
            kd=(k * jnp.exp(gc_last - gc)).astype(BF16),
            gl=jnp.exp(gc_last))

    insts = [(b, ci, h) for ci in range(cb) for b in range(nbatch) for h in range(GDN_HEADS)]
    done = {}
    for g0 in range(0, len(insts), group):
        grp = insts[g0:g0 + group]
        st = [front(*key) for key in grp]
        kq = [_dot_nt(s["lhs"], s["kbf"]) for s in st]
        xp = [-jnp.where(strict, m[:C] * s["decay"], 0.0) for m, s in zip(kq, st)]
        qk = [(m[C:] * s["decay"]).astype(BF16) for m, s in zip(kq, st)]
        tinv = [eyef + x for x in xp]
        for _ in range(5):
            xpb = [x.astype(BF16) for x in xp]
            xp = [_dot(x, x) for x in xpb]
            tinv = [t + _dot(t.astype(BF16), x.astype(BF16)) for t, x in zip(tinv, xp)]
        uw = [_dot(t.astype(BF16), s["rhs"]) for t, s in zip(tinv, st)]
        for key, s, m, qkm in zip(grp, st, uw, qk):
            done[key] = dict(u=m[:, :GDN_DK],
                             wq=jnp.concatenate([m[:, GDN_DK:], s["qd"]], axis=0).astype(BF16),
                             qk=qkm, kd=s["kd"], gl=s["gl"])

    chains = [(b, h) for b in range(nbatch) for h in range(GDN_HEADS)]
    for ci in range(cb):
        rows = slice(ci * C, (ci + 1) * C)
        dd = [done[(b, ci, h)] for b, h in chains]
        s_old = [s_ref[b * GDN_HEADS + h] for b, h in chains]
        wq = [_dot(d["wq"], s.astype(BF16)) for d, s in zip(dd, s_old)]
        v_new = [(d["u"] - m[:C]).astype(BF16) for d, m in zip(dd, wq)]
        o = [m[C:] + _dot(d["qk"], vn) for d, m, vn in zip(dd, wq, v_new)]
        s_new = [s * d["gl"] + _dot_tn(d["kd"], vn) for d, s, vn in zip(dd, s_old, v_new)]
        for (b, h), sn, oo in zip(chains, s_new, o):
            cols = slice(h * GDN_DK, (h + 1) * GDN_DK)
            s_ref[b * GDN_HEADS + h] = sn
            on = oo * lax.rsqrt(jnp.mean(oo * oo, axis=-1, keepdims=True) + EPS) * gout
            y_ref[b, rows, cols] = (on * zs_ref[b, rows, cols].astype(F32)).astype(BF16)


def _router_kernel(x_ref, yg_ref, yc_ref, wo1_ref, wo2_ref, gmoe_ref, wrh_ref, wrl_ref, ltri_ref,
                   h1_ref, xn_ref, ri_ref, rf_ref, cnt_ref, carry, *, nsub):
    i = pl.program_id(0)

    @pl.when(i == 0)
    def _():
        carry[...] = jnp.zeros(carry.shape, F32)

    ts = x_ref.shape[0] // nsub
    subs = [slice(s * ts, (s + 1) * ts) for s in range(nsub)]
    each = lambda fn, *lists: [fn(*args) for args in zip(*lists)]
    lane_sum = lambda v: jnp.sum(v, axis=-1, keepdims=True)
    lane_max = lambda v: jnp.max(v, axis=-1, keepdims=True)
    lane_min = lambda v: jnp.min(v, axis=-1, keepdims=True)

    h1 = [x_ref[sl, :] + _dot(yg_ref[sl, :], wo1_ref[...]) + _dot(yc_ref[sl, :], wo2_ref[...]) for sl in subs]
    for sl, h in zip(subs, h1):
        h1_ref[sl, :] = h
    xn = each(lambda h: h * lax.rsqrt(jnp.mean(h * h, axis=-1, keepdims=True) + EPS) * gmoe_ref[...], h1)
    parts = each(_split_bf16, xn)
    for s, v in enumerate(xn):
        for f in range(TOKEN_TILE_ROWS):
            xn_ref[pl.ds(s * ts * TOKEN_TILE_ROWS + f, ts, stride=TOKEN_TILE_ROWS), :] = v[:, f * LANES:(f + 1) * LANES]
    lg = [_dot(hi, wrh_ref[...]) + _dot(lo, wrh_ref[...]) + _dot(hi, wrl_ref[...]) for hi, lo in parts]

    lane = lax.broadcasted_iota(I32, lg[0].shape, 1)
    lanef = lane.astype(F32)
    none = float(LANES)
    gl = each(lambda v: jnp.where(lane < N_GROUPS, v, NEG), lg)
    gmax = each(lane_max, gl)
    g_w = each(lambda v, m: 1.0 / lane_sum(jnp.exp(v - m)), gl, gmax)
    gidx = each(lambda v, m: lane_min(jnp.where(v == m, lanef, none)), gl, gmax)
    lane_group = ((lane - ROUTE_LANE0) >> (EXPERTS_PER_GROUP.bit_length() - 1)).astype(F32)
    is_expert = (lane >= ROUTE_LANE0) & (lane < ROUTE_LANE0 + N_EXPERTS)
    el = each(lambda v, g: jnp.where(is_expert & (lane_group == g), v, NEG), lg, gidx)
    m1 = each(lane_max, el)
    i1 = each(lambda v, m: lane_min(jnp.where(v == m, lanef, none)), el, m1)
    el2 = each(lambda v, a: jnp.where(lanef == a, NEG, v), el, i1)
    m2 = each(lane_max, el2)
    i2 = each(lambda v, m: lane_min(jnp.where(v == m, lanef, none)), el2, m2)
    ratio = each(lambda a, b: jnp.exp(b - a), m1, m2)
    gate1 = each(lambda w, r: w / (1.0 + r), g_w, ratio)
    gate2 = each(lambda w, r: w * r / (1.0 + r), g_w, ratio)

    oh1 = each(lambda a: lanef == a, i1)
    oh2 = each(lambda a: lanef == a, i2)
    oh = each(lambda a, b: jnp.where(a | b, 1.0, 0.0).astype(F32), oh1, oh2)
    within = each(lambda v: _dot(ltri_ref[...], v.astype(BF16)), oh)
    base = carry[...]
    for s in range(nsub):
        cum = within[s] + base
        rank1 = lane_sum(jnp.where(oh1[s], cum, 0.0))
        rank2 = lane_sum(jnp.where(oh2[s], cum, 0.0))
        routes = jnp.where(lane == 0, i1[s] - ROUTE_LANE0,
                           jnp.where(lane == 1, i2[s] - ROUTE_LANE0,
                                     jnp.where(lane == 2, rank1, jnp.where(lane == 3, rank2, 0.0))))
        ri_ref[subs[s], :] = routes.astype(I32)
        rf_ref[subs[s], :] = jnp.where(lane == 0, gate1[s], jnp.where(lane == 1, gate2[s], 0.0))
        base = base + jnp.sum(oh[s], axis=0, keepdims=True)
    carry[...] = base
    cnt_ref[...] = jnp.broadcast_to(base, cnt_ref.shape)


def _pos_kernel(ri_ref, cnt_ref, post_ref, blk_ref, *, tm):
    shift = EXPERT_BLOCK.bit_length() - 1
    cnt = cnt_ref[...].astype(I32)
    padded = ((cnt + (EXPERT_BLOCK - 1)) >> shift) << shift
    lane8 = lax.broadcasted_iota(I32, padded.shape, 1)
    pend = padded
    s = 1
    while s < LANES:
        pend = pend + jnp.where(lane8 >= s, pltpu.roll(pend, s, 1), 0)
        s *= 2
    pstart = (pend - padded)[0:1, :].astype(F32)

    ri = ri_ref[...]
    lane = lax.broadcasted_iota(I32, ri.shape, 1)
    e1 = ri[:, 0:1] + ROUTE_LANE0
    e2 = ri[:, 1:2] + ROUTE_LANE0
    ps1 = jnp.sum(jnp.where(lane == e1, pstart, 0.0), axis=-1, keepdims=True)
    ps2 = jnp.sum(jnp.where(lane == e2, pstart, 0.0), axis=-1, keepdims=True)
    pos1 = ps1 + ri[:, 2:3].astype(F32)
    pos2 = ps2 + ri[:, 3:4].astype(F32)
    posmat = jnp.where(lane == 0, pos1, jnp.where(lane == 1, pos2, 0.0))
    for j in range(tm // LANES):
        pt = posmat[j * LANES:(j + 1) * LANES, :].T
        post_ref[:, j * LANES:(j + 1) * LANES] = pt[0:8, :].astype(I32)

    @pl.when(pl.program_id(0) == 0)
    def _():
        pendf = pend.astype(F32)[0:1, :]
        rr = lax.broadcasted_iota(I32, (LANES, LANES), 0)
        cc = lax.broadcasted_iota(I32, (LANES, LANES), 1)
        pend_col = jnp.sum(jnp.where(rr == cc, pendf, 0.0), axis=1, keepdims=True)
        nbl = blk_ref.shape[1]
        er = lax.broadcasted_iota(I32, (LANES, nbl), 0)
        jb = (lax.broadcasted_iota(I32, (LANES, nbl), 1) * EXPERT_BLOCK).astype(F32)
        valid = (er >= ROUTE_LANE0) & (er < ROUTE_LANE0 + N_EXPERTS)
        be = jnp.sum(jnp.where(valid & (pend_col <= jb), 1.0, 0.0), axis=0, keepdims=True)
        be = jnp.minimum(be, float(N_EXPERTS - 1))
        nused = jnp.max(pendf, axis=-1, keepdims=True) * (1.0 / EXPERT_BLOCK)
        rowi = lax.broadcasted_iota(I32, blk_ref.shape, 0)
        blk_ref[...] = jnp.where(rowi == 0, be, nused).astype(I32)


def _slots_kernel(pos1_ref, pos2_ref, slot_ref, fill, sem, *, tm):
    i = pl.program_id(0)

    @pl.when(i == 0)
    def _():
        fill[...] = lax.broadcasted_iota(I32, fill.shape, 0) & (EXPERT_BLOCK - 1)
        cp = pltpu.make_async_copy(fill, slot_ref, sem)
        cp.start()
        cp.wait()

    def body(r, carry):
        t = i * tm + r
        slot_ref[pos1_ref[0, 0, r]] = t
        slot_ref[pos2_ref[0, 0, r]] = t
        return carry

    lax.fori_loop(0, tm, body, 0, unroll=16)


def _expert_kernel(be_ref, nu_ref, sl0_ref, sl1_ref, sl2_ref, xn_ref, wup_ref, wdn_ref, y_ref,
                   wupb, wdnb, wup_stage, wdn_stage, xbuf, gsem, wsem, wslot):
    assert GATHER_RING == 3, "the three slot BlockSpecs (blocks j, j+1, j+2) fix the ring depth"
    j = pl.program_id(0)
    cur = lax.rem(j, GATHER_RING)
    n_used = nu_ref[0]
    R = TOKEN_TILE_ROWS
    buf_rows = EXPERT_BLOCK * R

    def gather_row(slot_ref, b, r, prio):
        pltpu.make_async_copy(xn_ref.at[pl.ds(slot_ref[0, 0, r] * R, R), :],
                              xbuf.at[pl.ds(b * buf_rows + r * R, R), :], gsem.at[b]).start(priority=prio)

    def wait_buffer(b):
        pltpu.make_async_copy(xn_ref.at[pl.ds(0, buf_rows), :], xbuf.at[pl.ds(0, buf_rows), :], gsem.at[b]).wait()

    @pl.when(j == 0)
    def _():
        def body(g, carry):
            for u in range(ISSUE_UNROLL):
                gather_row(sl0_ref, 0, g * ISSUE_UNROLL + u, u % 2)
                gather_row(sl1_ref, 1, g * ISSUE_UNROLL + u, (u + 1) % 2)
            return carry

        lax.fori_loop(0, EXPERT_BLOCK // ISSUE_UNROLL, body, 0)

    def weight_copies(e, s):
        return (pltpu.make_async_copy(wup_ref.at[e], wup_stage.at[s], wsem.at[s]),
                pltpu.make_async_copy(wdn_ref.at[e], wdn_stage.at[s], wsem.at[s]))

    @pl.when(j == 0)
    def _():
        wslot[0] = 0
        for cp in weight_copies(be_ref[0], 0):
            cp.start()

    prev = be_ref[jnp.maximum(j - 1, 0)]

    @pl.when((j < n_used) & ((j == 0) | (be_ref[j] != prev)))
    def _():
        s = wslot[0]
        for cp in weight_copies(0, s):
            cp.wait()
        wupb[...] = wup_stage[s].astype(BF16)
        wdnb[...] = wdn_stage[s].astype(BF16)
        e = be_ref[j]
        k = lax.while_loop(lambda k: (k < n_used) & (be_ref[k] == e), lambda k: k + 1, j + 1)

        @pl.when(k < n_used)
        def _():
            for cp in weight_copies(be_ref[k], 1 - s):
                cp.start()

        wslot[0] = 1 - s

    @pl.when(j < n_used)
    def _():
        wait_buffer(cur)
        base = cur * buf_rows
        x = jnp.concatenate([xbuf[pl.ds(base + f, EXPERT_BLOCK, stride=R), :] for f in range(R)], axis=1)
        ab = _dot(x.astype(BF16), wupb[...])
        ahead = lax.rem(j + (GATHER_RING - 1), GATHER_RING)
        for r in range(EXPERT_BLOCK):
            gather_row(sl2_ref, ahead, r, r % 2)
        hmid = _silu(ab[:, :D_EXPERT]) * ab[:, D_EXPERT:]
        y = _dot(hmid.astype(BF16), wdnb[...])
        for f in range(R):
            y_ref[pl.ds(f, EXPERT_BLOCK, stride=R), :] = y[:, f * LANES:(f + 1) * LANES]

    @pl.when(j == n_used)
    def _():
        wait_buffer(cur)
        wait_buffer(lax.rem(j + 1, GATHER_RING))

    @pl.when(j >= n_used)
    def _():
        y_ref[...] = jnp.zeros(y_ref.shape, F32)


def _ple_kernel(p1c_ref, p2c_ref, p1n_ref, p2n_ref, h1_ref, rf_ref, p_ref, ys_ref, gple_ref, wg_ref, wp_ref,
                gfin_ref, out_ref, ybuf, sem):
    i = pl.program_id(0)
    last = pl.num_programs(0) - 1
    cur = i % 2
    nxt = 1 - cur
    tm = h1_ref.shape[0]
    R = TOKEN_TILE_ROWS
    slot_rows = 2 * tm * R

    def gather_row(pos_ref, b, k, r, prio):
        pltpu.make_async_copy(ys_ref.at[pl.ds(pos_ref[0, 0, r] * R, R), :],
                              ybuf.at[pl.ds(b * slot_rows + (k * tm + r) * R, R), :], sem.at[b]).start(priority=prio)

    def wait_slot(b):
        pltpu.make_async_copy(ys_ref.at[pl.ds(0, slot_rows), :], ybuf.at[pl.ds(0, slot_rows), :], sem.at[b]).wait()

    @pl.when(i == 0)
    def _():
        def body(g, carry):
            for u in range(ISSUE_UNROLL):
                gather_row(p1c_ref, 0, 0, g * ISSUE_UNROLL + u, u % 2)
                gather_row(p2c_ref, 0, 1, g * ISSUE_UNROLL + u, (u + 1) % 2)
            return carry

        lax.fori_loop(0, tm // ISSUE_UNROLL, body, 0)

    bursts = 4
    per_burst = tm // bursts

    def start_rows(burst):
        for r in range(burst * per_burst, (burst + 1) * per_burst):
            gather_row(p1n_ref, nxt, 0, r, r % 2)
            gather_row(p2n_ref, nxt, 1, r, (r + 1) % 2)

    pp = _dot(p_ref[...].astype(BF16), wp_ref[...])
    start_rows(0)
    wait_slot(cur)
    rows = lambda k: jnp.concatenate(
        [ybuf[pl.ds(cur * slot_rows + k * tm * R + f, tm, stride=R), :] for f in range(R)], axis=1)
    rf = rf_ref[...]
    h2 = h1_ref[...] + (rf[:, 0:1] * rows(0) + rf[:, 1:2] * rows(1))
    hn = h2 * lax.rsqrt(jnp.mean(h2 * h2, axis=-1, keepdims=True) + EPS) * gple_ref[...]
    start_rows(1)
    gate = jax.nn.sigmoid(_dot(hn.astype(BF16), wg_ref[...]))
    start_rows(2)
    h3 = h2 + gate * pp
    out_ref[...] = h3 * lax.rsqrt(jnp.mean(h3 * h3, axis=-1, keepdims=True) + EPS) * gfin_ref[...]
    start_rows(3)

    @pl.when(i == last)
    def _():
        wait_slot(nxt)


def _params(sem):
    return pltpu.CompilerParams(dimension_semantics=sem, vmem_limit_bytes=VMEM_LIMIT)


def _full(shape):
    nd = len(shape)
    return pl.BlockSpec(shape, lambda *_: (0,) * nd)


def _pad_lanes(a, lane0=0):
    out = jnp.zeros((a.shape[0], LANES), a.dtype)
    return out.at[:, lane0:lane0 + a.shape[1]].set(a)


def _layer(x2, p2, B, T, g_mix, w_in, w_conv_qkv, a_log, dt_bias, g_gdn_out, b_glu, w_conv_dw,
           b_conv_dw, ln_conv_g, ln_conv_b, w_out, g_moe, w_group_router, w_expert_router, w_up,
           w_down, g_ple, w_ple_gate, w_ple_proj, g_out):
    N, D = x2.shape
    hw = GDN_HEADS * GDN_DK
    tm = min(512, T)
    nt = N // tm

    o1 = 3 * hw
    o2 = o1 + hw
    nba = 2 * GDN_HEADS
    wqkv, wz, wbah, wbal, wglu = pl.pallas_call(
        functools.partial(_wsplit_kernel, o1=o1, o2=o2, nba=nba),
        out_shape=[jax.ShapeDtypeStruct((D, o1), BF16), jax.ShapeDtypeStruct((D, hw), BF16),
                   jax.ShapeDtypeStruct((D, LANES), BF16), jax.ShapeDtypeStruct((D, LANES), BF16),
                   jax.ShapeDtypeStruct((D, 2 * CONF_CH), BF16)],
        compiler_params=pltpu.CompilerParams(vmem_limit_bytes=VMEM_LIMIT),
        name="wsplit",
    )(jnp.swapaxes(w_in, 0, 1))
    pa = jnp.concatenate([_pad_lanes(a_log[None, :], GDN_HEADS), _pad_lanes(dt_bias[None, :], GDN_HEADS),
                          jnp.zeros((6, LANES), F32)], axis=0)
    cdw = jnp.concatenate([w_conv_dw, jnp.zeros((1, CONF_CH), F32)], axis=0)
    cvec = jnp.concatenate([b_conv_dw[None], ln_conv_g[None], ln_conv_b[None], jnp.zeros((5, CONF_CH), F32)], 0)

    row_spec = lambda w: pl.BlockSpec((tm, w), lambda i: (i, 0))
    q, k, v, bg, zs, yc = pl.pallas_call(
        functools.partial(_inproj_kernel, tiles_per_seq=T // tm, tm=tm),
        grid=(nt,),
        in_specs=[row_spec(D), _full((1, D)), _full(wqkv.shape), _full(wz.shape), _full(wbah.shape),
                  _full(wbal.shape), _full(wglu.shape), _full(w_conv_qkv.shape), _full(pa.shape),
                  _full((1, 2 * CONF_CH)), _full(cdw.shape), _full(cvec.shape)],
        out_specs=[row_spec(hw), row_spec(hw), row_spec(hw), row_spec(LANES), row_spec(hw), row_spec(CONF_CH)],
        out_shape=[jax.ShapeDtypeStruct((N, hw), F32)] * 3 + [jax.ShapeDtypeStruct((N, LANES), F32),
                   jax.ShapeDtypeStruct((N, hw), BF16), jax.ShapeDtypeStruct((N, CONF_CH), BF16)],
        scratch_shapes=[pltpu.VMEM((tm + QKV_HALO, 3 * hw), F32), pltpu.VMEM((tm + CONF_HALO, CONF_CH), F32)],
        compiler_params=_params(("arbitrary",)),
        name="inproj",
    )(x2, g_mix[None], wqkv, wz, wbah, wbal, wglu, w_conv_qkv, pa, b_glu[None], cdw, cvec)

    cb = 8
    rows = cb * GDN_CHUNK
    seq_spec = lambda w: pl.BlockSpec((B, rows, w), lambda n: (0, n, 0))
    r3 = lambda a: a.reshape(B, T, a.shape[-1])
    ygdn = pl.pallas_call(
        functools.partial(_gdn_kernel, cb=cb, nbatch=B, group=8 * GDN_HEADS),
        grid=(T // rows,),
        in_specs=[seq_spec(hw), seq_spec(hw), seq_spec(hw), seq_spec(LANES), seq_spec(hw),
                  pl.BlockSpec((1, GDN_DK), lambda n: (0, 0))],
        out_specs=seq_spec(hw),
        out_shape=jax.ShapeDtypeStruct((B, T, hw), BF16),
        scratch_shapes=[pltpu.VMEM((B * GDN_HEADS, GDN_DK, GDN_DK), F32)],
        compiler_params=_params(("arbitrary",)),
        name="gdn",
    )(r3(q), r3(k), r3(v), r3(bg), r3(zs), g_gdn_out[None]).reshape(N, hw)

    wo1 = w_out[:hw].astype(BF16)
    wo2 = w_out[hw:].astype(BF16)
    wrh, wrl = _split_bf16(_pad_lanes(jnp.concatenate([w_group_router, w_expert_router], axis=1)))
    nsub = 2 if tm % 256 == 0 else 1
    ltri = jnp.tril(jnp.ones((tm // nsub, tm // nsub), F32), -1).astype(BF16)
    assert D == TOKEN_TILE_ROWS * LANES
    tile_spec = pl.BlockSpec((tm * TOKEN_TILE_ROWS, LANES), lambda i: (i, 0))
    h1, xn, ri, rf, cnt = pl.pallas_call(
        functools.partial(_router_kernel, nsub=nsub),
        grid=(nt,),
        in_specs=[row_spec(D), row_spec(hw), row_spec(CONF_CH), _full(wo1.shape), _full(wo2.shape),
                  _full((1, D)), _full(wrh.shape), _full(wrl.shape), _full(ltri.shape)],
        out_specs=[row_spec(D), tile_spec, row_spec(LANES), row_spec(LANES), _full((8, LANES))],
        out_shape=[jax.ShapeDtypeStruct((N, D), F32), jax.ShapeDtypeStruct((N * TOKEN_TILE_ROWS, LANES), F32),
                   jax.ShapeDtypeStruct((N, LANES), I32), jax.ShapeDtypeStruct((N, LANES), F32),
                   jax.ShapeDtypeStruct((8, LANES), F32)],
        scratch_shapes=[pltpu.VMEM((1, LANES), F32)],
        compiler_params=_params(("arbitrary",)),
        name="router",
    )(x2, ygdn, yc, wo1, wo2, g_moe[None], wrh, wrl, ltri)

    nb = (2 * N) // EXPERT_BLOCK + N_EXPERTS
    nbl = -(-nb // LANES) * LANES
    tp = min(4 * tm, N)
    post, blk = pl.pallas_call(
        functools.partial(_pos_kernel, tm=tp),
        grid=(N // tp,),
        in_specs=[pl.BlockSpec((tp, LANES), lambda i: (i, 0)), _full((8, LANES))],
        out_specs=[pl.BlockSpec((8, tp), lambda i: (0, i)), _full((8, nbl))],
        out_shape=[jax.ShapeDtypeStruct((8, N), I32), jax.ShapeDtypeStruct((8, nbl), I32)],
        compiler_params=_params(("arbitrary",)),
        name="positions",
    )(ri, cnt)
    pos1 = post[0].reshape(nt, 1, tm)
    pos2 = post[1].reshape(nt, 1, tm)
    block_expert = blk[0, :nb]
    n_used = blk[1, :1]

    P = nb * EXPERT_BLOCK
    smem_spec = pl.BlockSpec((1, 1, tp), lambda i: (i, 0, 0), memory_space=pltpu.SMEM)
    flat_smem = pl.BlockSpec((P,), lambda i: (0,), memory_space=pltpu.SMEM)
    slots = pl.pallas_call(
        functools.partial(_slots_kernel, tm=tp),
        grid=(N // tp,),
        in_specs=[smem_spec, smem_spec],
        out_specs=flat_smem,
        out_shape=jax.ShapeDtypeStruct((P,), I32),
        scratch_shapes=[pltpu.VMEM((P,), I32), pltpu.SemaphoreType.DMA],
        compiler_params=_params(("arbitrary",)),
        name="slots",
    )(post[0].reshape(N // tp, 1, tp), post[1].reshape(N // tp, 1, tp)).reshape(nb, 1, EXPERT_BLOCK)

    any_spec = pl.BlockSpec(memory_space=pl.ANY)
    blk_smem = lambda off: pl.BlockSpec((1, 1, EXPERT_BLOCK),
                                        lambda j, be, nu: (jnp.minimum(j + off, nb - 1), 0, 0),
                                        memory_space=pltpu.SMEM)
    ys = pl.pallas_call(
        _expert_kernel,
        grid_spec=pltpu.PrefetchScalarGridSpec(
            num_scalar_prefetch=2,
            grid=(nb,),
            in_specs=[blk_smem(0), blk_smem(1), blk_smem(2), any_spec, any_spec, any_spec],
            out_specs=pl.BlockSpec((EXPERT_BLOCK * TOKEN_TILE_ROWS, LANES), lambda j, be, nu: (j, 0)),
            scratch_shapes=[pltpu.VMEM((D, 2 * D_EXPERT), BF16), pltpu.VMEM((D_EXPERT, D), BF16),
                            pltpu.VMEM((2, D, 2 * D_EXPERT), F32), pltpu.VMEM((2, D_EXPERT, D), F32),
                            pltpu.VMEM((GATHER_RING * EXPERT_BLOCK * TOKEN_TILE_ROWS, LANES), F32),
                            pltpu.SemaphoreType.DMA((GATHER_RING,)), pltpu.SemaphoreType.DMA((2,)),
                            pltpu.SMEM((1,), I32)]),
        out_shape=jax.ShapeDtypeStruct((P * TOKEN_TILE_ROWS, LANES), F32),
        compiler_params=_params(("arbitrary",)),
        name="experts",
    )(block_expert, n_used, slots, slots, slots, xn, w_up, w_down)

    wg = w_ple_gate.astype(BF16)
    wp = w_ple_proj.astype(BF16)
    pos_spec = lambda off: pl.BlockSpec((1, 1, tm), lambda i: (jnp.minimum(i + off, nt - 1), 0, 0),
                                        memory_space=pltpu.SMEM)
    out = pl.pallas_call(
        _ple_kernel,
        grid=(nt,),
        in_specs=[pos_spec(0), pos_spec(0), pos_spec(1), pos_spec(1),
                  row_spec(D), row_spec(LANES), row_spec(p2.shape[1]), any_spec,
                  _full((1, D)), _full(wg.shape), _full(wp.shape), _full((1, D))],
        out_specs=row_spec(D),
        out_shape=jax.ShapeDtypeStruct((N, D), F32),
        scratch_shapes=[pltpu.VMEM((2 * 2 * tm * TOKEN_TILE_ROWS, LANES), F32), pltpu.SemaphoreType.DMA((2,))],
        compiler_params=_params(("arbitrary",)),
        name="ple",
    )(pos1, pos2, pos1, pos2, h1, rf, p2, ys, g_ple[None], wg, wp, g_out[None])
    return out


def kernel(x, p, g_mix, w_in, w_conv_qkv, a_log, dt_bias, g_gdn_out, b_glu, w_conv_dw, b_conv_dw, ln_conv_g, ln_conv_b, w_out, g_moe, w_group_router, w_expert_router, w_up, w_down, g_ple, w_ple_gate, w_ple_proj, g_final):
    B, T, D = x.shape
    depth = p.shape[0]
    assert depth == 1, "the final RMSNorm is fused into the (single) layer's last kernel"
    h = _layer(x.reshape(B * T, D), p[0].reshape(B * T, -1), B, T, g_mix[0], w_in[0], w_conv_qkv[0],
               a_log[0], dt_bias[0], g_gdn_out[0], b_glu[0], w_conv_dw[0], b_conv_dw[0], ln_conv_g[0],
               ln_conv_b[0], w_out[0], g_moe[0], w_group_router[0], w_expert_router[0], w_up[0],
               w_down[0], g_ple[0], w_ple_gate[0], w_ple_proj[0], g_final)
    return h.reshape(B, T, D)
```
